```python
import jax, jax.numpy as jnp
from jax import lax
import numpy as np

D_MODEL = 1024
BATCH = 32
SEQ = 256
DEPTH = 2
DEC_BATCH = 2
DEC_SEQ = 2048
PAST_LEN = 256

GRID_W = 64
HEAD_DIM = 64
N_MOD = 9
D_FF = 2816
FFN_RES = 0.5
H_A = 8
NA_ROWS = 8
NA_COLS = 16
NA_KEY_COLS = 2 * NA_COLS
C_B = 512
CONV_W = 3
C_POOL = 512
POOL_WINDOWS = (2, 4, 8, 16)
N_POOL = 4
HQ_D = 8
HKV_D = 2
GQA_GROUP = HQ_D // HKV_D
ROPE_BASE = 10000.0
Q_BLOCK = 128
N_EVEN = (DEPTH + 1) // 2
N_ODD = DEPTH // 2
EVEN_IN = 3 * H_A * HEAD_DIM + 3 * C_B
EVEN_MIX = H_A * HEAD_DIM + C_B
ODD_IN = C_POOL + (HQ_D + 2 * HKV_D) * HEAD_DIM
ODD_MIX = C_POOL + HQ_D * HEAD_DIM
RMS_EPS = 1e-6
NEG_INF = -1e30

kernel_name = "hybrid_diffusion_prefix_trunk_step"


def rms_norm(x, g):
    xf = x.astype(jnp.float32)
    y = xf * lax.rsqrt(jnp.mean(xf * xf, axis=-1, keepdims=True) + RMS_EPS)
    return (y * g.astype(jnp.float32)).astype(x.dtype)


def modulate_norm(x, g, shift, scale):
    return rms_norm(x, g) * (1.0 + scale[:, None, :]) + shift[:, None, :]


def split_heads(t, n_heads):
    b, l, _ = t.shape
    return t.reshape(b, l, n_heads, HEAD_DIM).transpose(0, 2, 1, 3)


def merge_heads(t):
    b, h, l, d = t.shape
    return t.transpose(0, 2, 1, 3).reshape(b, l, h * d)


def swiglu(h, w1, w2):
    g, u = jnp.split(h @ w1, 2, axis=-1)
    return (jax.nn.silu(g) * u) @ w2


def axial_rope_tables(n_tokens):
    t = jnp.arange(n_tokens)
    row = (t // GRID_W).astype(jnp.float32)
    col = (t % GRID_W).astype(jnp.float32)
    n_freq = HEAD_DIM // 4
    inv = ROPE_BASE ** (-jnp.arange(n_freq, dtype=jnp.float32) / n_freq)
    ang = jnp.concatenate([row[:, None] * inv, col[:, None] * inv], axis=-1)
    return jnp.cos(ang), jnp.sin(ang)


def apply_rope(x, cos, sin):
    xr = x.astype(jnp.float32).reshape(*x.shape[:-1], HEAD_DIM // 2, 2)
    x0, x1 = xr[..., 0], xr[..., 1]
    out = jnp.stack([x0 * cos - x1 * sin, x0 * sin + x1 * cos], axis=-1)
    return out.reshape(x.shape).astype(x.dtype)


def dense_attention(q, k, v):
    b, hkv, g, lq, d = q.shape
    nb = lq // Q_BLOCK
    qb = q.reshape(b, hkv, g, nb, Q_BLOCK, d).transpose(3, 0, 1, 2, 4, 5)
    scale = HEAD_DIM ** -0.5

    def one_block(qi):
        s = jnp.einsum('bhgqd,bhkd->bhgqk', qi, k).astype(jnp.float32) * scale
        p = jax.nn.softmax(s, axis=-1).astype(v.dtype)
        return jnp.einsum('bhgqk,bhkd->bhgqd', p, v)

    o = lax.map(one_block, qb)
    return o.transpose(1, 2, 3, 0, 4, 5).reshape(b, hkv, g, lq, d)


def neighbourhood_attention(q, k, v, ck, cv, rpb):
    b, h, l, d = q.shape
    rows = l // GRID_W
    kr = min(NA_ROWS, rows)
    ncb = GRID_W // NA_COLS
    r = jnp.arange(rows)
    rs = jnp.clip(r - kr // 2, 0, rows - kr)
    rows_idx = rs[:, None] + jnp.arange(kr)[None, :]
    cq = jnp.arange(GRID_W).reshape(ncb, NA_COLS)
    cs_q = jnp.clip(cq - NA_COLS // 2, 0, GRID_W - NA_COLS)
    kb0 = jnp.clip(jnp.arange(ncb) * NA_COLS - NA_COLS // 2, 0, GRID_W - NA_KEY_COLS)
    cols_idx = kb0[:, None] + jnp.arange(NA_KEY_COLS)[None, :]
    kg = k.reshape(b, h, rows, GRID_W, d)
    vg = v.reshape(b, h, rows, GRID_W, d)
    ridx = rows_idx[:, None, :, None]
    cidx = cols_idx[None, :, None, :]
    kb = kg[:, :, ridx, cidx].reshape(b, h, rows, ncb, kr * NA_KEY_COLS, d)
    vb = vg[:, :, ridx, cidx].reshape(b, h, rows, ncb, kr * NA_KEY_COLS, d)
    kcol = cols_idx[:, None, :]
    valid = (kcol >= cs_q[..., None]) & (kcol < cs_q[..., None] + NA_COLS)
    col_off = jnp.clip(kcol - cq[..., None] + NA_COLS - 1, 0, 2 * NA_COLS - 2)
    row_off = rows_idx - r[:, None] + NA_ROWS - 1
    bias = rpb.astype(jnp.float32)[:, row_off[:, None, None, :, None], col_off[None, :, :, None, :]]
    bias = jnp.where(valid[None, None, :, :, None, :], bias, NEG_INF)
    bias = bias.reshape(h, rows, ncb, NA_COLS, kr * NA_KEY_COLS)
    qg = q.reshape(b, h, rows, ncb, NA_COLS, d)
    scale = HEAD_DIM ** -0.5
    s_loc = jnp.einsum('bhrjqd,bhrjkd->bhrjqk', qg, kb).astype(jnp.float32) * scale + bias[None]
    s_ctx = jnp.einsum('bhrjqd,bhkd->bhrjqk', qg, ck).astype(jnp.float32) * scale
    p = jax.nn.softmax(jnp.concatenate([s_loc, s_ctx], axis=-1), axis=-1).astype(v.dtype)
    n_loc = kr * NA_KEY_COLS
    o = (jnp.einsum('bhrjqk,bhrjkd->bhrjqd', p[..., :n_loc], vb)
         + jnp.einsum('bhrjqk,bhkd->bhrjqd', p[..., n_loc:], cv))
    return o.reshape(b, h, l, d)


def short_conv(x, w, bias):
    l = x.shape[1]
    pad = CONV_W // 2
    xp = jnp.pad(x, ((0, 0), (pad, CONV_W - 1 - pad), (0, 0)))
    y = xp[:, 0:l] * w[0]
    for j in range(1, CONV_W):
        y = y + xp[:, j:j + l] * w[j]
    return y + bias


def window_mean(csum, l, win):
    t = jnp.arange(l)
    lo = jnp.maximum(t - win // 2, 0)
    hi = jnp.minimum(t + win - win // 2, l)
    return (csum[:, hi] - csum[:, lo]) / (hi - lo).astype(jnp.float32)[None, :, None]


def even_mixer(h, w_in, rpb, conv_w, conv_b, w_out, ctx_kv):
    u = h @ w_in
    qa, ka, va, bg, cg, xb = jnp.split(u, 6, axis=-1)
    qa, ka, va = split_heads(qa, H_A), split_heads(ka, H_A), split_heads(va, H_A)
    if ctx_kv is None:
        a = dense_attention(qa[:, :, None], ka, va)[:, :, 0]
        new_kv = (ka, va)
    else:
        a = neighbourhood_attention(qa, ka, va, ctx_kv[0], ctx_kv[1], rpb)
        new_kv = None
    y_b = bg * short_conv(cg * xb, conv_w, conv_b)
    out = jnp.concatenate([merge_heads(a), y_b], axis=-1) @ w_out
    return out, new_kv


def odd_mixer(h, w_in, pool_w, pool_scale, q_norm, k_norm, w_out, ctx_kv):
    b, l, _ = h.shape
    u = h @ w_in
    uc, qd, kd, vd = jnp.split(u, [C_POOL, C_POOL + HQ_D * HEAD_DIM, C_POOL + (HQ_D + HKV_D) * HEAD_DIM], axis=-1)
    ug = uc.reshape(b, l, N_POOL, C_POOL // N_POOL)
    csum = jnp.concatenate([jnp.zeros_like(ug[:, :1], dtype=jnp.float32),
                            jnp.cumsum(ug.astype(jnp.float32), axis=1)], axis=1)
    pooled = jnp.stack([window_mean(csum[:, :, gi], l, w) for gi, w in enumerate(POOL_WINDOWS)], axis=2)
    pooled = (pooled - ug.astype(jnp.float32)).astype(h.dtype)
    yc = jnp.einsum('blgc,gce->blge', pooled, pool_w).reshape(b, l, C_POOL) * pool_scale
    q = rms_norm(split_heads(qd, HQ_D), q_norm)
    k = rms_norm(split_heads(kd, HKV_D), k_norm)
    v = split_heads(vd, HKV_D)
    if ctx_kv is None:
        k_all, v_all = k, v
        new_kv = (k, v)
    else:
        cos, sin = axial_rope_tables(l)
        q = apply_rope(q, cos, sin)
        k_all = jnp.concatenate([apply_rope(k, cos, sin), ctx_kv[0]], axis=2)
        v_all = jnp.concatenate([v, ctx_kv[1]], axis=2)
        new_kv = None
    o = dense_attention(q.reshape(b, HKV_D, GQA_GROUP, l, HEAD_DIM), k_all, v_all).reshape(b, HQ_D, l, HEAD_DIM)
    out = jnp.concatenate([yc, merge_heads(o)], axis=-1) @ w_out
    return out, new_kv


def trunk(x, cond, cache, mod_w, mod_b, norm_w, ffn_w1, ffn_w2,
          ev_w_in, ev_rpb, ev_conv_w, ev_conv_b, ev_w_out,
          od_w_in, od_pool_w, od_pool_scale, od_q_norm, od_k_norm, od_w_out):
    a_ks, a_vs, d_ks, d_vs = [], [], [], []
    for l in range(DEPTH):
        m = (jax.nn.silu(cond) @ mod_w[l] + mod_b[l]).reshape(cond.shape[0], N_MOD, D_MODEL)
        g = norm_w[l]
        hf = modulate_norm(x, g[0], m[:, 0], m[:, 1])
        x = x + FFN_RES * m[:, 2, None] * rms_norm(swiglu(hf, ffn_w1[l, 0], ffn_w2[l, 0]), g[1])
        hm = modulate_norm(x, g[2], m[:, 3], m[:, 4])
        i = l // 2
        if l % 2 == 0:
            ctx = None if cache is None else (cache[0][:, i], cache[1][:, i])
            out, kv = even_mixer(hm, ev_w_in[i], ev_rpb[i], ev_conv_w[i], ev_conv_b[i], ev_w_out[i], ctx)
            if kv is not None:
                a_ks.append(kv[0])
                a_vs.append(kv[1])
        else:
            ctx = None if cache is None else (cache[2][:, i], cache[3][:, i])
            out, kv = odd_mixer(hm, od_w_in[i], od_pool_w[i], od_pool_scale[i], od_q_norm[i], od_k_norm[i],
                                od_w_out[i], ctx)
            if kv is not None:
                d_ks.append(kv[0])
                d_vs.append(kv[1])
        x = x + m[:, 5, None] * rms_norm(out, g[3])
        hf = modulate_norm(x, g[4], m[:, 6], m[:, 7])
        x = x + FFN_RES * m[:, 8, None] * rms_norm(swiglu(hf, ffn_w1[l, 1], ffn_w2[l, 1]), g[5])
    return x, (a_ks, a_vs, d_ks, d_vs)


def setup_inputs(seed: int = 0) -> dict:
    key = jax.random.key(seed)
    ks = jax.random.split(key, 32)
    D = D_MODEL

    def nrm(k, shape, s):
        return jax.random.normal(k, shape, jnp.float32) * s

    return {
        "x_prompt": nrm(ks[0], (BATCH, SEQ, D), 1.0),
        "x_sample": nrm(ks[1], (DEC_BATCH, DEC_SEQ, D), 1.0),
        "cache_a_k": nrm(ks[2], (DEC_BATCH, N_EVEN, H_A, PAST_LEN, HEAD_DIM), 1.0),
        "cache_a_v": nrm(ks[3], (DEC_BATCH, N_EVEN, H_A, PAST_LEN, HEAD_DIM), 1.0),
        "cache_d_k": nrm(ks[4], (DEC_BATCH, N_ODD, HKV_D, PAST_LEN, HEAD_DIM), 1.0),
        "cache_d_v": nrm(ks[5], (DEC_BATCH, N_ODD, HKV_D, PAST_LEN, HEAD_DIM), 1.0),
        "c": nrm(ks[6], (DEC_BATCH, D), 1.0),
        "c_ctx": nrm(ks[7], (D,), 1.0),
        "mod_w": nrm(ks[8], (DEPTH, D, N_MOD * D), 0.5 * D ** -0.5),
        "mod_b": nrm(ks[9], (DEPTH, N_MOD * D), 0.01),
        "norm_w": 1.0 + nrm(ks[10], (DEPTH, 6, D), 0.05),
        "ffn_w1": nrm(ks[11], (DEPTH, 2, D, 2 * D_FF), D ** -0.5),
        "ffn_w2": nrm(ks[12], (DEPTH, 2, D_FF, D), D_FF ** -0.5),
        "ev_w_in": nrm(ks[13], (N_EVEN, D, EVEN_IN), D ** -0.5),
        "ev_rpb": nrm(ks[14], (N_EVEN, H_A, 2 * NA_ROWS - 1, 2 * NA_COLS - 1), 0.1),
        "ev_conv_w": nrm(ks[15], (N_EVEN, CONV_W, C_B), CONV_W ** -0.5),
        "ev_conv_b": nrm(ks[16], (N_EVEN, C_B), 0.01),
        "ev_w_out": nrm(ks[17], (N_EVEN, EVEN_MIX, D), EVEN_MIX ** -0.5),
        "od_w_in": nrm(ks[18], (N_ODD, D, ODD_IN), D ** -0.5),
        "od_pool_w": nrm(ks[19], (N_ODD, N_POOL, C_POOL // N_POOL, C_POOL // N_POOL), (C_POOL // N_POOL) ** -0.5),
        "od_pool_scale": 1.0 + nrm(ks[20], (N_ODD, C_POOL), 0.05),
        "od_q_norm": 1.0 + nrm(ks[21], (N_ODD, HEAD_DIM), 0.05),
        "od_k_norm": 1.0 + nrm(ks[22], (N_ODD, HEAD_DIM), 0.05),
        "od_w_out": nrm(ks[23], (N_ODD, ODD_MIX, D), ODD_MIX ** -0.5),
    }


def reference(x_prompt, x_sample, cache_a_k, cache_a_v, cache_d_k, cache_d_v, c, c_ctx,
              mod_w, mod_b, norm_w, ffn_w1, ffn_w2,
              ev_w_in, ev_rpb, ev_conv_w, ev_conv_b, ev_w_out,
              od_w_in, od_pool_w, od_pool_scale, od_q_norm, od_k_norm, od_w_out):
    weights = (mod_w, mod_b, norm_w, ffn_w1, ffn_w2,
               ev_w_in, ev_rpb, ev_conv_w, ev_conv_b, ev_w_out,
               od_w_in, od_pool_w, od_pool_scale, od_q_norm, od_k_norm, od_w_out)
    y_prompt, (a_ks, a_vs, d_ks, d_vs) = trunk(x_prompt, c_ctx[None, :], None, *weights)
    state_a_k = jnp.stack(a_ks, axis=1)
    state_a_v = jnp.stack(a_vs, axis=1)
    state_d_k = jnp.stack(d_ks, axis=1)
    state_d_v = jnp.stack(d_vs, axis=1)
    y_sample, _ = trunk(x_sample, c, (cache_a_k, cache_a_v, cache_d_k, cache_d_v), *weights)
    return (y_prompt, y_sample, state_a_k, state_a_v, state_d_k, state_d_v)
```

```python
import functools

import jax
import jax.numpy as jnp
import numpy as np
from jax import lax
from jax.experimental import pallas as pl
from jax.experimental.pallas import tpu as pltpu

D_MODEL = 1024
BATCH = 32
SEQ = 256
DEPTH = 2
DEC_BATCH = 2
DEC_SEQ = 2048
PAST_LEN = 256
GRID_W = 64
HEAD_DIM = 64
N_MOD = 9
D_FF = 2816
FFN_RES = 0.5
H_A = 8
NA_ROWS = 8
NA_COLS = 16
C_B = 512
CONV_W = 3
C_POOL = 512
POOL_WINDOWS = (2, 4, 8, 16)
N_POOL = 4
POOL_C = C_POOL // N_POOL
HQ_D = 8
HKV_D = 2
GQA_GROUP = HQ_D // HKV_D
ROPE_BASE = 10000.0
EVEN_IN = 3 * H_A * HEAD_DIM + 3 * C_B
ODD_IN = C_POOL + (HQ_D + 2 * HKV_D) * HEAD_DIM
RMS_EPS = 1e-6
NEG_INF = -1e30
ATT_SCALE = HEAD_DIM ** -0.5

GRID_ROWS = DEC_SEQ // GRID_W
N_PROMPT = BATCH * SEQ
N_SAMPLE = DEC_BATCH * DEC_SEQ
N_TOK = N_PROMPT + N_SAMPLE
N_COND = 1 + DEC_BATCH
COND_PAD = 8
QK_W = (HQ_D + HKV_D) * HEAD_DIM
POOL_PAD = 8

TM = 512
N_TILES = N_TOK // TM
PROMPT_TILES = N_PROMPT // TM
TILES_PER_SAMPLE = DEC_SEQ // TM
FF_CHUNK = 1408
MOD_TN = 1536
Q_BLK = 256
VMEM_LIMIT = 56 * 1024 * 1024

F32 = jnp.float32
BF16 = jnp.bfloat16


def _params(n_grid):
    return pltpu.CompilerParams(dimension_semantics=("arbitrary",) * n_grid, vmem_limit_bytes=VMEM_LIMIT)


def _cond_of_tile(i):
    return jnp.where(i < PROMPT_TILES, 0, 1 + (i - PROMPT_TILES) // TILES_PER_SAMPLE)


def _rms(x, g):
    return x * lax.rsqrt(jnp.mean(x * x, axis=-1, keepdims=True) + RMS_EPS) * g


def _sigmoid(x):
    return 1.0 / (1.0 + jnp.exp(-x))


def _dot(a, b):
    return jnp.dot(a, b, preferred_element_type=F32)


def _dot_t(a, b):
    return lax.dot_general(a, b, (((1,), (1,)), ((), ())), preferred_element_type=F32)


def _mod_kernel(c_ref, w_ref, b_ref, o_ref):
    c = c_ref[...]
    sc = (c * _sigmoid(c)).astype(BF16)
    o_ref[...] = _dot(sc, w_ref[...].astype(BF16)) + b_ref[...]


def _modulation(cond, mod_w, mod_b):
    n_col = N_MOD * D_MODEL
    return pl.pallas_call(
        _mod_kernel,
        grid=(DEPTH, n_col // MOD_TN),
        in_specs=[
            pl.BlockSpec((COND_PAD, D_MODEL), lambda l, j: (0, 0)),
            pl.BlockSpec((None, D_MODEL, MOD_TN), lambda l, j: (l, 0, j)),
            pl.BlockSpec((None, 1, MOD_TN), lambda l, j: (l, 0, j)),
        ],
        out_specs=pl.BlockSpec((None, COND_PAD, MOD_TN), lambda l, j: (l, 0, j)),
        out_shape=jax.ShapeDtypeStruct((DEPTH, COND_PAD, n_col), F32),
        compiler_params=_params(2),
        name="modulation",
    )(cond, mod_w, mod_b.reshape(DEPTH, 1, n_col))


def _ffn_kernel(x_ref, m_ref, g_ref, w1_ref, w2_ref, o_ref, *, sub):
    x = x_ref[...]
    shift = m_ref[3 * sub:3 * sub + 1, :]
    scale = m_ref[3 * sub + 1:3 * sub + 2, :]
    gate = m_ref[3 * sub + 2:3 * sub + 3, :]
    h = (_rms(x, g_ref[2 * sub:2 * sub + 1, :]) * (1.0 + scale) + shift).astype(BF16)
    acc = jnp.zeros((TM, D_MODEL), F32)
    for j in range(D_FF // FF_CHUNK):
        lo = j * FF_CHUNK
        a = _dot(h, w1_ref[:, lo:lo + FF_CHUNK])
        u = _dot(h, w1_ref[:, D_FF + lo:D_FF + lo + FF_CHUNK])
        act = (a * _sigmoid(a) * u).astype(BF16)
        acc = acc + _dot(act, w2_ref[lo:lo + FF_CHUNK, :])
    y = _rms(acc, g_ref[2 * sub + 1:2 * sub + 2, :])
    o_ref[...] = x + FFN_RES * gate * y


def _ffn(x, m_l, g_l, w1, w2, sub):
    resident = pl.Buffered(1)
    return pl.pallas_call(
        functools.partial(_ffn_kernel, sub=sub),
        grid=(N_TILES,),
        in_specs=[
            pl.BlockSpec((TM, D_MODEL), lambda i: (i, 0)),
            pl.BlockSpec((None, N_MOD, D_MODEL), lambda i: (_cond_of_tile(i), 0, 0)),
            pl.BlockSpec((6, D_MODEL), lambda i: (0, 0)),
            pl.BlockSpec((D_MODEL, 2 * D_FF), lambda i: (0, 0), pipeline_mode=resident),
            pl.BlockSpec((D_FF, D_MODEL), lambda i: (0, 0), pipeline_mode=resident),
        ],
        out_specs=pl.BlockSpec((TM, D_MODEL), lambda i: (i, 0)),
        out_shape=jax.ShapeDtypeStruct((N_TOK, D_MODEL), F32),
        compiler_params=_params(1),
        name=f"ffn{sub}",
    )(x, m_l, g_l, w1, w2)


def _mixer_norm(x_ref, m_ref, g_ref):
    return (_rms(x_ref[...], g_ref[2:3, :]) * (1.0 + m_ref[4:5, :]) + m_ref[3:4, :]).astype(BF16)


def _even_in_kernel(x_ref, m_ref, g_ref, w_ref, q_ref, k_ref, v_ref, bg_ref, z_ref):
    u = _dot(_mixer_norm(x_ref, m_ref, g_ref), w_ref[...])
    hd = H_A * HEAD_DIM
    q_ref[...] = (u[:, 0:hd] * ATT_SCALE).astype(BF16)
    k_ref[...] = u[:, hd:2 * hd]
    v_ref[...] = u[:, 2 * hd:3 * hd]
    bg_ref[...] = u[:, 3 * hd:3 * hd + C_B]
    z_ref[...] = u[:, 3 * hd + C_B:3 * hd + 2 * C_B] * u[:, 3 * hd + 2 * C_B:3 * hd + 3 * C_B]


def _even_in(x, m_l, g_l, w_in):
    hd = H_A * HEAD_DIM
    tok = lambda w: pl.BlockSpec((TM, w), lambda i: (i, 0))
    return pl.pallas_call(
        _even_in_kernel,
        grid=(N_TILES,),
        in_specs=[
            tok(D_MODEL),
            pl.BlockSpec((None, N_MOD, D_MODEL), lambda i: (_cond_of_tile(i), 0, 0)),
            pl.BlockSpec((6, D_MODEL), lambda i: (0, 0)),
            pl.BlockSpec((D_MODEL, EVEN_IN), lambda i: (0, 0)),
        ],
        out_specs=[tok(hd), tok(hd), tok(hd), tok(C_B), tok(C_B)],
        out_shape=[
            jax.ShapeDtypeStruct((N_TOK, hd), BF16),
            jax.ShapeDtypeStruct((N_TOK, hd), F32),
            jax.ShapeDtypeStruct((N_TOK, hd), F32),
            jax.ShapeDtypeStruct((N_TOK, C_B), F32),
            jax.ShapeDtypeStruct((N_TOK, C_B), F32),
        ],
        compiler_params=_params(1),
        name="even_in",
    )(x, m_l, g_l, w_in)


def _swap_pairs(x):
    n = x.shape[-1]
    lane = lax.broadcasted_iota(jnp.int32, x.shape, x.ndim - 1)
    return jnp.where(lane % 2 == 0, pltpu.roll(x, n - 1, x.ndim - 1), pltpu.roll(x, 1, x.ndim - 1))


def _odd_in_kernel(x_ref, m_ref, g_ref, w_ref, ng_ref, cos_ref, sin_ref, ones_ref,
                   uc_ref, q_ref, k_ref, v_ref):
    u = _dot(_mixer_norm(x_ref, m_ref, g_ref), w_ref[...])
    uc_ref[...] = u[:, 0:C_POOL]
    qk = u[:, C_POOL:C_POOL + QK_W]
    sq = qk * qk
    hi = sq.astype(BF16)
    lo = (sq - hi.astype(F32)).astype(BF16)
    ones = ones_ref[...]
    ms = (_dot(hi, ones) + _dot(lo, ones)) * (1.0 / HEAD_DIM)
    n = qk * lax.rsqrt(ms + RMS_EPS) * ng_ref[...]
    r = n * cos_ref[...] + _swap_pairs(n) * sin_ref[...]
    hq = HQ_D * HEAD_DIM
    q_ref[...] = (r[:, 0:hq] * ATT_SCALE).astype(BF16)
    k_ref[...] = r[:, hq:QK_W]
    v_ref[...] = u[:, C_POOL + QK_W:ODD_IN]


def _rope_tile_index(i):
    return jnp.where(i < PROMPT_TILES, 0, 1 + (i - PROMPT_TILES) % TILES_PER_SAMPLE)


def _odd_in(x, m_l, g_l, w_in, qk_gain, cos_t, sin_t, ones_bd):
    hq, hkv = HQ_D * HEAD_DIM, HKV_D * HEAD_DIM
    tok = lambda w: pl.BlockSpec((TM, w), lambda i: (i, 0))
    return pl.pallas_call(
        _odd_in_kernel,
        grid=(N_TILES,),
        in_specs=[
            tok(D_MODEL),
            pl.BlockSpec((None, N_MOD, D_MODEL), lambda i: (_cond_of_tile(i), 0, 0)),
            pl.BlockSpec((6, D_MODEL), lambda i: (0, 0)),
            pl.BlockSpec((D_MODEL, ODD_IN), lambda i: (0, 0)),
            pl.BlockSpec((1, QK_W), lambda i: (0, 0)),
            pl.BlockSpec((TM, QK_W), lambda i: (_rope_tile_index(i), 0)),
            pl.BlockSpec((TM, QK_W), lambda i: (_rope_tile_index(i), 0)),
            pl.BlockSpec((QK_W, QK_W), lambda i: (0, 0)),
        ],
        out_specs=[tok(C_POOL), tok(hq), tok(hkv), tok(hkv)],
        out_shape=[
            jax.ShapeDtypeStruct((N_TOK, C_POOL), F32),
            jax.ShapeDtypeStruct((N_TOK, hq), BF16),
            jax.ShapeDtypeStruct((N_TOK, hkv), F32),
            jax.ShapeDtypeStruct((N_TOK, hkv), F32),
        ],
        compiler_params=_params(1),
        name="odd_in",
    )(x, m_l, g_l, w_in, qk_gain, cos_t, sin_t, ones_bd)


def _softmax_pv(scores, values):
    m = scores[0].max(axis=-1, keepdims=True)
    for s in scores[1:]:
        m = jnp.maximum(m, s.max(axis=-1, keepdims=True))
    den = None
    num = None
    for s, v in zip(scores, values):
        e = jnp.exp(s - m)
        d = e.sum(axis=-1, keepdims=True)
        o = _dot(e.astype(BF16), v)
        den = d if den is None else den + d
        num = o if num is None else num + o
    return num / den


def _short_conv(z, cw_ref, cb_ref, seq_len):
    rows = z.shape[0]
    pos = lax.broadcasted_iota(jnp.int32, z.shape, 0)
    z_prev = jnp.where(pos == 0, 0.0, pltpu.roll(z, 1, 0))
    z_next = jnp.where(pos == seq_len - 1, 0.0, pltpu.roll(z, rows - 1, 0))
    y = z_prev * cw_ref[0:1, :]
    y = y + z * cw_ref[1:2, :]
    y = y + z_next * cw_ref[2:3, :]
    return y + cb_ref[...]


def _pool_mix(uc, pw_ref, ps_ref, seq_len):
    pos = lax.broadcasted_iota(jnp.int32, (seq_len, POOL_C), 0)
    pad = jnp.zeros((POOL_PAD, POOL_C), F32)
    n_ext = seq_len + 2 * POOL_PAD
    outs = []
    for gi, win in enumerate(POOL_WINDOWS):
        ug = uc[:, gi * POOL_C:(gi + 1) * POOL_C]
        run = jnp.concatenate([pad, ug, pad], axis=0)
        span = 1
        while span < win:
            run = run + pltpu.roll(run, span, 0)
            span *= 2
        back = win // 2 - 1
        if back:
            run = pltpu.roll(run, n_ext - back, 0)
        wsum = run[POOL_PAD:POOL_PAD + seq_len, :]
        cnt = jnp.minimum(pos + (win - win // 2), seq_len) - jnp.maximum(pos - win // 2, 0)
        pooled = (wsum / cnt.astype(F32) - ug).astype(BF16)
        outs.append(_dot(pooled, pw_ref[gi]))
    return jnp.concatenate(outs, axis=-1) * ps_ref[...]


def _even_prompt_kernel(q_ref, k_ref, v_ref, bg_ref, z_ref, cw_ref, cb_ref,
                        fa_ref, fb_ref, sk_ref, sv_ref):
    q = q_ref[...]
    k = k_ref[...]
    v = v_ref[...]
    outs = []
    for h in range(H_A):
        sl = slice(h * HEAD_DIM, (h + 1) * HEAD_DIM)
        kh = k[:, sl]
        vh = v[:, sl]
        sk_ref[0, 0, h] = kh
        sv_ref[0, 0, h] = vh
        s = _dot_t(q[:, sl], kh.astype(BF16))
        outs.append(_softmax_pv([s], [vh.astype(BF16)]))
    fa_ref[...] = jnp.concatenate(outs, axis=-1).astype(BF16)
    fb_ref[...] = (bg_ref[...] * _short_conv(z_ref[...], cw_ref, cb_ref, SEQ)).astype(BF16)


def _even_prompt(q, k, v, bg, z, conv_w, conv_b):
    hd = H_A * HEAD_DIM
    seq = lambda w: pl.BlockSpec((SEQ, w), lambda b: (b, 0))
    state = pl.BlockSpec((1, 1, H_A, SEQ, HEAD_DIM), lambda b: (b, 0, 0, 0, 0))
    return pl.pallas_call(
        _even_prompt_kernel,
        grid=(BATCH,),
        in_specs=[seq(hd), seq(hd), seq(hd), seq(C_B), seq(C_B),
                  pl.BlockSpec((CONV_W, C_B), lambda b: (0, 0)),
                  pl.BlockSpec((1, C_B), lambda b: (0, 0))],
        out_specs=[seq(hd), seq(C_B), state, state],
        out_shape=[
            jax.ShapeDtypeStruct((N_TOK, hd), BF16),
            jax.ShapeDtypeStruct((N_TOK, C_B), BF16),
            jax.ShapeDtypeStruct((BATCH, 1, H_A, SEQ, HEAD_DIM), F32),
            jax.ShapeDtypeStruct((BATCH, 1, H_A, SEQ, HEAD_DIM), F32),
        ],
        compiler_params=_params(1),
        name="even_prompt",
    )(q, k, v, bg, z, conv_w, conv_b)


def _even_sample_kernel(q_ref, k_ref, v_ref, bg_ref, z_ref, ck_ref, cv_ref, bias_ref, cw_ref, cb_ref,
                        fa_in_ref, fb_in_ref, fa_ref, fb_ref):
    del fa_in_ref, fb_in_ref
    ck = [ck_ref[0, h].astype(BF16) for h in range(2)]
    cv = [cv_ref[0, h].astype(BF16) for h in range(2)]
    win = NA_ROWS * GRID_W

    def row(r, carry):
        rs = jnp.clip(r - NA_ROWS // 2, 0, GRID_ROWS - NA_ROWS)
        q0 = pl.multiple_of(r * GRID_W, GRID_W)
        k0 = pl.multiple_of(rs * GRID_W, GRID_W)
        qr = q_ref[pl.ds(q0, GRID_W), :]
        kw = k_ref[pl.ds(k0, win), :].astype(BF16)
        vw = v_ref[pl.ds(k0, win), :].astype(BF16)
        outs = []
        for h in range(2):
            sl = slice(h * HEAD_DIM, (h + 1) * HEAD_DIM)
            qh = qr[:, sl]
            s_loc = _dot_t(qh, kw[:, sl]) + bias_ref[h, r - rs]
            s_ctx = _dot_t(qh, ck[h])
            outs.append(_softmax_pv([s_loc, s_ctx], [vw[:, sl], cv[h]]))
        fa_ref[pl.ds(q0, GRID_W), :] = jnp.concatenate(outs, axis=-1).astype(BF16)
        return carry

    lax.fori_loop(0, GRID_ROWS, row, 0)
    fb_ref[...] = (bg_ref[...] * _short_conv(z_ref[...], cw_ref, cb_ref, DEC_SEQ)).astype(BF16)


def _even_sample(q, k, v, bg, z, ctx_k, ctx_v, bias, conv_w, conv_b, fa, fb):
    hd = H_A * HEAD_DIM
    pair = 2 * HEAD_DIM
    first = N_PROMPT // DEC_SEQ
    seq = pl.BlockSpec((DEC_SEQ, pair), lambda b, p: (first + b, p))
    ctx = pl.BlockSpec((1, 2, PAST_LEN, HEAD_DIM), lambda b, p: (b, p, 0, 0))
    return pl.pallas_call(
        _even_sample_kernel,
        grid=(DEC_BATCH, H_A // 2),
        in_specs=[seq, seq, seq, seq, seq, ctx, ctx,
                  pl.BlockSpec((2, NA_ROWS, GRID_W, NA_ROWS * GRID_W), lambda b, p: (p, 0, 0, 0)),
                  pl.BlockSpec((CONV_W, pair), lambda b, p: (0, p)),
                  pl.BlockSpec((1, pair), lambda b, p: (0, p)),
                  pl.BlockSpec(memory_space=pl.ANY),
                  pl.BlockSpec(memory_space=pl.ANY)],
        out_specs=[seq, seq],
        out_shape=[
            jax.ShapeDtypeStruct((N_TOK, hd), BF16),
            jax.ShapeDtypeStruct((N_TOK, C_B), BF16),
        ],
        input_output_aliases={10: 0, 11: 1},
        compiler_params=_params(2),
        name="even_sample",
    )(q, k, v, bg, z, ctx_k, ctx_v, bias, conv_w, conv_b, fa, fb)


def _odd_prompt_kernel(uc_ref, q_ref, k_ref, v_ref, pw_ref, ps_ref, fc_ref, fd_ref, sk_ref, sv_ref):
    q = q_ref[...]
    k = k_ref[...]
    v = v_ref[...]
    outs = []
    for g in range(HKV_D):
        sl = slice(g * HEAD_DIM, (g + 1) * HEAD_DIM)
        kg = k[:, sl]
        vg = v[:, sl]
        sk_ref[0, 0, g] = kg
        sv_ref[0, 0, g] = vg
        kb = kg.astype(BF16)
        vb = vg.astype(BF16)
        for j in range(GQA_GROUP):
            h = g * GQA_GROUP + j
            s = _dot_t(q[:, h * HEAD_DIM:(h + 1) * HEAD_DIM], kb)
            outs.append(_softmax_pv([s], [vb]))
    fd_ref[...] = jnp.concatenate(outs, axis=-1).astype(BF16)
    fc_ref[...] = _pool_mix(uc_ref[...], pw_ref, ps_ref, SEQ).astype(BF16)


def _odd_prompt(uc, q, k, v, pool_w, pool_scale):
    hq, hkv = HQ_D * HEAD_DIM, HKV_D * HEAD_DIM
    seq = lambda w: pl.BlockSpec((SEQ, w), lambda b: (b, 0))
    state = pl.BlockSpec((1, 1, HKV_D, SEQ, HEAD_DIM), lambda b: (b, 0, 0, 0, 0))
    return pl.pallas_call(
        _odd_prompt_kernel,
        grid=(BATCH,),
        in_specs=[seq(C_POOL), seq(hq), seq(hkv), seq(hkv),
                  pl.BlockSpec((N_POOL, POOL_C, POOL_C), lambda b: (0, 0, 0)),
                  pl.BlockSpec((1, C_POOL), lambda b: (0, 0))],
        out_specs=[seq(C_POOL), seq(hq), state, state],
        out_shape=[
            jax.ShapeDtypeStruct((N_TOK, C_POOL), BF16),
            jax.ShapeDtypeStruct((N_TOK, hq), BF16),
            jax.ShapeDtypeStruct((BATCH, 1, HKV_D, SEQ, HEAD_DIM), F32),
            jax.ShapeDtypeStruct((BATCH, 1, HKV_D, SEQ, HEAD_DIM), F32),
        ],
        compiler_params=_params(1),
        name="odd_prompt",
    )(uc, q, k, v, pool_w, pool_scale)


def _odd_sample_kernel(uc_ref, q_ref, k_ref, v_ref, ck_ref, cv_ref, pw_ref, ps_ref, fc_in_ref, fd_in_ref,
                       fc_ref, fd_ref):
    del fc_in_ref, fd_in_ref

    @pl.when(pl.program_id(1) == 0)
    def _():
        fc_ref[...] = _pool_mix(uc_ref[...], pw_ref, ps_ref, DEC_SEQ).astype(BF16)

    q = q_ref[...]
    outs = []
    for g in range(HKV_D):
        sl = slice(g * HEAD_DIM, (g + 1) * HEAD_DIM)
        kb = k_ref[:, sl].astype(BF16)
        vb = v_ref[:, sl].astype(BF16)
        ckb = ck_ref[0, g].astype(BF16)
        cvb = cv_ref[0, g].astype(BF16)
        for j in range(GQA_GROUP):
            h = g * GQA_GROUP + j
            qh = q[:, h * HEAD_DIM:(h + 1) * HEAD_DIM]
            outs.append(_softmax_pv([_dot_t(qh, kb), _dot_t(qh, ckb)], [vb, cvb]))
    fd_ref[...] = jnp.concatenate(outs, axis=-1).astype(BF16)


def _odd_sample(uc, q, k, v, ctx_k, ctx_v, pool_w, pool_scale, fc, fd):
    hq, hkv = HQ_D * HEAD_DIM, HKV_D * HEAD_DIM
    first = N_PROMPT // DEC_SEQ
    n_qb = DEC_SEQ // Q_BLK
    first_q = N_PROMPT // Q_BLK
    whole = lambda w: pl.BlockSpec((DEC_SEQ, w), lambda b, i: (first + b, 0))
    qblk = pl.BlockSpec((Q_BLK, hq), lambda b, i: (first_q + b * n_qb + i, 0))
    ctx = pl.BlockSpec((1, HKV_D, PAST_LEN, HEAD_DIM), lambda b, i: (b, 0, 0, 0))
    return pl.pallas_call(
        _odd_sample_kernel,
        grid=(DEC_BATCH, n_qb),
        in_specs=[whole(C_POOL), qblk, whole(hkv), whole(hkv), ctx, ctx,
                  pl.BlockSpec((N_POOL, POOL_C, POOL_C), lambda b, i: (0, 0, 0)),
                  pl.BlockSpec((1, C_POOL), lambda b, i: (0, 0)),
                  pl.BlockSpec(memory_space=pl.ANY),
                  pl.BlockSpec(memory_space=pl.ANY)],
        out_specs=[whole(C_POOL), qblk],
        out_shape=[
            jax.ShapeDtypeStruct((N_TOK, C_POOL), BF16),
            jax.ShapeDtypeStruct((N_TOK, hq), BF16),
        ],
        input_output_aliases={8: 0, 9: 1},
        compiler_params=_params(2),
        name="odd_sample",
    )(uc, q, k, v, ctx_k, ctx_v, pool_w, pool_scale, fc, fd)


def _mix_out_kernel(x_ref, fa_ref, fb_ref, m_ref, g_ref, w_ref, o_ref):
    half = fa_ref.shape[1]
    out = _dot(fa_ref[...], w_ref[0:half, :]) + _dot(fb_ref[...], w_ref[half:, :])
    o_ref[...] = x_ref[...] + m_ref[5:6, :] * _rms(out, g_ref[3:4, :])


def _mix_out(x, fa, fb, m_l, g_l, w_out):
    tok = lambda w: pl.BlockSpec((TM, w), lambda i: (i, 0))
    return pl.pallas_call(
        _mix_out_kernel,
        grid=(N_TILES,),
        in_specs=[
            tok(D_MODEL), tok(fa.shape[1]), tok(fb.shape[1]),
            pl.BlockSpec((None, N_MOD, D_MODEL), lambda i: (_cond_of_tile(i), 0, 0)),
            pl.BlockSpec((6, D_MODEL), lambda i: (0, 0)),
            pl.BlockSpec((D_MODEL, D_MODEL), lambda i: (0, 0)),
        ],
        out_specs=tok(D_MODEL),
        out_shape=jax.ShapeDtypeStruct((N_TOK, D_MODEL), F32),
        compiler_params=_params(1),
        name="mix_out",
    )(x, fa, fb, m_l, g_l, w_out)


def _na_bias_table(rpb):
    var = np.arange(NA_ROWS)[:, None]
    i = np.arange(NA_ROWS)[None, :]
    row_off = i - var + NA_ROWS - 1
    c = np.arange(GRID_W)[:, None]
    kc = np.arange(GRID_W)[None, :]
    cs = np.clip(c - NA_COLS // 2, 0, GRID_W - NA_COLS)
    valid = (kc >= cs) & (kc < cs + NA_COLS)
    col_off = np.clip(kc - c + NA_COLS - 1, 0, 2 * NA_COLS - 2)
    b = rpb.astype(F32)[:, row_off[:, :, None, None], col_off[None, None, :, :]]
    b = jnp.where(valid[None, None, None], b, NEG_INF)
    return b.transpose(0, 1, 3, 2, 4).reshape(H_A, NA_ROWS, GRID_W, NA_ROWS * GRID_W)


def _rope_tables():
    t = jnp.arange(DEC_SEQ)
    n_freq = HEAD_DIM // 4
    inv = ROPE_BASE ** (-jnp.arange(n_freq, dtype=F32) / n_freq)
    ang = jnp.concatenate([(t // GRID_W).astype(F32)[:, None] * inv,
                           (t % GRID_W).astype(F32)[:, None] * inv], axis=-1)
    cos = jnp.repeat(jnp.cos(ang), 2, axis=-1)
    sin = jnp.repeat(jnp.sin(ang), 2, axis=-1)
    sign = jnp.asarray(np.tile(np.array([-1.0, 1.0], np.float32), HEAD_DIM // 2))
    n_heads = HQ_D + HKV_D
    cos = jnp.tile(cos, (1, n_heads))
    sin = jnp.tile(sin * sign, (1, n_heads))
    cos = jnp.concatenate([jnp.ones((TM, QK_W), F32), cos], axis=0)
    sin = jnp.concatenate([jnp.zeros((TM, QK_W), F32), sin], axis=0)
    return cos, sin


def _head_ones():
    head = np.arange(QK_W) // HEAD_DIM
    return jnp.asarray(head[:, None] == head[None, :], dtype=BF16)


def kernel(x_prompt, x_sample, cache_a_k, cache_a_v, cache_d_k, cache_d_v, c, c_ctx, mod_w, mod_b, norm_w,
           ffn_w1, ffn_w2, ev_w_in, ev_rpb, ev_conv_w, ev_conv_b, ev_w_out, od_w_in, od_pool_w,
           od_pool_scale, od_q_norm, od_k_norm, od_w_out):
    x = jnp.concatenate([x_prompt.reshape(N_PROMPT, D_MODEL), x_sample.reshape(N_SAMPLE, D_MODEL)], axis=0)
    cond = jnp.concatenate([c_ctx[None, :], c, jnp.zeros((COND_PAD - N_COND, D_MODEL), F32)], axis=0)
    mods = _modulation(cond, mod_w, mod_b).reshape(DEPTH, COND_PAD, N_MOD, D_MODEL)
    cos_t, sin_t = _rope_tables()
    ones_bd = _head_ones()
    states = []
    for l in range(DEPTH):
        m_l, g_l = mods[l], norm_w[l]
        i = l // 2
        x = _ffn(x, m_l, g_l, ffn_w1[l, 0].astype(BF16), ffn_w2[l, 0].astype(BF16), 0)
        if l % 2 == 0:
            q, k, v, bg, z = _even_in(x, m_l, g_l, ev_w_in[i].astype(BF16))
            cw, cb = ev_conv_w[i], ev_conv_b[i][None, :]
            fa, fb, s_k, s_v = _even_prompt(q, k, v, bg, z, cw, cb)
            fa, fb = _even_sample(q, k, v, bg, z, cache_a_k[:, i], cache_a_v[:, i],
                                  _na_bias_table(ev_rpb[i]), cw, cb, fa, fb)
            w_out = ev_w_out[i]
        else:
            gain = jnp.concatenate([jnp.tile(od_q_norm[i], HQ_D), jnp.tile(od_k_norm[i], HKV_D)])[None, :]
            uc, q, k, v = _odd_in(x, m_l, g_l, od_w_in[i].astype(BF16), gain, cos_t, sin_t, ones_bd)
            pw, ps = od_pool_w[i].astype(BF16), od_pool_scale[i][None, :]
            fa, fb, s_k, s_v = _odd_prompt(uc, q, k, v, pw, ps)
            fa, fb = _odd_sample(uc, q, k, v, cache_d_k[:, i], cache_d_v[:, i], pw, ps, fa, fb)
            w_out = od_w_out[i]
        states.append((s_k, s_v))
        x = _mix_out(x, fa, fb, m_l, g_l, w_out.astype(BF16))
        x = _ffn(x, m_l, g_l, ffn_w1[l, 1].astype(BF16), ffn_w2[l, 1].astype(BF16), 2)
    y_prompt = x[:N_PROMPT].reshape(BATCH, SEQ, D_MODEL)
    y_sample = x[N_PROMPT:].reshape(DEC_BATCH, DEC_SEQ, D_MODEL)
    return (y_prompt, y_sample, states[0][0], states[0][1], states[1][0], states[1][1])
```

```python
import functools

import jax
import jax.numpy as jnp
import numpy as np
from jax import lax
from jax.experimental import pallas as pl
from jax.experimental.pallas import tpu as pltpu

D_MODEL = 1024
BATCH = 32
SEQ = 256
DEPTH = 2
DEC_BATCH = 2
DEC_SEQ = 2048
PAST_LEN = 256
GRID_W = 64
HEAD_DIM = 64
N_MOD = 9
D_FF = 2816
FFN_RES = 0.5
H_A = 8
NA_ROWS = 8
NA_COLS = 16
C_B = 512
CONV_W = 3
C_POOL = 512
POOL_WINDOWS = (2, 4, 8, 16)
N_POOL = 4
POOL_C = C_POOL // N_POOL
HQ_D = 8
HKV_D = 2
GQA_GROUP = HQ_D // HKV_D
ROPE_BASE = 10000.0
EVEN_IN = 3 * H_A * HEAD_DIM + 3 * C_B
ODD_IN = C_POOL + (HQ_D + 2 * HKV_D) * HEAD_DIM
RMS_EPS = 1e-6
NEG_INF = -1e30
ATT_SCALE = HEAD_DIM ** -0.5

GRID_ROWS = DEC_SEQ // GRID_W
N_PROMPT = BATCH * SEQ
N_SAMPLE = DEC_BATCH * DEC_SEQ
N_TOK = N_PROMPT + N_SAMPLE
N_COND = 1 + DEC_BATCH
COND_PAD = 8
QK_W = (HQ_D + HKV_D) * HEAD_DIM
POOL_PAD = 8

TM = 512
N_TILES = N_TOK // TM
PROMPT_TILES = N_PROMPT // TM
TILES_PER_SAMPLE = DEC_SEQ // TM
FF_CHUNK = 1408
MOD_TN = 1536
Q_BLK = 256
VMEM_LIMIT = 56 * 1024 * 1024

F32 = jnp.float32
BF16 = jnp.bfloat16


def _params(n_grid):
    return pltpu.CompilerParams(dimension_semantics=("arbitrary",) * n_grid, vmem_limit_bytes=VMEM_LIMIT)


def _cond_of_tile(i):
    return jnp.where(i < PROMPT_TILES, 0, 1 + (i - PROMPT_TILES) // TILES_PER_SAMPLE)


def _rms(x, g):
    return x * lax.rsqrt(jnp.mean(x * x, axis=-1, keepdims=True) + RMS_EPS) * g


def _sigmoid(x):
    return 1.0 / (1.0 + jnp.exp(-x))


def _dot(a, b):
    return jnp.dot(a, b, preferred_element_type=F32)


def _dot_t(a, b):
    return lax.dot_general(a, b, (((1,), (1,)), ((), ())), preferred_element_type=F32)


def _mod_kernel(c_ref, w_ref, b_ref, o_ref):
    c = c_ref[...]
    sc = (c * _sigmoid(c)).astype(BF16)
    o_ref[...] = _dot(sc, w_ref[...].astype(BF16)) + b_ref[...]


def _modulation(cond, mod_w, mod_b):
    n_col = N_MOD * D_MODEL
    return pl.pallas_call(
        _mod_kernel,
        grid=(DEPTH, n_col // MOD_TN),
        in_specs=[
            pl.BlockSpec((COND_PAD, D_MODEL), lambda l, j: (0, 0)),
            pl.BlockSpec((None, D_MODEL, MOD_TN), lambda l, j: (l, 0, j)),
            pl.BlockSpec((None, 1, MOD_TN), lambda l, j: (l, 0, j)),
        ],
        out_specs=pl.BlockSpec((None, COND_PAD, MOD_TN), lambda l, j: (l, 0, j)),
        out_shape=jax.ShapeDtypeStruct((DEPTH, COND_PAD, n_col), F32),
        compiler_params=_params(2),
        name="modulation",
    )(cond, mod_w, mod_b.reshape(DEPTH, 1, n_col))


def _ffn_kernel(*refs, sub, split_in, split_out):
    n_x = 2 if split_in else 1
    m_ref, g_ref, w1_ref, w2_ref = refs[n_x:n_x + 4]
    is_prompt = pl.program_id(0) < PROMPT_TILES
    if split_in:
        x = jnp.where(is_prompt, refs[0][...], refs[1][...])
    else:
        x = refs[0][...]
    shift = m_ref[3 * sub:3 * sub + 1, :]
    scale = m_ref[3 * sub + 1:3 * sub + 2, :]
    gate = m_ref[3 * sub + 2:3 * sub + 3, :]
    h = (_rms(x, g_ref[2 * sub:2 * sub + 1, :]) * (1.0 + scale) + shift).astype(BF16)
    acc = jnp.zeros((TM, D_MODEL), F32)
    for j in range(D_FF // FF_CHUNK):
        lo = j * FF_CHUNK
        a = _dot(h, w1_ref[:, lo:lo + FF_CHUNK])
        u = _dot(h, w1_ref[:, D_FF + lo:D_FF + lo + FF_CHUNK])
        act = (a * _sigmoid(a) * u).astype(BF16)
        acc = acc + _dot(act, w2_ref[lo:lo + FF_CHUNK, :])
    y = x + FFN_RES * gate * _rms(acc, g_ref[2 * sub + 1:2 * sub + 2, :])
    if split_out:
        @pl.when(is_prompt)
        def _():
            refs[n_x + 4][...] = y

        @pl.when(jnp.logical_not(is_prompt))
        def _():
            refs[n_x + 5][...] = y
    else:
        refs[n_x + 4][...] = y


def _ffn(xs, m_l, g_l, w1, w2, sub, split_out=False):
    split_in = len(xs) == 2
    resident = pl.Buffered(1)
    tok = pl.BlockSpec((TM, D_MODEL), lambda i: (i, 0))
    prompt_tok = pl.BlockSpec((TM, D_MODEL), lambda i: (jnp.minimum(i, PROMPT_TILES - 1), 0))
    sample_tok = pl.BlockSpec((TM, D_MODEL), lambda i: (jnp.maximum(i - PROMPT_TILES, 0), 0))
    if split_out:
        out_specs = [prompt_tok, sample_tok]
        out_shape = [jax.ShapeDtypeStruct((N_PROMPT, D_MODEL), F32),
                     jax.ShapeDtypeStruct((N_SAMPLE, D_MODEL), F32)]
    else:
        out_specs = tok
        out_shape = jax.ShapeDtypeStruct((N_TOK, D_MODEL), F32)
    return pl.pallas_call(
        functools.partial(_ffn_kernel, sub=sub, split_in=split_in, split_out=split_out),
        grid=(N_TILES,),
        in_specs=([prompt_tok, sample_tok] if split_in else [tok]) + [
            pl.BlockSpec((None, N_MOD, D_MODEL), lambda i: (_cond_of_tile(i), 0, 0)),
            pl.BlockSpec((6, D_MODEL), lambda i: (0, 0)),
            pl.BlockSpec((D_MODEL, 2 * D_FF), lambda i: (0, 0), pipeline_mode=resident),
            pl.BlockSpec((D_FF, D_MODEL), lambda i: (0, 0), pipeline_mode=resident),
        ],
        out_specs=out_specs,
        out_shape=out_shape,
        compiler_params=_params(1),
        name=f"ffn{sub}",
    )(*xs, m_l, g_l, w1, w2)


def _mixer_norm(x_ref, m_ref, g_ref):
    return (_rms(x_ref[...], g_ref[2:3, :]) * (1.0 + m_ref[4:5, :]) + m_ref[3:4, :]).astype(BF16)


def _even_in_kernel(x_ref, m_ref, g_ref, w_ref, q_ref, k_ref, v_ref, bg_ref, z_ref):
    u = _dot(_mixer_norm(x_ref, m_ref, g_ref), w_ref[...])
    hd = H_A * HEAD_DIM
    q_ref[...] = (u[:, 0:hd] * ATT_SCALE).astype(BF16)
    k_ref[...] = u[:, hd:2 * hd]
    v_ref[...] = u[:, 2 * hd:3 * hd]
    bg_ref[...] = u[:, 3 * hd:3 * hd + C_B]
    z_ref[...] = u[:, 3 * hd + C_B:3 * hd + 2 * C_B] * u[:, 3 * hd + 2 * C_B:3 * hd + 3 * C_B]


def _even_in(x, m_l, g_l, w_in):
    hd = H_A * HEAD_DIM
    tok = lambda w: pl.BlockSpec((TM, w), lambda i: (i, 0))
    return pl.pallas_call(
        _even_in_kernel,
        grid=(N_TILES,),
        in_specs=[
            tok(D_MODEL),
            pl.BlockSpec((None, N_MOD, D_MODEL), lambda i: (_cond_of_tile(i), 0, 0)),
            pl.BlockSpec((6, D_MODEL), lambda i: (0, 0)),
            pl.BlockSpec((D_MODEL, EVEN_IN), lambda i: (0, 0)),
        ],
        out_specs=[tok(hd), tok(hd), tok(hd), tok(C_B), tok(C_B)],
        out_shape=[
            jax.ShapeDtypeStruct((N_TOK, hd), BF16),
            jax.ShapeDtypeStruct((N_TOK, hd), F32),
            jax.ShapeDtypeStruct((N_TOK, hd), F32),
            jax.ShapeDtypeStruct((N_TOK, C_B), F32),
            jax.ShapeDtypeStruct((N_TOK, C_B), F32),
        ],
        compiler_params=_params(1),
        name="even_in",
    )(x, m_l, g_l, w_in)


def _swap_pairs(x):
    n = x.shape[-1]
    lane = lax.broadcasted_iota(jnp.int32, x.shape, x.ndim - 1)
    return jnp.where(lane % 2 == 0, pltpu.roll(x, n - 1, x.ndim - 1), pltpu.roll(x, 1, x.ndim - 1))


def _odd_in_kernel(x_ref, m_ref, g_ref, w_ref, ng_ref, cos_ref, sin_ref, ones_ref,
                   uc_ref, q_ref, k_ref, v_ref):
    u = _dot(_mixer_norm(x_ref, m_ref, g_ref), w_ref[...])
    uc_ref[...] = u[:, 0:C_POOL]
    qk = u[:, C_POOL:C_POOL + QK_W]
    sq = qk * qk
    hi = sq.astype(BF16)
    lo = (sq - hi.astype(F32)).astype(BF16)
    ones = ones_ref[...]
    ms = (_dot(hi, ones) + _dot(lo, ones)) * (1.0 / HEAD_DIM)
    n = qk * lax.rsqrt(ms + RMS_EPS) * ng_ref[...]
    r = n * cos_ref[...] + _swap_pairs(n) * sin_ref[...]
    hq = HQ_D * HEAD_DIM
    q_ref[...] = (r[:, 0:hq] * ATT_SCALE).astype(BF16)
    k_ref[...] = r[:, hq:QK_W]
    v_ref[...] = u[:, C_POOL + QK_W:ODD_IN]


def _rope_tile_index(i):
    return jnp.where(i < PROMPT_TILES, 0, 1 + (i - PROMPT_TILES) % TILES_PER_SAMPLE)


def _odd_in(x, m_l, g_l, w_in, qk_gain, cos_t, sin_t, ones_bd):
    hq, hkv = HQ_D * HEAD_DIM, HKV_D * HEAD_DIM
    tok = lambda w: pl.BlockSpec((TM, w), lambda i: (i, 0))
    return pl.pallas_call(
        _odd_in_kernel,
        grid=(N_TILES,),
        in_specs=[
            tok(D_MODEL),
            pl.BlockSpec((None, N_MOD, D_MODEL), lambda i: (_cond_of_tile(i), 0, 0)),
            pl.BlockSpec((6, D_MODEL), lambda i: (0, 0)),
            pl.BlockSpec((D_MODEL, ODD_IN), lambda i: (0, 0)),
            pl.BlockSpec((1, QK_W), lambda i: (0, 0)),
            pl.BlockSpec((TM, QK_W), lambda i: (_rope_tile_index(i), 0)),
            pl.BlockSpec((TM, QK_W), lambda i: (_rope_tile_index(i), 0)),
            pl.BlockSpec((QK_W, QK_W), lambda i: (0, 0)),
        ],
        out_specs=[tok(C_POOL), tok(hq), tok(hkv), tok(hkv)],
        out_shape=[
            jax.ShapeDtypeStruct((N_TOK, C_POOL), F32),
            jax.ShapeDtypeStruct((N_TOK, hq), BF16),
            jax.ShapeDtypeStruct((N_TOK, hkv), F32),
            jax.ShapeDtypeStruct((N_TOK, hkv), F32),
        ],
        compiler_params=_params(1),
        name="odd_in",
    )(x, m_l, g_l, w_in, qk_gain, cos_t, sin_t, ones_bd)


def _softmax_pv(scores, values):
    m = scores[0].max(axis=-1, keepdims=True)
    for s in scores[1:]:
        m = jnp.maximum(m, s.max(axis=-1, keepdims=True))
    den = None
    num = None
    for s, v in zip(scores, values):
        e = jnp.exp(s - m)
        d = e.sum(axis=-1, keepdims=True)
        o = _dot(e.astype(BF16), v)
        den = d if den is None else den + d
        num = o if num is None else num + o
    return num / den


def _short_conv(z, cw_ref, cb_ref, seq_len):
    rows = z.shape[0]
    pos = lax.broadcasted_iota(jnp.int32, z.shape, 0)
    z_prev = jnp.where(pos == 0, 0.0, pltpu.roll(z, 1, 0))
    z_next = jnp.where(pos == seq_len - 1, 0.0, pltpu.roll(z, rows - 1, 0))
    y = z_prev * cw_ref[0:1, :]
    y = y + z * cw_ref[1:2, :]
    y = y + z_next * cw_ref[2:3, :]
    return y + cb_ref[...]


def _pool_mix(uc, pw_ref, ps_ref, seq_len):
    pos = lax.broadcasted_iota(jnp.int32, (seq_len, POOL_C), 0)
    pad = jnp.zeros((POOL_PAD, POOL_C), F32)
    n_ext = seq_len + 2 * POOL_PAD
    outs = []
    for gi, win in enumerate(POOL_WINDOWS):
        ug = uc[:, gi * POOL_C:(gi + 1) * POOL_C]
        run = jnp.concatenate([pad, ug, pad], axis=0)
        span = 1
        while span < win:
            run = run + pltpu.roll(run, span, 0)
            span *= 2
        back = win // 2 - 1
        if back:
            run = pltpu.roll(run, n_ext - back, 0)
        wsum = run[POOL_PAD:POOL_PAD + seq_len, :]
        cnt = jnp.minimum(pos + (win - win // 2), seq_len) - jnp.maximum(pos - win // 2, 0)
        pooled = (wsum / cnt.astype(F32) - ug).astype(BF16)
        outs.append(_dot(pooled, pw_ref[gi]))
    return jnp.concatenate(outs, axis=-1) * ps_ref[...]


def _even_prompt_kernel(q_ref, k_ref, v_ref, bg_ref, z_ref, cw_ref, cb_ref,
                        fa_ref, fb_ref, sk_ref, sv_ref):
    q = q_ref[...]
    k = k_ref[...]
    v = v_ref[...]
    outs = []
    for h in range(H_A):
        sl = slice(h * HEAD_DIM, (h + 1) * HEAD_DIM)
        kh = k[:, sl]
        vh = v[:, sl]
        sk_ref[0, 0, h] = kh
        sv_ref[0, 0, h] = vh
        s = _dot_t(q[:, sl], kh.astype(BF16))
        outs.append(_softmax_pv([s], [vh.astype(BF16)]))
    fa_ref[...] = jnp.concatenate(outs, axis=-1).astype(BF16)
    fb_ref[...] = (bg_ref[...] * _short_conv(z_ref[...], cw_ref, cb_ref, SEQ)).astype(BF16)


def _even_prompt(q, k, v, bg, z, conv_w, conv_b):
    hd = H_A * HEAD_DIM
    seq = lambda w: pl.BlockSpec((SEQ, w), lambda b: (b, 0))
    state = pl.BlockSpec((1, 1, H_A, SEQ, HEAD_DIM), lambda b: (b, 0, 0, 0, 0))
    return pl.pallas_call(
        _even_prompt_kernel,
        grid=(BATCH,),
        in_specs=[seq(hd), seq(hd), seq(hd), seq(C_B), seq(C_B),
                  pl.BlockSpec((CONV_W, C_B), lambda b: (0, 0)),
                  pl.BlockSpec((1, C_B), lambda b: (0, 0))],
        out_specs=[seq(hd), seq(C_B), state, state],
        out_shape=[
            jax.ShapeDtypeStruct((N_TOK, hd), BF16),
            jax.ShapeDtypeStruct((N_TOK, C_B), BF16),
            jax.ShapeDtypeStruct((BATCH, 1, H_A, SEQ, HEAD_DIM), F32),
            jax.ShapeDtypeStruct((BATCH, 1, H_A, SEQ, HEAD_DIM), F32),
        ],
        compiler_params=_params(1),
        name="even_prompt",
    )(q, k, v, bg, z, conv_w, conv_b)


def _na_bias_tiles(rpb_h):
    c = lax.broadcasted_iota(jnp.int32, (GRID_W, GRID_W), 0)
    kc = lax.broadcasted_iota(jnp.int32, (GRID_W, GRID_W), 1)
    cs = jnp.clip(c - NA_COLS // 2, 0, GRID_W - NA_COLS)
    valid = (kc >= cs) & (kc < cs + NA_COLS)
    tiles = []
    for ro in range(2 * NA_ROWS - 1):
        g = jnp.broadcast_to(rpb_h[ro:ro + 1, :], (GRID_W, 128))
        skew = pltpu.roll(g, 128 - (NA_COLS - 1), 1, stride=1, stride_axis=0)
        tiles.append(jnp.where(valid, skew[:, 0:GRID_W], NEG_INF))
    return tiles


def _even_sample_kernel(q_ref, k_ref, v_ref, bg_ref, z_ref, ck_ref, cv_ref, rpb_ref, cw_ref, cb_ref,
                        fa_in_ref, fb_in_ref, fa_ref, fb_ref, bias_ref):
    del fa_in_ref, fb_in_ref
    ck = [ck_ref[0, h].astype(BF16) for h in range(2)]
    cv = [cv_ref[0, h].astype(BF16) for h in range(2)]
    win = NA_ROWS * GRID_W
    for h in range(2):
        tiles = _na_bias_tiles(rpb_ref[h])
        for var in range(NA_ROWS):
            bias_ref[h, var] = jnp.concatenate([tiles[i - var + NA_ROWS - 1] for i in range(NA_ROWS)], axis=1)

    def row(r, carry):
        rs = jnp.clip(r - NA_ROWS // 2, 0, GRID_ROWS - NA_ROWS)
        q0 = pl.multiple_of(r * GRID_W, GRID_W)
        k0 = pl.multiple_of(rs * GRID_W, GRID_W)
        qr = q_ref[pl.ds(q0, GRID_W), :]
        kw = k_ref[pl.ds(k0, win), :].astype(BF16)
        vw = v_ref[pl.ds(k0, win), :].astype(BF16)
        outs = []
        for h in range(2):
            sl = slice(h * HEAD_DIM, (h + 1) * HEAD_DIM)
            qh = qr[:, sl]
            s_loc = _dot_t(qh, kw[:, sl]) + bias_ref[h, r - rs]
            s_ctx = _dot_t(qh, ck[h])
            outs.append(_softmax_pv([s_loc, s_ctx], [vw[:, sl], cv[h]]))
        fa_ref[pl.ds(q0, GRID_W), :] = jnp.concatenate(outs, axis=-1).astype(BF16)
        return carry

    lax.fori_loop(0, GRID_ROWS, row, 0)
    fb_ref[...] = (bg_ref[...] * _short_conv(z_ref[...], cw_ref, cb_ref, DEC_SEQ)).astype(BF16)


def _even_sample(q, k, v, bg, z, ctx_k, ctx_v, rpb, conv_w, conv_b, fa, fb):
    hd = H_A * HEAD_DIM
    pair = 2 * HEAD_DIM
    first = N_PROMPT // DEC_SEQ
    seq = pl.BlockSpec((DEC_SEQ, pair), lambda b, p: (first + b, p))
    ctx = pl.BlockSpec((1, 2, PAST_LEN, HEAD_DIM), lambda b, p: (b, p, 0, 0))
    return pl.pallas_call(
        _even_sample_kernel,
        grid=(DEC_BATCH, H_A // 2),
        in_specs=[seq, seq, seq, seq, seq, ctx, ctx,
                  pl.BlockSpec((2, 2 * NA_ROWS - 1, 128), lambda b, p: (p, 0, 0)),
                  pl.BlockSpec((CONV_W, pair), lambda b, p: (0, p)),
                  pl.BlockSpec((1, pair), lambda b, p: (0, p)),
                  pl.BlockSpec(memory_space=pl.ANY),
                  pl.BlockSpec(memory_space=pl.ANY)],
        out_specs=[seq, seq],
        out_shape=[
            jax.ShapeDtypeStruct((N_TOK, hd), BF16),
            jax.ShapeDtypeStruct((N_TOK, C_B), BF16),
        ],
        scratch_shapes=[pltpu.VMEM((2, NA_ROWS, GRID_W, NA_ROWS * GRID_W), F32)],
        input_output_aliases={10: 0, 11: 1},
        compiler_params=_params(2),
        name="even_sample",
    )(q, k, v, bg, z, ctx_k, ctx_v, rpb, conv_w, conv_b, fa, fb)


def _odd_prompt_kernel(uc_ref, q_ref, k_ref, v_ref, pw_ref, ps_ref, fc_ref, fd_ref, sk_ref, sv_ref):
    q = q_ref[...]
    k = k_ref[...]
    v = v_ref[...]
    outs = []
    for g in range(HKV_D):
        sl = slice(g * HEAD_DIM, (g + 1) * HEAD_DIM)
        kg = k[:, sl]
        vg = v[:, sl]
        sk_ref[0, 0, g] = kg
        sv_ref[0, 0, g] = vg
        kb = kg.astype(BF16)
        vb = vg.astype(BF16)
        for j in range(GQA_GROUP):
            h = g * GQA_GROUP + j
            s = _dot_t(q[:, h * HEAD_DIM:(h + 1) * HEAD_DIM], kb)
            outs.append(_softmax_pv([s], [vb]))
    fd_ref[...] = jnp.concatenate(outs, axis=-1).astype(BF16)
    fc_ref[...] = _pool_mix(uc_ref[...], pw_ref, ps_ref, SEQ).astype(BF16)


def _odd_prompt(uc, q, k, v, pool_w, pool_scale):
    hq, hkv = HQ_D * HEAD_DIM, HKV_D * HEAD_DIM
    seq = lambda w: pl.BlockSpec((SEQ, w), lambda b: (b, 0))
    state = pl.BlockSpec((1, 1, HKV_D, SEQ, HEAD_DIM), lambda b: (b, 0, 0, 0, 0))
    return pl.pallas_call(
        _odd_prompt_kernel,
        grid=(BATCH,),
        in_specs=[seq(C_POOL), seq(hq), seq(hkv), seq(hkv),
                  pl.BlockSpec((N_POOL, POOL_C, POOL_C), lambda b: (0, 0, 0)),
                  pl.BlockSpec((1, C_POOL), lambda b: (0, 0))],
        out_specs=[seq(C_POOL), seq(hq), state, state],
        out_shape=[
            jax.ShapeDtypeStruct((N_TOK, C_POOL), BF16),
            jax.ShapeDtypeStruct((N_TOK, hq), BF16),
            jax.ShapeDtypeStruct((BATCH, 1, HKV_D, SEQ, HEAD_DIM), F32),
            jax.ShapeDtypeStruct((BATCH, 1, HKV_D, SEQ, HEAD_DIM), F32),
        ],
        compiler_params=_params(1),
        name="odd_prompt",
    )(uc, q, k, v, pool_w, pool_scale)


def _odd_sample_kernel(uc_ref, q_ref, k_ref, v_ref, ck_ref, cv_ref, pw_ref, ps_ref, fc_in_ref, fd_in_ref,
                       fc_ref, fd_ref):
    del fc_in_ref, fd_in_ref

    @pl.when(pl.program_id(1) == 0)
    def _():
        fc_ref[...] = _pool_mix(uc_ref[...], pw_ref, ps_ref, DEC_SEQ).astype(BF16)

    q = q_ref[...]
    outs = []
    for g in range(HKV_D):
        sl = slice(g * HEAD_DIM, (g + 1) * HEAD_DIM)
        kb = k_ref[:, sl].astype(BF16)
        vb = v_ref[:, sl].astype(BF16)
        ckb = ck_ref[0, g].astype(BF16)
        cvb = cv_ref[0, g].astype(BF16)
        for j in range(GQA_GROUP):
            h = g * GQA_GROUP + j
            qh = q[:, h * HEAD_DIM:(h + 1) * HEAD_DIM]
            outs.append(_softmax_pv([_dot_t(qh, kb), _dot_t(qh, ckb)], [vb, cvb]))
    fd_ref[...] = jnp.concatenate(outs, axis=-1).astype(BF16)


def _odd_sample(uc, q, k, v, ctx_k, ctx_v, pool_w, pool_scale, fc, fd):
    hq, hkv = HQ_D * HEAD_DIM, HKV_D * HEAD_DIM
    first = N_PROMPT // DEC_SEQ
    n_qb = DEC_SEQ // Q_BLK
    first_q = N_PROMPT // Q_BLK
    whole = lambda w: pl.BlockSpec((DEC_SEQ, w), lambda b, i: (first + b, 0))
    qblk = pl.BlockSpec((Q_BLK, hq), lambda b, i: (first_q + b * n_qb + i, 0))
    ctx = pl.BlockSpec((1, HKV_D, PAST_LEN, HEAD_DIM), lambda b, i: (b, 0, 0, 0))
    return pl.pallas_call(
        _odd_sample_kernel,
        grid=(DEC_BATCH, n_qb),
        in_specs=[whole(C_POOL), qblk, whole(hkv), whole(hkv), ctx, ctx,
                  pl.BlockSpec((N_POOL, POOL_C, POOL_C), lambda b, i: (0, 0, 0)),
                  pl.BlockSpec((1, C_POOL), lambda b, i: (0, 0)),
                  pl.BlockSpec(memory_space=pl.ANY),
                  pl.BlockSpec(memory_space=pl.ANY)],
        out_specs=[whole(C_POOL), qblk],
        out_shape=[
            jax.ShapeDtypeStruct((N_TOK, C_POOL), BF16),
            jax.ShapeDtypeStruct((N_TOK, hq), BF16),
        ],
        input_output_aliases={8: 0, 9: 1},
        compiler_params=_params(2),
        name="odd_sample",
    )(uc, q, k, v, ctx_k, ctx_v, pool_w, pool_scale, fc, fd)


def _mix_out_kernel(x_ref, fa_ref, fb_ref, m_ref, g_ref, w_ref, o_ref):
    half = fa_ref.shape[1]
    out = _dot(fa_ref[...], w_ref[0:half, :]) + _dot(fb_ref[...], w_ref[half:, :])
    o_ref[...] = x_ref[...] + m_ref[5:6, :] * _rms(out, g_ref[3:4, :])


def _mix_out(x, fa, fb, m_l, g_l, w_out):
    tok = lambda w: pl.BlockSpec((TM, w), lambda i: (i, 0))
    return pl.pallas_call(
        _mix_out_kernel,
        grid=(N_TILES,),
        in_specs=[
            tok(D_MODEL), tok(fa.shape[1]), tok(fb.shape[1]),
            pl.BlockSpec((None, N_MOD, D_MODEL), lambda i: (_cond_of_tile(i), 0, 0)),
            pl.BlockSpec((6, D_MODEL), lambda i: (0, 0)),
            pl.BlockSpec((D_MODEL, D_MODEL), lambda i: (0, 0)),
        ],
        out_specs=tok(D_MODEL),
        out_shape=jax.ShapeDtypeStruct((N_TOK, D_MODEL), F32),
        compiler_params=_params(1),
        name="mix_out",
    )(x, fa, fb, m_l, g_l, w_out)


def _rope_tables():
    t = jnp.arange(DEC_SEQ)
    n_freq = HEAD_DIM // 4
    inv = ROPE_BASE ** (-jnp.arange(n_freq, dtype=F32) / n_freq)
    ang = jnp.concatenate([(t // GRID_W).astype(F32)[:, None] * inv,
                           (t % GRID_W).astype(F32)[:, None] * inv], axis=-1)
    cos = jnp.repeat(jnp.cos(ang), 2, axis=-1)
    sin = jnp.repeat(jnp.sin(ang), 2, axis=-1)
    sign = jnp.asarray(np.tile(np.array([-1.0, 1.0], np.float32), HEAD_DIM // 2))
    n_heads = HQ_D + HKV_D
    cos = jnp.tile(cos, (1, n_heads))
    sin = jnp.tile(sin * sign, (1, n_heads))
    cos = jnp.concatenate([jnp.ones((TM, QK_W), F32), cos], axis=0)
    sin = jnp.concatenate([jnp.zeros((TM, QK_W), F32), sin], axis=0)
    return cos, sin


def _head_ones():
    head = np.arange(QK_W) // HEAD_DIM
    return jnp.asarray(head[:, None] == head[None, :], dtype=BF16)


def kernel(x_prompt, x_sample, cache_a_k, cache_a_v, cache_d_k, cache_d_v, c, c_ctx, mod_w, mod_b, norm_w,
           ffn_w1, ffn_w2, ev_w_in, ev_rpb, ev_conv_w, ev_conv_b, ev_w_out, od_w_in, od_pool_w,
           od_pool_scale, od_q_norm, od_k_norm, od_w_out):
    xs = (x_prompt.reshape(N_PROMPT, D_MODEL), x_sample.reshape(N_SAMPLE, D_MODEL))
    cond =jnp.concatenate([c_ctx[None, :], c, jnp.zeros((COND_PAD - N_COND, D_MODEL), F32)], axis=0)
    mods = _modulation(cond, mod_w, mod_b).reshape(DEPTH, COND_PAD, N_MOD, D_MODEL)
    cos_t, sin_t = _rope_tables()
    ones_bd = _head_ones()
    states = []
    for l in range(DEPTH):
        m_l, g_l = mods[l], norm_w[l]
        i = l // 2
        x = _ffn(xs, m_l, g_l, ffn_w1[l, 0].astype(BF16), ffn_w2[l, 0].astype(BF16), 0)
        if l % 2 == 0:
            q, k, v, bg, z = _even_in(x, m_l, g_l, ev_w_in[i].astype(BF16))
            cw, cb = ev_conv_w[i], ev_conv_b[i][None, :]
            fa, fb, s_k, s_v = _even_prompt(q, k, v, bg, z, cw, cb)
            fa, fb = _even_sample(q, k, v, bg, z, cache_a_k[:, i], cache_a_v[:, i],
                                  jnp.pad(ev_rpb[i], ((0, 0), (0, 0), (0, 128 - ev_rpb.shape[-1]))), cw, cb, fa, fb)
            w_out = ev_w_out[i]
        else:
            gain = jnp.concatenate([jnp.tile(od_q_norm[i], HQ_D), jnp.tile(od_k_norm[i], HKV_D)])[None, :]
            uc, q, k, v = _odd_in(x, m_l, g_l, od_w_in[i].astype(BF16), gain, cos_t, sin_t, ones_bd)
            pw, ps = od_pool_w[i].astype(BF16), od_pool_scale[i][None, :]
            fa, fb, s_k, s_v = _odd_prompt(uc, q, k, v, pw, ps)
            fa, fb = _odd_sample(uc, q, k, v, cache_d_k[:, i], cache_d_v[:, i], pw, ps, fa, fb)
            w_out = od_w_out[i]
        states.append((s_k, s_v))
        x = _mix_out(x, fa, fb, m_l, g_l, w_out.astype(BF16))
        xs = _ffn((x,), m_l, g_l, ffn_w1[l, 1].astype(BF16), ffn_w2[l, 1].astype(BF16), 2,
                  split_out=(l == DEPTH - 1))
        xs = tuple(xs) if l == DEPTH - 1 else (xs,)
    y_prompt = xs[0].reshape(BATCH, SEQ, D_MODEL)
    y_sample = xs[1].reshape(DEC_BATCH, DEC_SEQ, D_MODEL)
    return (y_prompt, y_sample, states[0][0], states[0][1], states[1][0], states[1][1])
```

```python
import functools

import jax
import jax.numpy as jnp
from jax import lax
from jax.experimental import pallas as pl
from jax.experimental.pallas import tpu as pltpu

D_MODEL = 1024
BATCH = 32
SEQ = 256
DEPTH = 2
DEC_BATCH = 2
DEC_SEQ = 2048
PAST_LEN = 256
GRID_W = 64
HEAD_DIM = 64
N_MOD = 9
D_FF = 2816
FFN_RES = 0.5
H_A = 8
NA_ROWS = 8
NA_COLS = 16
C_B = 512
CONV_W = 3
C_POOL = 512
POOL_WINDOWS = (2, 4, 8, 16)
N_POOL = 4
POOL_C = C_POOL // N_POOL
HQ_D = 8
HKV_D = 2
GQA_GROUP = HQ_D // HKV_D
ROPE_BASE = 10000.0
EVEN_IN = 3 * H_A * HEAD_DIM + 3 * C_B
ODD_IN = C_POOL + (HQ_D + 2 * HKV_D) * HEAD_DIM
RMS_EPS = 1e-6
NEG_INF = -1e30
ATT_SCALE = HEAD_DIM ** -0.5

LANES = 128
GRID_ROWS = DEC_SEQ // GRID_W
N_PROMPT = BATCH * SEQ
N_SAMPLE = DEC_BATCH * DEC_SEQ
N_TOK = N_PROMPT + N_SAMPLE
N_COND = 1 + DEC_BATCH
COND_PAD = 8
HD_A = H_A * HEAD_DIM
HD_Q = HQ_D * HEAD_DIM
HD_KV = HKV_D * HEAD_DIM
QK_W = HD_Q + HD_KV
POOL_PAD = 8

TM = 512
N_TILES = N_TOK // TM
PROMPT_TILES = N_PROMPT // TM
TILES_PER_SAMPLE = DEC_SEQ // TM
SEQ_PER_TILE = TM // SEQ
FF_CHUNK = 1408
MOD_TN = 1536
Q_BLK = 256
ROW_UNROLL = 4
VMEM_LIMIT = 56 * 1024 * 1024

F32 = jnp.float32
BF16 = jnp.bfloat16


def _params(n_grid):
    return pltpu.CompilerParams(dimension_semantics=("arbitrary",) * n_grid, vmem_limit_bytes=VMEM_LIMIT)


def _cond_of_tile(i):
    return jnp.where(i < PROMPT_TILES, 0, 1 + (i - PROMPT_TILES) // TILES_PER_SAMPLE)


def _prompt_tile(i):
    return jnp.minimum(i, PROMPT_TILES - 1)


def _sample_tile(i):
    return jnp.maximum(i - PROMPT_TILES, 0)


def _rms(x, g):
    return x * lax.rsqrt(jnp.mean(x * x, axis=-1, keepdims=True) + RMS_EPS) * g


def _sigmoid(x):
    return 1.0 / (1.0 + jnp.exp(-x))


def _dot(a, b):
    return jnp.dot(a, b, preferred_element_type=F32)


def _dot_t(a, b):
    return lax.dot_general(a, b, (((1,), (1,)), ((), ())), preferred_element_type=F32)


def _mod_kernel(c_ref, w_ref, b_ref, o_ref):
    c = c_ref[...]
    sc = (c * _sigmoid(c)).astype(BF16)
    o_ref[...] = _dot(sc, w_ref[...].astype(BF16)) + b_ref[...]


def _modulation(cond, mod_w, mod_b):
    n_col = N_MOD * D_MODEL
    return pl.pallas_call(
        _mod_kernel,
        grid=(DEPTH, n_col // MOD_TN),
        in_specs=[
            pl.BlockSpec((COND_PAD, D_MODEL), lambda l, j: (0, 0)),
            pl.BlockSpec((None, D_MODEL, MOD_TN), lambda l, j: (l, 0, j)),
            pl.BlockSpec((None, 1, MOD_TN), lambda l, j: (l, 0, j)),
        ],
        out_specs=pl.BlockSpec((None, COND_PAD, MOD_TN), lambda l, j: (l, 0, j)),
        out_shape=jax.ShapeDtypeStruct((DEPTH, COND_PAD, n_col), F32),
        compiler_params=_params(2),
        name="modulation",
    )(cond, mod_w, mod_b.reshape(DEPTH, 1, n_col))


def _mod_spec(layer):
    return pl.BlockSpec((None, None, N_MOD, D_MODEL), lambda i: (layer, _cond_of_tile(i), 0, 0))


def _gain_spec(layer):
    return pl.BlockSpec((None, 6, D_MODEL), lambda i: (layer, 0, 0))


def _ffn_kernel(*refs, sub, split_in, split_out):
    n_x = 2 if split_in else 1
    m_ref, g_ref, w1_ref, w2_ref = refs[n_x:n_x + 4]
    is_prompt = pl.program_id(0) < PROMPT_TILES
    if split_in:
        x = jnp.where(is_prompt, refs[0][...], refs[1][...])
    else:
        x = refs[0][...]
    shift = m_ref[3 * sub:3 * sub + 1, :]
    scale = m_ref[3 * sub + 1:3 * sub + 2, :]
    gate = m_ref[3 * sub + 2:3 * sub + 3, :]
    h = (_rms(x, g_ref[2 * sub:2 * sub + 1, :]) * (1.0 + scale) + shift).astype(BF16)
    acc = jnp.zeros((TM, D_MODEL), F32)
    for j in range(D_FF // FF_CHUNK):
        lo = j * FF_CHUNK
        a = _dot(h, w1_ref[:, lo:lo + FF_CHUNK])
        u = _dot(h, w1_ref[:, D_FF + lo:D_FF + lo + FF_CHUNK])
        act = (a * _sigmoid(a) * u).astype(BF16)
        acc = acc + _dot(act, w2_ref[lo:lo + FF_CHUNK, :])
    y = x + FFN_RES * gate * _rms(acc, g_ref[2 * sub + 1:2 * sub + 2, :])
    if split_out:
        @pl.when(is_prompt)
        def _():
            refs[n_x + 4][...] = y

        @pl.when(jnp.logical_not(is_prompt))
        def _():
            refs[n_x + 5][...] = y
    else:
        refs[n_x + 4][...] = y


def _ffn(xs, mods, gains, w1, w2, layer, sub, split_out=False):
    split_in = len(xs) == 2
    resident = pl.Buffered(1)
    which = sub // 2
    tok = pl.BlockSpec((TM, D_MODEL), lambda i: (i, 0))
    prompt_tok = pl.BlockSpec((TM, D_MODEL), lambda i: (_prompt_tile(i), 0))
    sample_tok = pl.BlockSpec((TM, D_MODEL), lambda i: (_sample_tile(i), 0))
    if split_out:
        out_specs = [prompt_tok, sample_tok]
        out_shape = [jax.ShapeDtypeStruct((N_PROMPT, D_MODEL), F32),
                     jax.ShapeDtypeStruct((N_SAMPLE, D_MODEL), F32)]
    else:
        out_specs = tok
        out_shape = jax.ShapeDtypeStruct((N_TOK, D_MODEL), F32)
    return pl.pallas_call(
        functools.partial(_ffn_kernel, sub=sub, split_in=split_in, split_out=split_out),
        grid=(N_TILES,),
        in_specs=([prompt_tok, sample_tok] if split_in else [tok]) + [
            _mod_spec(layer),
            _gain_spec(layer),
            pl.BlockSpec((None, None, D_MODEL, 2 * D_FF), lambda i: (layer, which, 0, 0), pipeline_mode=resident),
            pl.BlockSpec((None, None, D_FF, D_MODEL), lambda i: (layer, which, 0, 0), pipeline_mode=resident),
        ],
        out_specs=out_specs,
        out_shape=out_shape,
        compiler_params=_params(1),
        name=f"ffn{sub}",
    )(*xs, mods, gains, w1, w2)


def _mixer_norm(x_ref, m_ref, g_ref):
    return (_rms(x_ref[...], g_ref[2:3, :]) * (1.0 + m_ref[4:5, :]) + m_ref[3:4, :]).astype(BF16)


def _store_heads(state_ref, x, n_heads):
    for b in range(SEQ_PER_TILE):
        for h in range(n_heads):
            state_ref[b, 0, h] = x[b * SEQ:(b + 1) * SEQ, h * HEAD_DIM:(h + 1) * HEAD_DIM]


def _even_in_kernel(x_ref, m_ref, g_ref, w_ref, q_ref, k_ref, v_ref, bg_ref, z_ref, sk_ref, sv_ref):
    u = _dot(_mixer_norm(x_ref, m_ref, g_ref), w_ref[...])
    k = u[:, HD_A:2 * HD_A]
    v = u[:, 2 * HD_A:3 * HD_A]
    q_ref[...] = (u[:, 0:HD_A] * ATT_SCALE).astype(BF16)
    k_ref[...] = k.astype(BF16)
    v_ref[...] = v.astype(BF16)
    bg_ref[...] = u[:, 3 * HD_A:3 * HD_A + C_B]
    z_ref[...] = u[:, 3 * HD_A + C_B:3 * HD_A + 2 * C_B] * u[:, 3 * HD_A + 2 * C_B:3 * HD_A + 3 * C_B]

    @pl.when(pl.program_id(0) < PROMPT_TILES)
    def _():
        _store_heads(sk_ref, k, H_A)
        _store_heads(sv_ref, v, H_A)


def _state_spec(n_heads):
    return pl.BlockSpec((SEQ_PER_TILE, 1, n_heads, SEQ, HEAD_DIM), lambda i: (_prompt_tile(i), 0, 0, 0, 0))


def _even_in(x, mods, gains, w_in, layer, idx):
    tok = lambda w: pl.BlockSpec((TM, w), lambda i: (i, 0))
    state = jax.ShapeDtypeStruct((BATCH, 1, H_A, SEQ, HEAD_DIM), F32)
    return pl.pallas_call(
        _even_in_kernel,
        grid=(N_TILES,),
        in_specs=[
            tok(D_MODEL), _mod_spec(layer), _gain_spec(layer),
            pl.BlockSpec((None, D_MODEL, EVEN_IN), lambda i: (idx, 0, 0)),
        ],
        out_specs=[tok(HD_A), tok(HD_A), tok(HD_A), tok(C_B), tok(C_B), _state_spec(H_A), _state_spec(H_A)],
        out_shape=[
            jax.ShapeDtypeStruct((N_TOK, HD_A), BF16),
            jax.ShapeDtypeStruct((N_TOK, HD_A), BF16),
            jax.ShapeDtypeStruct((N_TOK, HD_A), BF16),
            jax.ShapeDtypeStruct((N_TOK, C_B), F32),
            jax.ShapeDtypeStruct((N_TOK, C_B), F32),
            state, state,
        ],
        compiler_params=_params(1),
        name="even_in",
    )(x, mods, gains, w_in)


def _swap_pairs(x):
    n = x.shape[-1]
    lane = lax.broadcasted_iota(jnp.int32, x.shape, x.ndim - 1)
    return jnp.where(lane % 2 == 0, pltpu.roll(x, n - 1, x.ndim - 1), pltpu.roll(x, 1, x.ndim - 1))


def _odd_in_kernel(x_ref, m_ref, g_ref, w_ref, ng_ref, cos_ref, sin_ref, ones_ref,
                   uc_ref, q_ref, k_ref, v_ref, sk_ref, sv_ref):
    u = _dot(_mixer_norm(x_ref, m_ref, g_ref), w_ref[...])
    uc_ref[...] = u[:, 0:C_POOL]
    qk = u[:, C_POOL:C_POOL + QK_W]
    sq = qk * qk
    hi = sq.astype(BF16)
    lo = (sq - hi.astype(F32)).astype(BF16)
    ones = ones_ref[...]
    ms = (_dot(hi, ones) + _dot(lo, ones)) * (1.0 / HEAD_DIM)
    n = qk * lax.rsqrt(ms + RMS_EPS) * ng_ref[...]
    r = n * cos_ref[...] + _swap_pairs(n) * sin_ref[...]
    k = r[:, HD_Q:QK_W]
    v = u[:, C_POOL + QK_W:ODD_IN]
    q_ref[...] = (r[:, 0:HD_Q] * ATT_SCALE).astype(BF16)
    k_ref[...] = k.astype(BF16)
    v_ref[...] = v.astype(BF16)

    @pl.when(pl.program_id(0) < PROMPT_TILES)
    def _():
        _store_heads(sk_ref, k, HKV_D)
        _store_heads(sv_ref, v, HKV_D)


def _rope_tile_index(i):
    return jnp.where(i < PROMPT_TILES, 0, 1 + (i - PROMPT_TILES) % TILES_PER_SAMPLE)


def _odd_in(x, mods, gains, w_in, qk_gain, cos_t, sin_t, ones_bd, layer, idx):
    tok = lambda w: pl.BlockSpec((TM, w), lambda i: (i, 0))
    state = jax.ShapeDtypeStruct((BATCH, 1, HKV_D, SEQ, HEAD_DIM), F32)
    return pl.pallas_call(
        _odd_in_kernel,
        grid=(N_TILES,),
        in_specs=[
            tok(D_MODEL), _mod_spec(layer), _gain_spec(layer),
            pl.BlockSpec((None, D_MODEL, ODD_IN), lambda i: (idx, 0, 0)),
            pl.BlockSpec((1, QK_W), lambda i: (0, 0)),
            pl.BlockSpec((TM, QK_W), lambda i: (_rope_tile_index(i), 0)),
            pl.BlockSpec((TM, QK_W), lambda i: (_rope_tile_index(i), 0)),
            pl.BlockSpec((QK_W, QK_W), lambda i: (0, 0)),
        ],
        out_specs=[tok(C_POOL), tok(HD_Q), tok(HD_KV), tok(HD_KV), _state_spec(HKV_D), _state_spec(HKV_D)],
        out_shape=[
            jax.ShapeDtypeStruct((N_TOK, C_POOL), F32),
            jax.ShapeDtypeStruct((N_TOK, HD_Q), BF16),
            jax.ShapeDtypeStruct((N_TOK, HD_KV), BF16),
            jax.ShapeDtypeStruct((N_TOK, HD_KV), BF16),
            state, state,
        ],
        compiler_params=_params(1),
        name="odd_in",
    )(x, mods, gains, w_in, qk_gain, cos_t, sin_t, ones_bd)


def _low_half(rows):
    return lax.broadcasted_iota(jnp.int32, (rows, LANES), 1) < HEAD_DIM


def _split_pair(qp):
    low = _low_half(qp.shape[0])
    zero = jnp.zeros_like(qp)
    return jnp.concatenate([jnp.where(low, qp, zero), jnp.where(low, zero, qp)], axis=0)


def _with_ones(v):
    return jnp.concatenate([v, jnp.ones_like(v)], axis=1)


def _softmax_pv(scores, values):
    m = scores[0].max(axis=-1, keepdims=True)
    for s in scores[1:]:
        m = jnp.maximum(m, s.max(axis=-1, keepdims=True))
    acc = None
    for s, v in zip(scores, values):
        o = _dot(jnp.exp(s - m).astype(BF16), v)
        acc = o if acc is None else acc + o
    return acc[:, 0:LANES] / acc[:, LANES:2 * LANES]


def _merge_pair(o):
    m = o.shape[0] // 2
    return jnp.where(_low_half(m), o[0:m], o[m:2 * m])


def _short_conv(z, cw_ref, cb_ref, seq_len):
    rows = z.shape[0]
    pos = lax.broadcasted_iota(jnp.int32, z.shape, 0)
    z_prev = jnp.where(pos == 0, 0.0, pltpu.roll(z, 1, 0))
    z_next = jnp.where(pos == seq_len - 1, 0.0, pltpu.roll(z, rows - 1, 0))
    y = z_prev * cw_ref[0:1, :]
    y = y + z * cw_ref[1:2, :]
    y = y + z_next * cw_ref[2:3, :]
    return y + cb_ref[...]


def _pool_mix(uc, pw_ref, ps_ref, seq_len):
    pos = lax.broadcasted_iota(jnp.int32, (seq_len, POOL_C), 0)
    pad = jnp.zeros((POOL_PAD, POOL_C), F32)
    n_ext = seq_len + 2 * POOL_PAD
    outs = []
    for gi, win in enumerate(POOL_WINDOWS):
        ug = uc[:, gi * POOL_C:(gi + 1) * POOL_C]
        run = jnp.concatenate([pad, ug, pad], axis=0)
        span = 1
        while span < win:
            run = run + pltpu.roll(run, span, 0)
            span *= 2
        back = win // 2 - 1
        if back:
            run = pltpu.roll(run, n_ext - back, 0)
        wsum = run[POOL_PAD:POOL_PAD + seq_len, :]
        cnt = jnp.minimum(pos + (win - win // 2), seq_len) - jnp.maximum(pos - win // 2, 0)
        pooled = (wsum / cnt.astype(F32) - ug).astype(BF16)
        outs.append(_dot(pooled, pw_ref[gi]))
    return jnp.concatenate(outs, axis=-1) * ps_ref[...]


def _even_prompt_kernel(q_ref, k_ref, v_ref, bg_ref, z_ref, cw_ref, cb_ref, fa_ref, fb_ref):
    outs = []
    for p in range(H_A // 2):
        sl = slice(p * LANES, (p + 1) * LANES)
        s = _dot_t(_split_pair(q_ref[:, sl]), k_ref[:, sl])
        outs.append(_merge_pair(_softmax_pv([s], [_with_ones(v_ref[:, sl])])))
    fa_ref[...] = jnp.concatenate(outs, axis=-1).astype(BF16)
    fb_ref[...] = (bg_ref[...] * _short_conv(z_ref[...], cw_ref, cb_ref, SEQ)).astype(BF16)


def _even_prompt(q, k, v, bg, z, conv_w, conv_b, idx):
    seq = lambda w: pl.BlockSpec((SEQ, w), lambda b: (b, 0))
    return pl.pallas_call(
        _even_prompt_kernel,
        grid=(BATCH,),
        in_specs=[seq(HD_A), seq(HD_A), seq(HD_A), seq(C_B), seq(C_B),
                  pl.BlockSpec((None, CONV_W, C_B), lambda b: (idx, 0, 0)),
                  pl.BlockSpec((None, 1, C_B), lambda b: (idx, 0, 0))],
        out_specs=[seq(HD_A), seq(C_B)],
        out_shape=[
            jax.ShapeDtypeStruct((N_PROMPT, HD_A), BF16),
            jax.ShapeDtypeStruct((N_PROMPT, C_B), BF16),
        ],
        compiler_params=_params(1),
        name="even_prompt",
    )(q, k, v, bg, z, conv_w, conv_b)


def _na_bias_tiles(rpb_h):
    c = lax.broadcasted_iota(jnp.int32, (GRID_W, GRID_W), 0)
    kc = lax.broadcasted_iota(jnp.int32, (GRID_W, GRID_W), 1)
    cs = jnp.clip(c - NA_COLS // 2, 0, GRID_W - NA_COLS)
    valid = (kc >= cs) & (kc < cs + NA_COLS)
    tiles = []
    for ro in range(2 * NA_ROWS - 1):
        g = jnp.broadcast_to(rpb_h[ro:ro + 1, :], (GRID_W, LANES))
        skew = pltpu.roll(g, LANES - (NA_COLS - 1), 1, stride=1, stride_axis=0)
        tiles.append(jnp.where(valid, skew[:, 0:GRID_W], NEG_INF))
    return tiles


def _even_sample_kernel(q_ref, k_ref, v_ref, bg_ref, z_ref, ck_ref, cv_ref, rpb_ref, cw_ref, cb_ref,
                        fa_ref, fb_ref, bias_ref):
    win = NA_ROWS * GRID_W
    for h in range(2):
        tiles = _na_bias_tiles(rpb_ref[h])
        for var in range(NA_ROWS):
            bias_ref[var, h * GRID_W:(h + 1) * GRID_W, :] = jnp.concatenate(
                [tiles[i - var + NA_ROWS - 1] for i in range(NA_ROWS)], axis=1)
    ck = ck_ref[...]
    cv_ext = _with_ones(cv_ref[...])

    def rows(it, carry):
        for j in range(ROW_UNROLL):
            r = it * ROW_UNROLL + j
            rs = jnp.clip(r - NA_ROWS // 2, 0, GRID_ROWS - NA_ROWS)
            q0 = pl.multiple_of(r * GRID_W, GRID_W)
            k0 = pl.multiple_of(rs * GRID_W, GRID_W)
            q2 = _split_pair(q_ref[pl.ds(q0, GRID_W), :])
            s_loc = _dot_t(q2, k_ref[pl.ds(k0, win), :]) + bias_ref[r - rs]
            s_ctx = _dot_t(q2, ck)
            o = _softmax_pv([s_loc, s_ctx], [_with_ones(v_ref[pl.ds(k0, win), :]), cv_ext])
            fa_ref[pl.ds(q0, GRID_W), :] = _merge_pair(o).astype(BF16)
        return carry

    lax.fori_loop(0, GRID_ROWS // ROW_UNROLL, rows, 0)
    fb_ref[...] = (bg_ref[...] * _short_conv(z_ref[...], cw_ref, cb_ref, DEC_SEQ)).astype(BF16)


def _even_sample(q, k, v, bg, z, ctx_k, ctx_v, rpb, conv_w, conv_b, idx):
    first = N_PROMPT // DEC_SEQ
    seq_in = pl.BlockSpec((DEC_SEQ, LANES), lambda b, p: (first + b, p))
    seq_out = pl.BlockSpec((DEC_SEQ, LANES), lambda b, p: (b, p))
    ctx = pl.BlockSpec((None, PAST_LEN, LANES), lambda b, p: (b, 0, p))
    return pl.pallas_call(
        _even_sample_kernel,
        grid=(DEC_BATCH, H_A // 2),
        in_specs=[seq_in, seq_in, seq_in, seq_in, seq_in, ctx, ctx,
                  pl.BlockSpec((2, 2 * NA_ROWS - 1, LANES), lambda b, p: (p, 0, 0)),
                  pl.BlockSpec((None, CONV_W, LANES), lambda b, p: (idx, 0, p)),
                  pl.BlockSpec((None, 1, LANES), lambda b, p: (idx, 0, p))],
        out_specs=[seq_out, seq_out],
        out_shape=[
            jax.ShapeDtypeStruct((N_SAMPLE, HD_A), BF16),
            jax.ShapeDtypeStruct((N_SAMPLE, C_B), BF16),
        ],
        scratch_shapes=[pltpu.VMEM((NA_ROWS, 2 * GRID_W, NA_ROWS * GRID_W), F32)],
        compiler_params=_params(2),
        name="even_sample",
    )(q, k, v, bg, z, ctx_k, ctx_v, rpb, conv_w, conv_b)


def _split_groups(q):
    low = _low_half(q.shape[0])
    zero = jnp.zeros((q.shape[0], LANES), BF16)
    pairs = [q[:, j * LANES:(j + 1) * LANES] for j in range(GQA_GROUP)]
    return jnp.concatenate([jnp.where(low, p, zero) for p in pairs] + [jnp.where(low, zero, p) for p in pairs],
                           axis=0)


def _merge_groups(o, m):
    low = _low_half(m)
    half = GQA_GROUP * m
    return jnp.concatenate([jnp.where(low, o[j * m:(j + 1) * m], o[half + j * m:half + (j + 1) * m])
                            for j in range(GQA_GROUP)], axis=-1)


def _odd_prompt_kernel(uc_ref, q_ref, k_ref, v_ref, pw_ref, ps_ref, fc_ref, fd_ref):
    s = _dot_t(_split_groups(q_ref[...]), k_ref[...])
    o = _softmax_pv([s], [_with_ones(v_ref[...])])
    fd_ref[...] = _merge_groups(o, SEQ).astype(BF16)
    fc_ref[...] = _pool_mix(uc_ref[...], pw_ref, ps_ref, SEQ).astype(BF16)


def _odd_prompt(uc, q, k, v, pool_w, pool_scale, idx):
    seq = lambda w: pl.BlockSpec((SEQ, w), lambda b: (b, 0))
    return pl.pallas_call(
        _odd_prompt_kernel,
        grid=(BATCH,),
        in_specs=[seq(C_POOL), seq(HD_Q), seq(HD_KV), seq(HD_KV),
                  pl.BlockSpec((None, N_POOL, POOL_C, POOL_C), lambda b: (idx, 0, 0, 0)),
                  pl.BlockSpec((None, 1, C_POOL), lambda b: (idx, 0, 0))],
        out_specs=[seq(C_POOL), seq(HD_Q)],
        out_shape=[
            jax.ShapeDtypeStruct((N_PROMPT, C_POOL), BF16),
            jax.ShapeDtypeStruct((N_PROMPT, HD_Q), BF16),
        ],
        compiler_params=_params(1),
        name="odd_prompt",
    )(uc, q, k, v, pool_w, pool_scale)


def _odd_sample_kernel(uc_ref, q_ref, k_ref, v_ref, ck_ref, cv_ref, pw_ref, ps_ref, fc_ref, fd_ref,
                       vext_ref, cvext_ref):
    @pl.when(pl.program_id(1) == 0)
    def _():
        fc_ref[...] = _pool_mix(uc_ref[...], pw_ref, ps_ref, DEC_SEQ).astype(BF16)
        vext_ref[...] = _with_ones(v_ref[...])
        cvext_ref[...] = _with_ones(cv_ref[...])

    k = k_ref[...]
    ck = ck_ref[...]
    low = _low_half(Q_BLK)
    outs = []
    for j in range(GQA_GROUP):
        q2 = _split_pair(q_ref[:, j * LANES:(j + 1) * LANES])
        o = _softmax_pv([_dot_t(q2, k), _dot_t(q2, ck)], [vext_ref[...], cvext_ref[...]])
        outs.append(jnp.where(low, o[0:Q_BLK], o[Q_BLK:2 * Q_BLK]))
    fd_ref[...] = jnp.concatenate(outs, axis=-1).astype(BF16)


def _odd_sample(uc, q, k, v, ctx_k, ctx_v, pool_w, pool_scale, idx):
    first = N_PROMPT // DEC_SEQ
    n_qb = DEC_SEQ // Q_BLK
    first_q = N_PROMPT // Q_BLK
    whole_in = lambda w: pl.BlockSpec((DEC_SEQ, w), lambda b, i: (first + b, 0))
    ctx = pl.BlockSpec((None, PAST_LEN, HD_KV), lambda b, i: (b, 0, 0))
    return pl.pallas_call(
        _odd_sample_kernel,
        grid=(DEC_BATCH, n_qb),
        in_specs=[whole_in(C_POOL),
                  pl.BlockSpec((Q_BLK, HD_Q), lambda b, i: (first_q + b * n_qb + i, 0)),
                  whole_in(HD_KV), whole_in(HD_KV), ctx, ctx,
                  pl.BlockSpec((None, N_POOL, POOL_C, POOL_C), lambda b, i: (idx, 0, 0, 0)),
                  pl.BlockSpec((None, 1, C_POOL), lambda b, i: (idx, 0, 0))],
        out_specs=[pl.BlockSpec((DEC_SEQ, C_POOL), lambda b, i: (b, 0)),
                   pl.BlockSpec((Q_BLK, HD_Q), lambda b, i: (b * n_qb + i, 0))],
        out_shape=[
            jax.ShapeDtypeStruct((N_SAMPLE, C_POOL), BF16),
            jax.ShapeDtypeStruct((N_SAMPLE, HD_Q), BF16),
        ],
        scratch_shapes=[pltpu.VMEM((DEC_SEQ, 2 * LANES), BF16), pltpu.VMEM((PAST_LEN, 2 * LANES), BF16)],
        compiler_params=_params(2),
        name="odd_sample",
    )(uc, q, k, v, ctx_k, ctx_v, pool_w, pool_scale)


def _mix_out_kernel(x_ref, fa_p_ref, fb_p_ref, fa_s_ref, fb_s_ref, m_ref, g_ref, w_ref, o_ref):
    is_prompt = pl.program_id(0) < PROMPT_TILES
    fa = jnp.where(is_prompt, fa_p_ref[...], fa_s_ref[...])
    fb = jnp.where(is_prompt, fb_p_ref[...], fb_s_ref[...])
    half = fa.shape[1]
    out = _dot(fa, w_ref[0:half, :]) + _dot(fb, w_ref[half:, :])
    o_ref[...] = x_ref[...] + m_ref[5:6, :] * _rms(out, g_ref[3:4, :])


def _mix_out(x, fa_p, fb_p, fa_s, fb_s, mods, gains, w_out, layer, idx):
    tok = pl.BlockSpec((TM, D_MODEL), lambda i: (i, 0))
    half = D_MODEL // 2
    feat_p = pl.BlockSpec((TM, half), lambda i: (_prompt_tile(i), 0))
    feat_s = pl.BlockSpec((TM, half), lambda i: (_sample_tile(i), 0))
    return pl.pallas_call(
        _mix_out_kernel,
        grid=(N_TILES,),
        in_specs=[tok, feat_p, feat_p, feat_s, feat_s, _mod_spec(layer), _gain_spec(layer),
                  pl.BlockSpec((None, D_MODEL, D_MODEL), lambda i: (idx, 0, 0))],
        out_specs=tok,
        out_shape=jax.ShapeDtypeStruct((N_TOK, D_MODEL), F32),
        compiler_params=_params(1),
        name="mix_out",
    )(x, fa_p, fb_p, fa_s, fb_s, mods, gains, w_out)


def _rope_tables():
    t = jnp.arange(DEC_SEQ)
    n_freq = HEAD_DIM // 4
    inv = ROPE_BASE ** (-jnp.arange(n_freq, dtype=F32) / n_freq)
    ang = jnp.concatenate([(t // GRID_W).astype(F32)[:, None] * inv,
                           (t % GRID_W).astype(F32)[:, None] * inv], axis=-1)
    cos = jnp.repeat(jnp.cos(ang), 2, axis=-1)
    sin = jnp.repeat(jnp.sin(ang), 2, axis=-1)
    sign = jnp.tile(jnp.asarray([-1.0, 1.0], F32), HEAD_DIM // 2)
    n_heads = HQ_D + HKV_D
    cos = jnp.tile(cos, (1, n_heads))
    sin = jnp.tile(sin * sign, (1, n_heads))
    cos = jnp.concatenate([jnp.ones((TM, QK_W), F32), cos], axis=0)
    sin = jnp.concatenate([jnp.zeros((TM, QK_W), F32), sin], axis=0)
    return cos, sin


def _head_ones():
    head = jnp.arange(QK_W) // HEAD_DIM
    return (head[:, None] == head[None, :]).astype(BF16)


def _merge_ctx_heads(cache):
    n, h = cache.shape[1], cache.shape[2]
    return cache.transpose(1, 0, 3, 2, 4).reshape(n, DEC_BATCH, PAST_LEN, h * HEAD_DIM).astype(BF16)


def _pair_gqa_heads(w, axis):
    shape = w.shape
    w = w.reshape(shape[:axis] + (HKV_D, GQA_GROUP, HEAD_DIM) + shape[axis + 1:])
    return jnp.swapaxes(w, axis, axis + 1).reshape(shape)


def kernel(x_prompt, x_sample, cache_a_k, cache_a_v, cache_d_k, cache_d_v, c, c_ctx, mod_w, mod_b, norm_w,
           ffn_w1, ffn_w2, ev_w_in, ev_rpb, ev_conv_w, ev_conv_b, ev_w_out, od_w_in, od_pool_w,
           od_pool_scale, od_q_norm, od_k_norm, od_w_out):
    xs = (x_prompt.reshape(N_PROMPT, D_MODEL), x_sample.reshape(N_SAMPLE, D_MODEL))
    cond = jnp.concatenate([c_ctx[None, :], c, jnp.zeros((COND_PAD - N_COND, D_MODEL), F32)], axis=0)
    mods = _modulation(cond, mod_w, mod_b).reshape(DEPTH, COND_PAD, N_MOD, D_MODEL)
    w1, w2 = ffn_w1.astype(BF16), ffn_w2.astype(BF16)
    ev_w_in_b, ev_w_out_b = ev_w_in.astype(BF16), ev_w_out.astype(BF16)
    od_w_in_b = jnp.concatenate([od_w_in[..., :C_POOL],
                                 _pair_gqa_heads(od_w_in[..., C_POOL:C_POOL + HD_Q], 2),
                                 od_w_in[..., C_POOL + HD_Q:]], axis=-1).astype(BF16)
    od_w_out_b = jnp.concatenate([od_w_out[:, :C_POOL], _pair_gqa_heads(od_w_out[:, C_POOL:], 1)],
                                 axis=1).astype(BF16)
    ctx_a_k, ctx_a_v = _merge_ctx_heads(cache_a_k), _merge_ctx_heads(cache_a_v)
    ctx_d_k, ctx_d_v = _merge_ctx_heads(cache_d_k), _merge_ctx_heads(cache_d_v)
    rpb = jnp.pad(ev_rpb, ((0, 0), (0, 0), (0, 0), (0, LANES - ev_rpb.shape[-1])))
    conv_b = ev_conv_b[:, None, :]
    pool_w, pool_scale = od_pool_w.astype(BF16), od_pool_scale[:, None, :]
    cos_t, sin_t = _rope_tables()
    ones_bd = _head_ones()
    states = []
    for l in range(DEPTH):
        i = l // 2
        x = _ffn(xs, mods, norm_w, w1, w2, l, 0)
        if l % 2 == 0:
            q, k, v, bg, z, s_k, s_v = _even_in(x, mods, norm_w, ev_w_in_b, l, i)
            fa_p, fb_p = _even_prompt(q, k, v, bg, z, ev_conv_w, conv_b, i)
            fa_s, fb_s = _even_sample(q, k, v, bg, z, ctx_a_k[i], ctx_a_v[i], rpb[i], ev_conv_w, conv_b, i)
            w_out = ev_w_out_b
        else:
            gain = jnp.concatenate([jnp.tile(od_q_norm[i], HQ_D), jnp.tile(od_k_norm[i], HKV_D)])[None, :]
            uc, q, k, v, s_k, s_v = _odd_in(x, mods, norm_w, od_w_in_b, gain, cos_t, sin_t, ones_bd, l, i)
            fa_p, fb_p = _odd_prompt(uc, q, k, v, pool_w, pool_scale, i)
            fa_s, fb_s = _odd_sample(uc, q, k, v, ctx_d_k[i], ctx_d_v[i], pool_w, pool_scale, i)
            w_out = od_w_out_b
        states.append((s_k, s_v))
        x = _mix_out(x, fa_p, fb_p, fa_s, fb_s, mods, norm_w, w_out, l, i)
        last = l == DEPTH - 1
        xs = _ffn((x,), mods, norm_w, w1, w2, l, 2, split_out=last)
        xs = tuple(xs) if last else (xs,)
    y_prompt = xs[0].reshape(BATCH, SEQ, D_MODEL)
    y_sample = xs[1].reshape(DEC_BATCH, DEC_SEQ, D_MODEL)
    return (y_prompt, y_sample, states[0][0], states[0][1], states[1][0], states[1][1])
```

```python
import functools

import jax
import jax.numpy as jnp
from jax import lax
from jax.experimental import pallas as pl
from jax.experimental.pallas import tpu as pltpu

D_MODEL = 1024
BATCH = 32
SEQ = 256
DEPTH = 2
DEC_BATCH = 2
DEC_SEQ = 2048
PAST_LEN = 256
GRID_W = 64
HEAD_DIM = 64
N_MOD = 9
D_FF = 2816
FFN_RES = 0.5
H_A = 8
NA_ROWS = 8
NA_COLS = 16
C_B = 512
CONV_W = 3
C_POOL = 512
POOL_WINDOWS = (2, 4, 8, 16)
N_POOL = 4
POOL_C = C_POOL // N_POOL
HQ_D = 8
HKV_D = 2
GQA_GROUP = HQ_D // HKV_D
ROPE_BASE = 10000.0
EVEN_IN = 3 * H_A * HEAD_DIM + 3 * C_B
ODD_IN = C_POOL + (HQ_D + 2 * HKV_D) * HEAD_DIM
RMS_EPS = 1e-6
NEG_INF = -1e30
ATT_SCALE = HEAD_DIM ** -0.5

LANES = 128
GRID_ROWS = DEC_SEQ // GRID_W
N_PROMPT = BATCH * SEQ
N_SAMPLE = DEC_BATCH * DEC_SEQ
N_TOK = N_PROMPT + N_SAMPLE
N_COND = 1 + DEC_BATCH
COND_PAD = 8
HD_A = H_A * HEAD_DIM
HD_Q = HQ_D * HEAD_DIM
HD_KV = HKV_D * HEAD_DIM
QK_W = HD_Q + HD_KV
POOL_PAD = 8

TM = 512
N_TILES = N_TOK // TM
PROMPT_TILES = N_PROMPT // TM
TILES_PER_SAMPLE = DEC_SEQ // TM
SEQ_PER_TILE = TM // SEQ
TM_FFN = 1024
FFN_HALF = TM_FFN // 2
MXU_DIM = 256
FF_CHUNKS = tuple((lo, min(lo + 3 * MXU_DIM, D_FF)) for lo in range(0, D_FF, 3 * MXU_DIM))
MOD_TN = 1536
Q_BLK = 256
ROW_UNROLL = 4
VMEM_LIMIT = 60 * 1024 * 1024

F32 = jnp.float32
BF16 = jnp.bfloat16


def _params(n_grid):
    return pltpu.CompilerParams(dimension_semantics=("arbitrary",) * n_grid, vmem_limit_bytes=VMEM_LIMIT)


def _cond_of_tile(i, tm=TM):
    n_prompt = N_PROMPT // tm
    return jnp.where(i < n_prompt, 0, 1 + (i - n_prompt) // (DEC_SEQ // tm))


def _prompt_tile(i, tm=TM):
    return jnp.minimum(i, N_PROMPT // tm - 1)


def _sample_tile(i, tm=TM):
    return jnp.maximum(i - N_PROMPT // tm, 0)


def _rms(x, g):
    return x * lax.rsqrt(jnp.mean(x * x, axis=-1, keepdims=True) + RMS_EPS) * g


def _sigmoid(x):
    return 1.0 / (1.0 + jnp.exp(-x))


def _dot(a, b):
    return jnp.dot(a, b, preferred_element_type=F32)


def _dot_t(a, b):
    return lax.dot_general(a, b, (((1,), (1,)), ((), ())), preferred_element_type=F32)


def _mod_kernel(c_ref, w_ref, b_ref, o_ref):
    c = c_ref[...]
    sc = (c * _sigmoid(c)).astype(BF16)
    o_ref[...] = _dot(sc, w_ref[...].astype(BF16)) + b_ref[...]


def _modulation(cond, mod_w, mod_b):
    n_col = N_MOD * D_MODEL
    return pl.pallas_call(
        _mod_kernel,
        grid=(DEPTH, n_col // MOD_TN),
        in_specs=[
            pl.BlockSpec((COND_PAD, D_MODEL), lambda l, j: (0, 0)),
            pl.BlockSpec((None, D_MODEL, MOD_TN), lambda l, j: (l, 0, j)),
            pl.BlockSpec((None, 1, MOD_TN), lambda l, j: (l, 0, j)),
        ],
        out_specs=pl.BlockSpec((None, COND_PAD, MOD_TN), lambda l, j: (l, 0, j)),
        out_shape=jax.ShapeDtypeStruct((DEPTH, COND_PAD, n_col), F32),
        compiler_params=_params(2),
        name="modulation",
    )(cond, mod_w, mod_b.reshape(DEPTH, 1, n_col))


def _mod_spec(layer, tm=TM):
    return pl.BlockSpec((None, None, N_MOD, D_MODEL), lambda i: (layer, _cond_of_tile(i, tm), 0, 0))


def _gain_spec(layer):
    return pl.BlockSpec((None, 6, D_MODEL), lambda i: (layer, 0, 0))


def _ffn_kernel(*refs, sub, split_in, mix, split_out):
    refs = list(refs)
    x_refs = [refs.pop(0) for _ in range(2 if split_in else 1)]
    feat_refs = [refs.pop(0) for _ in range(4 if mix else 0)]
    m_ref, g_ref = refs.pop(0), refs.pop(0)
    wo_ref = refs.pop(0) if mix else None
    w1_ref, w2_ref = refs.pop(0), refs.pop(0)
    out_refs = refs
    is_prompt = pl.program_id(0) < N_PROMPT // TM_FFN
    shift = m_ref[3 * sub:3 * sub + 1, :]
    scale = m_ref[3 * sub + 1:3 * sub + 2, :]
    gate = m_ref[3 * sub + 2:3 * sub + 3, :]
    for half in range(TM_FFN // FFN_HALF):
        rows = slice(half * FFN_HALF, (half + 1) * FFN_HALF)
        if split_in:
            x = jnp.where(is_prompt, x_refs[0][rows, :], x_refs[1][rows, :])
        else:
            x = x_refs[0][rows, :]
        if mix:
            fa = jnp.where(is_prompt, feat_refs[0][rows, :], feat_refs[2][rows, :])
            fb = jnp.where(is_prompt, feat_refs[1][rows, :], feat_refs[3][rows, :])
            n_a = fa.shape[1]
            mixed = _dot(fa, wo_ref[0:n_a, :]) + _dot(fb, wo_ref[n_a:, :])
            x = x + m_ref[5:6, :] * _rms(mixed, g_ref[3:4, :])
        h =(_rms(x, g_ref[2 * sub:2 * sub + 1, :]) * (1.0 + scale) + shift).astype(BF16)
        acc = jnp.zeros((FFN_HALF, D_MODEL), F32)
        for lo, hi in FF_CHUNKS:
            a = _dot(h, w1_ref[:, lo:hi])
            u = _dot(h, w1_ref[:, D_FF + lo:D_FF + hi])
            act = (a * _sigmoid(a) * u).astype(BF16)
            acc = acc + _dot(act, w2_ref[lo:hi, :])
        y = x + FFN_RES * gate * _rms(acc, g_ref[2 * sub + 1:2 * sub + 2, :])
        if split_out:
            @pl.when(is_prompt)
            def _():
                out_refs[0][rows, :] = y

            @pl.when(jnp.logical_not(is_prompt))
            def _():
                out_refs[1][rows, :] = y
        else:
            out_refs[0][rows, :] = y


def _ffn(xs, mods, gains, w1, w2, layer, sub, feats=None, w_out=None, idx=0, split_out=False):
    split_in = len(xs) == 2
    mix = feats is not None
    resident = pl.Buffered(1)
    which = sub // 2
    half_w = D_MODEL // 2
    feat_specs, mix_w_spec = [], []
    if mix:
        feat_p = pl.BlockSpec((TM_FFN, half_w), lambda i: (_prompt_tile(i, TM_FFN), 0))
        feat_s = pl.BlockSpec((TM_FFN, half_w), lambda i: (_sample_tile(i, TM_FFN), 0))
        feat_specs = [feat_p, feat_p, feat_s, feat_s]
        mix_w_spec = [pl.BlockSpec((None, D_MODEL, D_MODEL), lambda i: (idx, 0, 0), pipeline_mode=resident)]
    tok = pl.BlockSpec((TM_FFN, D_MODEL), lambda i: (i, 0))
    prompt_tok = pl.BlockSpec((TM_FFN, D_MODEL), lambda i: (_prompt_tile(i, TM_FFN), 0))
    sample_tok = pl.BlockSpec((TM_FFN, D_MODEL), lambda i: (_sample_tile(i, TM_FFN), 0))
    if split_out:
        out_specs = [prompt_tok, sample_tok]
        out_shape = [jax.ShapeDtypeStruct((N_PROMPT, D_MODEL), F32),
                     jax.ShapeDtypeStruct((N_SAMPLE, D_MODEL), F32)]
    else:
        out_specs = tok
        out_shape = jax.ShapeDtypeStruct((N_TOK, D_MODEL), F32)
    return pl.pallas_call(
        functools.partial(_ffn_kernel, sub=sub, split_in=split_in, mix=mix, split_out=split_out),
        grid=(N_TOK // TM_FFN,),
        in_specs=([prompt_tok, sample_tok] if split_in else [tok]) + feat_specs + [
            _mod_spec(layer, TM_FFN),
            _gain_spec(layer),
        ] + mix_w_spec + [
            pl.BlockSpec((None, None, D_MODEL, 2 * D_FF), lambda i: (layer, which, 0, 0), pipeline_mode=resident),
            pl.BlockSpec((None, None, D_FF, D_MODEL), lambda i: (layer, which, 0, 0), pipeline_mode=resident),
        ],
        out_specs=out_specs,
        out_shape=out_shape,
        compiler_params=_params(1),
        name=f"ffn{sub}",
    )(*xs, *(feats or ()), mods, gains, *((w_out,) if mix else ()), w1, w2)


def _mixer_norm(x_ref, m_ref, g_ref):
    return (_rms(x_ref[...], g_ref[2:3, :]) * (1.0 + m_ref[4:5, :]) + m_ref[3:4, :]).astype(BF16)


def _store_heads(state_ref, x, n_heads):
    for b in range(SEQ_PER_TILE):
        for h in range(n_heads):
            state_ref[b, 0, h] = x[b * SEQ:(b + 1) * SEQ, h * HEAD_DIM:(h + 1) * HEAD_DIM]


def _even_in_kernel(x_ref, m_ref, g_ref, w_ref, q_ref, k_ref, v_ref, bg_ref, z_ref, sk_ref, sv_ref):
    u = _dot(_mixer_norm(x_ref, m_ref, g_ref), w_ref[...])
    k = u[:, HD_A:2 * HD_A]
    v = u[:, 2 * HD_A:3 * HD_A]
    q_ref[...] = (u[:, 0:HD_A] * ATT_SCALE).astype(BF16)
    k_ref[...] = k.astype(BF16)
    v_ref[...] = v.astype(BF16)
    bg_ref[...] = u[:, 3 * HD_A:3 * HD_A + C_B]
    z_ref[...] = u[:, 3 * HD_A + C_B:3 * HD_A + 2 * C_B] * u[:, 3 * HD_A + 2 * C_B:3 * HD_A + 3 * C_B]

    @pl.when(pl.program_id(0) < PROMPT_TILES)
    def _():
        _store_heads(sk_ref, k, H_A)
        _store_heads(sv_ref, v, H_A)


def _state_spec(n_heads):
    return pl.BlockSpec((SEQ_PER_TILE, 1, n_heads, SEQ, HEAD_DIM), lambda i: (_prompt_tile(i), 0, 0, 0, 0))


def _even_in(x, mods, gains, w_in, layer, idx):
    tok = lambda w: pl.BlockSpec((TM, w), lambda i: (i, 0))
    state = jax.ShapeDtypeStruct((BATCH, 1, H_A, SEQ, HEAD_DIM), F32)
    return pl.pallas_call(
        _even_in_kernel,
        grid=(N_TILES,),
        in_specs=[
            tok(D_MODEL), _mod_spec(layer), _gain_spec(layer),
            pl.BlockSpec((None, D_MODEL, EVEN_IN), lambda i: (idx, 0, 0)),
        ],
        out_specs=[tok(HD_A), tok(HD_A), tok(HD_A), tok(C_B), tok(C_B), _state_spec(H_A), _state_spec(H_A)],
        out_shape=[
            jax.ShapeDtypeStruct((N_TOK, HD_A), BF16),
            jax.ShapeDtypeStruct((N_TOK, HD_A), BF16),
            jax.ShapeDtypeStruct((N_TOK, HD_A), BF16),
            jax.ShapeDtypeStruct((N_TOK, C_B), F32),
            jax.ShapeDtypeStruct((N_TOK, C_B), F32),
            state, state,
        ],
        compiler_params=_params(1),
        name="even_in",
    )(x, mods, gains, w_in)


def _swap_pairs(x):
    n = x.shape[-1]
    lane = lax.broadcasted_iota(jnp.int32, x.shape, x.ndim - 1)
    return jnp.where(lane % 2 == 0, pltpu.roll(x, n - 1, x.ndim - 1), pltpu.roll(x, 1, x.ndim - 1))


def _odd_in_kernel(x_ref, m_ref, g_ref, w_ref, ng_ref, cos_ref, sin_ref, ones_ref,
                   uc_ref, q_ref, k_ref, v_ref, sk_ref, sv_ref):
    u = _dot(_mixer_norm(x_ref, m_ref, g_ref), w_ref[...])
    uc_ref[...] = u[:, 0:C_POOL]
    qk = u[:, C_POOL:C_POOL + QK_W]
    sq = qk * qk
    hi = sq.astype(BF16)
    lo = (sq - hi.astype(F32)).astype(BF16)
    ones = ones_ref[...]
    sums = []
    for c0 in range(0, QK_W, MXU_DIM):
        c1 = min(c0 + MXU_DIM, QK_W)
        sums.append(_dot(hi[:, c0:c1], ones[0:c1 - c0, 0:c1 - c0]) + _dot(lo[:, c0:c1], ones[0:c1 - c0, 0:c1 - c0]))
    ms = jnp.concatenate(sums, axis=-1) * (1.0 / HEAD_DIM)
    n = qk * lax.rsqrt(ms + RMS_EPS) * ng_ref[...]
    r = n * cos_ref[...] + _swap_pairs(n) * sin_ref[...]
    k = r[:, HD_Q:QK_W]
    v = u[:, C_POOL + QK_W:ODD_IN]
    q_ref[...] = (r[:, 0:HD_Q] * ATT_SCALE).astype(BF16)
    k_ref[...] = k.astype(BF16)
    v_ref[...] = v.astype(BF16)

    @pl.when(pl.program_id(0) < PROMPT_TILES)
    def _():
        _store_heads(sk_ref, k, HKV_D)
        _store_heads(sv_ref, v, HKV_D)


def _rope_tile_index(i):
    return jnp.where(i < PROMPT_TILES, 0, 1 + (i - PROMPT_TILES) % TILES_PER_SAMPLE)


def _odd_in(x, mods, gains, w_in, qk_gain, cos_t, sin_t, ones_bd, layer, idx):
    tok = lambda w: pl.BlockSpec((TM, w), lambda i: (i, 0))
    state = jax.ShapeDtypeStruct((BATCH, 1, HKV_D, SEQ, HEAD_DIM), F32)
    return pl.pallas_call(
        _odd_in_kernel,
        grid=(N_TILES,),
        in_specs=[
            tok(D_MODEL), _mod_spec(layer), _gain_spec(layer),
            pl.BlockSpec((None, D_MODEL, ODD_IN), lambda i: (idx, 0, 0)),
            pl.BlockSpec((1, QK_W), lambda i: (0, 0)),
            pl.BlockSpec((TM, QK_W), lambda i: (_rope_tile_index(i), 0)),
            pl.BlockSpec((TM, QK_W), lambda i: (_rope_tile_index(i), 0)),
            pl.BlockSpec((MXU_DIM, MXU_DIM), lambda i: (0, 0)),
        ],
        out_specs=[tok(C_POOL), tok(HD_Q), tok(HD_KV), tok(HD_KV), _state_spec(HKV_D), _state_spec(HKV_D)],
        out_shape=[
            jax.ShapeDtypeStruct((N_TOK, C_POOL), F32),
            jax.ShapeDtypeStruct((N_TOK, HD_Q), BF16),
            jax.ShapeDtypeStruct((N_TOK, HD_KV), BF16),
            jax.ShapeDtypeStruct((N_TOK, HD_KV), BF16),
            state, state,
        ],
        compiler_params=_params(1),
        name="odd_in",
    )(x, mods, gains, w_in, qk_gain, cos_t, sin_t, ones_bd)


def _low_half(rows):
    return lax.broadcasted_iota(jnp.int32, (rows, LANES), 1) < HEAD_DIM


def _split_pair(qp):
    low = _low_half(qp.shape[0])
    zero = jnp.zeros_like(qp)
    return jnp.concatenate([jnp.where(low, qp, zero), jnp.where(low, zero, qp)], axis=0)


def _with_ones(v):
    return jnp.concatenate([v, jnp.ones_like(v)], axis=1)


def _softmax_pv(scores, values):
    m = scores[0].max(axis=-1, keepdims=True)
    for s in scores[1:]:
        m = jnp.maximum(m, s.max(axis=-1, keepdims=True))
    acc = None
    for s, v in zip(scores, values):
        o = _dot(jnp.exp(s - m).astype(BF16), v)
        acc = o if acc is None else acc + o
    return acc[:, 0:LANES] / acc[:, LANES:2 * LANES]


def _merge_pair(o):
    m = o.shape[0] // 2
    return jnp.where(_low_half(m), o[0:m], o[m:2 * m])


def _short_conv(z, cw_ref, cb_ref, seq_len):
    rows = z.shape[0]
    pos = lax.broadcasted_iota(jnp.int32, z.shape, 0)
    z_prev = jnp.where(pos == 0, 0.0, pltpu.roll(z, 1, 0))
    z_next = jnp.where(pos == seq_len - 1, 0.0, pltpu.roll(z, rows - 1, 0))
    y = z_prev * cw_ref[0:1, :]
    y = y + z * cw_ref[1:2, :]
    y = y + z_next * cw_ref[2:3, :]
    return y + cb_ref[...]


def _pool_mix(uc, pw_ref, ps_ref, seq_len):
    pos = lax.broadcasted_iota(jnp.int32, (seq_len, POOL_C), 0)
    pad = jnp.zeros((POOL_PAD, POOL_C), F32)
    n_ext = seq_len + 2 * POOL_PAD
    outs = []
    for gi, win in enumerate(POOL_WINDOWS):
        ug = uc[:, gi * POOL_C:(gi + 1) * POOL_C]
        run = jnp.concatenate([pad, ug, pad], axis=0)
        span = 1
        while span < win:
            run = run + pltpu.roll(run, span, 0)
            span *= 2
        back = win // 2 - 1
        if back:
            run = pltpu.roll(run, n_ext - back, 0)
        wsum = run[POOL_PAD:POOL_PAD + seq_len, :]
        cnt = jnp.minimum(pos + (win - win // 2), seq_len) - jnp.maximum(pos - win // 2, 0)
        pooled = (wsum / cnt.astype(F32) - ug).astype(BF16)
        outs.append(_dot(pooled, pw_ref[gi]))
    return jnp.concatenate(outs, axis=-1) * ps_ref[...]


def _even_prompt_kernel(q_ref, k_ref, v_ref, bg_ref, z_ref, cw_ref, cb_ref, fa_ref, fb_ref):
    outs = []
    for p in range(H_A // 2):
        sl = slice(p * LANES, (p + 1) * LANES)
        s = _dot_t(_split_pair(q_ref[:, sl]), k_ref[:, sl])
        outs.append(_merge_pair(_softmax_pv([s], [_with_ones(v_ref[:, sl])])))
    fa_ref[...] = jnp.concatenate(outs, axis=-1).astype(BF16)
    fb_ref[...] = (bg_ref[...] * _short_conv(z_ref[...], cw_ref, cb_ref, SEQ)).astype(BF16)


def _even_prompt(q, k, v, bg, z, conv_w, conv_b, idx):
    seq = lambda w: pl.BlockSpec((SEQ, w), lambda b: (b, 0))
    return pl.pallas_call(
        _even_prompt_kernel,
        grid=(BATCH,),
        in_specs=[seq(HD_A), seq(HD_A), seq(HD_A), seq(C_B), seq(C_B),
                  pl.BlockSpec((None, CONV_W, C_B), lambda b: (idx, 0, 0)),
                  pl.BlockSpec((None, 1, C_B), lambda b: (idx, 0, 0))],
        out_specs=[seq(HD_A), seq(C_B)],
        out_shape=[
            jax.ShapeDtypeStruct((N_PROMPT, HD_A), BF16),
            jax.ShapeDtypeStruct((N_PROMPT, C_B), BF16),
        ],
        compiler_params=_params(1),
        name="even_prompt",
    )(q, k, v, bg, z, conv_w, conv_b)


def _na_bias_tiles(rpb_h):
    c = lax.broadcasted_iota(jnp.int32, (GRID_W, GRID_W), 0)
    kc = lax.broadcasted_iota(jnp.int32, (GRID_W, GRID_W), 1)
    cs = jnp.clip(c - NA_COLS // 2, 0, GRID_W - NA_COLS)
    valid = (kc >= cs) & (kc < cs + NA_COLS)
    tiles = []
    for ro in range(2 * NA_ROWS - 1):
        g = jnp.broadcast_to(rpb_h[ro:ro + 1, :], (GRID_W, LANES))
        skew = pltpu.roll(g, LANES - (NA_COLS - 1), 1, stride=1, stride_axis=0)
        tiles.append(jnp.where(valid, skew[:, 0:GRID_W], NEG_INF))
    return tiles


def _even_sample_kernel(q_ref, k_ref, v_ref, bg_ref, z_ref, ck_ref, cv_ref, rpb_ref, cw_ref, cb_ref,
                        fa_ref, fb_ref, bias_ref):
    win = NA_ROWS * GRID_W
    for h in range(2):
        tiles = _na_bias_tiles(rpb_ref[h])
        for var in range(NA_ROWS):
            bias_ref[var, h * GRID_W:(h + 1) * GRID_W, :] = jnp.concatenate(
                [tiles[i - var + NA_ROWS - 1] for i in range(NA_ROWS)], axis=1)
    ck = ck_ref[...]
    cv_ext = _with_ones(cv_ref[...])

    def rows(it, carry):
        for j in range(ROW_UNROLL):
            r = it * ROW_UNROLL + j
            rs = jnp.clip(r - NA_ROWS // 2, 0, GRID_ROWS - NA_ROWS)
            q0 = pl.multiple_of(r * GRID_W, GRID_W)
            k0 = pl.multiple_of(rs * GRID_W, GRID_W)
            q2 = _split_pair(q_ref[pl.ds(q0, GRID_W), :])
            s_loc = _dot_t(q2, k_ref[pl.ds(k0, win), :]) + bias_ref[r - rs]
            s_ctx = _dot_t(q2, ck)
            o = _softmax_pv([s_loc, s_ctx], [_with_ones(v_ref[pl.ds(k0, win), :]), cv_ext])
            fa_ref[pl.ds(q0, GRID_W), :] = _merge_pair(o).astype(BF16)
        return carry

    lax.fori_loop(0, GRID_ROWS // ROW_UNROLL, rows, 0)
    fb_ref[...] = (bg_ref[...] * _short_conv(z_ref[...], cw_ref, cb_ref, DEC_SEQ)).astype(BF16)


def _even_sample(q, k, v, bg, z, ctx_k, ctx_v, rpb, conv_w, conv_b, idx):
    first = N_PROMPT // DEC_SEQ
    seq_in = pl.BlockSpec((DEC_SEQ, LANES), lambda b, p: (first + b, p))
    seq_out = pl.BlockSpec((DEC_SEQ, LANES), lambda b, p: (b, p))
    ctx = pl.BlockSpec((None, PAST_LEN, LANES), lambda b, p: (b, 0, p))
    return pl.pallas_call(
        _even_sample_kernel,
        grid=(DEC_BATCH, H_A // 2),
        in_specs=[seq_in, seq_in, seq_in, seq_in, seq_in, ctx, ctx,
                  pl.BlockSpec((2, 2 * NA_ROWS - 1, LANES), lambda b, p: (p, 0, 0)),
                  pl.BlockSpec((None, CONV_W, LANES), lambda b, p: (idx, 0, p)),
                  pl.BlockSpec((None, 1, LANES), lambda b, p: (idx, 0, p))],
        out_specs=[seq_out, seq_out],
        out_shape=[
            jax.ShapeDtypeStruct((N_SAMPLE, HD_A), BF16),
            jax.ShapeDtypeStruct((N_SAMPLE, C_B), BF16),
        ],
        scratch_shapes=[pltpu.VMEM((NA_ROWS, 2 * GRID_W, NA_ROWS * GRID_W), F32)],
        compiler_params=_params(2),
        name="even_sample",
    )(q, k, v, bg, z, ctx_k, ctx_v, rpb, conv_w, conv_b)


def _split_groups(q):
    low = _low_half(q.shape[0])
    zero = jnp.zeros((q.shape[0], LANES), BF16)
    pairs = [q[:, j * LANES:(j + 1) * LANES] for j in range(GQA_GROUP)]
    return jnp.concatenate([jnp.where(low, p, zero) for p in pairs] + [jnp.where(low, zero, p) for p in pairs],
                           axis=0)


def _merge_groups(o, m):
    low = _low_half(m)
    half = GQA_GROUP * m
    return jnp.concatenate([jnp.where(low, o[j * m:(j + 1) * m], o[half + j * m:half + (j + 1) * m])
                            for j in range(GQA_GROUP)], axis=-1)


def _odd_prompt_kernel(uc_ref, q_ref, k_ref, v_ref, pw_ref, ps_ref, fc_ref, fd_ref):
    s = _dot_t(_split_groups(q_ref[...]), k_ref[...])
    o = _softmax_pv([s], [_with_ones(v_ref[...])])
    fd_ref[...] = _merge_groups(o, SEQ).astype(BF16)
    fc_ref[...] = _pool_mix(uc_ref[...], pw_ref, ps_ref, SEQ).astype(BF16)


def _odd_prompt(uc, q, k, v, pool_w, pool_scale, idx):
    seq = lambda w: pl.BlockSpec((SEQ, w), lambda b: (b, 0))
    return pl.pallas_call(
        _odd_prompt_kernel,
        grid=(BATCH,),
        in_specs=[seq(C_POOL), seq(HD_Q), seq(HD_KV), seq(HD_KV),
                  pl.BlockSpec((None, N_POOL, POOL_C, POOL_C), lambda b: (idx, 0, 0, 0)),
                  pl.BlockSpec((None, 1, C_POOL), lambda b: (idx, 0, 0))],
        out_specs=[seq(C_POOL), seq(HD_Q)],
        out_shape=[
            jax.ShapeDtypeStruct((N_PROMPT, C_POOL), BF16),
            jax.ShapeDtypeStruct((N_PROMPT, HD_Q), BF16),
        ],
        compiler_params=_params(1),
        name="odd_prompt",
    )(uc, q, k, v, pool_w, pool_scale)


def _odd_sample_kernel(uc_ref, q_ref, k_ref, v_ref, ck_ref, cv_ref, pw_ref, ps_ref, fc_ref, fd_ref,
                       vext_ref, cvext_ref):
    @pl.when(pl.program_id(1) == 0)
    def _():
        fc_ref[...] = _pool_mix(uc_ref[...], pw_ref, ps_ref, DEC_SEQ).astype(BF16)
        vext_ref[...] = _with_ones(v_ref[...])
        cvext_ref[...] = _with_ones(cv_ref[...])

    k = k_ref[...]
    ck = ck_ref[...]
    low = _low_half(Q_BLK)
    outs = []
    for j in range(GQA_GROUP):
        q2 = _split_pair(q_ref[:, j * LANES:(j + 1) * LANES])
        o = _softmax_pv([_dot_t(q2, k), _dot_t(q2, ck)], [vext_ref[...], cvext_ref[...]])
        outs.append(jnp.where(low, o[0:Q_BLK], o[Q_BLK:2 * Q_BLK]))
    fd_ref[...] = jnp.concatenate(outs, axis=-1).astype(BF16)


def _odd_sample(uc, q, k, v, ctx_k, ctx_v, pool_w, pool_scale, idx):
    first = N_PROMPT // DEC_SEQ
    n_qb = DEC_SEQ // Q_BLK
    first_q = N_PROMPT // Q_BLK
    whole_in = lambda w: pl.BlockSpec((DEC_SEQ, w), lambda b, i: (first + b, 0))
    ctx = pl.BlockSpec((None, PAST_LEN, HD_KV), lambda b, i: (b, 0, 0))
    return pl.pallas_call(
        _odd_sample_kernel,
        grid=(DEC_BATCH, n_qb),
        in_specs=[whole_in(C_POOL),
                  pl.BlockSpec((Q_BLK, HD_Q), lambda b, i: (first_q + b * n_qb + i, 0)),
                  whole_in(HD_KV), whole_in(HD_KV), ctx, ctx,
                  pl.BlockSpec((None, N_POOL, POOL_C, POOL_C), lambda b, i: (idx, 0, 0, 0)),
                  pl.BlockSpec((None, 1, C_POOL), lambda b, i: (idx, 0, 0))],
        out_specs=[pl.BlockSpec((DEC_SEQ, C_POOL), lambda b, i: (b, 0)),
                   pl.BlockSpec((Q_BLK, HD_Q), lambda b, i: (b * n_qb + i, 0))],
        out_shape=[
            jax.ShapeDtypeStruct((N_SAMPLE, C_POOL), BF16),
            jax.ShapeDtypeStruct((N_SAMPLE, HD_Q), BF16),
        ],
        scratch_shapes=[pltpu.VMEM((DEC_SEQ, 2 * LANES), BF16), pltpu.VMEM((PAST_LEN, 2 * LANES), BF16)],
        compiler_params=_params(2),
        name="odd_sample",
    )(uc, q, k, v, ctx_k, ctx_v, pool_w, pool_scale)


def _rope_tables():
    t = jnp.arange(DEC_SEQ)
    n_freq = HEAD_DIM // 4
    inv = ROPE_BASE ** (-jnp.arange(n_freq, dtype=F32) / n_freq)
    ang = jnp.concatenate([(t // GRID_W).astype(F32)[:, None] * inv,
                           (t % GRID_W).astype(F32)[:, None] * inv], axis=-1)
    cos = jnp.repeat(jnp.cos(ang), 2, axis=-1)
    sin = jnp.repeat(jnp.sin(ang), 2, axis=-1)
    sign = jnp.tile(jnp.asarray([-1.0, 1.0], F32), HEAD_DIM // 2)
    n_heads = HQ_D + HKV_D
    cos = jnp.tile(cos, (1, n_heads))
    sin = jnp.tile(sin * sign, (1, n_heads))
    cos = jnp.concatenate([jnp.ones((TM, QK_W), F32), cos], axis=0)
    sin = jnp.concatenate([jnp.zeros((TM, QK_W), F32), sin], axis=0)
    return cos, sin


def _head_ones():
    head = jnp.arange(MXU_DIM) // HEAD_DIM
    return (head[:, None] == head[None, :]).astype(BF16)


def _merge_ctx_heads(cache):
    n, h = cache.shape[1], cache.shape[2]
    return cache.transpose(1, 0, 3, 2, 4).reshape(n, DEC_BATCH, PAST_LEN, h * HEAD_DIM).astype(BF16)


def _pair_gqa_heads(w, axis):
    shape = w.shape
    w = w.reshape(shape[:axis] + (HKV_D, GQA_GROUP, HEAD_DIM) + shape[axis + 1:])
    return jnp.swapaxes(w, axis, axis + 1).reshape(shape)


def kernel(x_prompt, x_sample, cache_a_k, cache_a_v, cache_d_k, cache_d_v, c, c_ctx, mod_w, mod_b, norm_w,
           ffn_w1, ffn_w2, ev_w_in, ev_rpb, ev_conv_w, ev_conv_b, ev_w_out, od_w_in, od_pool_w,
           od_pool_scale, od_q_norm, od_k_norm, od_w_out):
    xs = (x_prompt.reshape(N_PROMPT, D_MODEL), x_sample.reshape(N_SAMPLE, D_MODEL))
    cond = jnp.concatenate([c_ctx[None, :], c, jnp.zeros((COND_PAD - N_COND, D_MODEL), F32)], axis=0)
    mods = _modulation(cond, mod_w, mod_b).reshape(DEPTH, COND_PAD, N_MOD, D_MODEL)
    w1, w2 = ffn_w1.astype(BF16), ffn_w2.astype(BF16)
    ev_w_in_b, ev_w_out_b = ev_w_in.astype(BF16), ev_w_out.astype(BF16)
    od_w_in_b = jnp.concatenate([od_w_in[..., :C_POOL],
                                 _pair_gqa_heads(od_w_in[..., C_POOL:C_POOL + HD_Q], 2),
                                 od_w_in[..., C_POOL + HD_Q:]], axis=-1).astype(BF16)
    od_w_out_b = jnp.concatenate([od_w_out[:, :C_POOL], _pair_gqa_heads(od_w_out[:, C_POOL:], 1)],
                                 axis=1).astype(BF16)
    ctx_a_k, ctx_a_v = _merge_ctx_heads(cache_a_k), _merge_ctx_heads(cache_a_v)
    ctx_d_k, ctx_d_v = _merge_ctx_heads(cache_d_k), _merge_ctx_heads(cache_d_v)
    rpb = jnp.pad(ev_rpb, ((0, 0), (0, 0), (0, 0), (0, LANES - ev_rpb.shape[-1])))
    conv_b = ev_conv_b[:, None, :]
    pool_w, pool_scale = od_pool_w.astype(BF16), od_pool_scale[:, None, :]
    cos_t, sin_t = _rope_tables()
    ones_bd = _head_ones()
    states = []
    for l in range(DEPTH):
        i = l // 2
        x = _ffn(xs, mods, norm_w, w1, w2, l, 0)
        if l % 2 == 0:
            q, k, v, bg, z, s_k, s_v = _even_in(x, mods, norm_w, ev_w_in_b, l, i)
            fa_p, fb_p = _even_prompt(q, k, v, bg, z, ev_conv_w, conv_b, i)
            fa_s, fb_s = _even_sample(q, k, v, bg, z, ctx_a_k[i], ctx_a_v[i], rpb[i], ev_conv_w, conv_b, i)
            w_out = ev_w_out_b
        else:
            gain = jnp.concatenate([jnp.tile(od_q_norm[i], HQ_D), jnp.tile(od_k_norm[i], HKV_D)])[None, :]
            uc, q, k, v, s_k, s_v = _odd_in(x, mods, norm_w, od_w_in_b, gain, cos_t, sin_t, ones_bd, l, i)
            fa_p, fb_p = _odd_prompt(uc, q, k, v, pool_w, pool_scale, i)
            fa_s, fb_s = _odd_sample(uc, q, k, v, ctx_d_k[i], ctx_d_v[i], pool_w, pool_scale, i)
            w_out = od_w_out_b
        states.append((s_k, s_v))
        last = l == DEPTH - 1
        xs = _ffn((x,), mods, norm_w, w1, w2, l, 2, feats=(fa_p, fb_p, fa_s, fb_s), w_out=w_out, idx=i,
                  split_out=last)
        xs = tuple(xs) if last else (xs,)
    y_prompt = xs[0].reshape(BATCH, SEQ, D_MODEL)
    y_sample = xs[1].reshape(DEC_BATCH, DEC_SEQ, D_MODEL)
    return (y_prompt, y_sample, states[0][0], states[0][1], states[1][0], states[1][1])
```

```python
import functools

import jax
import jax.numpy as jnp
import numpy as np
from jax import lax
from jax.experimental import pallas as pl
from jax.experimental.pallas import tpu as pltpu

D_MODEL = 1024
BATCH = 32
SEQ = 256
DEPTH = 2
DEC_BATCH = 2
DEC_SEQ = 2048
PAST_LEN = 256
GRID_W = 64
HEAD_DIM = 64
N_MOD = 9
D_FF = 2816
FFN_RES = 0.5
H_A = 8
NA_ROWS = 8
NA_COLS = 16
C_B = 512
CONV_W = 3
C_POOL = 512
POOL_WINDOWS = (2, 4, 8, 16)
N_POOL = 4
POOL_C = C_POOL // N_POOL
HQ_D = 8
HKV_D = 2
GQA_GROUP = HQ_D // HKV_D
ROPE_BASE = 10000.0
EVEN_IN = 3 * H_A * HEAD_DIM + 3 * C_B
ODD_IN = C_POOL + (HQ_D + 2 * HKV_D) * HEAD_DIM
RMS_EPS = 1e-6
NEG_INF = -1e30
ATT_SCALE = HEAD_DIM ** -0.5

LANES = 128
GRID_ROWS = DEC_SEQ // GRID_W
N_PROMPT = BATCH * SEQ
N_SAMPLE = DEC_BATCH * DEC_SEQ
N_TOK = N_PROMPT + N_SAMPLE
N_COND = 1 + DEC_BATCH
COND_PAD = 8
HD_A = H_A * HEAD_DIM
HD_Q = HQ_D * HEAD_DIM
HD_KV = HKV_D * HEAD_DIM
QK_W = HD_Q + HD_KV
POOL_PAD = 8

TM = 512
N_TILES = N_TOK // TM
PROMPT_TILES = N_PROMPT // TM
TILES_PER_SAMPLE = DEC_SEQ // TM
SEQ_PER_TILE = TM // SEQ
CAST_STEPS = PROMPT_TILES
TM_FFN = 1024
FFN_HALF = TM_FFN // 2
MXU_DIM = 256
FF_CHUNKS = tuple((lo, min(lo + 3 * MXU_DIM, D_FF)) for lo in range(0, D_FF, 3 * MXU_DIM))
MOD_TN = 1536
Q_BLK = 256
ROW_UNROLL = 4
VMEM_LIMIT = 60 * 1024 * 1024

F32 = jnp.float32
BF16 = jnp.bfloat16


def _params(n_grid):
    return pltpu.CompilerParams(dimension_semantics=("arbitrary",) * n_grid, vmem_limit_bytes=VMEM_LIMIT)


def _cond_of_tile(i, tm=TM):
    n_prompt = N_PROMPT // tm
    return jnp.where(i < n_prompt, 0, 1 + (i - n_prompt) // (DEC_SEQ // tm))


def _prompt_tile(i, tm=TM):
    return jnp.minimum(i, N_PROMPT // tm - 1)


def _sample_tile(i, tm=TM):
    return jnp.maximum(i - N_PROMPT // tm, 0)


def _rms(x, g):
    return x * lax.rsqrt(jnp.mean(x * x, axis=-1, keepdims=True) + RMS_EPS) * g


def _sigmoid(x):
    return 1.0 / (1.0 + jnp.exp(-x))


def _dot(a, b):
    return jnp.dot(a, b, preferred_element_type=F32)


def _dot_t(a, b):
    return lax.dot_general(a, b, (((1,), (1,)), ((), ())), preferred_element_type=F32)


def _mod_kernel(c_ref, w_ref, b_ref, o_ref):
    c = c_ref[...]
    sc = (c * _sigmoid(c)).astype(BF16)
    o_ref[...] = _dot(sc, w_ref[...].astype(BF16)) + b_ref[...]


def _modulation(cond, mod_w, mod_b):
    n_col = N_MOD * D_MODEL
    return pl.pallas_call(
        _mod_kernel,
        grid=(DEPTH, n_col // MOD_TN),
        in_specs=[
            pl.BlockSpec((COND_PAD, D_MODEL), lambda l, j: (0, 0)),
            pl.BlockSpec((None, D_MODEL, MOD_TN), lambda l, j: (l, 0, j)),
            pl.BlockSpec((None, 1, MOD_TN), lambda l, j: (l, 0, j)),
        ],
        out_specs=pl.BlockSpec((None, COND_PAD, MOD_TN), lambda l, j: (l, 0, j)),
        out_shape=jax.ShapeDtypeStruct((DEPTH, COND_PAD, n_col), F32),
        compiler_params=_params(2),
        name="modulation",
    )(cond, mod_w, mod_b.reshape(DEPTH, 1, n_col))


def _mod_spec(layer, tm=TM):
    return pl.BlockSpec((None, None, N_MOD, D_MODEL), lambda i: (layer, _cond_of_tile(i, tm), 0, 0))


def _gain_spec(layer):
    return pl.BlockSpec((None, 6, D_MODEL), lambda i: (layer, 0, 0))


def _cast_step(i):
    return jnp.minimum(i, CAST_STEPS - 1)


def _cast_specs(layer, which):
    r1, r2 = D_MODEL // CAST_STEPS, D_FF // CAST_STEPS
    in_specs = [pl.BlockSpec((None, None, r1, 2 * D_FF), lambda i: (layer, which, _cast_step(i), 0)),
                pl.BlockSpec((None, None, r2, D_MODEL), lambda i: (layer, which, _cast_step(i), 0))]
    out_specs = [pl.BlockSpec((r1, 2 * D_FF), lambda i: (_cast_step(i), 0)),
                 pl.BlockSpec((r2, D_MODEL), lambda i: (_cast_step(i), 0))]
    out_shape = [jax.ShapeDtypeStruct((D_MODEL, 2 * D_FF), BF16), jax.ShapeDtypeStruct((D_FF, D_MODEL), BF16)]
    return in_specs, out_specs, out_shape


def _cast_weights(w1f_ref, w2f_ref, w1b_ref, w2b_ref):
    @pl.when(pl.program_id(0) < CAST_STEPS)
    def _():
        w1b_ref[...] = w1f_ref[...].astype(BF16)
        w2b_ref[...] = w2f_ref[...].astype(BF16)


def _ffn_kernel(*refs, sub, split_in, mix, split_out):
    refs = list(refs)
    x_refs = [refs.pop(0) for _ in range(2 if split_in else 1)]
    feat_refs = [refs.pop(0) for _ in range(4 if mix else 0)]
    m_ref, g_ref = refs.pop(0), refs.pop(0)
    wo_ref = refs.pop(0) if mix else None
    w1_ref, w2_ref = refs.pop(0), refs.pop(0)
    out_refs = refs
    is_prompt = pl.program_id(0) < N_PROMPT // TM_FFN
    shift = m_ref[3 * sub:3 * sub + 1, :]
    scale = m_ref[3 * sub + 1:3 * sub + 2, :]
    gate = m_ref[3 * sub + 2:3 * sub + 3, :]
    for half in range(TM_FFN // FFN_HALF):
        rows = slice(half * FFN_HALF, (half + 1) * FFN_HALF)
        if split_in:
            x = jnp.where(is_prompt, x_refs[0][rows, :], x_refs[1][rows, :])
        else:
            x = x_refs[0][rows, :]
        if mix:
            fa = jnp.where(is_prompt, feat_refs[0][rows, :], feat_refs[2][rows, :])
            fb = jnp.where(is_prompt, feat_refs[1][rows, :], feat_refs[3][rows, :])
            n_a = fa.shape[1]
            mixed = _dot(fa, wo_ref[0:n_a, :]) + _dot(fb, wo_ref[n_a:, :])
            x = x + m_ref[5:6, :] * _rms(mixed, g_ref[3:4, :])
        h =(_rms(x, g_ref[2 * sub:2 * sub + 1, :]) * (1.0 + scale) + shift).astype(BF16)
        acc = jnp.zeros((FFN_HALF, D_MODEL), F32)
        for lo, hi in FF_CHUNKS:
            a = _dot(h, w1_ref[:, lo:hi])
            u = _dot(h, w1_ref[:, D_FF + lo:D_FF + hi])
            act = (a * _sigmoid(a) * u).astype(BF16)
            acc = acc + _dot(act, w2_ref[lo:hi, :])
        y = x + FFN_RES * gate * _rms(acc, g_ref[2 * sub + 1:2 * sub + 2, :])
        if split_out:
            @pl.when(is_prompt)
            def _():
                out_refs[0][rows, :] = y

            @pl.when(jnp.logical_not(is_prompt))
            def _():
                out_refs[1][rows, :] = y
        else:
            out_refs[0][rows, :] = y


def _ffn(xs, mods, gains, w1, w2, layer, sub, feats=None, w_out=None, idx=0, split_out=False):
    split_in = len(xs) == 2
    mix = feats is not None
    resident = pl.Buffered(1)
    half_w = D_MODEL // 2
    feat_specs, mix_w_spec = [], []
    if mix:
        feat_p = pl.BlockSpec((TM_FFN, half_w), lambda i: (_prompt_tile(i, TM_FFN), 0))
        feat_s = pl.BlockSpec((TM_FFN, half_w), lambda i: (_sample_tile(i, TM_FFN), 0))
        feat_specs = [feat_p, feat_p, feat_s, feat_s]
        mix_w_spec = [pl.BlockSpec((None, D_MODEL, D_MODEL), lambda i: (idx, 0, 0), pipeline_mode=resident)]
    tok = pl.BlockSpec((TM_FFN, D_MODEL), lambda i: (i, 0))
    prompt_tok = pl.BlockSpec((TM_FFN, D_MODEL), lambda i: (_prompt_tile(i, TM_FFN), 0))
    sample_tok = pl.BlockSpec((TM_FFN, D_MODEL), lambda i: (_sample_tile(i, TM_FFN), 0))
    if split_out:
        out_specs = [prompt_tok, sample_tok]
        out_shape = [jax.ShapeDtypeStruct((N_PROMPT, D_MODEL), F32),
                     jax.ShapeDtypeStruct((N_SAMPLE, D_MODEL), F32)]
    else:
        out_specs = tok
        out_shape = jax.ShapeDtypeStruct((N_TOK, D_MODEL), F32)
    return pl.pallas_call(
        functools.partial(_ffn_kernel, sub=sub, split_in=split_in, mix=mix, split_out=split_out),
        grid=(N_TOK // TM_FFN,),
        in_specs=([prompt_tok, sample_tok] if split_in else [tok]) + feat_specs + [
            _mod_spec(layer, TM_FFN),
            _gain_spec(layer),
        ] + mix_w_spec + [
            pl.BlockSpec((D_MODEL, 2 * D_FF), lambda i: (0, 0), pipeline_mode=resident),
            pl.BlockSpec((D_FF, D_MODEL), lambda i: (0, 0), pipeline_mode=resident),
        ],
        out_specs=out_specs,
        out_shape=out_shape,
        compiler_params=_params(1),
        name=f"ffn{sub}",
    )(*xs, *(feats or ()), mods, gains, *((w_out,) if mix else ()), w1, w2)


def _mixer_norm(x_ref, m_ref, g_ref):
    return (_rms(x_ref[...], g_ref[2:3, :]) * (1.0 + m_ref[4:5, :]) + m_ref[3:4, :]).astype(BF16)


def _store_heads(state_ref, x, n_heads):
    for b in range(SEQ_PER_TILE):
        for h in range(n_heads):
            state_ref[b, 0, h] = x[b * SEQ:(b + 1) * SEQ, h * HEAD_DIM:(h + 1) * HEAD_DIM]


def _even_in_kernel(x_ref, m_ref, g_ref, w_ref, w1f_ref, w2f_ref,
                    q_ref, k_ref, v_ref, bg_ref, z_ref, sk_ref, sv_ref, w1b_ref, w2b_ref):
    _cast_weights(w1f_ref, w2f_ref, w1b_ref, w2b_ref)
    u = _dot(_mixer_norm(x_ref, m_ref, g_ref), w_ref[...])
    k = u[:, HD_A:2 * HD_A]
    v = u[:, 2 * HD_A:3 * HD_A]
    q_ref[...] = (u[:, 0:HD_A] * ATT_SCALE).astype(BF16)
    k_ref[...] = k.astype(BF16)
    v_ref[...] = v.astype(BF16)
    bg_ref[...] = u[:, 3 * HD_A:3 * HD_A + C_B]
    z_ref[...] = u[:, 3 * HD_A + C_B:3 * HD_A + 2 * C_B] * u[:, 3 * HD_A + 2 * C_B:3 * HD_A + 3 * C_B]

    @pl.when(pl.program_id(0) < PROMPT_TILES)
    def _():
        _store_heads(sk_ref, k, H_A)
        _store_heads(sv_ref, v, H_A)


def _state_spec(n_heads):
    return pl.BlockSpec((SEQ_PER_TILE, 1, n_heads, SEQ, HEAD_DIM), lambda i: (_prompt_tile(i), 0, 0, 0, 0))


def _even_in(x, mods, gains, w_in, ffn_w1, ffn_w2, layer, idx):
    tok = lambda w: pl.BlockSpec((TM, w), lambda i: (i, 0))
    state = jax.ShapeDtypeStruct((BATCH, 1, H_A, SEQ, HEAD_DIM), F32)
    cast_in, cast_out, cast_shape = _cast_specs(layer, 1)
    return pl.pallas_call(
        _even_in_kernel,
        grid=(N_TILES,),
        in_specs=[
            tok(D_MODEL), _mod_spec(layer), _gain_spec(layer),
            pl.BlockSpec((None, D_MODEL, EVEN_IN), lambda i: (idx, 0, 0)),
        ] + cast_in,
        out_specs=[tok(HD_A), tok(HD_A), tok(HD_A), tok(C_B), tok(C_B), _state_spec(H_A), _state_spec(H_A)]
        + cast_out,
        out_shape=[
            jax.ShapeDtypeStruct((N_TOK, HD_A), BF16),
            jax.ShapeDtypeStruct((N_TOK, HD_A), BF16),
            jax.ShapeDtypeStruct((N_TOK, HD_A), BF16),
            jax.ShapeDtypeStruct((N_TOK, C_B), F32),
            jax.ShapeDtypeStruct((N_TOK, C_B), F32),
            state, state,
        ] + cast_shape,
        compiler_params=_params(1),
        name="even_in",
    )(x, mods, gains, w_in, ffn_w1, ffn_w2)


def _swap_pairs(x):
    n = x.shape[-1]
    lane = lax.broadcasted_iota(jnp.int32, x.shape, x.ndim - 1)
    return jnp.where(lane % 2 == 0, pltpu.roll(x, n - 1, x.ndim - 1), pltpu.roll(x, 1, x.ndim - 1))


def _odd_in_kernel(x_ref, m_ref, g_ref, w_ref, ng_ref, cos_ref, sin_ref, ones_ref, w1f_ref, w2f_ref,
                   uc_ref, q_ref, k_ref, v_ref, sk_ref, sv_ref, w1b_ref, w2b_ref):
    _cast_weights(w1f_ref, w2f_ref, w1b_ref, w2b_ref)
    u = _dot(_mixer_norm(x_ref, m_ref, g_ref), w_ref[...])
    uc_ref[...] = u[:, 0:C_POOL]
    qk = u[:, C_POOL:C_POOL + QK_W]
    sq = qk * qk
    hi = sq.astype(BF16)
    lo = (sq - hi.astype(F32)).astype(BF16)
    ones = ones_ref[...]
    sums = []
    for c0 in range(0, QK_W, MXU_DIM):
        c1 = min(c0 + MXU_DIM, QK_W)
        sums.append(_dot(hi[:, c0:c1], ones[0:c1 - c0, 0:c1 - c0]) + _dot(lo[:, c0:c1], ones[0:c1 - c0, 0:c1 - c0]))
    ms = jnp.concatenate(sums, axis=-1) * (1.0 / HEAD_DIM)
    n = qk * lax.rsqrt(ms + RMS_EPS) * ng_ref[...]
    r = n * cos_ref[...] + _swap_pairs(n) * sin_ref[...]
    k = r[:, HD_Q:QK_W]
    v = u[:, C_POOL + QK_W:ODD_IN]
    q_ref[...] = (r[:, 0:HD_Q] * ATT_SCALE).astype(BF16)
    k_ref[...] = k.astype(BF16)
    v_ref[...] = v.astype(BF16)

    @pl.when(pl.program_id(0) < PROMPT_TILES)
    def _():
        _store_heads(sk_ref, k, HKV_D)
        _store_heads(sv_ref, v, HKV_D)


def _rope_tile_index(i):
    return jnp.where(i < PROMPT_TILES, 0, 1 + (i - PROMPT_TILES) % TILES_PER_SAMPLE)


def _odd_in(x, mods, gains, w_in, qk_gain, cos_t, sin_t, ones_bd, ffn_w1, ffn_w2, layer, idx):
    tok = lambda w: pl.BlockSpec((TM, w), lambda i: (i, 0))
    state = jax.ShapeDtypeStruct((BATCH, 1, HKV_D, SEQ, HEAD_DIM), F32)
    cast_in, cast_out, cast_shape = _cast_specs(layer, 1)
    return pl.pallas_call(
        _odd_in_kernel,
        grid=(N_TILES,),
        in_specs=[
            tok(D_MODEL), _mod_spec(layer), _gain_spec(layer),
            pl.BlockSpec((None, D_MODEL, ODD_IN), lambda i: (idx, 0, 0)),
            pl.BlockSpec((1, QK_W), lambda i: (0, 0)),
            pl.BlockSpec((TM, QK_W), lambda i: (_rope_tile_index(i), 0)),
            pl.BlockSpec((TM, QK_W), lambda i: (_rope_tile_index(i), 0)),
            pl.BlockSpec((MXU_DIM, MXU_DIM), lambda i: (0, 0)),
        ] + cast_in,
        out_specs=[tok(C_POOL), tok(HD_Q), tok(HD_KV), tok(HD_KV), _state_spec(HKV_D), _state_spec(HKV_D)]
        + cast_out,
        out_shape=[
            jax.ShapeDtypeStruct((N_TOK, C_POOL), F32),
            jax.ShapeDtypeStruct((N_TOK, HD_Q), BF16),
            jax.ShapeDtypeStruct((N_TOK, HD_KV), BF16),
            jax.ShapeDtypeStruct((N_TOK, HD_KV), BF16),
            state, state,
        ] + cast_shape,
        compiler_params=_params(1),
        name="odd_in",
    )(x, mods, gains, w_in, qk_gain, cos_t, sin_t, ones_bd, ffn_w1, ffn_w2)


def _low_half(rows):
    return lax.broadcasted_iota(jnp.int32, (rows, LANES), 1) < HEAD_DIM


def _split_pair(qp):
    low = _low_half(qp.shape[0])
    zero = jnp.zeros_like(qp)
    return jnp.concatenate([jnp.where(low, qp, zero), jnp.where(low, zero, qp)], axis=0)


def _pair_heads(ref):
    return jnp.concatenate([ref[0], ref[1]], axis=-1).astype(BF16)


def _with_ones(v):
    return jnp.concatenate([v, jnp.ones_like(v)], axis=1)


def _softmax_pv(scores, values):
    m = scores[0].max(axis=-1, keepdims=True)
    for s in scores[1:]:
        m = jnp.maximum(m, s.max(axis=-1, keepdims=True))
    acc = None
    for s, v in zip(scores, values):
        o = _dot(jnp.exp(s - m).astype(BF16), v)
        acc = o if acc is None else acc + o
    return acc[:, 0:LANES] / acc[:, LANES:2 * LANES]


def _merge_pair(o):
    m = o.shape[0] // 2
    return jnp.where(_low_half(m), o[0:m], o[m:2 * m])


def _short_conv(z, cw_ref, cb_ref, seq_len):
    rows = z.shape[0]
    pos = lax.broadcasted_iota(jnp.int32, z.shape, 0) & (seq_len - 1)
    z_prev = jnp.where(pos == 0, 0.0, pltpu.roll(z, 1, 0))
    z_next = jnp.where(pos == seq_len - 1, 0.0, pltpu.roll(z, rows - 1, 0))
    y = z_prev * cw_ref[0:1, :]
    y = y + z * cw_ref[1:2, :]
    y = y + z_next * cw_ref[2:3, :]
    return y + cb_ref[...]


def _pool_mix(uc, pw_ref, ps_ref, seq_len):
    pos = lax.broadcasted_iota(jnp.int32, (seq_len, POOL_C), 0)
    pad = jnp.zeros((POOL_PAD, POOL_C), F32)
    n_ext = seq_len + 2 * POOL_PAD
    outs = []
    for gi, win in enumerate(POOL_WINDOWS):
        ug = uc[:, gi * POOL_C:(gi + 1) * POOL_C]
        run = jnp.concatenate([pad, ug, pad], axis=0)
        span = 1
        while span < win:
            run = run + pltpu.roll(run, span, 0)
            span *= 2
        back = win // 2 - 1
        if back:
            run = pltpu.roll(run, n_ext - back, 0)
        wsum = run[POOL_PAD:POOL_PAD + seq_len, :]
        cnt = jnp.minimum(pos + (win - win // 2), seq_len) - jnp.maximum(pos - win // 2, 0)
        pooled = (wsum / cnt.astype(F32) - ug).astype(BF16)
        outs.append(_dot(pooled, pw_ref[gi]))
    return jnp.concatenate(outs, axis=-1) * ps_ref[...]


def _even_prompt_kernel(*refs, cast):
    q_ref, k_ref, v_ref, bg_ref, z_ref, cw_ref, cb_ref = refs[0:7]
    fa_ref, fb_ref = refs[7 + 2 * cast:9 + 2 * cast]
    if cast:
        _cast_weights(refs[7], refs[8], refs[11], refs[12])
    for b in range(SEQ_PER_TILE):
        rows = slice(b * SEQ, (b + 1) * SEQ)
        outs = []
        for p in range(H_A // 2):
            sl = slice(p * LANES, (p + 1) * LANES)
            s = _dot_t(_split_pair(q_ref[rows, sl]), k_ref[rows, sl])
            outs.append(_merge_pair(_softmax_pv([s], [_with_ones(v_ref[rows, sl])])))
        fa_ref[rows, :] = jnp.concatenate(outs, axis=-1).astype(BF16)
    fb_ref[...] = (bg_ref[...] * _short_conv(z_ref[...], cw_ref, cb_ref, SEQ)).astype(BF16)


def _even_prompt(q, k, v, bg, z, conv_w, conv_b, idx, cast_of=None):
    tile = lambda w: pl.BlockSpec((TM, w), lambda t: (t, 0))
    cast_in, cast_out, cast_shape, cast_args = [], [], [], ()
    if cast_of is not None:
        cast_in, cast_out, cast_shape = _cast_specs(cast_of[2], 0)
        cast_args = cast_of[0:2]
    return pl.pallas_call(
        functools.partial(_even_prompt_kernel, cast=cast_of is not None),
        grid=(PROMPT_TILES,),
        in_specs=[tile(HD_A), tile(HD_A), tile(HD_A), tile(C_B), tile(C_B),
                  pl.BlockSpec((None, CONV_W, C_B), lambda t: (idx, 0, 0)),
                  pl.BlockSpec((None, 1, C_B), lambda t: (idx, 0, 0))] + cast_in,
        out_specs=[tile(HD_A), tile(C_B)] + cast_out,
        out_shape=[
            jax.ShapeDtypeStruct((N_PROMPT, HD_A), BF16),
            jax.ShapeDtypeStruct((N_PROMPT, C_B), BF16),
        ] + cast_shape,
        compiler_params=_params(1),
        name="even_prompt",
    )(q, k, v, bg, z, conv_w, conv_b, *cast_args)


def _na_bias_tiles(rpb_h):
    c = lax.broadcasted_iota(jnp.int32, (GRID_W, GRID_W), 0)
    kc = lax.broadcasted_iota(jnp.int32, (GRID_W, GRID_W), 1)
    cs = jnp.clip(c - NA_COLS // 2, 0, GRID_W - NA_COLS)
    valid = (kc >= cs) & (kc < cs + NA_COLS)
    tiles = []
    for ro in range(2 * NA_ROWS - 1):
        g = jnp.broadcast_to(rpb_h[ro:ro + 1, :], (GRID_W, LANES))
        skew = pltpu.roll(g, LANES - (NA_COLS - 1), 1, stride=1, stride_axis=0)
        tiles.append(jnp.where(valid, skew[:, 0:GRID_W], NEG_INF))
    return tiles


def _even_sample_kernel(q_ref, k_ref, v_ref, bg_ref, z_ref, ck_ref, cv_ref, rpb_ref, cw_ref, cb_ref,
                        fa_ref, fb_ref, bias_ref):
    win = NA_ROWS * GRID_W
    for h in range(2):
        tiles = _na_bias_tiles(rpb_ref[h])
        for var in range(NA_ROWS):
            bias_ref[var, h * GRID_W:(h + 1) * GRID_W, :] = jnp.concatenate(
                [tiles[i - var + NA_ROWS - 1] for i in range(NA_ROWS)], axis=1)
    ck = _pair_heads(ck_ref)
    cv_ext = _with_ones(_pair_heads(cv_ref))

    def rows(it, carry):
        for j in range(ROW_UNROLL):
            r = it * ROW_UNROLL + j
            rs = jnp.clip(r - NA_ROWS // 2, 0, GRID_ROWS - NA_ROWS)
            q0 = pl.multiple_of(r * GRID_W, GRID_W)
            k0 = pl.multiple_of(rs * GRID_W, GRID_W)
            q2 = _split_pair(q_ref[pl.ds(q0, GRID_W), :])
            s_loc = _dot_t(q2, k_ref[pl.ds(k0, win), :]) + bias_ref[r - rs]
            s_ctx = _dot_t(q2, ck)
            o = _softmax_pv([s_loc, s_ctx], [_with_ones(v_ref[pl.ds(k0, win), :]), cv_ext])
            fa_ref[pl.ds(q0, GRID_W), :] = _merge_pair(o).astype(BF16)
        return carry

    lax.fori_loop(0, GRID_ROWS // ROW_UNROLL, rows, 0)
    fb_ref[...] = (bg_ref[...] * _short_conv(z_ref[...], cw_ref, cb_ref, DEC_SEQ)).astype(BF16)


def _even_sample(q, k, v, bg, z, ctx_k, ctx_v, rpb, conv_w, conv_b, idx):
    first = N_PROMPT // DEC_SEQ
    seq_in = pl.BlockSpec((DEC_SEQ, LANES), lambda b, p: (first + b, p))
    seq_out = pl.BlockSpec((DEC_SEQ, LANES), lambda b, p: (b, p))
    ctx = pl.BlockSpec((None, None, 2, PAST_LEN, HEAD_DIM), lambda b, p: (b, idx, p, 0, 0))
    return pl.pallas_call(
        _even_sample_kernel,
        grid=(DEC_BATCH, H_A // 2),
        in_specs=[seq_in, seq_in, seq_in, seq_in, seq_in, ctx, ctx,
                  pl.BlockSpec((None, 2, 2 * NA_ROWS - 1, LANES), lambda b, p: (idx, p, 0, 0)),
                  pl.BlockSpec((None, CONV_W, LANES), lambda b, p: (idx, 0, p)),
                  pl.BlockSpec((None, 1, LANES), lambda b, p: (idx, 0, p))],
        out_specs=[seq_out, seq_out],
        out_shape=[
            jax.ShapeDtypeStruct((N_SAMPLE, HD_A), BF16),
            jax.ShapeDtypeStruct((N_SAMPLE, C_B), BF16),
        ],
        scratch_shapes=[pltpu.VMEM((NA_ROWS, 2 * GRID_W, NA_ROWS * GRID_W), F32)],
        compiler_params=_params(2),
        name="even_sample",
    )(q, k, v, bg, z, ctx_k, ctx_v, rpb, conv_w, conv_b)


def _split_groups(q):
    low = _low_half(q.shape[0])
    zero = jnp.zeros((q.shape[0], LANES), BF16)
    pairs = [q[:, j * LANES:(j + 1) * LANES] for j in range(GQA_GROUP)]
    return jnp.concatenate([jnp.where(low, p, zero) for p in pairs] + [jnp.where(low, zero, p) for p in pairs],
                           axis=0)


def _merge_groups(o, m):
    low = _low_half(m)
    half = GQA_GROUP * m
    return jnp.concatenate([jnp.where(low, o[j * m:(j + 1) * m], o[half + j * m:half + (j + 1) * m])
                            for j in range(GQA_GROUP)], axis=-1)


def _odd_prompt_kernel(uc_ref, q_ref, k_ref, v_ref, pw_ref, ps_ref, fc_ref, fd_ref):
    for b in range(SEQ_PER_TILE):
        rows = slice(b * SEQ, (b + 1) * SEQ)
        s = _dot_t(_split_groups(q_ref[rows, :]), k_ref[rows, :])
        o = _softmax_pv([s], [_with_ones(v_ref[rows, :])])
        fd_ref[rows, :] = _merge_groups(o, SEQ).astype(BF16)
        fc_ref[rows, :] = _pool_mix(uc_ref[rows, :], pw_ref, ps_ref, SEQ).astype(BF16)


def _odd_prompt(uc, q, k, v, pool_w, pool_scale, idx):
    seq = lambda w: pl.BlockSpec((TM, w), lambda t: (t, 0))
    return pl.pallas_call(
        _odd_prompt_kernel,
        grid=(PROMPT_TILES,),
        in_specs=[seq(C_POOL), seq(HD_Q), seq(HD_KV), seq(HD_KV),
                  pl.BlockSpec((None, N_POOL, POOL_C, POOL_C), lambda t: (idx, 0, 0, 0)),
                  pl.BlockSpec((None, 1, C_POOL), lambda t: (idx, 0, 0))],
        out_specs=[seq(C_POOL), seq(HD_Q)],
        out_shape=[
            jax.ShapeDtypeStruct((N_PROMPT, C_POOL), BF16),
            jax.ShapeDtypeStruct((N_PROMPT, HD_Q), BF16),
        ],
        compiler_params=_params(1),
        name="odd_prompt",
    )(uc, q, k, v, pool_w, pool_scale)


def _odd_sample_kernel(uc_ref, q_ref, k_ref, v_ref, ck_ref, cv_ref, pw_ref, ps_ref, fc_ref, fd_ref,
                       vext_ref, ckpair_ref, cvext_ref):
    @pl.when(pl.program_id(1) == 0)
    def _():
        fc_ref[...] = _pool_mix(uc_ref[...], pw_ref, ps_ref, DEC_SEQ).astype(BF16)
        vext_ref[...] = _with_ones(v_ref[...])
        ckpair_ref[...] = _pair_heads(ck_ref)
        cvext_ref[...] = _with_ones(_pair_heads(cv_ref))

    k = k_ref[...]
    ck = ckpair_ref[...]
    low = _low_half(Q_BLK)
    outs = []
    for j in range(GQA_GROUP):
        q2 = _split_pair(q_ref[:, j * LANES:(j + 1) * LANES])
        o = _softmax_pv([_dot_t(q2, k), _dot_t(q2, ck)], [vext_ref[...], cvext_ref[...]])
        outs.append(jnp.where(low, o[0:Q_BLK], o[Q_BLK:2 * Q_BLK]))
    fd_ref[...] = jnp.concatenate(outs, axis=-1).astype(BF16)


def _odd_sample(uc, q, k, v, ctx_k, ctx_v, pool_w, pool_scale, idx):
    first = N_PROMPT // DEC_SEQ
    n_qb = DEC_SEQ // Q_BLK
    first_q = N_PROMPT // Q_BLK
    whole_in = lambda w: pl.BlockSpec((DEC_SEQ, w), lambda b, i: (first + b, 0))
    ctx = pl.BlockSpec((None, None, HKV_D, PAST_LEN, HEAD_DIM), lambda b, i: (b, idx, 0, 0, 0))
    return pl.pallas_call(
        _odd_sample_kernel,
        grid=(DEC_BATCH, n_qb),
        in_specs=[whole_in(C_POOL),
                  pl.BlockSpec((Q_BLK, HD_Q), lambda b, i: (first_q + b * n_qb + i, 0)),
                  whole_in(HD_KV), whole_in(HD_KV), ctx, ctx,
                  pl.BlockSpec((None, N_POOL, POOL_C, POOL_C), lambda b, i: (idx, 0, 0, 0)),
                  pl.BlockSpec((None, 1, C_POOL), lambda b, i: (idx, 0, 0))],
        out_specs=[pl.BlockSpec((DEC_SEQ, C_POOL), lambda b, i: (b, 0)),
                   pl.BlockSpec((Q_BLK, HD_Q), lambda b, i: (b * n_qb + i, 0))],
        out_shape=[
            jax.ShapeDtypeStruct((N_SAMPLE, C_POOL), BF16),
            jax.ShapeDtypeStruct((N_SAMPLE, HD_Q), BF16),
        ],
        scratch_shapes=[pltpu.VMEM((DEC_SEQ, 2 * LANES), BF16), pltpu.VMEM((PAST_LEN, LANES), BF16),
                        pltpu.VMEM((PAST_LEN, 2 * LANES), BF16)],
        compiler_params=_params(2),
        name="odd_sample",
    )(uc, q, k, v, ctx_k, ctx_v, pool_w, pool_scale)


def _rope_tables():
    t = np.arange(DEC_SEQ)
    n_freq = HEAD_DIM // 4
    inv = ROPE_BASE ** (-np.arange(n_freq, dtype=np.float64) / n_freq)
    ang = np.concatenate([(t // GRID_W)[:, None] * inv, (t % GRID_W)[:, None] * inv], axis=-1)
    cos = np.repeat(np.cos(ang), 2, axis=-1)
    sin = np.repeat(np.sin(ang), 2, axis=-1) * np.tile([-1.0, 1.0], HEAD_DIM // 2)
    n_heads = HQ_D + HKV_D
    cos = np.concatenate([np.ones((TM, QK_W)), np.tile(cos, (1, n_heads))], axis=0)
    sin = np.concatenate([np.zeros((TM, QK_W)), np.tile(sin, (1, n_heads))], axis=0)
    return jnp.asarray(cos, F32), jnp.asarray(sin, F32)


def _head_ones():
    head = np.arange(MXU_DIM) // HEAD_DIM
    return jnp.asarray(head[:, None] == head[None, :], BF16)


def _pair_gqa_heads(w, axis):
    shape = w.shape
    w = w.reshape(shape[:axis] + (HKV_D, GQA_GROUP, HEAD_DIM) + shape[axis + 1:])
    return jnp.swapaxes(w, axis, axis + 1).reshape(shape)


def kernel(x_prompt, x_sample, cache_a_k, cache_a_v, cache_d_k, cache_d_v, c, c_ctx, mod_w, mod_b, norm_w,
           ffn_w1, ffn_w2, ev_w_in, ev_rpb, ev_conv_w, ev_conv_b, ev_w_out, od_w_in, od_pool_w,
           od_pool_scale, od_q_norm, od_k_norm, od_w_out):
    xs = (x_prompt.reshape(N_PROMPT, D_MODEL), x_sample.reshape(N_SAMPLE, D_MODEL))
    cond = jnp.concatenate([c_ctx[None, :], c, jnp.zeros((COND_PAD - N_COND, D_MODEL), F32)], axis=0)
    mods = _modulation(cond, mod_w, mod_b).reshape(DEPTH, COND_PAD, N_MOD, D_MODEL)
    ev_w_in_b, ev_w_out_b = ev_w_in.astype(BF16), ev_w_out.astype(BF16)
    od_w_in_b = jnp.concatenate([od_w_in[..., :C_POOL],
                                 _pair_gqa_heads(od_w_in[..., C_POOL:C_POOL + HD_Q], 2),
                                 od_w_in[..., C_POOL + HD_Q:]], axis=-1).astype(BF16)
    od_w_out_b = jnp.concatenate([od_w_out[:, :C_POOL], _pair_gqa_heads(od_w_out[:, C_POOL:], 1)],
                                 axis=1).astype(BF16)
    rpb =jnp.pad(ev_rpb, ((0, 0), (0, 0), (0, 0), (0, LANES - ev_rpb.shape[-1])))
    conv_b = ev_conv_b[:, None, :]
    pool_w, pool_scale = od_pool_w.astype(BF16), od_pool_scale[:, None, :]
    cos_t, sin_t = _rope_tables()
    ones_bd = _head_ones()
    states = []
    w1_a, w2_a = ffn_w1[0, 0].astype(BF16), ffn_w2[0, 0].astype(BF16)
    for l in range(DEPTH):
        i = l // 2
        last = l == DEPTH - 1
        x = _ffn(xs, mods, norm_w, w1_a, w2_a, l, 0)
        if l % 2 == 0:
            q, k, v, bg, z, s_k, s_v, w1_b, w2_b = _even_in(x, mods, norm_w, ev_w_in_b, ffn_w1, ffn_w2, l, i)
            prompt_out = _even_prompt(q, k, v, bg, z, ev_conv_w, conv_b, i,
                                      cast_of=None if last else (ffn_w1, ffn_w2, l + 1))
            fa_p, fb_p = prompt_out[0:2]
            if not last:
                w1_a, w2_a = prompt_out[2:4]
            fa_s, fb_s = _even_sample(q, k, v, bg, z, cache_a_k, cache_a_v, rpb, ev_conv_w, conv_b, i)
            w_out = ev_w_out_b
        else:
            gain = jnp.concatenate([jnp.tile(od_q_norm[i], HQ_D), jnp.tile(od_k_norm[i], HKV_D)])[None, :]
            uc, q, k, v, s_k, s_v, w1_b, w2_b = _odd_in(x, mods, norm_w, od_w_in_b, gain, cos_t, sin_t, ones_bd,
                                                        ffn_w1, ffn_w2, l, i)
            fa_p, fb_p = _odd_prompt(uc, q, k, v, pool_w, pool_scale, i)
            fa_s, fb_s = _odd_sample(uc, q, k, v, cache_d_k, cache_d_v, pool_w, pool_scale, i)
            w_out = od_w_out_b
            if not last:
                w1_a, w2_a = ffn_w1[l + 1, 0].astype(BF16), ffn_w2[l + 1, 0].astype(BF16)
        states.append((s_k, s_v))
        xs = _ffn((x,), mods, norm_w, w1_b, w2_b, l, 2, feats=(fa_p, fb_p, fa_s, fb_s), w_out=w_out, idx=i,
                  split_out=last)
        xs = tuple(xs) if last else (xs,)
    y_prompt = xs[0].reshape(BATCH, SEQ, D_MODEL)
    y_sample = xs[1].reshape(DEC_BATCH, DEC_SEQ, D_MODEL)
    return (y_prompt, y_sample, states[0][0], states[0][1], states[1][0], states[1][1])
```

```python
import functools

import jax
import jax.numpy as jnp
import numpy as np
from jax import lax
from jax.experimental import pallas as pl
from jax.experimental.pallas import tpu as pltpu

D_MODEL = 1024
BATCH = 32
SEQ = 256
DEPTH = 2
DEC_BATCH = 2
DEC_SEQ = 2048
PAST_LEN = 256
GRID_W = 64
HEAD_DIM = 64
N_MOD = 9
D_FF = 2816
FFN_RES = 0.5
H_A = 8
NA_ROWS = 8
NA_COLS = 16
C_B = 512
CONV_W = 3
C_POOL = 512
POOL_WINDOWS = (2, 4, 8, 16)
N_POOL = 4
POOL_C = C_POOL // N_POOL
HQ_D = 8
HKV_D = 2
GQA_GROUP = HQ_D // HKV_D
ROPE_BASE = 10000.0
EVEN_IN = 3 * H_A * HEAD_DIM + 3 * C_B
ODD_IN = C_POOL + (HQ_D + 2 * HKV_D) * HEAD_DIM
RMS_EPS = 1e-6
NEG_INF = -1e30
ATT_SCALE = HEAD_DIM ** -0.5

LANES = 128
GRID_ROWS = DEC_SEQ // GRID_W
N_PROMPT = BATCH * SEQ
N_SAMPLE = DEC_BATCH * DEC_SEQ
N_TOK = N_PROMPT + N_SAMPLE
N_COND = 1 + DEC_BATCH
COND_PAD = 8
HD_A = H_A * HEAD_DIM
HD_Q = HQ_D * HEAD_DIM
HD_KV = HKV_D * HEAD_DIM
QK_W = HD_Q + HD_KV
POOL_PAD = 8

TM = 512
N_TILES = N_TOK // TM
PROMPT_TILES = N_PROMPT // TM
TILES_PER_SAMPLE = DEC_SEQ // TM
SEQ_PER_TILE = TM // SEQ
CAST_STEPS = PROMPT_TILES
TM_FFN = 1024
FFN_HALF = TM_FFN // 2
MIX_CAST_STEPS = N_PROMPT // TM_FFN
MIX_CAST_ROWS = D_MODEL // MIX_CAST_STEPS
assert MIX_CAST_ROWS == 2 * HEAD_DIM and (DEPTH + 1) // 2 == 1 and DEPTH // 2 == 1
MXU_DIM = 256
FF_CHUNKS = tuple((lo, min(lo + 3 * MXU_DIM, D_FF)) for lo in range(0, D_FF, 3 * MXU_DIM))
MOD_TN = 1536
Q_BLK = 256
ROW_UNROLL = 4
VMEM_LIMIT = 60 * 1024 * 1024

F32 = jnp.float32
BF16 = jnp.bfloat16


def _params(n_grid):
    return pltpu.CompilerParams(dimension_semantics=("arbitrary",) * n_grid, vmem_limit_bytes=VMEM_LIMIT)


def _cond_of_tile(i, tm=TM):
    n_prompt = N_PROMPT // tm
    return jnp.where(i < n_prompt, 0, 1 + (i - n_prompt) // (DEC_SEQ // tm))


def _prompt_tile(i, tm=TM):
    return jnp.minimum(i, N_PROMPT // tm - 1)


def _sample_tile(i, tm=TM):
    return jnp.maximum(i - N_PROMPT // tm, 0)


def _rms(x, g):
    return x * lax.rsqrt(jnp.mean(x * x, axis=-1, keepdims=True) + RMS_EPS) * g


def _sigmoid(x):
    return 1.0 / (1.0 + jnp.exp(-x))


def _dot(a, b):
    return jnp.dot(a, b, preferred_element_type=F32)


def _dot_t(a, b):
    return lax.dot_general(a, b, (((1,), (1,)), ((), ())), preferred_element_type=F32)


def _mod_kernel(c_ref, w_ref, b_ref, o_ref):
    c = c_ref[...]
    sc = (c * _sigmoid(c)).astype(BF16)
    o_ref[...] = _dot(sc, w_ref[...].astype(BF16)) + b_ref[...]


def _modulation(cond, mod_w, mod_b):
    n_col = N_MOD * D_MODEL
    return pl.pallas_call(
        _mod_kernel,
        grid=(DEPTH, n_col // MOD_TN),
        in_specs=[
            pl.BlockSpec((COND_PAD, D_MODEL), lambda l, j: (0, 0)),
            pl.BlockSpec((None, D_MODEL, MOD_TN), lambda l, j: (l, 0, j)),
            pl.BlockSpec((None, 1, MOD_TN), lambda l, j: (l, 0, j)),
        ],
        out_specs=pl.BlockSpec((None, COND_PAD, MOD_TN), lambda l, j: (l, 0, j)),
        out_shape=jax.ShapeDtypeStruct((DEPTH, COND_PAD, n_col), F32),
        compiler_params=_params(2),
        name="modulation",
    )(cond, mod_w, mod_b.reshape(DEPTH, 1, n_col))


def _mod_spec(layer, tm=TM):
    return pl.BlockSpec((None, None, N_MOD, D_MODEL), lambda i: (layer, _cond_of_tile(i, tm), 0, 0))


def _gain_spec(layer):
    return pl.BlockSpec((None, 6, D_MODEL), lambda i: (layer, 0, 0))


def _cast_step(i):
    return jnp.minimum(i, CAST_STEPS - 1)


def _cast_specs(layer, which):
    r1, r2 = D_MODEL // CAST_STEPS, D_FF // CAST_STEPS
    in_specs = [pl.BlockSpec((None, None, r1, 2 * D_FF), lambda i: (layer, which, _cast_step(i), 0)),
                pl.BlockSpec((None, None, r2, D_MODEL), lambda i: (layer, which, _cast_step(i), 0))]
    out_specs = [pl.BlockSpec((r1, 2 * D_FF), lambda i: (_cast_step(i), 0)),
                 pl.BlockSpec((r2, D_MODEL), lambda i: (_cast_step(i), 0))]
    out_shape = [jax.ShapeDtypeStruct((D_MODEL, 2 * D_FF), BF16), jax.ShapeDtypeStruct((D_FF, D_MODEL), BF16)]
    return in_specs, out_specs, out_shape


def _cast_weights(w1f_ref, w2f_ref, w1b_ref, w2b_ref):
    @pl.when(pl.program_id(0) < CAST_STEPS)
    def _():
        w1b_ref[...] = w1f_ref[...].astype(BF16)
        w2b_ref[...] = w2f_ref[...].astype(BF16)


def _mixer_cast_specs():
    rows = MIX_CAST_ROWS
    n_pool_blk = C_POOL // rows
    step = lambda i: jnp.minimum(i, MIX_CAST_STEPS - 1)

    def od_out_block(i, g):
        s = step(i)
        return jnp.where(s < n_pool_blk, 2 * s + g, C_POOL // HEAD_DIM + g * GQA_GROUP + s - n_pool_blk)

    widths = (EVEN_IN, D_MODEL, ODD_IN, D_MODEL)
    in_specs = [pl.BlockSpec((None, rows, EVEN_IN), lambda i: (0, step(i), 0)),
                pl.BlockSpec((None, rows, D_MODEL), lambda i: (0, step(i), 0)),
                pl.BlockSpec((None, rows, ODD_IN), lambda i: (0, step(i), 0)),
                pl.BlockSpec((None, HEAD_DIM, D_MODEL), lambda i: (0, od_out_block(i, 0), 0)),
                pl.BlockSpec((None, HEAD_DIM, D_MODEL), lambda i: (0, od_out_block(i, 1), 0))]
    out_specs = [pl.BlockSpec((rows, w), lambda i: (step(i), 0)) for w in widths]
    out_shape = [jax.ShapeDtypeStruct((D_MODEL, w), BF16) for w in widths]
    return in_specs, out_specs, out_shape


def _cast_mixer_weights(ev_in_f, ev_out_f, od_in_f, od_out_lo_f, od_out_hi_f, ev_in_b, ev_out_b, od_in_b, od_out_b):
    @pl.when(pl.program_id(0) < MIX_CAST_STEPS)
    def _():
        ev_in_b[...] = ev_in_f[...].astype(BF16)
        ev_out_b[...] = ev_out_f[...].astype(BF16)
        w = od_in_f[...]
        q = w[:, C_POOL:C_POOL + HD_Q]
        heads = [q[:, (g * GQA_GROUP + j) * HEAD_DIM:(g * GQA_GROUP + j + 1) * HEAD_DIM]
                 for j in range(GQA_GROUP) for g in range(HKV_D)]
        od_in_b[...] = jnp.concatenate([w[:, 0:C_POOL]] + heads + [w[:, C_POOL + HD_Q:]], axis=1).astype(BF16)
        od_out_b[...] = jnp.concatenate([od_out_lo_f[...], od_out_hi_f[...]], axis=0).astype(BF16)


def _ffn_kernel(*refs, sub, split_in, mix, split_out, cast_mixers):
    refs = list(refs)
    x_refs = [refs.pop(0) for _ in range(2 if split_in else 1)]
    feat_refs = [refs.pop(0) for _ in range(4 if mix else 0)]
    m_ref, g_ref = refs.pop(0), refs.pop(0)
    wo_ref = refs.pop(0) if mix else None
    w1_ref, w2_ref = refs.pop(0), refs.pop(0)
    if cast_mixers:
        _cast_mixer_weights(*refs[0:5], *refs[-4:])
        refs = refs[5:-4]
    out_refs = refs
    is_prompt = pl.program_id(0) < N_PROMPT // TM_FFN
    shift = m_ref[3 * sub:3 * sub + 1, :]
    scale = m_ref[3 * sub + 1:3 * sub + 2, :]
    gate = m_ref[3 * sub + 2:3 * sub + 3, :]
    for half in range(TM_FFN // FFN_HALF):
        rows = slice(half * FFN_HALF, (half + 1) * FFN_HALF)
        if split_in:
            x = jnp.where(is_prompt, x_refs[0][rows, :], x_refs[1][rows, :])
        else:
            x = x_refs[0][rows, :]
        if mix:
            fa = jnp.where(is_prompt, feat_refs[0][rows, :], feat_refs[2][rows, :])
            fb = jnp.where(is_prompt, feat_refs[1][rows, :], feat_refs[3][rows, :])
            n_a = fa.shape[1]
            mixed = _dot(fa, wo_ref[0:n_a, :]) + _dot(fb, wo_ref[n_a:, :])
            x = x + m_ref[5:6, :] * _rms(mixed, g_ref[3:4, :])
        h =(_rms(x, g_ref[2 * sub:2 * sub + 1, :]) * (1.0 + scale) + shift).astype(BF16)
        acc = jnp.zeros((FFN_HALF, D_MODEL), F32)
        for lo, hi in FF_CHUNKS:
            a = _dot(h, w1_ref[:, lo:hi])
            u = _dot(h, w1_ref[:, D_FF + lo:D_FF + hi])
            act = (a * _sigmoid(a) * u).astype(BF16)
            acc = acc + _dot(act, w2_ref[lo:hi, :])
        y = x + FFN_RES * gate * _rms(acc, g_ref[2 * sub + 1:2 * sub + 2, :])
        if split_out:
            @pl.when(is_prompt)
            def _():
                out_refs[0][rows, :] = y

            @pl.when(jnp.logical_not(is_prompt))
            def _():
                out_refs[1][rows, :] = y
        else:
            out_refs[0][rows, :] = y


def _ffn(xs, mods, gains, w1, w2, layer, sub, feats=None, w_out=None, split_out=False, mixer_w=None):
    split_in = len(xs) == 2
    mix = feats is not None
    cast_mixers = mixer_w is not None
    resident = pl.Buffered(1)
    half_w = D_MODEL // 2
    feat_specs, mix_w_spec = [], []
    if mix:
        feat_p = pl.BlockSpec((TM_FFN, half_w), lambda i: (_prompt_tile(i, TM_FFN), 0))
        feat_s = pl.BlockSpec((TM_FFN, half_w), lambda i: (_sample_tile(i, TM_FFN), 0))
        feat_specs = [feat_p, feat_p, feat_s, feat_s]
        mix_w_spec = [pl.BlockSpec((D_MODEL, D_MODEL), lambda i: (0, 0), pipeline_mode=resident)]
    cast_in, cast_out, cast_shape, cast_args = [], [], [], ()
    if cast_mixers:
        cast_in, cast_out, cast_shape = _mixer_cast_specs()
        cast_args = (mixer_w[0], mixer_w[1], mixer_w[2], mixer_w[3], mixer_w[3])
    tok = pl.BlockSpec((TM_FFN, D_MODEL), lambda i: (i, 0))
    prompt_tok = pl.BlockSpec((TM_FFN, D_MODEL), lambda i: (_prompt_tile(i, TM_FFN), 0))
    sample_tok = pl.BlockSpec((TM_FFN, D_MODEL), lambda i: (_sample_tile(i, TM_FFN), 0))
    if split_out:
        out_specs = [prompt_tok, sample_tok]
        out_shape = [jax.ShapeDtypeStruct((N_PROMPT, D_MODEL), F32),
                     jax.ShapeDtypeStruct((N_SAMPLE, D_MODEL), F32)]
    else:
        out_specs = [tok]
        out_shape = [jax.ShapeDtypeStruct((N_TOK, D_MODEL), F32)]
    return pl.pallas_call(
        functools.partial(_ffn_kernel, sub=sub, split_in=split_in, mix=mix, split_out=split_out,
                          cast_mixers=cast_mixers),
        grid=(N_TOK // TM_FFN,),
        in_specs=([prompt_tok, sample_tok] if split_in else [tok]) + feat_specs + [
            _mod_spec(layer, TM_FFN),
            _gain_spec(layer),
        ] + mix_w_spec + [
            pl.BlockSpec((D_MODEL, 2 * D_FF), lambda i: (0, 0), pipeline_mode=resident),
            pl.BlockSpec((D_FF, D_MODEL), lambda i: (0, 0), pipeline_mode=resident),
        ] + cast_in,
        out_specs=out_specs + cast_out,
        out_shape=out_shape + cast_shape,
        compiler_params=_params(1),
        name=f"ffn{sub}",
    )(*xs, *(feats or ()), mods, gains, *((w_out,) if mix else ()), w1, w2, *cast_args)


def _mixer_norm(x_ref, m_ref, g_ref):
    return (_rms(x_ref[...], g_ref[2:3, :]) * (1.0 + m_ref[4:5, :]) + m_ref[3:4, :]).astype(BF16)


def _store_heads(state_ref, x, n_heads):
    for b in range(SEQ_PER_TILE):
        for h in range(n_heads):
            state_ref[b, 0, h] = x[b * SEQ:(b + 1) * SEQ, h * HEAD_DIM:(h + 1) * HEAD_DIM]


def _even_in_kernel(x_ref, m_ref, g_ref, w_ref, w1f_ref, w2f_ref,
                    q_ref, k_ref, v_ref, bg_ref, z_ref, sk_ref, sv_ref, w1b_ref, w2b_ref):
    _cast_weights(w1f_ref, w2f_ref, w1b_ref, w2b_ref)
    u = _dot(_mixer_norm(x_ref, m_ref, g_ref), w_ref[...])
    k = u[:, HD_A:2 * HD_A]
    v = u[:, 2 * HD_A:3 * HD_A]
    q_ref[...] = (u[:, 0:HD_A] * ATT_SCALE).astype(BF16)
    k_ref[...] = k.astype(BF16)
    v_ref[...] = v.astype(BF16)
    bg_ref[...] = u[:, 3 * HD_A:3 * HD_A + C_B]
    z_ref[...] = u[:, 3 * HD_A + C_B:3 * HD_A + 2 * C_B] * u[:, 3 * HD_A + 2 * C_B:3 * HD_A + 3 * C_B]

    @pl.when(pl.program_id(0) < PROMPT_TILES)
    def _():
        _store_heads(sk_ref, k, H_A)
        _store_heads(sv_ref, v, H_A)


def _state_spec(n_heads):
    return pl.BlockSpec((SEQ_PER_TILE, 1, n_heads, SEQ, HEAD_DIM), lambda i: (_prompt_tile(i), 0, 0, 0, 0))


def _even_in(x, mods, gains, w_in, ffn_w1, ffn_w2, layer, idx):
    tok = lambda w: pl.BlockSpec((TM, w), lambda i: (i, 0))
    state = jax.ShapeDtypeStruct((BATCH, 1, H_A, SEQ, HEAD_DIM), F32)
    cast_in, cast_out, cast_shape = _cast_specs(layer, 1)
    return pl.pallas_call(
        _even_in_kernel,
        grid=(N_TILES,),
        in_specs=[
            tok(D_MODEL), _mod_spec(layer), _gain_spec(layer),
            pl.BlockSpec((D_MODEL, EVEN_IN), lambda i: (0, 0)),
        ] + cast_in,
        out_specs=[tok(HD_A), tok(HD_A), tok(HD_A), tok(C_B), tok(C_B), _state_spec(H_A), _state_spec(H_A)]
        + cast_out,
        out_shape=[
            jax.ShapeDtypeStruct((N_TOK, HD_A), BF16),
            jax.ShapeDtypeStruct((N_TOK, HD_A), BF16),
            jax.ShapeDtypeStruct((N_TOK, HD_A), BF16),
            jax.ShapeDtypeStruct((N_TOK, C_B), F32),
            jax.ShapeDtypeStruct((N_TOK, C_B), F32),
            state, state,
        ] + cast_shape,
        compiler_params=_params(1),
        name="even_in",
    )(x, mods, gains, w_in, ffn_w1, ffn_w2)


def _swap_pairs(x):
    n = x.shape[-1]
    lane = lax.broadcasted_iota(jnp.int32, x.shape, x.ndim - 1)
    return jnp.where(lane % 2 == 0, pltpu.roll(x, n - 1, x.ndim - 1), pltpu.roll(x, 1, x.ndim - 1))


def _odd_in_kernel(x_ref, m_ref, g_ref, w_ref, ng_ref, cos_ref, sin_ref, ones_ref, w1f_ref, w2f_ref,
                   uc_ref, q_ref, k_ref, v_ref, sk_ref, sv_ref, w1b_ref, w2b_ref):
    _cast_weights(w1f_ref, w2f_ref, w1b_ref, w2b_ref)
    u = _dot(_mixer_norm(x_ref, m_ref, g_ref), w_ref[...])
    uc_ref[...] = u[:, 0:C_POOL]
    qk = u[:, C_POOL:C_POOL + QK_W]
    sq = qk * qk
    hi = sq.astype(BF16)
    lo = (sq - hi.astype(F32)).astype(BF16)
    ones = ones_ref[...]
    sums = []
    for c0 in range(0, QK_W, MXU_DIM):
        c1 = min(c0 + MXU_DIM, QK_W)
        sums.append(_dot(hi[:, c0:c1], ones[0:c1 - c0, 0:c1 - c0]) + _dot(lo[:, c0:c1], ones[0:c1 - c0, 0:c1 - c0]))
    ms = jnp.concatenate(sums, axis=-1) * (1.0 / HEAD_DIM)
    n = qk * lax.rsqrt(ms + RMS_EPS) * ng_ref[...]
    r = n * cos_ref[...] + _swap_pairs(n) * sin_ref[...]
    k = r[:, HD_Q:QK_W]
    v = u[:, C_POOL + QK_W:ODD_IN]
    q_ref[...] = (r[:, 0:HD_Q] * ATT_SCALE).astype(BF16)
    k_ref[...] = k.astype(BF16)
    v_ref[...] = v.astype(BF16)

    @pl.when(pl.program_id(0) < PROMPT_TILES)
    def _():
        _store_heads(sk_ref, k, HKV_D)
        _store_heads(sv_ref, v, HKV_D)


def _rope_tile_index(i):
    return jnp.where(i < PROMPT_TILES, 0, 1 + (i - PROMPT_TILES) % TILES_PER_SAMPLE)


def _odd_in(x, mods, gains, w_in, qk_gain, cos_t, sin_t, ones_bd, ffn_w1, ffn_w2, layer, idx):
    tok = lambda w: pl.BlockSpec((TM, w), lambda i: (i, 0))
    state = jax.ShapeDtypeStruct((BATCH, 1, HKV_D, SEQ, HEAD_DIM), F32)
    cast_in, cast_out, cast_shape = _cast_specs(layer, 1)
    return pl.pallas_call(
        _odd_in_kernel,
        grid=(N_TILES,),
        in_specs=[
            tok(D_MODEL), _mod_spec(layer), _gain_spec(layer),
            pl.BlockSpec((D_MODEL, ODD_IN), lambda i: (0, 0)),
            pl.BlockSpec((1, QK_W), lambda i: (0, 0)),
            pl.BlockSpec((TM, QK_W), lambda i: (_rope_tile_index(i), 0)),
            pl.BlockSpec((TM, QK_W), lambda i: (_rope_tile_index(i), 0)),
            pl.BlockSpec((MXU_DIM, MXU_DIM), lambda i: (0, 0)),
        ] + cast_in,
        out_specs=[tok(C_POOL), tok(HD_Q), tok(HD_KV), tok(HD_KV), _state_spec(HKV_D), _state_spec(HKV_D)]
        + cast_out,
        out_shape=[
            jax.ShapeDtypeStruct((N_TOK, C_POOL), F32),
            jax.ShapeDtypeStruct((N_TOK, HD_Q), BF16),
            jax.ShapeDtypeStruct((N_TOK, HD_KV), BF16),
            jax.ShapeDtypeStruct((N_TOK, HD_KV), BF16),
            state, state,
        ] + cast_shape,
        compiler_params=_params(1),
        name="odd_in",
    )(x, mods, gains, w_in, qk_gain, cos_t, sin_t, ones_bd, ffn_w1, ffn_w2)


def _low_half(rows):
    return lax.broadcasted_iota(jnp.int32, (rows, LANES), 1) < HEAD_DIM


def _split_pair(qp):
    low = _low_half(qp.shape[0])
    zero = jnp.zeros_like(qp)
    return jnp.concatenate([jnp.where(low, qp, zero), jnp.where(low, zero, qp)], axis=0)


def _pair_heads_t(ref):
    return jnp.concatenate([ref[0], ref[1]], axis=0).astype(BF16)


def _with_ones_t(vt):
    return jnp.concatenate([vt, jnp.ones_like(vt)], axis=0)


def _with_ones(v):
    return jnp.concatenate([v, jnp.ones_like(v)], axis=1)


def _softmax_pv(scores, values, values_t=()):
    m = scores[0].max(axis=-1, keepdims=True)
    for s in scores[1:]:
        m = jnp.maximum(m, s.max(axis=-1, keepdims=True))
    acc = None
    for i, s in enumerate(scores):
        e = jnp.exp(s - m).astype(BF16)
        o = _dot(e, values[i]) if i < len(values) else _dot_t(e, values_t[i - len(values)])
        acc = o if acc is None else acc + o
    return acc[:, 0:LANES] / acc[:, LANES:2 * LANES]


def _merge_pair(o):
    m = o.shape[0] // 2
    return jnp.where(_low_half(m), o[0:m], o[m:2 * m])


def _short_conv(z, cw_ref, cb_ref, seq_len):
    rows = z.shape[0]
    pos = lax.broadcasted_iota(jnp.int32, z.shape, 0) & (seq_len - 1)
    z_prev = jnp.where(pos == 0, 0.0, pltpu.roll(z, 1, 0))
    z_next = jnp.where(pos == seq_len - 1, 0.0, pltpu.roll(z, rows - 1, 0))
    y = z_prev * cw_ref[0:1, :]
    y = y + z * cw_ref[1:2, :]
    y = y + z_next * cw_ref[2:3, :]
    return y + cb_ref[...]


def _pool_mix(uc, pw_ref, ps_ref, seq_len):
    pos = lax.broadcasted_iota(jnp.int32, (seq_len, POOL_C), 0)
    pad = jnp.zeros((POOL_PAD, POOL_C), F32)
    n_ext = seq_len + 2 * POOL_PAD
    outs = []
    for gi, win in enumerate(POOL_WINDOWS):
        ug = uc[:, gi * POOL_C:(gi + 1) * POOL_C]
        run = jnp.concatenate([pad, ug, pad], axis=0)
        span = 1
        while span < win:
            run = run + pltpu.roll(run, span, 0)
            span *= 2
        back = win // 2 - 1
        if back:
            run = pltpu.roll(run, n_ext - back, 0)
        wsum = run[POOL_PAD:POOL_PAD + seq_len, :]
        cnt = jnp.minimum(pos + (win - win // 2), seq_len) - jnp.maximum(pos - win // 2, 0)
        pooled = (wsum / cnt.astype(F32) - ug).astype(BF16)
        outs.append(_dot(pooled, pw_ref[gi]))
    return jnp.concatenate(outs, axis=-1) * ps_ref[...]


def _even_prompt_kernel(*refs, cast):
    q_ref, k_ref, v_ref, bg_ref, z_ref, cw_ref, cb_ref = refs[0:7]
    fa_ref, fb_ref = refs[7 + 2 * cast:9 + 2 * cast]
    if cast:
        _cast_weights(refs[7], refs[8], refs[11], refs[12])
    for b in range(SEQ_PER_TILE):
        rows = slice(b * SEQ, (b + 1) * SEQ)
        outs = []
        for p in range(H_A // 2):
            sl = slice(p * LANES, (p + 1) * LANES)
            s = _dot_t(_split_pair(q_ref[rows, sl]), k_ref[rows, sl])
            outs.append(_merge_pair(_softmax_pv([s], [_with_ones(v_ref[rows, sl])])))
        fa_ref[rows, :] = jnp.concatenate(outs, axis=-1).astype(BF16)
    fb_ref[...] = (bg_ref[...] * _short_conv(z_ref[...], cw_ref, cb_ref, SEQ)).astype(BF16)


def _even_prompt(q, k, v, bg, z, conv_w, conv_b, idx, cast_of=None):
    tile = lambda w: pl.BlockSpec((TM, w), lambda t: (t, 0))
    cast_in, cast_out, cast_shape, cast_args = [], [], [], ()
    if cast_of is not None:
        cast_in, cast_out, cast_shape = _cast_specs(cast_of[2], 0)
        cast_args = cast_of[0:2]
    return pl.pallas_call(
        functools.partial(_even_prompt_kernel, cast=cast_of is not None),
        grid=(PROMPT_TILES,),
        in_specs=[tile(HD_A), tile(HD_A), tile(HD_A), tile(C_B), tile(C_B),
                  pl.BlockSpec((None, CONV_W, C_B), lambda t: (idx, 0, 0)),
                  pl.BlockSpec((None, 1, C_B), lambda t: (idx, 0, 0))] + cast_in,
        out_specs=[tile(HD_A), tile(C_B)] + cast_out,
        out_shape=[
            jax.ShapeDtypeStruct((N_PROMPT, HD_A), BF16),
            jax.ShapeDtypeStruct((N_PROMPT, C_B), BF16),
        ] + cast_shape,
        compiler_params=_params(1),
        name="even_prompt",
    )(q, k, v, bg, z, conv_w, conv_b, *cast_args)


def _na_bias_tiles(rpb_h):
    c = lax.broadcasted_iota(jnp.int32, (GRID_W, GRID_W), 0)
    kc = lax.broadcasted_iota(jnp.int32, (GRID_W, GRID_W), 1)
    cs = jnp.clip(c - NA_COLS // 2, 0, GRID_W - NA_COLS)
    valid = (kc >= cs) & (kc < cs + NA_COLS)
    tiles = []
    for ro in range(2 * NA_ROWS - 1):
        g = jnp.broadcast_to(rpb_h[ro:ro + 1, :], (GRID_W, LANES))
        skew = pltpu.roll(g, LANES - (NA_COLS - 1), 1, stride=1, stride_axis=0)
        tiles.append(jnp.where(valid, skew[:, 0:GRID_W], NEG_INF))
    return tiles


def _even_sample_kernel(q_ref, k_ref, v_ref, bg_ref, z_ref, ck_ref, cv_ref, rpb_ref, cw_ref, cb_ref,
                        fa_ref, fb_ref, bias_ref):
    win = NA_ROWS * GRID_W
    for h in range(2):
        tiles = _na_bias_tiles(rpb_ref[h])
        for var in range(NA_ROWS):
            bias_ref[var, h * GRID_W:(h + 1) * GRID_W, :] = jnp.concatenate(
                [tiles[i - var + NA_ROWS - 1] for i in range(NA_ROWS)], axis=1)
    ck_t = _pair_heads_t(ck_ref)
    cv_ext_t = _with_ones_t(_pair_heads_t(cv_ref))

    def rows(it, carry):
        for j in range(ROW_UNROLL):
            r = it * ROW_UNROLL + j
            rs = jnp.clip(r - NA_ROWS // 2, 0, GRID_ROWS - NA_ROWS)
            q0 = pl.multiple_of(r * GRID_W, GRID_W)
            k0 = pl.multiple_of(rs * GRID_W, GRID_W)
            q2 = _split_pair(q_ref[pl.ds(q0, GRID_W), :])
            s_loc = _dot_t(q2, k_ref[pl.ds(k0, win), :]) + bias_ref[r - rs]
            s_ctx = _dot(q2, ck_t)
            o = _softmax_pv([s_loc, s_ctx], [_with_ones(v_ref[pl.ds(k0, win), :])], [cv_ext_t])
            fa_ref[pl.ds(q0, GRID_W), :] = _merge_pair(o).astype(BF16)
        return carry

    lax.fori_loop(0, GRID_ROWS // ROW_UNROLL, rows, 0)
    fb_ref[...] = (bg_ref[...] * _short_conv(z_ref[...], cw_ref, cb_ref, DEC_SEQ)).astype(BF16)


def _even_sample(q, k, v, bg, z, ctx_k, ctx_v, rpb, conv_w, conv_b, idx):
    first = N_PROMPT // DEC_SEQ
    seq_in = pl.BlockSpec((DEC_SEQ, LANES), lambda b, p: (first + b, p))
    seq_out = pl.BlockSpec((DEC_SEQ, LANES), lambda b, p: (b, p))
    ctx = pl.BlockSpec((None, None, 2, HEAD_DIM, PAST_LEN), lambda b, p: (b, idx, p, 0, 0))
    return pl.pallas_call(
        _even_sample_kernel,
        grid=(DEC_BATCH, H_A // 2),
        in_specs=[seq_in, seq_in, seq_in, seq_in, seq_in, ctx, ctx,
                  pl.BlockSpec((None, 2, 2 * NA_ROWS - 1, LANES), lambda b, p: (idx, p, 0, 0)),
                  pl.BlockSpec((None, CONV_W, LANES), lambda b, p: (idx, 0, p)),
                  pl.BlockSpec((None, 1, LANES), lambda b, p: (idx, 0, p))],
        out_specs=[seq_out, seq_out],
        out_shape=[
            jax.ShapeDtypeStruct((N_SAMPLE, HD_A), BF16),
            jax.ShapeDtypeStruct((N_SAMPLE, C_B), BF16),
        ],
        scratch_shapes=[pltpu.VMEM((NA_ROWS, 2 * GRID_W, NA_ROWS * GRID_W), F32)],
        compiler_params=_params(2),
        name="even_sample",
    )(q, k, v, bg, z, ctx_k, ctx_v, rpb, conv_w, conv_b)


def _split_groups(q):
    low = _low_half(q.shape[0])
    zero = jnp.zeros((q.shape[0], LANES), BF16)
    pairs = [q[:, j * LANES:(j + 1) * LANES] for j in range(GQA_GROUP)]
    return jnp.concatenate([jnp.where(low, p, zero) for p in pairs] + [jnp.where(low, zero, p) for p in pairs],
                           axis=0)


def _merge_groups(o, m):
    low = _low_half(m)
    half = GQA_GROUP * m
    return jnp.concatenate([jnp.where(low, o[j * m:(j + 1) * m], o[half + j * m:half + (j + 1) * m])
                            for j in range(GQA_GROUP)], axis=-1)


def _odd_prompt_kernel(uc_ref, q_ref, k_ref, v_ref, pw_ref, ps_ref, fc_ref, fd_ref):
    for b in range(SEQ_PER_TILE):
        rows = slice(b * SEQ, (b + 1) * SEQ)
        s = _dot_t(_split_groups(q_ref[rows, :]), k_ref[rows, :])
        o = _softmax_pv([s], [_with_ones(v_ref[rows, :])])
        fd_ref[rows, :] = _merge_groups(o, SEQ).astype(BF16)
        fc_ref[rows, :] = _pool_mix(uc_ref[rows, :], pw_ref, ps_ref, SEQ).astype(BF16)


def _odd_prompt(uc, q, k, v, pool_w, pool_scale, idx):
    seq = lambda w: pl.BlockSpec((TM, w), lambda t: (t, 0))
    return pl.pallas_call(
        _odd_prompt_kernel,
        grid=(PROMPT_TILES,),
        in_specs=[seq(C_POOL), seq(HD_Q), seq(HD_KV), seq(HD_KV),
                  pl.BlockSpec((None, N_POOL, POOL_C, POOL_C), lambda t: (idx, 0, 0, 0)),
                  pl.BlockSpec((None, 1, C_POOL), lambda t: (idx, 0, 0))],
        out_specs=[seq(C_POOL), seq(HD_Q)],
        out_shape=[
            jax.ShapeDtypeStruct((N_PROMPT, C_POOL), BF16),
            jax.ShapeDtypeStruct((N_PROMPT, HD_Q), BF16),
        ],
        compiler_params=_params(1),
        name="odd_prompt",
    )(uc, q, k, v, pool_w, pool_scale)


def _odd_sample_kernel(uc_ref, q_ref, k_ref, v_ref, ck_ref, cv_ref, pw_ref, ps_ref, fc_ref, fd_ref,
                       vext_ref, ckpair_ref, cvext_ref):
    @pl.when(pl.program_id(1) == 0)
    def _():
        fc_ref[...] = _pool_mix(uc_ref[...], pw_ref, ps_ref, DEC_SEQ).astype(BF16)
        vext_ref[...] = _with_ones(v_ref[...])
        ckpair_ref[...] = _pair_heads_t(ck_ref)
        cvext_ref[...] = _with_ones_t(_pair_heads_t(cv_ref))

    k = k_ref[...]
    ck_t = ckpair_ref[...]
    low = _low_half(Q_BLK)
    outs = []
    for j in range(GQA_GROUP):
        q2 = _split_pair(q_ref[:, j * LANES:(j + 1) * LANES])
        o = _softmax_pv([_dot_t(q2, k), _dot(q2, ck_t)], [vext_ref[...]], [cvext_ref[...]])
        outs.append(jnp.where(low, o[0:Q_BLK], o[Q_BLK:2 * Q_BLK]))
    fd_ref[...] = jnp.concatenate(outs, axis=-1).astype(BF16)


def _odd_sample(uc, q, k, v, ctx_k, ctx_v, pool_w, pool_scale, idx):
    first = N_PROMPT // DEC_SEQ
    n_qb = DEC_SEQ // Q_BLK
    first_q = N_PROMPT // Q_BLK
    whole_in = lambda w: pl.BlockSpec((DEC_SEQ, w), lambda b, i: (first + b, 0))
    ctx = pl.BlockSpec((None, None, HKV_D, HEAD_DIM, PAST_LEN), lambda b, i: (b, idx, 0, 0, 0))
    return pl.pallas_call(
        _odd_sample_kernel,
        grid=(DEC_BATCH, n_qb),
        in_specs=[whole_in(C_POOL),
                  pl.BlockSpec((Q_BLK, HD_Q), lambda b, i: (first_q + b * n_qb + i, 0)),
                  whole_in(HD_KV), whole_in(HD_KV), ctx, ctx,
                  pl.BlockSpec((None, N_POOL, POOL_C, POOL_C), lambda b, i: (idx, 0, 0, 0)),
                  pl.BlockSpec((None, 1, C_POOL), lambda b, i: (idx, 0, 0))],
        out_specs=[pl.BlockSpec((DEC_SEQ, C_POOL), lambda b, i: (b, 0)),
                   pl.BlockSpec((Q_BLK, HD_Q), lambda b, i: (b * n_qb + i, 0))],
        out_shape=[
            jax.ShapeDtypeStruct((N_SAMPLE, C_POOL), BF16),
            jax.ShapeDtypeStruct((N_SAMPLE, HD_Q), BF16),
        ],
        scratch_shapes=[pltpu.VMEM((DEC_SEQ, 2 * LANES), BF16), pltpu.VMEM((LANES, PAST_LEN), BF16),
                        pltpu.VMEM((2 * LANES, PAST_LEN), BF16)],
        compiler_params=_params(2),
        name="odd_sample",
    )(uc, q, k, v, ctx_k, ctx_v, pool_w, pool_scale)


def _rope_tables():
    t = np.arange(DEC_SEQ)
    n_freq = HEAD_DIM // 4
    inv = ROPE_BASE ** (-np.arange(n_freq, dtype=np.float64) / n_freq)
    ang = np.concatenate([(t // GRID_W)[:, None] * inv, (t % GRID_W)[:, None] * inv], axis=-1)
    cos = np.repeat(np.cos(ang), 2, axis=-1)
    sin = np.repeat(np.sin(ang), 2, axis=-1) * np.tile([-1.0, 1.0], HEAD_DIM // 2)
    n_heads = HQ_D + HKV_D
    cos = np.concatenate([np.ones((TM, QK_W)), np.tile(cos, (1, n_heads))], axis=0)
    sin = np.concatenate([np.zeros((TM, QK_W)), np.tile(sin, (1, n_heads))], axis=0)
    return jnp.asarray(cos, F32), jnp.asarray(sin, F32)


def _head_ones():
    head = np.arange(MXU_DIM) // HEAD_DIM
    return jnp.asarray(head[:, None] == head[None, :], BF16)


def kernel(x_prompt, x_sample, cache_a_k, cache_a_v, cache_d_k, cache_d_v, c, c_ctx, mod_w, mod_b, norm_w,
           ffn_w1, ffn_w2, ev_w_in, ev_rpb, ev_conv_w, ev_conv_b, ev_w_out, od_w_in, od_pool_w,
           od_pool_scale, od_q_norm, od_k_norm, od_w_out):
    xs = (x_prompt.reshape(N_PROMPT, D_MODEL), x_sample.reshape(N_SAMPLE, D_MODEL))
    cond = jnp.concatenate([c_ctx[None, :], c, jnp.zeros((COND_PAD - N_COND, D_MODEL), F32)], axis=0)
    mods = _modulation(cond, mod_w, mod_b).reshape(DEPTH, COND_PAD, N_MOD, D_MODEL)
    cache_a_k, cache_a_v, cache_d_k, cache_d_v = (jnp.swapaxes(t, -1, -2)
                                                  for t in (cache_a_k, cache_a_v, cache_d_k, cache_d_v))
    rpb = jnp.pad(ev_rpb, ((0, 0), (0, 0), (0, 0), (0, LANES - ev_rpb.shape[-1])))
    conv_b = ev_conv_b[:, None, :]
    pool_w, pool_scale = od_pool_w.astype(BF16), od_pool_scale[:, None, :]
    cos_t, sin_t = _rope_tables()
    ones_bd = _head_ones()
    states = []
    w1_a, w2_a = ffn_w1[0, 0].astype(BF16), ffn_w2[0, 0].astype(BF16)
    mixer_w = None
    for l in range(DEPTH):
        i = l // 2
        last = l == DEPTH - 1
        if l == 0:
            x, *mixer_w = _ffn(xs, mods, norm_w, w1_a, w2_a, l, 0, mixer_w=(ev_w_in, ev_w_out, od_w_in, od_w_out))
        else:
            x, = _ffn(xs, mods, norm_w, w1_a, w2_a, l, 0)
        ev_w_in_b, ev_w_out_b, od_w_in_b, od_w_out_b = mixer_w
        if l % 2 == 0:
            q, k, v, bg, z, s_k, s_v, w1_b, w2_b = _even_in(x, mods, norm_w, ev_w_in_b, ffn_w1, ffn_w2, l, i)
            prompt_out = _even_prompt(q, k, v, bg, z, ev_conv_w, conv_b, i,
                                      cast_of=None if last else (ffn_w1, ffn_w2, l + 1))
            fa_p, fb_p = prompt_out[0:2]
            if not last:
                w1_a, w2_a = prompt_out[2:4]
            fa_s, fb_s = _even_sample(q, k, v, bg, z, cache_a_k, cache_a_v, rpb, ev_conv_w, conv_b, i)
            w_out = ev_w_out_b
        else:
            gain = jnp.concatenate([jnp.tile(od_q_norm[i], HQ_D), jnp.tile(od_k_norm[i], HKV_D)])[None, :]
            uc, q, k, v, s_k, s_v, w1_b, w2_b = _odd_in(x, mods, norm_w, od_w_in_b, gain, cos_t, sin_t, ones_bd,
                                                        ffn_w1, ffn_w2, l, i)
            fa_p, fb_p = _odd_prompt(uc, q, k, v, pool_w, pool_scale, i)
            fa_s, fb_s = _odd_sample(uc, q, k, v, cache_d_k, cache_d_v, pool_w, pool_scale, i)
            w_out = od_w_out_b
            if not last:
                w1_a, w2_a = ffn_w1[l + 1, 0].astype(BF16), ffn_w2[l + 1, 0].astype(BF16)
        states.append((s_k, s_v))
        xs = tuple(_ffn((x,), mods, norm_w, w1_b, w2_b, l, 2, feats=(fa_p, fb_p, fa_s, fb_s), w_out=w_out,
                        split_out=last))
    y_prompt = xs[0].reshape(BATCH, SEQ, D_MODEL)
    y_sample = xs[1].reshape(DEC_BATCH, DEC_SEQ, D_MODEL)
    return (y_prompt, y_sample, states[0][0], states[0][1], states[1][0], states[1][1])
```

```python
import functools

import jax
import jax.numpy as jnp
import numpy as np
from jax import lax
from jax.experimental import pallas as pl
from jax.experimental.pallas import tpu as pltpu

D_MODEL = 1024
BATCH = 32
SEQ = 256
DEPTH = 2
DEC_BATCH = 2
DEC_SEQ = 2048
PAST_LEN = 256
GRID_W = 64
HEAD_DIM = 64
N_MOD = 9
D_FF = 2816
FFN_RES = 0.5
H_A = 8
NA_ROWS = 8
NA_COLS = 16
C_B = 512
CONV_W = 3
C_POOL = 512
POOL_WINDOWS = (2, 4, 8, 16)
N_POOL = 4
POOL_C = C_POOL // N_POOL
HQ_D = 8
HKV_D = 2
GQA_GROUP = HQ_D // HKV_D
ROPE_BASE = 10000.0
EVEN_IN = 3 * H_A * HEAD_DIM + 3 * C_B
ODD_IN = C_POOL + (HQ_D + 2 * HKV_D) * HEAD_DIM
RMS_EPS = 1e-6
NEG_INF = -1e30
ATT_SCALE = HEAD_DIM ** -0.5

LANES = 128
GRID_ROWS = DEC_SEQ // GRID_W
N_PROMPT = BATCH * SEQ
N_SAMPLE = DEC_BATCH * DEC_SEQ
N_TOK = N_PROMPT + N_SAMPLE
N_COND = 1 + DEC_BATCH
COND_PAD = 8
HD_A = H_A * HEAD_DIM
HD_Q = HQ_D * HEAD_DIM
HD_KV = HKV_D * HEAD_DIM
QK_W = HD_Q + HD_KV
POOL_PAD = 8

TM = 512
N_TILES = N_TOK // TM
PROMPT_TILES = N_PROMPT // TM
TILES_PER_SAMPLE = DEC_SEQ // TM
SEQ_PER_TILE = TM // SEQ
CAST_STEPS = PROMPT_TILES
TM_FFN = 1024
FFN_HALF = TM_FFN // 2
MIX_CAST_STEPS = N_PROMPT // TM_FFN
MIX_CAST_ROWS = D_MODEL // MIX_CAST_STEPS
assert MIX_CAST_ROWS == 2 * HEAD_DIM and (DEPTH + 1) // 2 == 1 and DEPTH // 2 == 1
MXU_DIM = 256
FF_CHUNKS = tuple((lo, min(lo + 3 * MXU_DIM, D_FF)) for lo in range(0, D_FF, 3 * MXU_DIM))
MOD_TN = 1536
Q_BLK = 256
ROW_UNROLL = 4
VMEM_LIMIT = 60 * 1024 * 1024

F32 = jnp.float32
BF16 = jnp.bfloat16


def _params(n_grid):
    return pltpu.CompilerParams(dimension_semantics=("arbitrary",) * n_grid, vmem_limit_bytes=VMEM_LIMIT)


def _cond_of_tile(i, tm=TM):
    n_prompt = N_PROMPT // tm
    return jnp.where(i < n_prompt, 0, 1 + (i - n_prompt) // (DEC_SEQ // tm))


def _prompt_tile(i, tm=TM):
    return jnp.minimum(i, N_PROMPT // tm - 1)


def _sample_tile(i, tm=TM):
    return jnp.maximum(i - N_PROMPT // tm, 0)


def _rms(x, g):
    return x * lax.rsqrt(jnp.mean(x * x, axis=-1, keepdims=True) + RMS_EPS) * g


def _sigmoid(x):
    return 1.0 / (1.0 + jnp.exp(-x))


def _dot(a, b):
    return jnp.dot(a, b, preferred_element_type=F32)


def _dot_t(a, b):
    return lax.dot_general(a, b, (((1,), (1,)), ((), ())), preferred_element_type=F32)


def _mod_kernel(c_ref, w_ref, b_ref, o_ref):
    c = c_ref[...]
    sc = (c * _sigmoid(c)).astype(BF16)
    o_ref[...] = _dot(sc, w_ref[...].astype(BF16)) + b_ref[...]


def _modulation(cond, mod_w, mod_b):
    n_col = N_MOD * D_MODEL
    return pl.pallas_call(
        _mod_kernel,
        grid=(DEPTH, n_col // MOD_TN),
        in_specs=[
            pl.BlockSpec((COND_PAD, D_MODEL), lambda l, j: (0, 0)),
            pl.BlockSpec((None, D_MODEL, MOD_TN), lambda l, j: (l, 0, j)),
            pl.BlockSpec((None, 1, MOD_TN), lambda l, j: (l, 0, j)),
        ],
        out_specs=pl.BlockSpec((None, COND_PAD, MOD_TN), lambda l, j: (l, 0, j)),
        out_shape=jax.ShapeDtypeStruct((DEPTH, COND_PAD, n_col), F32),
        compiler_params=_params(2),
        name="modulation",
    )(cond, mod_w, mod_b.reshape(DEPTH, 1, n_col))


def _mod_spec(layer, tm=TM):
    return pl.BlockSpec((None, None, N_MOD, D_MODEL), lambda i: (layer, _cond_of_tile(i, tm), 0, 0))


def _gain_spec(layer):
    return pl.BlockSpec((None, 6, D_MODEL), lambda i: (layer, 0, 0))


def _cast_step(i):
    return jnp.minimum(i, CAST_STEPS - 1)


def _cast_specs(layer, which):
    r1, r2 = D_MODEL // CAST_STEPS, D_FF // CAST_STEPS
    in_specs = [pl.BlockSpec((None, None, r1, 2 * D_FF), lambda i: (layer, which, _cast_step(i), 0)),
                pl.BlockSpec((None, None, r2, D_MODEL), lambda i: (layer, which, _cast_step(i), 0))]
    out_specs = [pl.BlockSpec((r1, 2 * D_FF), lambda i: (_cast_step(i), 0)),
                 pl.BlockSpec((r2, D_MODEL), lambda i: (_cast_step(i), 0))]
    out_shape = [jax.ShapeDtypeStruct((D_MODEL, 2 * D_FF), BF16), jax.ShapeDtypeStruct((D_FF, D_MODEL), BF16)]
    return in_specs, out_specs, out_shape


def _cast_weights(w1f_ref, w2f_ref, w1b_ref, w2b_ref):
    @pl.when(pl.program_id(0) < CAST_STEPS)
    def _():
        w1b_ref[...] = w1f_ref[...].astype(BF16)
        w2b_ref[...] = w2f_ref[...].astype(BF16)


def _mixer_cast_specs():
    rows = MIX_CAST_ROWS
    n_pool_blk = C_POOL // rows
    step = lambda i: jnp.minimum(i, MIX_CAST_STEPS - 1)

    def od_out_block(i, g):
        s = step(i)
        return jnp.where(s < n_pool_blk, 2 * s + g, C_POOL // HEAD_DIM + g * GQA_GROUP + s - n_pool_blk)

    widths = (EVEN_IN, D_MODEL, ODD_IN, D_MODEL)
    in_specs = [pl.BlockSpec((None, rows, EVEN_IN), lambda i: (0, step(i), 0)),
                pl.BlockSpec((None, rows, D_MODEL), lambda i: (0, step(i), 0)),
                pl.BlockSpec((None, rows, ODD_IN), lambda i: (0, step(i), 0)),
                pl.BlockSpec((None, HEAD_DIM, D_MODEL), lambda i: (0, od_out_block(i, 0), 0)),
                pl.BlockSpec((None, HEAD_DIM, D_MODEL), lambda i: (0, od_out_block(i, 1), 0))]
    out_specs = [pl.BlockSpec((rows, w), lambda i: (step(i), 0)) for w in widths]
    out_shape = [jax.ShapeDtypeStruct((D_MODEL, w), BF16) for w in widths]
    return in_specs, out_specs, out_shape


def _cast_mixer_weights(ev_in_f, ev_out_f, od_in_f, od_out_lo_f, od_out_hi_f, ev_in_b, ev_out_b, od_in_b, od_out_b):
    @pl.when(pl.program_id(0) < MIX_CAST_STEPS)
    def _():
        ev_in_b[...] = ev_in_f[...].astype(BF16)
        ev_out_b[...] = ev_out_f[...].astype(BF16)
        w = od_in_f[...]
        q = w[:, C_POOL:C_POOL + HD_Q]
        heads = [q[:, (g * GQA_GROUP + j) * HEAD_DIM:(g * GQA_GROUP + j + 1) * HEAD_DIM]
                 for j in range(GQA_GROUP) for g in range(HKV_D)]
        od_in_b[...] = jnp.concatenate([w[:, 0:C_POOL]] + heads + [w[:, C_POOL + HD_Q:]], axis=1).astype(BF16)
        od_out_b[...] = jnp.concatenate([od_out_lo_f[...], od_out_hi_f[...]], axis=0).astype(BF16)


def _ffn_kernel(*refs, sub, split_in, mix, split_out, cast_mixers):
    refs = list(refs)
    x_refs = [refs.pop(0) for _ in range(2 if split_in else 1)]
    feat_refs = [refs.pop(0) for _ in range(4 if mix else 0)]
    m_ref, g_ref = refs.pop(0), refs.pop(0)
    wo_ref = refs.pop(0) if mix else None
    w1_ref, w2_ref = refs.pop(0), refs.pop(0)
    if cast_mixers:
        _cast_mixer_weights(*refs[0:5], *refs[-4:])
        refs = refs[5:-4]
    out_refs = refs
    is_prompt = pl.program_id(0) < N_PROMPT // TM_FFN
    shift = m_ref[3 * sub:3 * sub + 1, :]
    scale = m_ref[3 * sub + 1:3 * sub + 2, :]
    gate = m_ref[3 * sub + 2:3 * sub + 3, :]
    for half in range(TM_FFN // FFN_HALF):
        rows = slice(half * FFN_HALF, (half + 1) * FFN_HALF)
        if split_in:
            x = jnp.where(is_prompt, x_refs[0][rows, :], x_refs[1][rows, :])
        else:
            x = x_refs[0][rows, :]
        if mix:
            fa = jnp.where(is_prompt, feat_refs[0][rows, :], feat_refs[2][rows, :])
            fb = jnp.where(is_prompt, feat_refs[1][rows, :], feat_refs[3][rows, :])
            n_a = fa.shape[1]
            mixed = _dot(fa, wo_ref[0:n_a, :]) + _dot(fb, wo_ref[n_a:, :])
            x = x + m_ref[5:6, :] * _rms(mixed, g_ref[3:4, :])
        h =(_rms(x, g_ref[2 * sub:2 * sub + 1, :]) * (1.0 + scale) + shift).astype(BF16)
        acc = jnp.zeros((FFN_HALF, D_MODEL), F32)
        for lo, hi in FF_CHUNKS:
            a = _dot(h, w1_ref[:, lo:hi])
            u = _dot(h, w1_ref[:, D_FF + lo:D_FF + hi])
            act = (a * _sigmoid(a) * u).astype(BF16)
            acc = acc + _dot(act, w2_ref[lo:hi, :])
        y = x + FFN_RES * gate * _rms(acc, g_ref[2 * sub + 1:2 * sub + 2, :])
        if split_out:
            @pl.when(is_prompt)
            def _():
                out_refs[0][rows, :] = y

            @pl.when(jnp.logical_not(is_prompt))
            def _():
                out_refs[1][rows, :] = y
        else:
            out_refs[0][rows, :] = y


def _ffn(xs, mods, gains, w1, w2, layer, sub, feats=None, w_out=None, split_out=False, mixer_w=None):
    split_in = len(xs) == 2
    mix = feats is not None
    cast_mixers = mixer_w is not None
    resident = pl.Buffered(1)
    half_w = D_MODEL // 2
    feat_specs, mix_w_spec = [], []
    if mix:
        feat_p = pl.BlockSpec((TM_FFN, half_w), lambda i: (_prompt_tile(i, TM_FFN), 0))
        feat_s = pl.BlockSpec((TM_FFN, half_w), lambda i: (_sample_tile(i, TM_FFN), 0))
        feat_specs = [feat_p, feat_p, feat_s, feat_s]
        mix_w_spec = [pl.BlockSpec((D_MODEL, D_MODEL), lambda i: (0, 0), pipeline_mode=resident)]
    cast_in, cast_out, cast_shape, cast_args = [], [], [], ()
    if cast_mixers:
        cast_in, cast_out, cast_shape = _mixer_cast_specs()
        cast_args = (mixer_w[0], mixer_w[1], mixer_w[2], mixer_w[3], mixer_w[3])
    tok = pl.BlockSpec((TM_FFN, D_MODEL), lambda i: (i, 0))
    prompt_tok = pl.BlockSpec((TM_FFN, D_MODEL), lambda i: (_prompt_tile(i, TM_FFN), 0))
    sample_tok = pl.BlockSpec((TM_FFN, D_MODEL), lambda i: (_sample_tile(i, TM_FFN), 0))
    if split_out:
        out_specs = [prompt_tok, sample_tok]
        out_shape = [jax.ShapeDtypeStruct((N_PROMPT, D_MODEL), F32),
                     jax.ShapeDtypeStruct((N_SAMPLE, D_MODEL), F32)]
    else:
        out_specs = [tok]
        out_shape = [jax.ShapeDtypeStruct((N_TOK, D_MODEL), F32)]
    return pl.pallas_call(
        functools.partial(_ffn_kernel, sub=sub, split_in=split_in, mix=mix, split_out=split_out,
                          cast_mixers=cast_mixers),
        grid=(N_TOK // TM_FFN,),
        in_specs=([prompt_tok, sample_tok] if split_in else [tok]) + feat_specs + [
            _mod_spec(layer, TM_FFN),
            _gain_spec(layer),
        ] + mix_w_spec + [
            pl.BlockSpec((D_MODEL, 2 * D_FF), lambda i: (0, 0), pipeline_mode=resident),
            pl.BlockSpec((D_FF, D_MODEL), lambda i: (0, 0), pipeline_mode=resident),
        ] + cast_in,
        out_specs=out_specs + cast_out,
        out_shape=out_shape + cast_shape,
        compiler_params=_params(1),
        name=f"ffn{sub}",
    )(*xs, *(feats or ()), mods, gains, *((w_out,) if mix else ()), w1, w2, *cast_args)


def _mixer_norm(x_ref, m_ref, g_ref):
    return (_rms(x_ref[...], g_ref[2:3, :]) * (1.0 + m_ref[4:5, :]) + m_ref[3:4, :]).astype(BF16)


def _store_heads(state_ref, x, n_heads):
    for b in range(SEQ_PER_TILE):
        xt = x[b * SEQ:(b + 1) * SEQ, :].T
        for h in range(n_heads):
            state_ref[b, 0, h] = xt[h * HEAD_DIM:(h + 1) * HEAD_DIM, :]


def _even_in_kernel(x_ref, m_ref, g_ref, w_ref, w1f_ref, w2f_ref,
                    q_ref, k_ref, v_ref, bg_ref, z_ref, sk_ref, sv_ref, w1b_ref, w2b_ref):
    _cast_weights(w1f_ref, w2f_ref, w1b_ref, w2b_ref)
    u = _dot(_mixer_norm(x_ref, m_ref, g_ref), w_ref[...])
    k = u[:, HD_A:2 * HD_A]
    v = u[:, 2 * HD_A:3 * HD_A]
    q_ref[...] = (u[:, 0:HD_A] * ATT_SCALE).astype(BF16)
    k_ref[...] = k.astype(BF16)
    v_ref[...] = v.astype(BF16)
    bg_ref[...] = u[:, 3 * HD_A:3 * HD_A + C_B]
    z_ref[...] = u[:, 3 * HD_A + C_B:3 * HD_A + 2 * C_B] * u[:, 3 * HD_A + 2 * C_B:3 * HD_A + 3 * C_B]

    @pl.when(pl.program_id(0) < PROMPT_TILES)
    def _():
        _store_heads(sk_ref, k, H_A)
        _store_heads(sv_ref, v, H_A)


def _state_spec(n_heads):
    return pl.BlockSpec((SEQ_PER_TILE, 1, n_heads, HEAD_DIM, SEQ), lambda i: (_prompt_tile(i), 0, 0, 0, 0))


def _even_in(x, mods, gains, w_in, ffn_w1, ffn_w2, layer, idx):
    tok = lambda w: pl.BlockSpec((TM, w), lambda i: (i, 0))
    state = jax.ShapeDtypeStruct((BATCH, 1, H_A, HEAD_DIM, SEQ), F32)
    cast_in, cast_out, cast_shape = _cast_specs(layer, 1)
    return pl.pallas_call(
        _even_in_kernel,
        grid=(N_TILES,),
        in_specs=[
            tok(D_MODEL), _mod_spec(layer), _gain_spec(layer),
            pl.BlockSpec((D_MODEL, EVEN_IN), lambda i: (0, 0)),
        ] + cast_in,
        out_specs=[tok(HD_A), tok(HD_A), tok(HD_A), tok(C_B), tok(C_B), _state_spec(H_A), _state_spec(H_A)]
        + cast_out,
        out_shape=[
            jax.ShapeDtypeStruct((N_TOK, HD_A), BF16),
            jax.ShapeDtypeStruct((N_TOK, HD_A), BF16),
            jax.ShapeDtypeStruct((N_TOK, HD_A), BF16),
            jax.ShapeDtypeStruct((N_TOK, C_B), F32),
            jax.ShapeDtypeStruct((N_TOK, C_B), F32),
            state, state,
        ] + cast_shape,
        compiler_params=_params(1),
        name="even_in",
    )(x, mods, gains, w_in, ffn_w1, ffn_w2)


def _swap_pairs(x):
    n = x.shape[-1]
    lane = lax.broadcasted_iota(jnp.int32, x.shape, x.ndim - 1)
    return jnp.where(lane % 2 == 0, pltpu.roll(x, n - 1, x.ndim - 1), pltpu.roll(x, 1, x.ndim - 1))


def _odd_in_kernel(x_ref, m_ref, g_ref, w_ref, ng_ref, cos_ref, sin_ref, ones_ref, w1f_ref, w2f_ref,
                   uc_ref, q_ref, k_ref, v_ref, sk_ref, sv_ref, w1b_ref, w2b_ref):
    _cast_weights(w1f_ref, w2f_ref, w1b_ref, w2b_ref)
    u = _dot(_mixer_norm(x_ref, m_ref, g_ref), w_ref[...])
    uc_ref[...] = u[:, 0:C_POOL]
    qk = u[:, C_POOL:C_POOL + QK_W]
    sq = qk * qk
    hi = sq.astype(BF16)
    lo = (sq - hi.astype(F32)).astype(BF16)
    ones = ones_ref[...]
    sums = []
    for c0 in range(0, QK_W, MXU_DIM):
        c1 = min(c0 + MXU_DIM, QK_W)
        sums.append(_dot(hi[:, c0:c1], ones[0:c1 - c0, 0:c1 - c0]) + _dot(lo[:, c0:c1], ones[0:c1 - c0, 0:c1 - c0]))
    ms = jnp.concatenate(sums, axis=-1) * (1.0 / HEAD_DIM)
    n = qk * lax.rsqrt(ms + RMS_EPS) * ng_ref[...]
    r = n * cos_ref[...] + _swap_pairs(n) * sin_ref[...]
    k = r[:, HD_Q:QK_W]
    v = u[:, C_POOL + QK_W:ODD_IN]
    q_ref[...] = (r[:, 0:HD_Q] * ATT_SCALE).astype(BF16)
    k_ref[...] = k.astype(BF16)
    v_ref[...] = v.astype(BF16)

    @pl.when(pl.program_id(0) < PROMPT_TILES)
    def _():
        _store_heads(sk_ref, k, HKV_D)
        _store_heads(sv_ref, v, HKV_D)


def _rope_tile_index(i):
    return jnp.where(i < PROMPT_TILES, 0, 1 + (i - PROMPT_TILES) % TILES_PER_SAMPLE)


def _odd_in(x, mods, gains, w_in, qk_gain, cos_t, sin_t, ones_bd, ffn_w1, ffn_w2, layer, idx):
    tok = lambda w: pl.BlockSpec((TM, w), lambda i: (i, 0))
    state = jax.ShapeDtypeStruct((BATCH, 1, HKV_D, HEAD_DIM, SEQ), F32)
    cast_in, cast_out, cast_shape = _cast_specs(layer, 1)
    return pl.pallas_call(
        _odd_in_kernel,
        grid=(N_TILES,),
        in_specs=[
            tok(D_MODEL), _mod_spec(layer), _gain_spec(layer),
            pl.BlockSpec((D_MODEL, ODD_IN), lambda i: (0, 0)),
            pl.BlockSpec((1, QK_W), lambda i: (0, 0)),
            pl.BlockSpec((TM, QK_W), lambda i: (_rope_tile_index(i), 0)),
            pl.BlockSpec((TM, QK_W), lambda i: (_rope_tile_index(i), 0)),
            pl.BlockSpec((MXU_DIM, MXU_DIM), lambda i: (0, 0)),
        ] + cast_in,
        out_specs=[tok(C_POOL), tok(HD_Q), tok(HD_KV), tok(HD_KV), _state_spec(HKV_D), _state_spec(HKV_D)]
        + cast_out,
        out_shape=[
            jax.ShapeDtypeStruct((N_TOK, C_POOL), F32),
            jax.ShapeDtypeStruct((N_TOK, HD_Q), BF16),
            jax.ShapeDtypeStruct((N_TOK, HD_KV), BF16),
            jax.ShapeDtypeStruct((N_TOK, HD_KV), BF16),
            state, state,
        ] + cast_shape,
        compiler_params=_params(1),
        name="odd_in",
    )(x, mods, gains, w_in, qk_gain, cos_t, sin_t, ones_bd, ffn_w1, ffn_w2)


def _low_half(rows):
    return lax.broadcasted_iota(jnp.int32, (rows, LANES), 1) < HEAD_DIM


def _split_pair(qp):
    low = _low_half(qp.shape[0])
    zero = jnp.zeros_like(qp)
    return jnp.concatenate([jnp.where(low, qp, zero), jnp.where(low, zero, qp)], axis=0)


def _pair_heads_t(ref):
    return jnp.concatenate([ref[0], ref[1]], axis=0)


def _with_ones(v):
    return jnp.concatenate([v, jnp.ones_like(v)], axis=1)


def _softmax_pv(scores, values):
    m = scores[0].max(axis=-1, keepdims=True)
    for s in scores[1:]:
        m = jnp.maximum(m, s.max(axis=-1, keepdims=True))
    acc = None
    for s, v in zip(scores, values):
        o = _dot(jnp.exp(s - m).astype(BF16), v)
        acc = o if acc is None else acc + o
    return acc[:, 0:LANES] / acc[:, LANES:2 * LANES]


def _merge_pair(o):
    m = o.shape[0] // 2
    return jnp.where(_low_half(m), o[0:m], o[m:2 * m])


def _short_conv(z, cw_ref, cb_ref, seq_len):
    rows = z.shape[0]
    pos = lax.broadcasted_iota(jnp.int32, z.shape, 0) & (seq_len - 1)
    z_prev = jnp.where(pos == 0, 0.0, pltpu.roll(z, 1, 0))
    z_next = jnp.where(pos == seq_len - 1, 0.0, pltpu.roll(z, rows - 1, 0))
    y = z_prev * cw_ref[0:1, :]
    y = y + z * cw_ref[1:2, :]
    y = y + z_next * cw_ref[2:3, :]
    return y + cb_ref[...]


def _pool_mix(uc, pw_ref, ps_ref, seq_len):
    pos = lax.broadcasted_iota(jnp.int32, (seq_len, POOL_C), 0)
    pad = jnp.zeros((POOL_PAD, POOL_C), F32)
    n_ext = seq_len + 2 * POOL_PAD
    outs = []
    for gi, win in enumerate(POOL_WINDOWS):
        ug = uc[:, gi * POOL_C:(gi + 1) * POOL_C]
        run = jnp.concatenate([pad, ug, pad], axis=0)
        span = 1
        while span < win:
            run = run + pltpu.roll(run, span, 0)
            span *= 2
        back = win // 2 - 1
        if back:
            run = pltpu.roll(run, n_ext - back, 0)
        wsum = run[POOL_PAD:POOL_PAD + seq_len, :]
        cnt = jnp.minimum(pos + (win - win // 2), seq_len) - jnp.maximum(pos - win // 2, 0)
        pooled = (wsum / cnt.astype(F32) - ug).astype(BF16)
        outs.append(_dot(pooled, pw_ref[gi]))
    return jnp.concatenate(outs, axis=-1) * ps_ref[...]


def _even_prompt_kernel(*refs, cast):
    q_ref, k_ref, v_ref, bg_ref, z_ref, cw_ref, cb_ref = refs[0:7]
    fa_ref, fb_ref = refs[7 + 2 * cast:9 + 2 * cast]
    if cast:
        _cast_weights(refs[7], refs[8], refs[11], refs[12])
    for b in range(SEQ_PER_TILE):
        rows = slice(b * SEQ, (b + 1) * SEQ)
        outs = []
        for p in range(H_A // 2):
            sl = slice(p * LANES, (p + 1) * LANES)
            s = _dot_t(_split_pair(q_ref[rows, sl]), k_ref[rows, sl])
            outs.append(_merge_pair(_softmax_pv([s], [_with_ones(v_ref[rows, sl])])))
        fa_ref[rows, :] = jnp.concatenate(outs, axis=-1).astype(BF16)
    fb_ref[...] = (bg_ref[...] * _short_conv(z_ref[...], cw_ref, cb_ref, SEQ)).astype(BF16)


def _even_prompt(q, k, v, bg, z, conv_w, conv_b, idx, cast_of=None):
    tile = lambda w: pl.BlockSpec((TM, w), lambda t: (t, 0))
    cast_in, cast_out, cast_shape, cast_args = [], [], [], ()
    if cast_of is not None:
        cast_in, cast_out, cast_shape = _cast_specs(cast_of[2], 0)
        cast_args = cast_of[0:2]
    return pl.pallas_call(
        functools.partial(_even_prompt_kernel, cast=cast_of is not None),
        grid=(PROMPT_TILES,),
        in_specs=[tile(HD_A), tile(HD_A), tile(HD_A), tile(C_B), tile(C_B),
                  pl.BlockSpec((None, CONV_W, C_B), lambda t: (idx, 0, 0)),
                  pl.BlockSpec((None, 1, C_B), lambda t: (idx, 0, 0))] + cast_in,
        out_specs=[tile(HD_A), tile(C_B)] + cast_out,
        out_shape=[
            jax.ShapeDtypeStruct((N_PROMPT, HD_A), BF16),
            jax.ShapeDtypeStruct((N_PROMPT, C_B), BF16),
        ] + cast_shape,
        compiler_params=_params(1),
        name="even_prompt",
    )(q, k, v, bg, z, conv_w, conv_b, *cast_args)


def _na_bias_tiles(rpb_h):
    c = lax.broadcasted_iota(jnp.int32, (GRID_W, GRID_W), 0)
    kc = lax.broadcasted_iota(jnp.int32, (GRID_W, GRID_W), 1)
    cs = jnp.clip(c - NA_COLS // 2, 0, GRID_W - NA_COLS)
    valid = (kc >= cs) & (kc < cs + NA_COLS)
    tiles = []
    for ro in range(2 * NA_ROWS - 1):
        g = jnp.broadcast_to(rpb_h[ro:ro + 1, :], (GRID_W, LANES))
        skew = pltpu.roll(g, LANES - (NA_COLS - 1), 1, stride=1, stride_axis=0)
        tiles.append(jnp.where(valid, skew[:, 0:GRID_W], NEG_INF))
    return tiles


def _even_sample_kernel(q_ref, k_ref, v_ref, bg_ref, z_ref, ck_ref, cv_ref, rpb_ref, cw_ref, cb_ref,
                        fa_ref, fb_ref, bias_ref):
    win = NA_ROWS * GRID_W
    for h in range(2):
        tiles = _na_bias_tiles(rpb_ref[h])
        for var in range(NA_ROWS):
            bias_ref[var, h * GRID_W:(h + 1) * GRID_W, :] = jnp.concatenate(
                [tiles[i - var + NA_ROWS - 1] for i in range(NA_ROWS)], axis=1)
    ck_t = _pair_heads_t(ck_ref).astype(BF16)
    cv_ext = _with_ones(_pair_heads_t(cv_ref).T.astype(BF16))

    def rows(it, carry):
        for j in range(ROW_UNROLL):
            r = it * ROW_UNROLL + j
            rs = jnp.clip(r - NA_ROWS // 2, 0, GRID_ROWS - NA_ROWS)
            q0 = pl.multiple_of(r * GRID_W, GRID_W)
            k0 = pl.multiple_of(rs * GRID_W, GRID_W)
            q2 = _split_pair(q_ref[pl.ds(q0, GRID_W), :])
            s_loc = _dot_t(q2, k_ref[pl.ds(k0, win), :]) + bias_ref[r - rs]
            s_ctx = _dot(q2, ck_t)
            o = _softmax_pv([s_loc, s_ctx], [_with_ones(v_ref[pl.ds(k0, win), :]), cv_ext])
            fa_ref[pl.ds(q0, GRID_W), :] = _merge_pair(o).astype(BF16)
        return carry

    lax.fori_loop(0, GRID_ROWS // ROW_UNROLL, rows, 0)
    fb_ref[...] = (bg_ref[...] * _short_conv(z_ref[...], cw_ref, cb_ref, DEC_SEQ)).astype(BF16)


def _even_sample(q, k, v, bg, z, ctx_k, ctx_v, rpb, conv_w, conv_b, idx):
    first = N_PROMPT // DEC_SEQ
    seq_in = pl.BlockSpec((DEC_SEQ, LANES), lambda b, p: (first + b, p))
    seq_out = pl.BlockSpec((DEC_SEQ, LANES), lambda b, p: (b, p))
    ctx = pl.BlockSpec((None, None, 2, HEAD_DIM, PAST_LEN), lambda b, p: (b, idx, p, 0, 0))
    return pl.pallas_call(
        _even_sample_kernel,
        grid=(DEC_BATCH, H_A // 2),
        in_specs=[seq_in, seq_in, seq_in, seq_in, seq_in, ctx, ctx,
                  pl.BlockSpec((None, 2, 2 * NA_ROWS - 1, LANES), lambda b, p: (idx, p, 0, 0)),
                  pl.BlockSpec((None, CONV_W, LANES), lambda b, p: (idx, 0, p)),
                  pl.BlockSpec((None, 1, LANES), lambda b, p: (idx, 0, p))],
        out_specs=[seq_out, seq_out],
        out_shape=[
            jax.ShapeDtypeStruct((N_SAMPLE, HD_A), BF16),
            jax.ShapeDtypeStruct((N_SAMPLE, C_B), BF16),
        ],
        scratch_shapes=[pltpu.VMEM((NA_ROWS, 2 * GRID_W, NA_ROWS * GRID_W), F32)],
        compiler_params=_params(2),
        name="even_sample",
    )(q, k, v, bg, z, ctx_k, ctx_v, rpb, conv_w, conv_b)


def _split_groups(q):
    low = _low_half(q.shape[0])
    zero = jnp.zeros((q.shape[0], LANES), BF16)
    pairs = [q[:, j * LANES:(j + 1) * LANES] for j in range(GQA_GROUP)]
    return jnp.concatenate([jnp.where(low, p, zero) for p in pairs] + [jnp.where(low, zero, p) for p in pairs],
                           axis=0)


def _merge_groups(o, m):
    low = _low_half(m)
    half = GQA_GROUP * m
    return jnp.concatenate([jnp.where(low, o[j * m:(j + 1) * m], o[half + j * m:half + (j + 1) * m])
                            for j in range(GQA_GROUP)], axis=-1)


def _odd_prompt_kernel(uc_ref, q_ref, k_ref, v_ref, pw_ref, ps_ref, fc_ref, fd_ref):
    for b in range(SEQ_PER_TILE):
        rows = slice(b * SEQ, (b + 1) * SEQ)
        s = _dot_t(_split_groups(q_ref[rows, :]), k_ref[rows, :])
        o = _softmax_pv([s], [_with_ones(v_ref[rows, :])])
        fd_ref[rows, :] = _merge_groups(o, SEQ).astype(BF16)
        fc_ref[rows, :] = _pool_mix(uc_ref[rows, :], pw_ref, ps_ref, SEQ).astype(BF16)


def _odd_prompt(uc, q, k, v, pool_w, pool_scale, idx):
    seq = lambda w: pl.BlockSpec((TM, w), lambda t: (t, 0))
    return pl.pallas_call(
        _odd_prompt_kernel,
        grid=(PROMPT_TILES,),
        in_specs=[seq(C_POOL), seq(HD_Q), seq(HD_KV), seq(HD_KV),
                  pl.BlockSpec((None, N_POOL, POOL_C, POOL_C), lambda t: (idx, 0, 0, 0)),
                  pl.BlockSpec((None, 1, C_POOL), lambda t: (idx, 0, 0))],
        out_specs=[seq(C_POOL), seq(HD_Q)],
        out_shape=[
            jax.ShapeDtypeStruct((N_PROMPT, C_POOL), BF16),
            jax.ShapeDtypeStruct((N_PROMPT, HD_Q), BF16),
        ],
        compiler_params=_params(1),
        name="odd_prompt",
    )(uc, q, k, v, pool_w, pool_scale)


def _odd_sample_kernel(uc_ref, q_ref, k_ref, v_ref, ck_ref, cv_ref, pw_ref, ps_ref, fc_ref, fd_ref,
                       kt_ref, vext_ref, ckpair_ref, cvext_ref):
    @pl.when(pl.program_id(1) == 0)
    def _():
        fc_ref[...] = _pool_mix(uc_ref[...], pw_ref, ps_ref, DEC_SEQ).astype(BF16)
        kt_ref[...] = k_ref[...].astype(F32).T.astype(BF16)
        vext_ref[...] = _with_ones(v_ref[...])
        ckpair_ref[...] = _pair_heads_t(ck_ref).astype(BF16)
        cvext_ref[...] = _with_ones(_pair_heads_t(cv_ref).T.astype(BF16))

    k_t = kt_ref[...]
    ck_t = ckpair_ref[...]
    low = _low_half(Q_BLK)
    outs = []
    for j in range(GQA_GROUP):
        q2 = _split_pair(q_ref[:, j * LANES:(j + 1) * LANES])
        o = _softmax_pv([_dot(q2, k_t), _dot(q2, ck_t)], [vext_ref[...], cvext_ref[...]])
        outs.append(jnp.where(low, o[0:Q_BLK], o[Q_BLK:2 * Q_BLK]))
    fd_ref[...] = jnp.concatenate(outs, axis=-1).astype(BF16)


def _odd_sample(uc, q, k, v, ctx_k, ctx_v, pool_w, pool_scale, idx):
    first = N_PROMPT // DEC_SEQ
    n_qb = DEC_SEQ // Q_BLK
    first_q = N_PROMPT // Q_BLK
    whole_in = lambda w: pl.BlockSpec((DEC_SEQ, w), lambda b, i: (first + b, 0))
    ctx = pl.BlockSpec((None, None, HKV_D, HEAD_DIM, PAST_LEN), lambda b, i: (b, idx, 0, 0, 0))
    return pl.pallas_call(
        _odd_sample_kernel,
        grid=(DEC_BATCH, n_qb),
        in_specs=[whole_in(C_POOL),
                  pl.BlockSpec((Q_BLK, HD_Q), lambda b, i: (first_q + b * n_qb + i, 0)),
                  whole_in(HD_KV), whole_in(HD_KV), ctx, ctx,
                  pl.BlockSpec((None, N_POOL, POOL_C, POOL_C), lambda b, i: (idx, 0, 0, 0)),
                  pl.BlockSpec((None, 1, C_POOL), lambda b, i: (idx, 0, 0))],
        out_specs=[pl.BlockSpec((DEC_SEQ, C_POOL), lambda b, i: (b, 0)),
                   pl.BlockSpec((Q_BLK, HD_Q), lambda b, i: (b * n_qb + i, 0))],
        out_shape=[
            jax.ShapeDtypeStruct((N_SAMPLE, C_POOL), BF16),
            jax.ShapeDtypeStruct((N_SAMPLE, HD_Q), BF16),
        ],
        scratch_shapes=[pltpu.VMEM((LANES, DEC_SEQ), BF16), pltpu.VMEM((DEC_SEQ, 2 * LANES), BF16),
                        pltpu.VMEM((LANES, PAST_LEN), BF16), pltpu.VMEM((PAST_LEN, 2 * LANES), BF16)],
        compiler_params=_params(2),
        name="odd_sample",
    )(uc, q, k, v, ctx_k, ctx_v, pool_w, pool_scale)


def _rope_tables():
    t = np.arange(DEC_SEQ)
    n_freq = HEAD_DIM // 4
    inv = ROPE_BASE ** (-np.arange(n_freq, dtype=np.float64) / n_freq)
    ang = np.concatenate([(t // GRID_W)[:, None] * inv, (t % GRID_W)[:, None] * inv], axis=-1)
    cos = np.repeat(np.cos(ang), 2, axis=-1)
    sin = np.repeat(np.sin(ang), 2, axis=-1) * np.tile([-1.0, 1.0], HEAD_DIM // 2)
    n_heads = HQ_D + HKV_D
    cos = np.concatenate([np.ones((TM, QK_W)), np.tile(cos, (1, n_heads))], axis=0)
    sin = np.concatenate([np.zeros((TM, QK_W)), np.tile(sin, (1, n_heads))], axis=0)
    return jnp.asarray(cos, F32), jnp.asarray(sin, F32)


def _head_ones():
    head = np.arange(MXU_DIM) // HEAD_DIM
    return jnp.asarray(head[:, None] == head[None, :], BF16)


def kernel(x_prompt, x_sample, cache_a_k, cache_a_v, cache_d_k, cache_d_v, c, c_ctx, mod_w, mod_b, norm_w,
           ffn_w1, ffn_w2, ev_w_in, ev_rpb, ev_conv_w, ev_conv_b, ev_w_out, od_w_in, od_pool_w,
           od_pool_scale, od_q_norm, od_k_norm, od_w_out):
    xs = (x_prompt.reshape(N_PROMPT, D_MODEL), x_sample.reshape(N_SAMPLE, D_MODEL))
    cond = jnp.concatenate([c_ctx[None, :], c, jnp.zeros((COND_PAD - N_COND, D_MODEL), F32)], axis=0)
    mods = _modulation(cond, mod_w, mod_b).reshape(DEPTH, COND_PAD, N_MOD, D_MODEL)
    cache_a_k, cache_a_v, cache_d_k, cache_d_v = (jnp.swapaxes(t, -1, -2)
                                                  for t in (cache_a_k, cache_a_v, cache_d_k, cache_d_v))
    rpb = jnp.pad(ev_rpb, ((0, 0), (0, 0), (0, 0), (0, LANES - ev_rpb.shape[-1])))
    conv_b = ev_conv_b[:, None, :]
    pool_w, pool_scale = od_pool_w.astype(BF16), od_pool_scale[:, None, :]
    cos_t, sin_t = _rope_tables()
    ones_bd = _head_ones()
    states = []
    w1_a, w2_a = ffn_w1[0, 0].astype(BF16), ffn_w2[0, 0].astype(BF16)
    mixer_w = None
    for l in range(DEPTH):
        i = l // 2
        last = l == DEPTH - 1
        if l == 0:
            x, *mixer_w = _ffn(xs, mods, norm_w, w1_a, w2_a, l, 0, mixer_w=(ev_w_in, ev_w_out, od_w_in, od_w_out))
        else:
            x, = _ffn(xs, mods, norm_w, w1_a, w2_a, l, 0)
        ev_w_in_b, ev_w_out_b, od_w_in_b, od_w_out_b = mixer_w
        if l % 2 == 0:
            q, k, v, bg, z, s_k, s_v, w1_b, w2_b = _even_in(x, mods, norm_w, ev_w_in_b, ffn_w1, ffn_w2, l, i)
            prompt_out = _even_prompt(q, k, v, bg, z, ev_conv_w, conv_b, i,
                                      cast_of=None if last else (ffn_w1, ffn_w2, l + 1))
            fa_p, fb_p = prompt_out[0:2]
            if not last:
                w1_a, w2_a = prompt_out[2:4]
            fa_s, fb_s = _even_sample(q, k, v, bg, z, cache_a_k, cache_a_v, rpb, ev_conv_w, conv_b, i)
            w_out = ev_w_out_b
        else:
            gain = jnp.concatenate([jnp.tile(od_q_norm[i], HQ_D), jnp.tile(od_k_norm[i], HKV_D)])[None, :]
            uc, q, k, v, s_k, s_v, w1_b, w2_b = _odd_in(x, mods, norm_w, od_w_in_b, gain, cos_t, sin_t, ones_bd,
                                                        ffn_w1, ffn_w2, l, i)
            fa_p, fb_p = _odd_prompt(uc, q, k, v, pool_w, pool_scale, i)
            fa_s, fb_s = _odd_sample(uc, q, k, v, cache_d_k, cache_d_v, pool_w, pool_scale, i)
            w_out = od_w_out_b
            if not last:
                w1_a, w2_a = ffn_w1[l + 1, 0].astype(BF16), ffn_w2[l + 1, 0].astype(BF16)
        states.append((jnp.swapaxes(s_k, -1, -2), jnp.swapaxes(s_v, -1, -2)))
        xs = tuple(_ffn((x,), mods, norm_w, w1_b, w2_b, l, 2, feats=(fa_p, fb_p, fa_s, fb_s), w_out=w_out,
                        split_out=last))
    y_prompt = xs[0].reshape(BATCH, SEQ, D_MODEL)
    y_sample = xs[1].reshape(DEC_BATCH, DEC_SEQ, D_MODEL)
    return (y_prompt, y_sample, states[0][0], states[0][1], states[1][0], states[1][1])
```

```python
import functools

import jax
import jax.numpy as jnp
import numpy as np
from jax import lax
from jax.experimental import pallas as pl
from jax.experimental.pallas import tpu as pltpu

D_MODEL = 1024
BATCH = 32
SEQ = 256
DEPTH = 2
DEC_BATCH = 2
DEC_SEQ = 2048
PAST_LEN = 256
GRID_W = 64
HEAD_DIM = 64
N_MOD = 9
D_FF = 2816
FFN_RES = 0.5
H_A = 8
NA_ROWS = 8
NA_COLS = 16
C_B = 512
CONV_W = 3
C_POOL = 512
POOL_WINDOWS = (2, 4, 8, 16)
N_POOL = 4
POOL_C = C_POOL // N_POOL
HQ_D = 8
HKV_D = 2
GQA_GROUP = HQ_D // HKV_D
ROPE_BASE = 10000.0
EVEN_IN = 3 * H_A * HEAD_DIM + 3 * C_B
ODD_IN = C_POOL + (HQ_D + 2 * HKV_D) * HEAD_DIM
RMS_EPS = 1e-6
NEG_INF = -1e30
ATT_SCALE = HEAD_DIM ** -0.5

LANES = 128
GRID_ROWS = DEC_SEQ // GRID_W
N_PROMPT = BATCH * SEQ
N_SAMPLE = DEC_BATCH * DEC_SEQ
N_TOK = N_PROMPT + N_SAMPLE
N_COND = 1 + DEC_BATCH
COND_PAD = 8
HD_A = H_A * HEAD_DIM
HD_Q = HQ_D * HEAD_DIM
HD_KV = HKV_D * HEAD_DIM
QK_W = HD_Q + HD_KV
POOL_PAD = 8

TM = 512
N_TILES = N_TOK // TM
PROMPT_TILES = N_PROMPT // TM
TILES_PER_SAMPLE = DEC_SEQ // TM
SEQ_PER_TILE = TM // SEQ
TM_FFN = 1024
FFN_HALF = TM_FFN // 2
MIX_CAST_STEPS = N_PROMPT // TM_FFN
MIX_CAST_ROWS = D_MODEL // MIX_CAST_STEPS
assert MIX_CAST_ROWS == 2 * HEAD_DIM and (DEPTH + 1) // 2 == 1 and DEPTH // 2 == 1
MXU_DIM = 256
FF_CHUNKS = tuple((lo, min(lo + 3 * MXU_DIM, D_FF)) for lo in range(0, D_FF, 3 * MXU_DIM))
MOD_TN = 1536
Q_BLK = 512
ROW_UNROLL = 4
VMEM_LIMIT = 60 * 1024 * 1024

F32 = jnp.float32
BF16 = jnp.bfloat16


def _params(n_grid):
    return pltpu.CompilerParams(dimension_semantics=("arbitrary",) * n_grid, vmem_limit_bytes=VMEM_LIMIT)


def _cond_of_tile(i, tm=TM):
    n_prompt = N_PROMPT // tm
    return jnp.where(i < n_prompt, 0, 1 + (i - n_prompt) // (DEC_SEQ // tm))


def _prompt_tile(i, tm=TM):
    return jnp.minimum(i, N_PROMPT // tm - 1)


def _sample_tile(i, tm=TM):
    return jnp.maximum(i - N_PROMPT // tm, 0)


def _rms(x, g):
    return x * lax.rsqrt(jnp.mean(x * x, axis=-1, keepdims=True) + RMS_EPS) * g


def _sigmoid(x):
    return 1.0 / (1.0 + jnp.exp(-x))


def _dot(a, b):
    return jnp.dot(a, b, preferred_element_type=F32)


def _dot_t(a, b):
    return lax.dot_general(a, b, (((1,), (1,)), ((), ())), preferred_element_type=F32)


def _mod_kernel(c_ref, w_ref, b_ref, o_ref):
    c = c_ref[...]
    sc = (c * _sigmoid(c)).astype(BF16)
    o_ref[...] = _dot(sc, w_ref[...].astype(BF16)) + b_ref[...]


def _modulation(cond, mod_w, mod_b):
    n_col = N_MOD * D_MODEL
    return pl.pallas_call(
        _mod_kernel,
        grid=(DEPTH, n_col // MOD_TN),
        in_specs=[
            pl.BlockSpec((COND_PAD, D_MODEL), lambda l, j: (0, 0)),
            pl.BlockSpec((None, D_MODEL, MOD_TN), lambda l, j: (l, 0, j)),
            pl.BlockSpec((None, 1, MOD_TN), lambda l, j: (l, 0, j)),
        ],
        out_specs=pl.BlockSpec((None, COND_PAD, MOD_TN), lambda l, j: (l, 0, j)),
        out_shape=jax.ShapeDtypeStruct((DEPTH, COND_PAD, n_col), F32),
        compiler_params=_params(2),
        name="modulation",
    )(cond, mod_w, mod_b.reshape(DEPTH, 1, n_col))


def _mod_spec(layer, tm=TM):
    return pl.BlockSpec((None, None, N_MOD, D_MODEL), lambda i: (layer, _cond_of_tile(i, tm), 0, 0))


def _gain_spec(layer):
    return pl.BlockSpec((None, 6, D_MODEL), lambda i: (layer, 0, 0))


def _cast_specs(jobs, n_steps, step_of):
    r1, r2 = D_MODEL // n_steps, D_FF // n_steps
    in_specs, out_specs, out_shape = [], [], []
    for layer, which in jobs:
        in_specs += [pl.BlockSpec((None, None, r1, 2 * D_FF), lambda *g, lw=(layer, which): (*lw, step_of(*g), 0)),
                     pl.BlockSpec((None, None, r2, D_MODEL), lambda *g, lw=(layer, which): (*lw, step_of(*g), 0))]
        out_specs += [pl.BlockSpec((r1, 2 * D_FF), lambda *g: (step_of(*g), 0)),
                      pl.BlockSpec((r2, D_MODEL), lambda *g: (step_of(*g), 0))]
        out_shape += [jax.ShapeDtypeStruct((D_MODEL, 2 * D_FF), BF16), jax.ShapeDtypeStruct((D_FF, D_MODEL), BF16)]
    return in_specs, out_specs, out_shape


def _cast_weights(f32_refs, bf16_refs):
    for src, dst in zip(f32_refs, bf16_refs):
        dst[...] = src[...].astype(BF16)


def _mixer_cast_specs():
    rows = MIX_CAST_ROWS
    n_pool_blk = C_POOL // rows
    step = lambda i: jnp.minimum(i, MIX_CAST_STEPS - 1)

    def od_out_block(i, g):
        s = step(i)
        return jnp.where(s < n_pool_blk, 2 * s + g, C_POOL // HEAD_DIM + g * GQA_GROUP + s - n_pool_blk)

    widths = (EVEN_IN, D_MODEL, ODD_IN, D_MODEL)
    in_specs = [pl.BlockSpec((None, rows, EVEN_IN), lambda i: (0, step(i), 0)),
                pl.BlockSpec((None, rows, D_MODEL), lambda i: (0, step(i), 0)),
                pl.BlockSpec((None, rows, ODD_IN), lambda i: (0, step(i), 0)),
                pl.BlockSpec((None, HEAD_DIM, D_MODEL), lambda i: (0, od_out_block(i, 0), 0)),
                pl.BlockSpec((None, HEAD_DIM, D_MODEL), lambda i: (0, od_out_block(i, 1), 0))]
    out_specs = [pl.BlockSpec((rows, w), lambda i: (step(i), 0)) for w in widths]
    out_shape = [jax.ShapeDtypeStruct((D_MODEL, w), BF16) for w in widths]
    return in_specs, out_specs, out_shape


def _cast_mixer_weights(ev_in_f, ev_out_f, od_in_f, od_out_lo_f, od_out_hi_f, ev_in_b, ev_out_b, od_in_b, od_out_b):
    @pl.when(pl.program_id(0) < MIX_CAST_STEPS)
    def _():
        ev_in_b[...] = ev_in_f[...].astype(BF16)
        ev_out_b[...] = ev_out_f[...].astype(BF16)
        w = od_in_f[...]
        q = w[:, C_POOL:C_POOL + HD_Q]
        heads = [q[:, (g * GQA_GROUP + j) * HEAD_DIM:(g * GQA_GROUP + j + 1) * HEAD_DIM]
                 for j in range(GQA_GROUP) for g in range(HKV_D)]
        od_in_b[...] = jnp.concatenate([w[:, 0:C_POOL]] + heads + [w[:, C_POOL + HD_Q:]], axis=1).astype(BF16)
        od_out_b[...] = jnp.concatenate([od_out_lo_f[...], od_out_hi_f[...]], axis=0).astype(BF16)


def _ffn_kernel(*refs, sub, split_in, mix, split_out, cast_mixers):
    refs = list(refs)
    x_refs = [refs.pop(0) for _ in range(2 if split_in else 1)]
    feat_refs = [refs.pop(0) for _ in range(4 if mix else 0)]
    m_ref, g_ref = refs.pop(0), refs.pop(0)
    wo_ref = refs.pop(0) if mix else None
    w1_ref, w2_ref = refs.pop(0), refs.pop(0)
    if cast_mixers:
        _cast_mixer_weights(*refs[0:5], *refs[-4:])
        refs = refs[5:-4]
    out_refs = refs
    is_prompt = pl.program_id(0) < N_PROMPT // TM_FFN
    shift = m_ref[3 * sub:3 * sub + 1, :]
    scale = m_ref[3 * sub + 1:3 * sub + 2, :]
    gate = m_ref[3 * sub + 2:3 * sub + 3, :]
    for half in range(TM_FFN // FFN_HALF):
        rows = slice(half * FFN_HALF, (half + 1) * FFN_HALF)
        if split_in:
            x = jnp.where(is_prompt, x_refs[0][rows, :], x_refs[1][rows, :])
        else:
            x = x_refs[0][rows, :]
        if mix:
            fa = jnp.where(is_prompt, feat_refs[0][rows, :], feat_refs[2][rows, :])
            fb = jnp.where(is_prompt, feat_refs[1][rows, :], feat_refs[3][rows, :])
            n_a = fa.shape[1]
            mixed = _dot(fa, wo_ref[0:n_a, :]) + _dot(fb, wo_ref[n_a:, :])
            x = x + m_ref[5:6, :] * _rms(mixed, g_ref[3:4, :])
        h =(_rms(x, g_ref[2 * sub:2 * sub + 1, :]) * (1.0 + scale) + shift).astype(BF16)
        acc = jnp.zeros((FFN_HALF, D_MODEL), F32)
        for lo, hi in FF_CHUNKS:
            a = _dot(h, w1_ref[:, lo:hi])
            u = _dot(h, w1_ref[:, D_FF + lo:D_FF + hi])
            act = (a * _sigmoid(a) * u).astype(BF16)
            acc = acc + _dot(act, w2_ref[lo:hi, :])
        y = x + FFN_RES * gate * _rms(acc, g_ref[2 * sub + 1:2 * sub + 2, :])
        if split_out:
            @pl.when(is_prompt)
            def _():
                out_refs[0][rows, :] = y

            @pl.when(jnp.logical_not(is_prompt))
            def _():
                out_refs[1][rows, :] = y
        else:
            out_refs[0][rows, :] = y


def _ffn(xs, mods, gains, w1, w2, layer, sub, feats=None, w_out=None, split_out=False, mixer_w=None):
    split_in = len(xs) == 2
    mix = feats is not None
    cast_mixers = mixer_w is not None
    resident = pl.Buffered(1)
    half_w = D_MODEL // 2
    feat_specs, mix_w_spec = [], []
    if mix:
        feat_p = pl.BlockSpec((TM_FFN, half_w), lambda i: (_prompt_tile(i, TM_FFN), 0))
        feat_s = pl.BlockSpec((TM_FFN, half_w), lambda i: (_sample_tile(i, TM_FFN), 0))
        feat_specs = [feat_p, feat_p, feat_s, feat_s]
        mix_w_spec = [pl.BlockSpec((D_MODEL, D_MODEL), lambda i: (0, 0), pipeline_mode=resident)]
    cast_in, cast_out, cast_shape, cast_args = [], [], [], ()
    if cast_mixers:
        cast_in, cast_out, cast_shape = _mixer_cast_specs()
        cast_args = (mixer_w[0], mixer_w[1], mixer_w[2], mixer_w[3], mixer_w[3])
    tok = pl.BlockSpec((TM_FFN, D_MODEL), lambda i: (i, 0))
    prompt_tok = pl.BlockSpec((TM_FFN, D_MODEL), lambda i: (_prompt_tile(i, TM_FFN), 0))
    sample_tok = pl.BlockSpec((TM_FFN, D_MODEL), lambda i: (_sample_tile(i, TM_FFN), 0))
    if split_out:
        out_specs = [prompt_tok, sample_tok]
        out_shape = [jax.ShapeDtypeStruct((N_PROMPT, D_MODEL), F32),
                     jax.ShapeDtypeStruct((N_SAMPLE, D_MODEL), F32)]
    else:
        out_specs = [tok]
        out_shape = [jax.ShapeDtypeStruct((N_TOK, D_MODEL), F32)]
    return pl.pallas_call(
        functools.partial(_ffn_kernel, sub=sub, split_in=split_in, mix=mix, split_out=split_out,
                          cast_mixers=cast_mixers),
        grid=(N_TOK // TM_FFN,),
        in_specs=([prompt_tok, sample_tok] if split_in else [tok]) + feat_specs + [
            _mod_spec(layer, TM_FFN),
            _gain_spec(layer),
        ] + mix_w_spec + [
            pl.BlockSpec((D_MODEL, 2 * D_FF), lambda i: (0, 0), pipeline_mode=resident),
            pl.BlockSpec((D_FF, D_MODEL), lambda i: (0, 0), pipeline_mode=resident),
        ] + cast_in,
        out_specs=out_specs + cast_out,
        out_shape=out_shape + cast_shape,
        compiler_params=_params(1),
        name=f"ffn{sub}",
    )(*xs, *(feats or ()), mods, gains, *((w_out,) if mix else ()), w1, w2, *cast_args)


def _mixer_norm(x, m_ref, g_ref):
    return (_rms(x, g_ref[2:3, :]) * (1.0 + m_ref[4:5, :]) + m_ref[3:4, :]).astype(BF16)


def _store_heads(state_ref, b, x, n_heads):
    xt = x.T
    for h in range(n_heads):
        state_ref[b, 0, h] = xt[h * HEAD_DIM:(h + 1) * HEAD_DIM, :]


def _even_in_kernel(x_ref, m_ref, g_ref, w_ref, q_ref, k_ref, v_ref, bg_ref, z_ref, sk_ref, sv_ref):
    kv = []
    for b in range(SEQ_PER_TILE):
        rows = slice(b * SEQ, (b + 1) * SEQ)
        u = _dot(_mixer_norm(x_ref[rows, :], m_ref, g_ref), w_ref[...])
        k = u[:, HD_A:2 * HD_A]
        v = u[:, 2 * HD_A:3 * HD_A]
        q_ref[rows, :] = (u[:, 0:HD_A] * ATT_SCALE).astype(BF16)
        k_ref[rows, :] = k.astype(BF16)
        v_ref[rows, :] = v.astype(BF16)
        bg_ref[rows, :] = u[:, 3 * HD_A:3 * HD_A + C_B]
        z_ref[rows, :] = u[:, 3 * HD_A + C_B:3 * HD_A + 2 * C_B] * u[:, 3 * HD_A + 2 * C_B:3 * HD_A + 3 * C_B]
        kv.append((k, v))

    @pl.when(pl.program_id(0) < PROMPT_TILES)
    def _():
        for b, (k, v) in enumerate(kv):
            _store_heads(sk_ref, b, k, H_A)
            _store_heads(sv_ref, b, v, H_A)


def _state_spec(n_heads):
    return pl.BlockSpec((SEQ_PER_TILE, 1, n_heads, HEAD_DIM, SEQ), lambda i: (_prompt_tile(i), 0, 0, 0, 0))


def _even_in(x, mods, gains, w_in, layer):
    tok = lambda w: pl.BlockSpec((TM, w), lambda i: (i, 0))
    state = jax.ShapeDtypeStruct((BATCH, 1, H_A, HEAD_DIM, SEQ), F32)
    return pl.pallas_call(
        _even_in_kernel,
        grid=(N_TILES,),
        in_specs=[
            tok(D_MODEL), _mod_spec(layer), _gain_spec(layer),
            pl.BlockSpec((D_MODEL, EVEN_IN), lambda i: (0, 0)),
        ],
        out_specs=[tok(HD_A), tok(HD_A), tok(HD_A), tok(C_B), tok(C_B), _state_spec(H_A), _state_spec(H_A)],
        out_shape=[
            jax.ShapeDtypeStruct((N_TOK, HD_A), BF16),
            jax.ShapeDtypeStruct((N_TOK, HD_A), BF16),
            jax.ShapeDtypeStruct((N_TOK, HD_A), BF16),
            jax.ShapeDtypeStruct((N_TOK, C_B), F32),
            jax.ShapeDtypeStruct((N_TOK, C_B), F32),
            state, state,
        ],
        compiler_params=_params(1),
        name="even_in",
    )(x, mods, gains, w_in)


def _swap_pairs(x):
    n = x.shape[-1]
    lane = lax.broadcasted_iota(jnp.int32, x.shape, x.ndim - 1)
    return jnp.where(lane % 2 == 0, pltpu.roll(x, n - 1, x.ndim - 1), pltpu.roll(x, 1, x.ndim - 1))


def _odd_in_kernel(x_ref, m_ref, g_ref, w_ref, ng_ref, cos_ref, sin_ref, ones_ref,
                   uc_ref, q_ref, k_ref, v_ref, sk_ref, sv_ref):
    ones = ones_ref[...]
    kv = []
    for b in range(SEQ_PER_TILE):
        rows = slice(b * SEQ, (b + 1) * SEQ)
        u = _dot(_mixer_norm(x_ref[rows, :], m_ref, g_ref), w_ref[...])
        uc_ref[rows, :] = u[:, 0:C_POOL]
        qk = u[:, C_POOL:C_POOL + QK_W]
        sq = qk * qk
        hi = sq.astype(BF16)
        lo = (sq - hi.astype(F32)).astype(BF16)
        sums = []
        for c0 in range(0, QK_W, MXU_DIM):
            c1 = min(c0 + MXU_DIM, QK_W)
            tile = ones[0:c1 - c0, 0:c1 - c0]
            sums.append(_dot(hi[:, c0:c1], tile) + _dot(lo[:, c0:c1], tile))
        ms = jnp.concatenate(sums, axis=-1) * (1.0 / HEAD_DIM)
        n = qk * lax.rsqrt(ms + RMS_EPS) * ng_ref[...]
        r = n * cos_ref[rows, :] + _swap_pairs(n) * sin_ref[rows, :]
        k = r[:, HD_Q:QK_W]
        v = u[:, C_POOL + QK_W:ODD_IN]
        q_ref[rows, :] = (r[:, 0:HD_Q] * ATT_SCALE).astype(BF16)
        k_ref[rows, :] = k.astype(BF16)
        v_ref[rows, :] = v.astype(BF16)
        kv.append((k, v))

    @pl.when(pl.program_id(0) < PROMPT_TILES)
    def _():
        for b, (k, v) in enumerate(kv):
            _store_heads(sk_ref, b, k, HKV_D)
            _store_heads(sv_ref, b, v, HKV_D)


def _rope_tile_index(i):
    return jnp.where(i < PROMPT_TILES, 0, 1 + (i - PROMPT_TILES) % TILES_PER_SAMPLE)


def _odd_in(x, mods, gains, w_in, qk_gain, cos_t, sin_t, ones_bd, layer):
    tok = lambda w: pl.BlockSpec((TM, w), lambda i: (i, 0))
    state = jax.ShapeDtypeStruct((BATCH, 1, HKV_D, HEAD_DIM, SEQ), F32)
    return pl.pallas_call(
        _odd_in_kernel,
        grid=(N_TILES,),
        in_specs=[
            tok(D_MODEL), _mod_spec(layer), _gain_spec(layer),
            pl.BlockSpec((D_MODEL, ODD_IN), lambda i: (0, 0)),
            pl.BlockSpec((1, QK_W), lambda i: (0, 0)),
            pl.BlockSpec((TM, QK_W), lambda i: (_rope_tile_index(i), 0)),
            pl.BlockSpec((TM, QK_W), lambda i: (_rope_tile_index(i), 0)),
            pl.BlockSpec((MXU_DIM, MXU_DIM), lambda i: (0, 0)),
        ],
        out_specs=[tok(C_POOL), tok(HD_Q), tok(HD_KV), tok(HD_KV), _state_spec(HKV_D), _state_spec(HKV_D)],
        out_shape=[
            jax.ShapeDtypeStruct((N_TOK, C_POOL), F32),
            jax.ShapeDtypeStruct((N_TOK, HD_Q), BF16),
            jax.ShapeDtypeStruct((N_TOK, HD_KV), BF16),
            jax.ShapeDtypeStruct((N_TOK, HD_KV), BF16),
            state, state,
        ],
        compiler_params=_params(1),
        name="odd_in",
    )(x, mods, gains, w_in, qk_gain, cos_t, sin_t, ones_bd)


def _low_half(rows):
    return lax.broadcasted_iota(jnp.int32, (rows, LANES), 1) < HEAD_DIM


def _split_pair(qp):
    low = _low_half(qp.shape[0])
    zero = jnp.zeros_like(qp)
    return jnp.concatenate([jnp.where(low, qp, zero), jnp.where(low, zero, qp)], axis=0)


def _pair_heads_t(ref):
    return jnp.concatenate([ref[0], ref[1]], axis=0)


def _with_ones(v):
    return jnp.concatenate([v, jnp.ones_like(v)], axis=1)


def _softmax_pv(scores, values):
    m = scores[0].max(axis=-1, keepdims=True)
    for s in scores[1:]:
        m = jnp.maximum(m, s.max(axis=-1, keepdims=True))
    acc = None
    for s, v in zip(scores, values):
        o = _dot(jnp.exp(s - m).astype(BF16), v)
        acc = o if acc is None else acc + o
    return acc[:, 0:LANES] / acc[:, LANES:2 * LANES]


def _merge_pair(o):
    m = o.shape[0] // 2
    return jnp.where(_low_half(m), o[0:m], o[m:2 * m])


def _short_conv(z, cw_ref, cb_ref, seq_len):
    rows = z.shape[0]
    pos = lax.broadcasted_iota(jnp.int32, z.shape, 0) & (seq_len - 1)
    z_prev = jnp.where(pos == 0, 0.0, pltpu.roll(z, 1, 0))
    z_next = jnp.where(pos == seq_len - 1, 0.0, pltpu.roll(z, rows - 1, 0))
    y = z_prev * cw_ref[0:1, :]
    y = y + z * cw_ref[1:2, :]
    y = y + z_next * cw_ref[2:3, :]
    return y + cb_ref[...]


def _pool_mix(uc, pw_ref, ps_ref, seq_len):
    pos = lax.broadcasted_iota(jnp.int32, (seq_len, POOL_C), 0)
    pad = jnp.zeros((POOL_PAD, POOL_C), F32)
    n_ext = seq_len + 2 * POOL_PAD
    outs = []
    for gi, win in enumerate(POOL_WINDOWS):
        ug = uc[:, gi * POOL_C:(gi + 1) * POOL_C]
        run = jnp.concatenate([pad, ug, pad], axis=0)
        span = 1
        while span < win:
            run = run + pltpu.roll(run, span, 0)
            span *= 2
        back = win // 2 - 1
        if back:
            run = pltpu.roll(run, n_ext - back, 0)
        wsum = run[POOL_PAD:POOL_PAD + seq_len, :]
        cnt = jnp.minimum(pos + (win - win // 2), seq_len) - jnp.maximum(pos - win // 2, 0)
        pooled = (wsum / cnt.astype(F32) - ug).astype(BF16)
        outs.append(_dot(pooled, pw_ref[gi]))
    return jnp.concatenate(outs, axis=-1) * ps_ref[...]


def _even_prompt_kernel(q_ref, k_ref, v_ref, bg_ref, z_ref, cw_ref, cb_ref, fa_ref, fb_ref):
    for b in range(SEQ_PER_TILE):
        rows = slice(b * SEQ, (b + 1) * SEQ)
        outs = []
        for p in range(H_A // 2):
            sl = slice(p * LANES, (p + 1) * LANES)
            s = _dot_t(_split_pair(q_ref[rows, sl]), k_ref[rows, sl])
            outs.append(_merge_pair(_softmax_pv([s], [_with_ones(v_ref[rows, sl])])))
        fa_ref[rows, :] = jnp.concatenate(outs, axis=-1).astype(BF16)
    fb_ref[...] = (bg_ref[...] * _short_conv(z_ref[...], cw_ref, cb_ref, SEQ)).astype(BF16)


def _even_prompt(q, k, v, bg, z, conv_w, conv_b, idx):
    tile = lambda w: pl.BlockSpec((TM, w), lambda t: (t, 0))
    return pl.pallas_call(
        _even_prompt_kernel,
        grid=(PROMPT_TILES,),
        in_specs=[tile(HD_A), tile(HD_A), tile(HD_A), tile(C_B), tile(C_B),
                  pl.BlockSpec((None, CONV_W, C_B), lambda t: (idx, 0, 0)),
                  pl.BlockSpec((None, 1, C_B), lambda t: (idx, 0, 0))],
        out_specs=[tile(HD_A), tile(C_B)],
        out_shape=[
            jax.ShapeDtypeStruct((N_PROMPT, HD_A), BF16),
            jax.ShapeDtypeStruct((N_PROMPT, C_B), BF16),
        ],
        compiler_params=_params(1),
        name="even_prompt",
    )(q, k, v, bg, z, conv_w, conv_b)


def _na_bias_tiles(rpb_h):
    c = lax.broadcasted_iota(jnp.int32, (GRID_W, GRID_W), 0)
    kc = lax.broadcasted_iota(jnp.int32, (GRID_W, GRID_W), 1)
    cs = jnp.clip(c - NA_COLS // 2, 0, GRID_W - NA_COLS)
    valid = (kc >= cs) & (kc < cs + NA_COLS)
    tiles = []
    for ro in range(2 * NA_ROWS - 1):
        g = jnp.broadcast_to(rpb_h[ro:ro + 1, :], (GRID_W, LANES))
        skew = pltpu.roll(g, LANES - (NA_COLS - 1), 1, stride=1, stride_axis=0)
        tiles.append(jnp.where(valid, skew[:, 0:GRID_W], NEG_INF))
    return tiles


def _even_sample_kernel(*refs, n_cast):
    q_ref, k_ref, v_ref, bg_ref, z_ref, ck_ref, cv_ref, rpb_ref, cw_ref, cb_ref = refs[0:10]
    fa_ref, fb_ref = refs[10 + n_cast:12 + n_cast]
    bias_ref = refs[12 + 2 * n_cast]
    _cast_weights(refs[10:10 + n_cast], refs[12 + n_cast:12 + 2 * n_cast])
    win = NA_ROWS * GRID_W
    for h in range(2):
        tiles = _na_bias_tiles(rpb_ref[h])
        for var in range(NA_ROWS):
            bias_ref[var, h * GRID_W:(h + 1) * GRID_W, :] = jnp.concatenate(
                [tiles[i - var + NA_ROWS - 1] for i in range(NA_ROWS)], axis=1)
    ck_t = _pair_heads_t(ck_ref).astype(BF16)
    cv_ext = _with_ones(_pair_heads_t(cv_ref).T.astype(BF16))

    def rows(it, carry):
        for j in range(ROW_UNROLL):
            r = it * ROW_UNROLL + j
            rs = jnp.clip(r - NA_ROWS // 2, 0, GRID_ROWS - NA_ROWS)
            q0 = pl.multiple_of(r * GRID_W, GRID_W)
            k0 = pl.multiple_of(rs * GRID_W, GRID_W)
            q2 = _split_pair(q_ref[pl.ds(q0, GRID_W), :])
            s_loc = _dot_t(q2, k_ref[pl.ds(k0, win), :]) + bias_ref[r - rs]
            s_ctx = _dot(q2, ck_t)
            o = _softmax_pv([s_loc, s_ctx], [_with_ones(v_ref[pl.ds(k0, win), :]), cv_ext])
            fa_ref[pl.ds(q0, GRID_W), :] = _merge_pair(o).astype(BF16)
        return carry

    lax.fori_loop(0, GRID_ROWS // ROW_UNROLL, rows, 0)
    fb_ref[...] = (bg_ref[...] * _short_conv(z_ref[...], cw_ref, cb_ref, DEC_SEQ)).astype(BF16)


def _even_sample(q, k, v, bg, z, ctx_k, ctx_v, rpb, conv_w, conv_b, idx, ffn_w1, ffn_w2, cast_jobs):
    first = N_PROMPT // DEC_SEQ
    n_pair = H_A // 2
    seq_in = pl.BlockSpec((DEC_SEQ, LANES), lambda b, p: (first + b, p))
    seq_out = pl.BlockSpec((DEC_SEQ, LANES), lambda b, p: (b, p))
    ctx = pl.BlockSpec((None, None, 2, HEAD_DIM, PAST_LEN), lambda b, p: (b, idx, p, 0, 0))
    cast_in, cast_out, cast_shape = _cast_specs(cast_jobs, DEC_BATCH * n_pair, lambda b, p: b * n_pair + p)
    return pl.pallas_call(
        functools.partial(_even_sample_kernel, n_cast=len(cast_in)),
        grid=(DEC_BATCH, n_pair),
        in_specs=[seq_in, seq_in, seq_in, seq_in, seq_in, ctx, ctx,
                  pl.BlockSpec((None, 2, 2 * NA_ROWS - 1, LANES), lambda b, p: (idx, p, 0, 0)),
                  pl.BlockSpec((None, CONV_W, LANES), lambda b, p: (idx, 0, p)),
                  pl.BlockSpec((None, 1, LANES), lambda b, p: (idx, 0, p))] + cast_in,
        out_specs=[seq_out, seq_out] + cast_out,
        out_shape=[
            jax.ShapeDtypeStruct((N_SAMPLE, HD_A), BF16),
            jax.ShapeDtypeStruct((N_SAMPLE, C_B), BF16),
        ] + cast_shape,
        scratch_shapes=[pltpu.VMEM((NA_ROWS, 2 * GRID_W, NA_ROWS * GRID_W), F32)],
        compiler_params=_params(2),
        name="even_sample",
    )(q, k, v, bg, z, ctx_k, ctx_v, rpb, conv_w, conv_b, *([ffn_w1, ffn_w2] * len(cast_jobs)))


def _split_groups(q):
    low = _low_half(q.shape[0])
    zero = jnp.zeros((q.shape[0], LANES), BF16)
    pairs = [q[:, j * LANES:(j + 1) * LANES] for j in range(GQA_GROUP)]
    return jnp.concatenate([jnp.where(low, p, zero) for p in pairs] + [jnp.where(low, zero, p) for p in pairs],
                           axis=0)


def _merge_groups(o, m):
    low = _low_half(m)
    half = GQA_GROUP * m
    return jnp.concatenate([jnp.where(low, o[j * m:(j + 1) * m], o[half + j * m:half + (j + 1) * m])
                            for j in range(GQA_GROUP)], axis=-1)


def _odd_prompt_kernel(uc_ref, q_ref, k_ref, v_ref, pw_ref, ps_ref, fc_ref, fd_ref):
    for b in range(SEQ_PER_TILE):
        rows = slice(b * SEQ, (b + 1) * SEQ)
        s = _dot_t(_split_groups(q_ref[rows, :]), k_ref[rows, :])
        o = _softmax_pv([s], [_with_ones(v_ref[rows, :])])
        fd_ref[rows, :] = _merge_groups(o, SEQ).astype(BF16)
        fc_ref[rows, :] = _pool_mix(uc_ref[rows, :], pw_ref, ps_ref, SEQ).astype(BF16)


def _odd_prompt(uc, q, k, v, pool_w, pool_scale, idx):
    seq = lambda w: pl.BlockSpec((TM, w), lambda t: (t, 0))
    return pl.pallas_call(
        _odd_prompt_kernel,
        grid=(PROMPT_TILES,),
        in_specs=[seq(C_POOL), seq(HD_Q), seq(HD_KV), seq(HD_KV),
                  pl.BlockSpec((None, N_POOL, POOL_C, POOL_C), lambda t: (idx, 0, 0, 0)),
                  pl.BlockSpec((None, 1, C_POOL), lambda t: (idx, 0, 0))],
        out_specs=[seq(C_POOL), seq(HD_Q)],
        out_shape=[
            jax.ShapeDtypeStruct((N_PROMPT, C_POOL), BF16),
            jax.ShapeDtypeStruct((N_PROMPT, HD_Q), BF16),
        ],
        compiler_params=_params(1),
        name="odd_prompt",
    )(uc, q, k, v, pool_w, pool_scale)


def _odd_sample_kernel(*refs, n_cast):
    uc_ref, q_ref, k_ref, v_ref, ck_ref, cv_ref, pw_ref, ps_ref = refs[0:8]
    fc_ref, fd_ref = refs[8 + n_cast:10 + n_cast]
    kt_ref, vext_ref, ckpair_ref, cvext_ref = refs[10 + 2 * n_cast:]
    _cast_weights(refs[8:8 + n_cast], refs[10 + n_cast:10 + 2 * n_cast])

    @pl.when(pl.program_id(1) == 0)
    def _():
        fc_ref[...] = _pool_mix(uc_ref[...], pw_ref, ps_ref, DEC_SEQ).astype(BF16)
        kt_ref[...] = k_ref[...].astype(F32).T.astype(BF16)
        vext_ref[...] = _with_ones(v_ref[...])
        ckpair_ref[...] = _pair_heads_t(ck_ref).astype(BF16)
        cvext_ref[...] = _with_ones(_pair_heads_t(cv_ref).T.astype(BF16))

    k_t = kt_ref[...]
    ck_t = ckpair_ref[...]
    low = _low_half(Q_BLK)
    outs = []
    for j in range(GQA_GROUP):
        q2 = _split_pair(q_ref[:, j * LANES:(j + 1) * LANES])
        o = _softmax_pv([_dot(q2, k_t), _dot(q2, ck_t)], [vext_ref[...], cvext_ref[...]])
        outs.append(jnp.where(low, o[0:Q_BLK], o[Q_BLK:2 * Q_BLK]))
    fd_ref[...] = jnp.concatenate(outs, axis=-1).astype(BF16)


def _odd_sample(uc, q, k, v, ctx_k, ctx_v, pool_w, pool_scale, idx, ffn_w1, ffn_w2, cast_jobs):
    first = N_PROMPT // DEC_SEQ
    n_qb = DEC_SEQ // Q_BLK
    first_q = N_PROMPT // Q_BLK
    whole_in = lambda w: pl.BlockSpec((DEC_SEQ, w), lambda b, i: (first + b, 0))
    ctx = pl.BlockSpec((None, None, HKV_D, HEAD_DIM, PAST_LEN), lambda b, i: (b, idx, 0, 0, 0))
    cast_in, cast_out, cast_shape = _cast_specs(cast_jobs, DEC_BATCH * n_qb, lambda b, i: b * n_qb + i)
    return pl.pallas_call(
        functools.partial(_odd_sample_kernel, n_cast=len(cast_in)),
        grid=(DEC_BATCH, n_qb),
        in_specs=[whole_in(C_POOL),
                  pl.BlockSpec((Q_BLK, HD_Q), lambda b, i: (first_q + b * n_qb + i, 0)),
                  whole_in(HD_KV), whole_in(HD_KV), ctx, ctx,
                  pl.BlockSpec((None, N_POOL, POOL_C, POOL_C), lambda b, i: (idx, 0, 0, 0)),
                  pl.BlockSpec((None, 1, C_POOL), lambda b, i: (idx, 0, 0))] + cast_in,
        out_specs=[pl.BlockSpec((DEC_SEQ, C_POOL), lambda b, i: (b, 0)),
                   pl.BlockSpec((Q_BLK, HD_Q), lambda b, i: (b * n_qb + i, 0))] + cast_out,
        out_shape=[
            jax.ShapeDtypeStruct((N_SAMPLE, C_POOL), BF16),
            jax.ShapeDtypeStruct((N_SAMPLE, HD_Q), BF16),
        ] + cast_shape,
        scratch_shapes=[pltpu.VMEM((LANES, DEC_SEQ), BF16), pltpu.VMEM((DEC_SEQ, 2 * LANES), BF16),
                        pltpu.VMEM((LANES, PAST_LEN), BF16), pltpu.VMEM((PAST_LEN, 2 * LANES), BF16)],
        compiler_params=_params(2),
        name="odd_sample",
    )(uc, q, k, v, ctx_k, ctx_v, pool_w, pool_scale, *([ffn_w1, ffn_w2] * len(cast_jobs)))


def _rope_tables():
    t = np.arange(DEC_SEQ)
    n_freq = HEAD_DIM // 4
    inv = ROPE_BASE ** (-np.arange(n_freq, dtype=np.float64) / n_freq)
    ang = np.concatenate([(t // GRID_W)[:, None] * inv, (t % GRID_W)[:, None] * inv], axis=-1)
    cos = np.repeat(np.cos(ang), 2, axis=-1)
    sin = np.repeat(np.sin(ang), 2, axis=-1) * np.tile([-1.0, 1.0], HEAD_DIM // 2)
    n_heads = HQ_D + HKV_D
    cos = np.concatenate([np.ones((TM, QK_W)), np.tile(cos, (1, n_heads))], axis=0)
    sin = np.concatenate([np.zeros((TM, QK_W)), np.tile(sin, (1, n_heads))], axis=0)
    return jnp.asarray(cos, F32), jnp.asarray(sin, F32)


def _head_ones():
    head = np.arange(MXU_DIM) // HEAD_DIM
    return jnp.asarray(head[:, None] == head[None, :], BF16)


def kernel(x_prompt, x_sample, cache_a_k, cache_a_v, cache_d_k, cache_d_v, c, c_ctx, mod_w, mod_b, norm_w,
           ffn_w1, ffn_w2, ev_w_in, ev_rpb, ev_conv_w, ev_conv_b, ev_w_out, od_w_in, od_pool_w,
           od_pool_scale, od_q_norm, od_k_norm, od_w_out):
    xs = (x_prompt.reshape(N_PROMPT, D_MODEL), x_sample.reshape(N_SAMPLE, D_MODEL))
    cond = jnp.concatenate([c_ctx[None, :], c, jnp.zeros((COND_PAD - N_COND, D_MODEL), F32)], axis=0)
    mods = _modulation(cond, mod_w, mod_b).reshape(DEPTH, COND_PAD, N_MOD, D_MODEL)
    cache_a_k, cache_a_v, cache_d_k, cache_d_v = (jnp.swapaxes(t, -1, -2)
                                                  for t in (cache_a_k, cache_a_v, cache_d_k, cache_d_v))
    rpb = jnp.pad(ev_rpb, ((0, 0), (0, 0), (0, 0), (0, LANES - ev_rpb.shape[-1])))
    conv_b = ev_conv_b[:, None, :]
    pool_w, pool_scale = od_pool_w.astype(BF16), od_pool_scale[:, None, :]
    cos_t, sin_t = _rope_tables()
    ones_bd = _head_ones()
    states = []
    w1_a, w2_a = ffn_w1[0, 0].astype(BF16), ffn_w2[0, 0].astype(BF16)
    mixer_w = None
    for l in range(DEPTH):
        i = l // 2
        last = l == DEPTH - 1
        if l == 0:
            x, *mixer_w = _ffn(xs, mods, norm_w, w1_a, w2_a, l, 0, mixer_w=(ev_w_in, ev_w_out, od_w_in, od_w_out))
        else:
            x, = _ffn(xs, mods, norm_w, w1_a, w2_a, l, 0)
        ev_w_in_b, ev_w_out_b, od_w_in_b, od_w_out_b = mixer_w
        cast_jobs = [(l, 1)] + ([] if last else [(l + 1, 0)])
        if l % 2 == 0:
            q, k, v, bg, z, s_k, s_v = _even_in(x, mods, norm_w, ev_w_in_b, l)
            fa_p, fb_p = _even_prompt(q, k, v, bg, z, ev_conv_w, conv_b, i)
            fa_s, fb_s, *cast = _even_sample(q, k, v, bg, z, cache_a_k, cache_a_v, rpb, ev_conv_w, conv_b, i,
                                             ffn_w1, ffn_w2, cast_jobs)
            w_out = ev_w_out_b
        else:
            gain = jnp.concatenate([jnp.tile(od_q_norm[i], HQ_D), jnp.tile(od_k_norm[i], HKV_D)])[None, :]
            uc, q, k, v, s_k, s_v = _odd_in(x, mods, norm_w, od_w_in_b, gain, cos_t, sin_t, ones_bd, l)
            fa_p, fb_p = _odd_prompt(uc, q, k, v, pool_w, pool_scale, i)
            fa_s, fb_s, *cast = _odd_sample(uc, q, k, v, cache_d_k, cache_d_v, pool_w, pool_scale, i,
                                            ffn_w1, ffn_w2, cast_jobs)
            w_out = od_w_out_b
        w1_b, w2_b = cast[0:2]
        if not last:
            w1_a, w2_a = cast[2:4]
        states.append((jnp.swapaxes(s_k, -1, -2), jnp.swapaxes(s_v, -1, -2)))
        xs = tuple(_ffn((x,), mods, norm_w, w1_b, w2_b, l, 2, feats=(fa_p, fb_p, fa_s, fb_s), w_out=w_out,
                        split_out=last))
    y_prompt = xs[0].reshape(BATCH, SEQ, D_MODEL)
    y_sample = xs[1].reshape(DEC_BATCH, DEC_SEQ, D_MODEL)
    return (y_prompt, y_sample, states[0][0], states[0][1], states[1][0], states[1][1])
```

```python
import functools

import jax
import jax.numpy as jnp
import numpy as np
from jax import lax
from jax.experimental import pallas as pl
from jax.experimental.pallas import tpu as pltpu

D_MODEL = 1024
BATCH = 32
SEQ = 256
DEPTH = 2
DEC_BATCH = 2
DEC_SEQ = 2048
PAST_LEN = 256
GRID_W = 64
HEAD_DIM = 64
N_MOD = 9
D_FF = 2816
FFN_RES = 0.5
H_A = 8
NA_ROWS = 8
NA_COLS = 16
C_B = 512
CONV_W = 3
C_POOL = 512
POOL_WINDOWS = (2, 4, 8, 16)
N_POOL = 4
POOL_C = C_POOL // N_POOL
HQ_D = 8
HKV_D = 2
GQA_GROUP = HQ_D // HKV_D
ROPE_BASE = 10000.0
EVEN_IN = 3 * H_A * HEAD_DIM + 3 * C_B
ODD_IN = C_POOL + (HQ_D + 2 * HKV_D) * HEAD_DIM
RMS_EPS = 1e-6
NEG_INF = -1e30
ATT_SCALE = HEAD_DIM ** -0.5

LANES = 128
GRID_ROWS = DEC_SEQ // GRID_W
N_PROMPT = BATCH * SEQ
N_SAMPLE = DEC_BATCH * DEC_SEQ
N_TOK = N_PROMPT + N_SAMPLE
N_COND = 1 + DEC_BATCH
COND_PAD = 8
HD_A = H_A * HEAD_DIM
HD_Q = HQ_D * HEAD_DIM
HD_KV = HKV_D * HEAD_DIM
QK_W = HD_Q + HD_KV
POOL_PAD = 8

TM = 512
N_TILES = N_TOK // TM
PROMPT_TILES = N_PROMPT // TM
TILES_PER_SAMPLE = DEC_SEQ // TM
SEQ_PER_TILE = TM // SEQ
TM_FFN = 1024
FFN_HALF = TM_FFN // 2
MIX_CAST_STEPS = N_PROMPT // TM_FFN
MIX_CAST_ROWS = D_MODEL // MIX_CAST_STEPS
assert MIX_CAST_ROWS == 2 * HEAD_DIM and (DEPTH + 1) // 2 == 1 and DEPTH // 2 == 1
MXU_DIM = 256
FF_CHUNKS = tuple((lo, min(lo + 3 * MXU_DIM, D_FF)) for lo in range(0, D_FF, 3 * MXU_DIM))
MOD_TN = 1536
Q_BLK = 256
ROW_UNROLL = 8
VMEM_LIMIT = 60 * 1024 * 1024

F32 = jnp.float32
BF16 = jnp.bfloat16


def _params(n_grid):
    return pltpu.CompilerParams(dimension_semantics=("arbitrary",) * n_grid, vmem_limit_bytes=VMEM_LIMIT)


def _cond_of_tile(i, tm=TM):
    n_prompt = N_PROMPT // tm
    return jnp.where(i < n_prompt, 0, 1 + (i - n_prompt) // (DEC_SEQ // tm))


def _prompt_tile(i, tm=TM):
    return jnp.minimum(i, N_PROMPT // tm - 1)


def _sample_tile(i, tm=TM):
    return jnp.maximum(i - N_PROMPT // tm, 0)


def _rms(x, g):
    return x * lax.rsqrt(jnp.mean(x * x, axis=-1, keepdims=True) + RMS_EPS) * g


def _sigmoid(x):
    return 1.0 / (1.0 + jnp.exp(-x))


def _dot(a, b):
    return jnp.dot(a, b, preferred_element_type=F32)


def _dot_t(a, b):
    return lax.dot_general(a, b, (((1,), (1,)), ((), ())), preferred_element_type=F32)


def _mod_kernel(c_ref, w_ref, b_ref, o_ref):
    c = c_ref[...]
    sc = (c * _sigmoid(c)).astype(BF16)
    o_ref[...] = _dot(sc, w_ref[...].astype(BF16)) + b_ref[...]


def _modulation(cond, mod_w, mod_b):
    n_col = N_MOD * D_MODEL
    return pl.pallas_call(
        _mod_kernel,
        grid=(DEPTH, n_col // MOD_TN),
        in_specs=[
            pl.BlockSpec((COND_PAD, D_MODEL), lambda l, j: (0, 0)),
            pl.BlockSpec((None, D_MODEL, MOD_TN), lambda l, j: (l, 0, j)),
            pl.BlockSpec((None, 1, MOD_TN), lambda l, j: (l, 0, j)),
        ],
        out_specs=pl.BlockSpec((None, COND_PAD, MOD_TN), lambda l, j: (l, 0, j)),
        out_shape=jax.ShapeDtypeStruct((DEPTH, COND_PAD, n_col), F32),
        compiler_params=_params(2),
        name="modulation",
    )(cond, mod_w, mod_b.reshape(DEPTH, 1, n_col))


def _mod_spec(layer, tm=TM):
    return pl.BlockSpec((None, None, N_MOD, D_MODEL), lambda i: (layer, _cond_of_tile(i, tm), 0, 0))


def _gain_spec(layer):
    return pl.BlockSpec((None, 6, D_MODEL), lambda i: (layer, 0, 0))


def _cast_specs(jobs, n_steps, step_of):
    r1, r2 = D_MODEL // n_steps, D_FF // n_steps
    in_specs, out_specs, out_shape = [], [], []
    for layer, which in jobs:
        in_specs += [pl.BlockSpec((None, None, r1, 2 * D_FF), lambda *g, lw=(layer, which): (*lw, step_of(*g), 0)),
                     pl.BlockSpec((None, None, r2, D_MODEL), lambda *g, lw=(layer, which): (*lw, step_of(*g), 0))]
        out_specs += [pl.BlockSpec((r1, 2 * D_FF), lambda *g: (step_of(*g), 0)),
                      pl.BlockSpec((r2, D_MODEL), lambda *g: (step_of(*g), 0))]
        out_shape += [jax.ShapeDtypeStruct((D_MODEL, 2 * D_FF), BF16), jax.ShapeDtypeStruct((D_FF, D_MODEL), BF16)]
    return in_specs, out_specs, out_shape


def _cast_weights(f32_refs, bf16_refs):
    for src, dst in zip(f32_refs, bf16_refs):
        dst[...] = src[...].astype(BF16)


def _ev_in_cast_specs():
    step = lambda i: jnp.minimum(i, MIX_CAST_STEPS - 1)
    return ([pl.BlockSpec((None, MIX_CAST_ROWS, EVEN_IN), lambda i: (0, step(i), 0))],
            [pl.BlockSpec((MIX_CAST_ROWS, EVEN_IN), lambda i: (step(i), 0))],
            [jax.ShapeDtypeStruct((D_MODEL, EVEN_IN), BF16)])


def _mixer_cast_specs(step_of):
    rows = MIX_CAST_ROWS
    n_pool_blk = C_POOL // rows

    def od_out_block(s, g):
        return jnp.where(s < n_pool_blk, 2 * s + g, C_POOL // HEAD_DIM + g * GQA_GROUP + s - n_pool_blk)

    widths = (D_MODEL, ODD_IN, D_MODEL)
    in_specs = [pl.BlockSpec((None, rows, D_MODEL), lambda *g: (0, step_of(*g), 0)),
                pl.BlockSpec((None, rows, ODD_IN), lambda *g: (0, step_of(*g), 0)),
                pl.BlockSpec((None, HEAD_DIM, D_MODEL), lambda *g: (0, od_out_block(step_of(*g), 0), 0)),
                pl.BlockSpec((None, HEAD_DIM, D_MODEL), lambda *g: (0, od_out_block(step_of(*g), 1), 0))]
    out_specs = [pl.BlockSpec((rows, w), lambda *g: (step_of(*g), 0)) for w in widths]
    out_shape = [jax.ShapeDtypeStruct((D_MODEL, w), BF16) for w in widths]
    return in_specs, out_specs, out_shape


def _cast_mixer_weights(ev_out_f, od_in_f, od_out_lo_f, od_out_hi_f, ev_out_b, od_in_b, od_out_b):
    ev_out_b[...] = ev_out_f[...].astype(BF16)
    w = od_in_f[...]
    q = w[:, C_POOL:C_POOL + HD_Q]
    heads = [q[:, (g * GQA_GROUP + j) * HEAD_DIM:(g * GQA_GROUP + j + 1) * HEAD_DIM]
             for j in range(GQA_GROUP) for g in range(HKV_D)]
    od_in_b[...] = jnp.concatenate([w[:, 0:C_POOL]] + heads + [w[:, C_POOL + HD_Q:]], axis=1).astype(BF16)
    od_out_b[...] = jnp.concatenate([od_out_lo_f[...], od_out_hi_f[...]], axis=0).astype(BF16)


def _ffn_kernel(*refs, sub, split_in, mix, split_out, cast_ev_in):
    refs = list(refs)
    x_refs = [refs.pop(0) for _ in range(2 if split_in else 1)]
    feat_refs = [refs.pop(0) for _ in range(4 if mix else 0)]
    m_ref, g_ref = refs.pop(0), refs.pop(0)
    wo_ref = refs.pop(0) if mix else None
    w1_ref, w2_ref = refs.pop(0), refs.pop(0)
    if cast_ev_in:
        @pl.when(pl.program_id(0) < MIX_CAST_STEPS)
        def _():
            refs[-1][...] = refs[0][...].astype(BF16)
        refs = refs[1:-1]
    out_refs = refs
    is_prompt = pl.program_id(0) < N_PROMPT // TM_FFN
    shift = m_ref[3 * sub:3 * sub + 1, :]
    scale = m_ref[3 * sub + 1:3 * sub + 2, :]
    gate = m_ref[3 * sub + 2:3 * sub + 3, :]
    for half in range(TM_FFN // FFN_HALF):
        rows = slice(half * FFN_HALF, (half + 1) * FFN_HALF)
        if split_in:
            x = jnp.where(is_prompt, x_refs[0][rows, :], x_refs[1][rows, :])
        else:
            x = x_refs[0][rows, :]
        if mix:
            fa = jnp.where(is_prompt, feat_refs[0][rows, :], feat_refs[2][rows, :])
            fb = jnp.where(is_prompt, feat_refs[1][rows, :], feat_refs[3][rows, :])
            mixed = _dot(jnp.concatenate([fa, fb], axis=1), wo_ref[...])
            x = x + m_ref[5:6, :] * _rms(mixed, g_ref[3:4, :])
        h = (_rms(x, g_ref[2 * sub:2 * sub + 1, :]) * (1.0 + scale) + shift).astype(BF16)
        acc = jnp.zeros((FFN_HALF, D_MODEL), F32)
        for lo, hi in FF_CHUNKS:
            a = _dot(h, w1_ref[:, lo:hi])
            u = _dot(h, w1_ref[:, D_FF + lo:D_FF + hi])
            act = (a * _sigmoid(a) * u).astype(BF16)
            acc = acc + _dot(act, w2_ref[lo:hi, :])
        y = x + FFN_RES * gate * _rms(acc, g_ref[2 * sub + 1:2 * sub + 2, :])
        if split_out:
            @pl.when(is_prompt)
            def _():
                out_refs[0][rows, :] = y

            @pl.when(jnp.logical_not(is_prompt))
            def _():
                out_refs[1][rows, :] = y
        else:
            out_refs[0][rows, :] = y


def _ffn(xs, mods, gains, w1, w2, layer, sub, feats=None, w_out=None, split_out=False, ev_w_in=None):
    split_in = len(xs) == 2
    mix = feats is not None
    cast_ev_in = ev_w_in is not None
    resident = pl.Buffered(1)
    half_w = D_MODEL // 2
    feat_specs, mix_w_spec = [], []
    if mix:
        feat_p = pl.BlockSpec((TM_FFN, half_w), lambda i: (_prompt_tile(i, TM_FFN), 0))
        feat_s = pl.BlockSpec((TM_FFN, half_w), lambda i: (_sample_tile(i, TM_FFN), 0))
        feat_specs = [feat_p, feat_p, feat_s, feat_s]
        mix_w_spec = [pl.BlockSpec((D_MODEL, D_MODEL), lambda i: (0, 0), pipeline_mode=resident)]
    cast_in, cast_out, cast_shape, cast_args = [], [], [], ()
    if cast_ev_in:
        cast_in, cast_out, cast_shape = _ev_in_cast_specs()
        cast_args = (ev_w_in,)
    tok = pl.BlockSpec((TM_FFN, D_MODEL), lambda i: (i, 0))
    prompt_tok = pl.BlockSpec((TM_FFN, D_MODEL), lambda i: (_prompt_tile(i, TM_FFN), 0))
    sample_tok = pl.BlockSpec((TM_FFN, D_MODEL), lambda i: (_sample_tile(i, TM_FFN), 0))
    if split_out:
        out_specs = [prompt_tok, sample_tok]
        out_shape = [jax.ShapeDtypeStruct((N_PROMPT, D_MODEL), F32),
                     jax.ShapeDtypeStruct((N_SAMPLE, D_MODEL), F32)]
    else:
        out_specs = [tok]
        out_shape = [jax.ShapeDtypeStruct((N_TOK, D_MODEL), F32)]
    return pl.pallas_call(
        functools.partial(_ffn_kernel, sub=sub, split_in=split_in, mix=mix, split_out=split_out,
                          cast_ev_in=cast_ev_in),
        grid=(N_TOK // TM_FFN,),
        in_specs=([prompt_tok, sample_tok] if split_in else [tok]) + feat_specs + [
            _mod_spec(layer, TM_FFN),
            _gain_spec(layer),
        ] + mix_w_spec + [
            pl.BlockSpec((D_MODEL, 2 * D_FF), lambda i: (0, 0), pipeline_mode=resident),
            pl.BlockSpec((D_FF, D_MODEL), lambda i: (0, 0), pipeline_mode=resident),
        ] + cast_in,
        out_specs=out_specs + cast_out,
        out_shape=out_shape + cast_shape,
        compiler_params=_params(1),
        name=f"ffn{sub}",
    )(*xs, *(feats or ()), mods, gains, *((w_out,) if mix else ()), w1, w2, *cast_args)


def _mixer_norm(x, m_ref, g_ref):
    return (_rms(x, g_ref[2:3, :]) * (1.0 + m_ref[4:5, :]) + m_ref[3:4, :]).astype(BF16)


def _store_heads(state_ref, b, x, n_heads):
    xt = x.T
    for h in range(n_heads):
        state_ref[b, 0, h] = xt[h * HEAD_DIM:(h + 1) * HEAD_DIM, :]


def _even_in_kernel(x_ref, m_ref, g_ref, w_ref, q_ref, k_ref, v_ref, bg_ref, z_ref, sk_ref, sv_ref):
    kv = []
    for b in range(SEQ_PER_TILE):
        rows = slice(b * SEQ, (b + 1) * SEQ)
        u = _dot(_mixer_norm(x_ref[rows, :], m_ref, g_ref), w_ref[...])
        k = u[:, HD_A:2 * HD_A]
        v = u[:, 2 * HD_A:3 * HD_A]
        q_ref[rows, :] = (u[:, 0:HD_A] * ATT_SCALE).astype(BF16)
        k_ref[rows, :] = k.astype(BF16)
        v_ref[rows, :] = v.astype(BF16)
        bg_ref[rows, :] = u[:, 3 * HD_A:3 * HD_A + C_B]
        z_ref[rows, :] = u[:, 3 * HD_A + C_B:3 * HD_A + 2 * C_B] * u[:, 3 * HD_A + 2 * C_B:3 * HD_A + 3 * C_B]
        kv.append((k, v))

    @pl.when(pl.program_id(0) < PROMPT_TILES)
    def _():
        for b, (k, v) in enumerate(kv):
            _store_heads(sk_ref, b, k, H_A)
            _store_heads(sv_ref, b, v, H_A)


def _state_spec(n_heads):
    return pl.BlockSpec((SEQ_PER_TILE, 1, n_heads, HEAD_DIM, SEQ), lambda i: (_prompt_tile(i), 0, 0, 0, 0))


def _even_in(x, mods, gains, w_in, layer):
    tok = lambda w: pl.BlockSpec((TM, w), lambda i: (i, 0))
    state = jax.ShapeDtypeStruct((BATCH, 1, H_A, HEAD_DIM, SEQ), F32)
    return pl.pallas_call(
        _even_in_kernel,
        grid=(N_TILES,),
        in_specs=[
            tok(D_MODEL), _mod_spec(layer), _gain_spec(layer),
            pl.BlockSpec((D_MODEL, EVEN_IN), lambda i: (0, 0)),
        ],
        out_specs=[tok(HD_A), tok(HD_A), tok(HD_A), tok(C_B), tok(C_B), _state_spec(H_A), _state_spec(H_A)],
        out_shape=[
            jax.ShapeDtypeStruct((N_TOK, HD_A), BF16),
            jax.ShapeDtypeStruct((N_TOK, HD_A), BF16),
            jax.ShapeDtypeStruct((N_TOK, HD_A), BF16),
            jax.ShapeDtypeStruct((N_TOK, C_B), F32),
            jax.ShapeDtypeStruct((N_TOK, C_B), F32),
            state, state,
        ],
        compiler_params=_params(1),
        name="even_in",
    )(x, mods, gains, w_in)


def _swap_pairs(x):
    n = x.shape[-1]
    lane = lax.broadcasted_iota(jnp.int32, x.shape, x.ndim - 1)
    return jnp.where(lane % 2 == 0, pltpu.roll(x, n - 1, x.ndim - 1), pltpu.roll(x, 1, x.ndim - 1))


def _odd_in_kernel(x_ref, m_ref, g_ref, w_ref, ng_ref, cos_ref, sin_ref, ones_ref,
                   uc_ref, q_ref, k_ref, v_ref, sk_ref, sv_ref):
    ones = ones_ref[...]
    kv = []
    for b in range(SEQ_PER_TILE):
        rows = slice(b * SEQ, (b + 1) * SEQ)
        u = _dot(_mixer_norm(x_ref[rows, :], m_ref, g_ref), w_ref[...])
        uc_ref[rows, :] = u[:, 0:C_POOL]
        qk = u[:, C_POOL:C_POOL + QK_W]
        sq = qk * qk
        hi = sq.astype(BF16)
        lo = (sq - hi.astype(F32)).astype(BF16)
        sums = []
        for c0 in range(0, QK_W, MXU_DIM):
            c1 = min(c0 + MXU_DIM, QK_W)
            tile = ones[0:c1 - c0, 0:c1 - c0]
            sums.append(_dot(hi[:, c0:c1], tile) + _dot(lo[:, c0:c1], tile))
        ms = jnp.concatenate(sums, axis=-1) * (1.0 / HEAD_DIM)
        n = qk * lax.rsqrt(ms + RMS_EPS) * ng_ref[...]
        r = n * cos_ref[rows, :] + _swap_pairs(n) * sin_ref[rows, :]
        k = r[:, HD_Q:QK_W]
        v = u[:, C_POOL + QK_W:ODD_IN]
        q_ref[rows, :] = (r[:, 0:HD_Q] * ATT_SCALE).astype(BF16)
        k_ref[rows, :] = k.astype(BF16)
        v_ref[rows, :] = v.astype(BF16)
        kv.append((k, v))

    @pl.when(pl.program_id(0) < PROMPT_TILES)
    def _():
        for b, (k, v) in enumerate(kv):
            _store_heads(sk_ref, b, k, HKV_D)
            _store_heads(sv_ref, b, v, HKV_D)


def _rope_tile_index(i):
    return jnp.where(i < PROMPT_TILES, 0, 1 + (i - PROMPT_TILES) % TILES_PER_SAMPLE)


def _odd_in(x, mods, gains, w_in, qk_gain, cos_t, sin_t, ones_bd, layer):
    tok = lambda w: pl.BlockSpec((TM, w), lambda i: (i, 0))
    state = jax.ShapeDtypeStruct((BATCH, 1, HKV_D, HEAD_DIM, SEQ), F32)
    return pl.pallas_call(
        _odd_in_kernel,
        grid=(N_TILES,),
        in_specs=[
            tok(D_MODEL), _mod_spec(layer), _gain_spec(layer),
            pl.BlockSpec((D_MODEL, ODD_IN), lambda i: (0, 0)),
            pl.BlockSpec((1, QK_W), lambda i: (0, 0)),
            pl.BlockSpec((TM, QK_W), lambda i: (_rope_tile_index(i), 0)),
            pl.BlockSpec((TM, QK_W), lambda i: (_rope_tile_index(i), 0)),
            pl.BlockSpec((MXU_DIM, MXU_DIM), lambda i: (0, 0)),
        ],
        out_specs=[tok(C_POOL), tok(HD_Q), tok(HD_KV), tok(HD_KV), _state_spec(HKV_D), _state_spec(HKV_D)],
        out_shape=[
            jax.ShapeDtypeStruct((N_TOK, C_POOL), F32),
            jax.ShapeDtypeStruct((N_TOK, HD_Q), BF16),
            jax.ShapeDtypeStruct((N_TOK, HD_KV), BF16),
            jax.ShapeDtypeStruct((N_TOK, HD_KV), BF16),
            state, state,
        ],
        compiler_params=_params(1),
        name="odd_in",
    )(x, mods, gains, w_in, qk_gain, cos_t, sin_t, ones_bd)


def _low_half(rows):
    return lax.broadcasted_iota(jnp.int32, (rows, LANES), 1) < HEAD_DIM


def _split_pair(qp):
    low = _low_half(qp.shape[0])
    zero = jnp.zeros_like(qp)
    return jnp.concatenate([jnp.where(low, qp, zero), jnp.where(low, zero, qp)], axis=0)


def _pair_heads_t(ref):
    return jnp.concatenate([ref[0], ref[1]], axis=0)


def _with_ones(v):
    return jnp.concatenate([v, jnp.ones_like(v)], axis=1)


def _softmax_pv(scores, values):
    m = scores[0].max(axis=-1, keepdims=True)
    for s in scores[1:]:
        m = jnp.maximum(m, s.max(axis=-1, keepdims=True))
    acc = None
    for s, v in zip(scores, values):
        o = _dot(jnp.exp(s - m).astype(BF16), v)
        acc = o if acc is None else acc + o
    return acc[:, 0:LANES] / acc[:, LANES:2 * LANES]


def _merge_pair(o):
    m = o.shape[0] // 2
    return jnp.where(_low_half(m), o[0:m], o[m:2 * m])


def _short_conv(z, cw_ref, cb_ref, seq_len):
    rows = z.shape[0]
    pos = lax.broadcasted_iota(jnp.int32, z.shape, 0) & (seq_len - 1)
    z_prev = jnp.where(pos == 0, 0.0, pltpu.roll(z, 1, 0))
    z_next = jnp.where(pos == seq_len - 1, 0.0, pltpu.roll(z, rows - 1, 0))
    y = z_prev * cw_ref[0:1, :]
    y = y + z * cw_ref[1:2, :]
    y = y + z_next * cw_ref[2:3, :]
    return y + cb_ref[...]


def _pool_mix(uc, pw_ref, ps_ref, seq_len):
    pos = lax.broadcasted_iota(jnp.int32, (seq_len, POOL_C), 0)
    pad = jnp.zeros((POOL_PAD, POOL_C), F32)
    n_ext = seq_len + 2 * POOL_PAD
    outs = []
    for gi, win in enumerate(POOL_WINDOWS):
        ug = uc[:, gi * POOL_C:(gi + 1) * POOL_C]
        run = jnp.concatenate([pad, ug, pad], axis=0)
        span = 1
        while span < win:
            run = run + pltpu.roll(run, span, 0)
            span *= 2
        back = win // 2 - 1
        if back:
            run = pltpu.roll(run, n_ext - back, 0)
        wsum = run[POOL_PAD:POOL_PAD + seq_len, :]
        cnt = jnp.minimum(pos + (win - win // 2), seq_len) - jnp.maximum(pos - win // 2, 0)
        pooled = (wsum / cnt.astype(F32) - ug).astype(BF16)
        outs.append(_dot(pooled, pw_ref[gi]))
    return jnp.concatenate(outs, axis=-1) * ps_ref[...]


def _even_prompt_kernel(q_ref, k_ref, v_ref, bg_ref, z_ref, cw_ref, cb_ref, fa_ref, fb_ref):
    for b in range(SEQ_PER_TILE):
        rows = slice(b * SEQ, (b + 1) * SEQ)
        pairs = [slice(p * LANES, (p + 1) * LANES) for p in range(H_A // 2)]
        scores = [_dot_t(_split_pair(q_ref[rows, sl]), k_ref[rows, sl]) for sl in pairs]
        outs = [_merge_pair(_softmax_pv([s], [_with_ones(v_ref[rows, sl])])) for s, sl in zip(scores, pairs)]
        fa_ref[rows, :] = jnp.concatenate(outs, axis=-1).astype(BF16)
    fb_ref[...] = (bg_ref[...] * _short_conv(z_ref[...], cw_ref, cb_ref, SEQ)).astype(BF16)


def _even_prompt(q, k, v, bg, z, conv_w, conv_b, idx):
    tile = lambda w: pl.BlockSpec((TM, w), lambda t: (t, 0))
    return pl.pallas_call(
        _even_prompt_kernel,
        grid=(PROMPT_TILES,),
        in_specs=[tile(HD_A), tile(HD_A), tile(HD_A), tile(C_B), tile(C_B),
                  pl.BlockSpec((None, CONV_W, C_B), lambda t: (idx, 0, 0)),
                  pl.BlockSpec((None, 1, C_B), lambda t: (idx, 0, 0))],
        out_specs=[tile(HD_A), tile(C_B)],
        out_shape=[
            jax.ShapeDtypeStruct((N_PROMPT, HD_A), BF16),
            jax.ShapeDtypeStruct((N_PROMPT, C_B), BF16),
        ],
        compiler_params=_params(1),
        name="even_prompt",
    )(q, k, v, bg, z, conv_w, conv_b)


def _na_bias_tiles(rpb_h):
    c = lax.broadcasted_iota(jnp.int32, (GRID_W, GRID_W), 0)
    kc = lax.broadcasted_iota(jnp.int32, (GRID_W, GRID_W), 1)
    cs = jnp.clip(c - NA_COLS // 2, 0, GRID_W - NA_COLS)
    valid = (kc >= cs) & (kc < cs + NA_COLS)
    tiles = []
    for ro in range(2 * NA_ROWS - 1):
        g = jnp.broadcast_to(rpb_h[ro:ro + 1, :], (GRID_W, LANES))
        skew = pltpu.roll(g, LANES - (NA_COLS - 1), 1, stride=1, stride_axis=0)
        tiles.append(jnp.where(valid, skew[:, 0:GRID_W], NEG_INF))
    return tiles


def _even_sample_kernel(*refs, n_cast, cast_mixers):
    n_mix_in, n_mix_out = (4, 3) if cast_mixers else (0, 0)
    q_ref, k_ref, v_ref, bg_ref, z_ref, ck_ref, cv_ref, rpb_ref, cw_ref, cb_ref = refs[0:10]
    out0 = 10 + n_cast + n_mix_in
    fa_ref, fb_ref = refs[out0:out0 + 2]
    bias_ref = refs[out0 + 2 + n_cast + n_mix_out]
    _cast_weights(refs[10:10 + n_cast], refs[out0 + 2:out0 + 2 + n_cast])
    if cast_mixers:
        _cast_mixer_weights(*refs[10 + n_cast:out0], *refs[out0 + 2 + n_cast:out0 + 2 + n_cast + n_mix_out])
    win = NA_ROWS * GRID_W
    for h in range(2):
        tiles = _na_bias_tiles(rpb_ref[h])
        for var in range(NA_ROWS):
            bias_ref[var, h * GRID_W:(h + 1) * GRID_W, :] = jnp.concatenate(
                [tiles[i - var + NA_ROWS - 1] for i in range(NA_ROWS)], axis=1)
    ck_t = _pair_heads_t(ck_ref).astype(BF16)
    cv_ext = _with_ones(_pair_heads_t(cv_ref).T.astype(BF16))

    def rows(it, carry):
        staged = []
        for j in range(ROW_UNROLL):
            r = it * ROW_UNROLL + j
            rs = jnp.clip(r - NA_ROWS // 2, 0, GRID_ROWS - NA_ROWS)
            q0 = pl.multiple_of(r * GRID_W, GRID_W)
            k0 = pl.multiple_of(rs * GRID_W, GRID_W)
            q2 = _split_pair(q_ref[pl.ds(q0, GRID_W), :])
            s_loc = _dot_t(q2, k_ref[pl.ds(k0, win), :]) + bias_ref[r - rs]
            s_ctx = _dot(q2, ck_t)
            staged.append((q0, k0, s_loc, s_ctx))
        for q0, k0, s_loc, s_ctx in staged:
            o = _softmax_pv([s_loc, s_ctx], [_with_ones(v_ref[pl.ds(k0, win), :]), cv_ext])
            fa_ref[pl.ds(q0, GRID_W), :] = _merge_pair(o).astype(BF16)
        return carry

    lax.fori_loop(0, GRID_ROWS // ROW_UNROLL, rows, 0)
    fb_ref[...] = (bg_ref[...] * _short_conv(z_ref[...], cw_ref, cb_ref, DEC_SEQ)).astype(BF16)


def _even_sample(q, k, v, bg, z, ctx_k, ctx_v, rpb, conv_w, conv_b, idx, ffn_w1, ffn_w2, cast_jobs, mixer_w=None):
    first = N_PROMPT // DEC_SEQ
    n_pair = H_A // 2
    n_steps = DEC_BATCH * n_pair
    step_of = lambda b, p: b * n_pair + p
    seq_in = pl.BlockSpec((DEC_SEQ, LANES), lambda b, p: (first + b, p))
    seq_out = pl.BlockSpec((DEC_SEQ, LANES), lambda b, p: (b, p))
    ctx = pl.BlockSpec((None, None, 2, HEAD_DIM, PAST_LEN), lambda b, p: (b, idx, p, 0, 0))
    cast_in, cast_out, cast_shape = _cast_specs(cast_jobs, n_steps, step_of)
    n_cast = len(cast_in)
    cast_args = [ffn_w1, ffn_w2] * len(cast_jobs)
    if mixer_w is not None:
        assert n_steps == MIX_CAST_STEPS
        mix_in, mix_out, mix_shape = _mixer_cast_specs(step_of)
        cast_in, cast_out, cast_shape = cast_in + mix_in, cast_out + mix_out, cast_shape + mix_shape
        cast_args += [mixer_w[0], mixer_w[1], mixer_w[2], mixer_w[2]]
    return pl.pallas_call(
        functools.partial(_even_sample_kernel, n_cast=n_cast, cast_mixers=mixer_w is not None),
        grid=(DEC_BATCH, n_pair),
        in_specs=[seq_in, seq_in, seq_in, seq_in, seq_in, ctx, ctx,
                  pl.BlockSpec((None, 2, 2 * NA_ROWS - 1, LANES), lambda b, p: (idx, p, 0, 0)),
                  pl.BlockSpec((None, CONV_W, LANES), lambda b, p: (idx, 0, p)),
                  pl.BlockSpec((None, 1, LANES), lambda b, p: (idx, 0, p))] + cast_in,
        out_specs=[seq_out, seq_out] + cast_out,
        out_shape=[
            jax.ShapeDtypeStruct((N_SAMPLE, HD_A), BF16),
            jax.ShapeDtypeStruct((N_SAMPLE, C_B), BF16),
        ] + cast_shape,
        scratch_shapes=[pltpu.VMEM((NA_ROWS, 2 * GRID_W, NA_ROWS * GRID_W), F32)],
        compiler_params=_params(2),
        name="even_sample",
    )(q, k, v, bg, z, ctx_k, ctx_v, rpb, conv_w, conv_b, *cast_args)


def _split_groups(q):
    low = _low_half(q.shape[0])
    zero = jnp.zeros((q.shape[0], LANES), BF16)
    pairs = [q[:, j * LANES:(j + 1) * LANES] for j in range(GQA_GROUP)]
    return jnp.concatenate([jnp.where(low, p, zero) for p in pairs] + [jnp.where(low, zero, p) for p in pairs],
                           axis=0)


def _merge_groups(o, m):
    low = _low_half(m)
    half = GQA_GROUP * m
    return jnp.concatenate([jnp.where(low, o[j * m:(j + 1) * m], o[half + j * m:half + (j + 1) * m])
                            for j in range(GQA_GROUP)], axis=-1)


def _odd_prompt_kernel(uc_ref, q_ref, k_ref, v_ref, pw_ref, ps_ref, fc_ref, fd_ref):
    seqs = [slice(b * SEQ, (b + 1) * SEQ) for b in range(SEQ_PER_TILE)]
    scores = [_dot_t(_split_groups(q_ref[rows, :]), k_ref[rows, :]) for rows in seqs]
    for s, rows in zip(scores, seqs):
        o = _softmax_pv([s], [_with_ones(v_ref[rows, :])])
        fd_ref[rows, :] = _merge_groups(o, SEQ).astype(BF16)
    for rows in seqs:
        fc_ref[rows, :] = _pool_mix(uc_ref[rows, :], pw_ref, ps_ref, SEQ).astype(BF16)


def _odd_prompt(uc, q, k, v, pool_w, pool_scale, idx):
    seq = lambda w: pl.BlockSpec((TM, w), lambda t: (t, 0))
    return pl.pallas_call(
        _odd_prompt_kernel,
        grid=(PROMPT_TILES,),
        in_specs=[seq(C_POOL), seq(HD_Q), seq(HD_KV), seq(HD_KV),
                  pl.BlockSpec((None, N_POOL, POOL_C, POOL_C), lambda t: (idx, 0, 0, 0)),
                  pl.BlockSpec((None, 1, C_POOL), lambda t: (idx, 0, 0))],
        out_specs=[seq(C_POOL), seq(HD_Q)],
        out_shape=[
            jax.ShapeDtypeStruct((N_PROMPT, C_POOL), BF16),
            jax.ShapeDtypeStruct((N_PROMPT, HD_Q), BF16),
        ],
        compiler_params=_params(1),
        name="odd_prompt",
    )(uc, q, k, v, pool_w, pool_scale)


def _odd_sample_kernel(*refs, n_cast):
    uc_ref, q_ref, k_ref, v_ref, ck_ref, cv_ref, pw_ref, ps_ref = refs[0:8]
    fc_ref, fd_ref = refs[8 + n_cast:10 + n_cast]
    kt_ref, vext_ref, ckpair_ref, cvext_ref = refs[10 + 2 * n_cast:]
    _cast_weights(refs[8:8 + n_cast], refs[10 + n_cast:10 + 2 * n_cast])

    @pl.when(pl.program_id(1) == 0)
    def _():
        fc_ref[...] = _pool_mix(uc_ref[...], pw_ref, ps_ref, DEC_SEQ).astype(BF16)
        kt_ref[...] = k_ref[...].astype(F32).T.astype(BF16)
        vext_ref[...] = _with_ones(v_ref[...])
        ckpair_ref[...] = _pair_heads_t(ck_ref).astype(BF16)
        cvext_ref[...] = _with_ones(_pair_heads_t(cv_ref).T.astype(BF16))

    k_t = kt_ref[...]
    ck_t = ckpair_ref[...]
    low = _low_half(Q_BLK)
    scores = []
    for j in range(GQA_GROUP):
        q2 = _split_pair(q_ref[:, j * LANES:(j + 1) * LANES])
        scores.append([_dot(q2, k_t), _dot(q2, ck_t)])
    outs = []
    for s in scores:
        o = _softmax_pv(s, [vext_ref[...], cvext_ref[...]])
        outs.append(jnp.where(low, o[0:Q_BLK], o[Q_BLK:2 * Q_BLK]))
    fd_ref[...] = jnp.concatenate(outs, axis=-1).astype(BF16)


def _odd_sample(uc, q, k, v, ctx_k, ctx_v, pool_w, pool_scale, idx, ffn_w1, ffn_w2, cast_jobs):
    first = N_PROMPT // DEC_SEQ
    n_qb = DEC_SEQ // Q_BLK
    first_q = N_PROMPT // Q_BLK
    whole_in = lambda w: pl.BlockSpec((DEC_SEQ, w), lambda b, i: (first + b, 0))
    ctx = pl.BlockSpec((None, None, HKV_D, HEAD_DIM, PAST_LEN), lambda b, i: (b, idx, 0, 0, 0))
    cast_in, cast_out, cast_shape = _cast_specs(cast_jobs, DEC_BATCH * n_qb, lambda b, i: b * n_qb + i)
    return pl.pallas_call(
        functools.partial(_odd_sample_kernel, n_cast=len(cast_in)),
        grid=(DEC_BATCH, n_qb),
        in_specs=[whole_in(C_POOL),
                  pl.BlockSpec((Q_BLK, HD_Q), lambda b, i: (first_q + b * n_qb + i, 0)),
                  whole_in(HD_KV), whole_in(HD_KV), ctx, ctx,
                  pl.BlockSpec((None, N_POOL, POOL_C, POOL_C), lambda b, i: (idx, 0, 0, 0)),
                  pl.BlockSpec((None, 1, C_POOL), lambda b, i: (idx, 0, 0))] + cast_in,
        out_specs=[pl.BlockSpec((DEC_SEQ, C_POOL), lambda b, i: (b, 0)),
                   pl.BlockSpec((Q_BLK, HD_Q), lambda b, i: (b * n_qb + i, 0))] + cast_out,
        out_shape=[
            jax.ShapeDtypeStruct((N_SAMPLE, C_POOL), BF16),
            jax.ShapeDtypeStruct((N_SAMPLE, HD_Q), BF16),
        ] + cast_shape,
        scratch_shapes=[pltpu.VMEM((LANES, DEC_SEQ), BF16), pltpu.VMEM((DEC_SEQ, 2 * LANES), BF16),
                        pltpu.VMEM((LANES, PAST_LEN), BF16), pltpu.VMEM((PAST_LEN, 2 * LANES), BF16)],
        compiler_params=_params(2),
        name="odd_sample",
    )(uc, q, k, v, ctx_k, ctx_v, pool_w, pool_scale, *([ffn_w1, ffn_w2] * len(cast_jobs)))


def _rope_tables():
    t = np.arange(DEC_SEQ)
    n_freq = HEAD_DIM // 4
    inv = ROPE_BASE ** (-np.arange(n_freq, dtype=np.float64) / n_freq)
    ang = np.concatenate([(t // GRID_W)[:, None] * inv, (t % GRID_W)[:, None] * inv], axis=-1)
    cos = np.repeat(np.cos(ang), 2, axis=-1)
    sin = np.repeat(np.sin(ang), 2, axis=-1) * np.tile([-1.0, 1.0], HEAD_DIM // 2)
    n_heads = HQ_D + HKV_D
    cos = np.concatenate([np.ones((TM, QK_W)), np.tile(cos, (1, n_heads))], axis=0)
    sin = np.concatenate([np.zeros((TM, QK_W)), np.tile(sin, (1, n_heads))], axis=0)
    return jnp.asarray(cos, F32), jnp.asarray(sin, F32)


def _head_ones():
    head = np.arange(MXU_DIM) // HEAD_DIM
    return jnp.asarray(head[:, None] == head[None, :], BF16)


def kernel(x_prompt, x_sample, cache_a_k, cache_a_v, cache_d_k, cache_d_v, c, c_ctx, mod_w, mod_b, norm_w,
           ffn_w1, ffn_w2, ev_w_in, ev_rpb, ev_conv_w, ev_conv_b, ev_w_out, od_w_in, od_pool_w,
           od_pool_scale, od_q_norm, od_k_norm, od_w_out):
    xs = (x_prompt.reshape(N_PROMPT, D_MODEL), x_sample.reshape(N_SAMPLE, D_MODEL))
    cond = jnp.concatenate([c_ctx[None, :], c, jnp.zeros((COND_PAD - N_COND, D_MODEL), F32)], axis=0)
    mods = _modulation(cond, mod_w, mod_b).reshape(DEPTH, COND_PAD, N_MOD, D_MODEL)
    cache_a_k, cache_a_v, cache_d_k, cache_d_v = (jnp.swapaxes(t, -1, -2)
                                                  for t in (cache_a_k, cache_a_v, cache_d_k, cache_d_v))
    rpb = jnp.pad(ev_rpb, ((0, 0), (0, 0), (0, 0), (0, LANES - ev_rpb.shape[-1])))
    conv_b = ev_conv_b[:, None, :]
    pool_w, pool_scale = od_pool_w.astype(BF16), od_pool_scale[:, None, :]
    cos_t, sin_t = _rope_tables()
    ones_bd = _head_ones()
    states = []
    w1_a, w2_a = ffn_w1[0, 0].astype(BF16), ffn_w2[0, 0].astype(BF16)
    for l in range(DEPTH):
        i = l // 2
        last = l == DEPTH - 1
        if l == 0:
            x, ev_w_in_b = _ffn(xs, mods, norm_w, w1_a, w2_a, l, 0, ev_w_in=ev_w_in)
        else:
            x, = _ffn(xs, mods, norm_w, w1_a, w2_a, l, 0)
        cast_jobs = [(l, 1)] + ([] if last else [(l + 1, 0)])
        if l % 2 == 0:
            q, k, v, bg, z, s_k, s_v = _even_in(x, mods, norm_w, ev_w_in_b, l)
            fa_p, fb_p = _even_prompt(q, k, v, bg, z, ev_conv_w, conv_b, i)
            fa_s, fb_s, *cast = _even_sample(q, k, v, bg, z, cache_a_k, cache_a_v, rpb, ev_conv_w, conv_b, i,
                                             ffn_w1, ffn_w2, cast_jobs, mixer_w=(ev_w_out, od_w_in, od_w_out))
            ev_w_out_b, od_w_in_b, od_w_out_b = cast[2 * len(cast_jobs):]
            w_out = ev_w_out_b
        else:
            gain = jnp.concatenate([jnp.tile(od_q_norm[i], HQ_D), jnp.tile(od_k_norm[i], HKV_D)])[None, :]
            uc, q, k, v, s_k, s_v = _odd_in(x, mods, norm_w, od_w_in_b, gain, cos_t, sin_t, ones_bd, l)
            fa_p, fb_p = _odd_prompt(uc, q, k, v, pool_w, pool_scale, i)
            fa_s, fb_s, *cast = _odd_sample(uc, q, k, v, cache_d_k, cache_d_v, pool_w, pool_scale, i,
                                            ffn_w1, ffn_w2, cast_jobs)
            w_out = od_w_out_b
        w1_b, w2_b = cast[0:2]
        if not last:
            w1_a, w2_a = cast[2:4]
        states.append((jnp.swapaxes(s_k, -1, -2), jnp.swapaxes(s_v, -1, -2)))
        xs = tuple(_ffn((x,), mods, norm_w, w1_b, w2_b, l, 2, feats=(fa_p, fb_p, fa_s, fb_s), w_out=w_out,
                        split_out=last))
    y_prompt = xs[0].reshape(BATCH, SEQ, D_MODEL)
    y_sample = xs[1].reshape(DEC_BATCH, DEC_SEQ, D_MODEL)
    return (y_prompt, y_sample, states[0][0], states[0][1], states[1][0], states[1][1])
```

```python
import functools

import jax
import jax.numpy as jnp
import numpy as np
from jax import lax
from jax.experimental import pallas as pl
from jax.experimental.pallas import tpu as pltpu

D_MODEL = 1024
BATCH = 32
SEQ = 256
DEPTH = 2
DEC_BATCH = 2
DEC_SEQ = 2048
PAST_LEN = 256
GRID_W = 64
HEAD_DIM = 64
N_MOD = 9
D_FF = 2816
FFN_RES = 0.5
H_A = 8
NA_ROWS = 8
NA_COLS = 16
C_B = 512
CONV_W = 3
C_POOL = 512
POOL_WINDOWS = (2, 4, 8, 16)
N_POOL = 4
POOL_C = C_POOL // N_POOL
HQ_D = 8
HKV_D = 2
GQA_GROUP = HQ_D // HKV_D
ROPE_BASE = 10000.0
EVEN_IN = 3 * H_A * HEAD_DIM + 3 * C_B
ODD_IN = C_POOL + (HQ_D + 2 * HKV_D) * HEAD_DIM
RMS_EPS = 1e-6
NEG_INF = -1e30
ATT_SCALE = HEAD_DIM ** -0.5

LANES = 128
GRID_ROWS = DEC_SEQ // GRID_W
N_PROMPT = BATCH * SEQ
N_SAMPLE = DEC_BATCH * DEC_SEQ
N_TOK = N_PROMPT + N_SAMPLE
N_COND = 1 + DEC_BATCH
COND_PAD = 8
HD_A = H_A * HEAD_DIM
HD_Q = HQ_D * HEAD_DIM
HD_KV = HKV_D * HEAD_DIM
QK_W = HD_Q + HD_KV
POOL_PAD = 8

TM = 512
N_TILES = N_TOK // TM
PROMPT_TILES = N_PROMPT // TM
TILES_PER_SAMPLE = DEC_SEQ // TM
SEQ_PER_TILE = TM // SEQ
TM_FFN = 1024
FFN_HALF = TM_FFN // 2
MIX_CAST_STEPS = N_PROMPT // TM_FFN
MIX_CAST_ROWS = D_MODEL // MIX_CAST_STEPS
assert MIX_CAST_ROWS == 2 * HEAD_DIM and (DEPTH + 1) // 2 == 1 and DEPTH // 2 == 1
MXU_DIM = 256
FF_CHUNKS = tuple((lo, min(lo + 3 * MXU_DIM, D_FF)) for lo in range(0, D_FF, 3 * MXU_DIM))
MOD_TN = 1152
MOD_TN_FFN = N_MOD * D_MODEL // (N_TOK // TM_FFN)
Q_BLK = 256
ROW_UNROLL = 8
VMEM_LIMIT = 60 * 1024 * 1024

F32 = jnp.float32
BF16 = jnp.bfloat16


def _params(n_grid):
    return pltpu.CompilerParams(dimension_semantics=("arbitrary",) * n_grid, vmem_limit_bytes=VMEM_LIMIT)


def _cond_of_tile(i, tm=TM):
    n_prompt = N_PROMPT // tm
    return jnp.where(i < n_prompt, 0, 1 + (i - n_prompt) // (DEC_SEQ // tm))


def _prompt_tile(i, tm=TM):
    return jnp.minimum(i, N_PROMPT // tm - 1)


def _sample_tile(i, tm=TM):
    return jnp.maximum(i - N_PROMPT // tm, 0)


def _rms(x, g):
    return x * lax.rsqrt(jnp.mean(x * x, axis=-1, keepdims=True) + RMS_EPS) * g


def _sigmoid(x):
    return 1.0 / (1.0 + jnp.exp(-x))


def _dot(a, b):
    return jnp.dot(a, b, preferred_element_type=F32)


def _dot_t(a, b):
    return lax.dot_general(a, b, (((1,), (1,)), ((), ())), preferred_element_type=F32)


def _modulate(c_ref, w_ref, b_ref, o_ref):
    c = c_ref[...]
    sc = (c * _sigmoid(c)).astype(BF16)
    o_ref[...] = _dot(sc, w_ref[...].astype(BF16)) + b_ref[...]


def _mod_specs(layer, tn):
    in_specs = [pl.BlockSpec((COND_PAD, D_MODEL), lambda j: (0, 0)),
                pl.BlockSpec((None, D_MODEL, tn), lambda j: (layer, 0, j)),
                pl.BlockSpec((None, 1, tn), lambda j: (layer, 0, j))]
    out_spec = pl.BlockSpec((COND_PAD, tn), lambda j: (0, j))
    out_shape = jax.ShapeDtypeStruct((COND_PAD, N_MOD * D_MODEL), F32)
    return in_specs, out_spec, out_shape


def _first_mod_kernel(c_ref, w_ref, b_ref, w1f_ref, w2f_ref, o_ref, w1b_ref, w2b_ref):
    _cast_weights([w1f_ref, w2f_ref], [w1b_ref, w2b_ref])
    _modulate(c_ref, w_ref, b_ref, o_ref)


def _first_modulation(cond, mod_w, mod_b, ffn_w1, ffn_w2):
    n_steps = N_MOD * D_MODEL // MOD_TN
    mod_in, mod_out, mod_shape = _mod_specs(0, MOD_TN)
    cast_in, cast_out, cast_shape = _cast_specs([(0, 0)], n_steps, lambda j: j)
    return pl.pallas_call(
        _first_mod_kernel,
        grid=(n_steps,),
        in_specs=mod_in + cast_in,
        out_specs=[mod_out] + cast_out,
        out_shape=[mod_shape] + cast_shape,
        compiler_params=_params(1),
        name="modulation",
    )(cond, mod_w, mod_b, ffn_w1, ffn_w2)


def _mod_spec(tm=TM):
    return pl.BlockSpec((None, N_MOD, D_MODEL), lambda i: (_cond_of_tile(i, tm), 0, 0))


def _gain_spec(layer):
    return pl.BlockSpec((None, 6, D_MODEL), lambda i: (layer, 0, 0))


def _cast_specs(jobs, n_steps, step_of):
    r1, r2 = D_MODEL // n_steps, D_FF // n_steps
    in_specs, out_specs, out_shape = [], [], []
    for layer, which in jobs:
        in_specs += [pl.BlockSpec((None, None, r1, 2 * D_FF), lambda *g, lw=(layer, which): (*lw, step_of(*g), 0)),
                     pl.BlockSpec((None, None, r2, D_MODEL), lambda *g, lw=(layer, which): (*lw, step_of(*g), 0))]
        out_specs += [pl.BlockSpec((r1, 2 * D_FF), lambda *g: (step_of(*g), 0)),
                      pl.BlockSpec((r2, D_MODEL), lambda *g: (step_of(*g), 0))]
        out_shape += [jax.ShapeDtypeStruct((D_MODEL, 2 * D_FF), BF16), jax.ShapeDtypeStruct((D_FF, D_MODEL), BF16)]
    return in_specs, out_specs, out_shape


def _cast_weights(f32_refs, bf16_refs):
    for src, dst in zip(f32_refs, bf16_refs):
        dst[...] = src[...].astype(BF16)


def _ev_in_cast_specs():
    step = lambda i: jnp.minimum(i, MIX_CAST_STEPS - 1)
    return ([pl.BlockSpec((None, MIX_CAST_ROWS, EVEN_IN), lambda i: (0, step(i), 0))],
            [pl.BlockSpec((MIX_CAST_ROWS, EVEN_IN), lambda i: (step(i), 0))],
            [jax.ShapeDtypeStruct((D_MODEL, EVEN_IN), BF16)])


def _mixer_cast_specs(step_of):
    rows = MIX_CAST_ROWS
    n_pool_blk = C_POOL // rows

    def od_out_block(s, g):
        return jnp.where(s < n_pool_blk, 2 * s + g, C_POOL // HEAD_DIM + g * GQA_GROUP + s - n_pool_blk)

    widths = (D_MODEL, ODD_IN, D_MODEL)
    in_specs = [pl.BlockSpec((None, rows, D_MODEL), lambda *g: (0, step_of(*g), 0)),
                pl.BlockSpec((None, rows, ODD_IN), lambda *g: (0, step_of(*g), 0)),
                pl.BlockSpec((None, HEAD_DIM, D_MODEL), lambda *g: (0, od_out_block(step_of(*g), 0), 0)),
                pl.BlockSpec((None, HEAD_DIM, D_MODEL), lambda *g: (0, od_out_block(step_of(*g), 1), 0))]
    out_specs = [pl.BlockSpec((rows, w), lambda *g: (step_of(*g), 0)) for w in widths]
    out_shape = [jax.ShapeDtypeStruct((D_MODEL, w), BF16) for w in widths]
    return in_specs, out_specs, out_shape


def _cast_mixer_weights(ev_out_f, od_in_f, od_out_lo_f, od_out_hi_f, ev_out_b, od_in_b, od_out_b):
    ev_out_b[...] = ev_out_f[...].astype(BF16)
    w = od_in_f[...]
    q = w[:, C_POOL:C_POOL + HD_Q]
    heads = [q[:, (g * GQA_GROUP + j) * HEAD_DIM:(g * GQA_GROUP + j + 1) * HEAD_DIM]
             for j in range(GQA_GROUP) for g in range(HKV_D)]
    od_in_b[...] = jnp.concatenate([w[:, 0:C_POOL]] + heads + [w[:, C_POOL + HD_Q:]], axis=1).astype(BF16)
    od_out_b[...] = jnp.concatenate([od_out_lo_f[...], od_out_hi_f[...]], axis=0).astype(BF16)


def _ffn_kernel(*refs, sub, split_in, mix, split_out, cast_ev_in, next_mod):
    refs = list(refs)
    x_refs = [refs.pop(0) for _ in range(2 if split_in else 1)]
    feat_refs = [refs.pop(0) for _ in range(4 if mix else 0)]
    m_ref, g_ref = refs.pop(0), refs.pop(0)
    wo_ref = refs.pop(0) if mix else None
    w1_ref, w2_ref = refs.pop(0), refs.pop(0)
    n_main_out = 2 if split_out else 1
    side_in = refs[0:len(refs) - n_main_out - cast_ev_in - next_mod]
    out_refs = refs[len(side_in):len(side_in) + n_main_out]
    side_out = refs[len(side_in) + n_main_out:]
    if cast_ev_in:
        @pl.when(pl.program_id(0) < MIX_CAST_STEPS)
        def _():
            side_out[0][...] = side_in[0][...].astype(BF16)
    if next_mod:
        _modulate(*side_in[-3:], side_out[-1])
    is_prompt = pl.program_id(0) < N_PROMPT // TM_FFN
    shift = m_ref[3 * sub:3 * sub + 1, :]
    scale = m_ref[3 * sub + 1:3 * sub + 2, :]
    gate = m_ref[3 * sub + 2:3 * sub + 3, :]
    for half in range(TM_FFN // FFN_HALF):
        rows = slice(half * FFN_HALF, (half + 1) * FFN_HALF)
        if split_in:
            x = jnp.where(is_prompt, x_refs[0][rows, :], x_refs[1][rows, :])
        else:
            x = x_refs[0][rows, :]
        if mix:
            fa = jnp.where(is_prompt, feat_refs[0][rows, :], feat_refs[2][rows, :])
            fb = jnp.where(is_prompt, feat_refs[1][rows, :], feat_refs[3][rows, :])
            mixed = _dot(jnp.concatenate([fa, fb], axis=1), wo_ref[...])
            x = x + m_ref[5:6, :] * _rms(mixed, g_ref[3:4, :])
        h = (_rms(x, g_ref[2 * sub:2 * sub + 1, :]) * (1.0 + scale) + shift).astype(BF16)
        acc = jnp.zeros((FFN_HALF, D_MODEL), F32)
        for lo, hi in FF_CHUNKS:
            a = _dot(h, w1_ref[:, lo:hi])
            u = _dot(h, w1_ref[:, D_FF + lo:D_FF + hi])
            act = (a * _sigmoid(a) * u).astype(BF16)
            acc = acc + _dot(act, w2_ref[lo:hi, :])
        y = x + FFN_RES * gate * _rms(acc, g_ref[2 * sub + 1:2 * sub + 2, :])
        if split_out:
            @pl.when(is_prompt)
            def _():
                out_refs[0][rows, :] = y

            @pl.when(jnp.logical_not(is_prompt))
            def _():
                out_refs[1][rows, :] = y
        else:
            out_refs[0][rows, :] = y


def _ffn(xs, mods, gains, w1, w2, layer, sub, feats=None, w_out=None, split_out=False, ev_w_in=None,
         next_mod=None):
    split_in = len(xs) == 2
    mix = feats is not None
    cast_ev_in = ev_w_in is not None
    resident = pl.Buffered(1)
    half_w = D_MODEL // 2
    feat_specs, mix_w_spec = [], []
    if mix:
        feat_p = pl.BlockSpec((TM_FFN, half_w), lambda i: (_prompt_tile(i, TM_FFN), 0))
        feat_s = pl.BlockSpec((TM_FFN, half_w), lambda i: (_sample_tile(i, TM_FFN), 0))
        feat_specs = [feat_p, feat_p, feat_s, feat_s]
        mix_w_spec = [pl.BlockSpec((D_MODEL, D_MODEL), lambda i: (0, 0), pipeline_mode=resident)]
    cast_in, cast_out, cast_shape, cast_args = [], [], [], ()
    if cast_ev_in:
        cast_in, cast_out, cast_shape = _ev_in_cast_specs()
        cast_args = (ev_w_in,)
    if next_mod is not None:
        mod_in, mod_out, mod_shape = _mod_specs(layer + 1, MOD_TN_FFN)
        cast_in, cast_out, cast_shape = cast_in + mod_in, cast_out + [mod_out], cast_shape + [mod_shape]
        cast_args = cast_args + tuple(next_mod)
    tok = pl.BlockSpec((TM_FFN, D_MODEL), lambda i: (i, 0))
    prompt_tok = pl.BlockSpec((TM_FFN, D_MODEL), lambda i: (_prompt_tile(i, TM_FFN), 0))
    sample_tok = pl.BlockSpec((TM_FFN, D_MODEL), lambda i: (_sample_tile(i, TM_FFN), 0))
    if split_out:
        out_specs = [prompt_tok, sample_tok]
        out_shape = [jax.ShapeDtypeStruct((N_PROMPT, D_MODEL), F32),
                     jax.ShapeDtypeStruct((N_SAMPLE, D_MODEL), F32)]
    else:
        out_specs = [tok]
        out_shape = [jax.ShapeDtypeStruct((N_TOK, D_MODEL), F32)]
    return pl.pallas_call(
        functools.partial(_ffn_kernel, sub=sub, split_in=split_in, mix=mix, split_out=split_out,
                          cast_ev_in=cast_ev_in, next_mod=next_mod is not None),
        grid=(N_TOK // TM_FFN,),
        in_specs=([prompt_tok, sample_tok] if split_in else [tok]) + feat_specs + [
            _mod_spec(TM_FFN),
            _gain_spec(layer),
        ] + mix_w_spec + [
            pl.BlockSpec((D_MODEL, 2 * D_FF), lambda i: (0, 0), pipeline_mode=resident),
            pl.BlockSpec((D_FF, D_MODEL), lambda i: (0, 0), pipeline_mode=resident),
        ] + cast_in,
        out_specs=out_specs + cast_out,
        out_shape=out_shape + cast_shape,
        compiler_params=_params(1),
        name=f"ffn{sub}",
    )(*xs, *(feats or ()), mods, gains, *((w_out,) if mix else ()), w1, w2, *cast_args)


def _mixer_norm(x, m_ref, g_ref):
    return (_rms(x, g_ref[2:3, :]) * (1.0 + m_ref[4:5, :]) + m_ref[3:4, :]).astype(BF16)


def _store_heads(state_ref, b, x, n_heads):
    xt = x.T
    for h in range(n_heads):
        state_ref[b, 0, h] = xt[h * HEAD_DIM:(h + 1) * HEAD_DIM, :]


def _even_in_kernel(x_ref, m_ref, g_ref, w_ref, q_ref, k_ref, v_ref, bg_ref, z_ref, sk_ref, sv_ref):
    kv = []
    for b in range(SEQ_PER_TILE):
        rows = slice(b * SEQ, (b + 1) * SEQ)
        u = _dot(_mixer_norm(x_ref[rows, :], m_ref, g_ref), w_ref[...])
        k = u[:, HD_A:2 * HD_A]
        v = u[:, 2 * HD_A:3 * HD_A]
        q_ref[rows, :] = (u[:, 0:HD_A] * ATT_SCALE).astype(BF16)
        k_ref[rows, :] = k.astype(BF16)
        v_ref[rows, :] = v.astype(BF16)
        bg_ref[rows, :] = u[:, 3 * HD_A:3 * HD_A + C_B]
        z_ref[rows, :] = u[:, 3 * HD_A + C_B:3 * HD_A + 2 * C_B] * u[:, 3 * HD_A + 2 * C_B:3 * HD_A + 3 * C_B]
        kv.append((k, v))

    @pl.when(pl.program_id(0) < PROMPT_TILES)
    def _():
        for b, (k, v) in enumerate(kv):
            _store_heads(sk_ref, b, k, H_A)
            _store_heads(sv_ref, b, v, H_A)


def _state_spec(n_heads):
    return pl.BlockSpec((SEQ_PER_TILE, 1, n_heads, HEAD_DIM, SEQ), lambda i: (_prompt_tile(i), 0, 0, 0, 0))


def _even_in(x, mods, gains, w_in, layer):
    tok = lambda w: pl.BlockSpec((TM, w), lambda i: (i, 0))
    state = jax.ShapeDtypeStruct((BATCH, 1, H_A, HEAD_DIM, SEQ), F32)
    return pl.pallas_call(
        _even_in_kernel,
        grid=(N_TILES,),
        in_specs=[
            tok(D_MODEL), _mod_spec(), _gain_spec(layer),
            pl.BlockSpec((D_MODEL, EVEN_IN), lambda i: (0, 0)),
        ],
        out_specs=[tok(HD_A), tok(HD_A), tok(HD_A), tok(C_B), tok(C_B), _state_spec(H_A), _state_spec(H_A)],
        out_shape=[
            jax.ShapeDtypeStruct((N_TOK, HD_A), BF16),
            jax.ShapeDtypeStruct((N_TOK, HD_A), BF16),
            jax.ShapeDtypeStruct((N_TOK, HD_A), BF16),
            jax.ShapeDtypeStruct((N_TOK, C_B), F32),
            jax.ShapeDtypeStruct((N_TOK, C_B), F32),
            state, state,
        ],
        compiler_params=_params(1),
        name="even_in",
    )(x, mods, gains, w_in)


def _swap_pairs(x):
    n = x.shape[-1]
    lane = lax.broadcasted_iota(jnp.int32, x.shape, x.ndim - 1)
    return jnp.where(lane % 2 == 0, pltpu.roll(x, n - 1, x.ndim - 1), pltpu.roll(x, 1, x.ndim - 1))


def _odd_in_kernel(x_ref, m_ref, g_ref, w_ref, ng_ref, cos_ref, sin_ref, ones_ref,
                   uc_ref, q_ref, k_ref, v_ref, sk_ref, sv_ref):
    ones = ones_ref[...]
    kv = []
    for b in range(SEQ_PER_TILE):
        rows = slice(b * SEQ, (b + 1) * SEQ)
        u = _dot(_mixer_norm(x_ref[rows, :], m_ref, g_ref), w_ref[...])
        uc_ref[rows, :] = u[:, 0:C_POOL]
        qk = u[:, C_POOL:C_POOL + QK_W]
        sq = qk * qk
        hi = sq.astype(BF16)
        lo = (sq - hi.astype(F32)).astype(BF16)
        sums = []
        for c0 in range(0, QK_W, MXU_DIM):
            c1 = min(c0 + MXU_DIM, QK_W)
            tile = ones[0:c1 - c0, 0:c1 - c0]
            sums.append(_dot(hi[:, c0:c1], tile) + _dot(lo[:, c0:c1], tile))
        ms = jnp.concatenate(sums, axis=-1) * (1.0 / HEAD_DIM)
        n = qk * lax.rsqrt(ms + RMS_EPS) * ng_ref[...]
        r = n * cos_ref[rows, :] + _swap_pairs(n) * sin_ref[rows, :]
        k = r[:, HD_Q:QK_W]
        v = u[:, C_POOL + QK_W:ODD_IN]
        q_ref[rows, :] = (r[:, 0:HD_Q] * ATT_SCALE).astype(BF16)
        k_ref[rows, :] = k.astype(BF16)
        v_ref[rows, :] = v.astype(BF16)
        kv.append((k, v))

    @pl.when(pl.program_id(0) < PROMPT_TILES)
    def _():
        for b, (k, v) in enumerate(kv):
            _store_heads(sk_ref, b, k, HKV_D)
            _store_heads(sv_ref, b, v, HKV_D)


def _rope_tile_index(i):
    return jnp.where(i < PROMPT_TILES, 0, 1 + (i - PROMPT_TILES) % TILES_PER_SAMPLE)


def _odd_in(x, mods, gains, w_in, qk_gain, cos_t, sin_t, ones_bd, layer):
    tok = lambda w: pl.BlockSpec((TM, w), lambda i: (i, 0))
    state = jax.ShapeDtypeStruct((BATCH, 1, HKV_D, HEAD_DIM, SEQ), F32)
    return pl.pallas_call(
        _odd_in_kernel,
        grid=(N_TILES,),
        in_specs=[
            tok(D_MODEL), _mod_spec(), _gain_spec(layer),
            pl.BlockSpec((D_MODEL, ODD_IN), lambda i: (0, 0)),
            pl.BlockSpec((1, QK_W), lambda i: (0, 0)),
            pl.BlockSpec((TM, QK_W), lambda i: (_rope_tile_index(i), 0)),
            pl.BlockSpec((TM, QK_W), lambda i: (_rope_tile_index(i), 0)),
            pl.BlockSpec((MXU_DIM, MXU_DIM), lambda i: (0, 0)),
        ],
        out_specs=[tok(C_POOL), tok(HD_Q), tok(HD_KV), tok(HD_KV), _state_spec(HKV_D), _state_spec(HKV_D)],
        out_shape=[
            jax.ShapeDtypeStruct((N_TOK, C_POOL), F32),
            jax.ShapeDtypeStruct((N_TOK, HD_Q), BF16),
            jax.ShapeDtypeStruct((N_TOK, HD_KV), BF16),
            jax.ShapeDtypeStruct((N_TOK, HD_KV), BF16),
            state, state,
        ],
        compiler_params=_params(1),
        name="odd_in",
    )(x, mods, gains, w_in, qk_gain, cos_t, sin_t, ones_bd)


def _low_half(rows):
    return lax.broadcasted_iota(jnp.int32, (rows, LANES), 1) < HEAD_DIM


def _split_pair(qp):
    low = _low_half(qp.shape[0])
    zero = jnp.zeros_like(qp)
    return jnp.concatenate([jnp.where(low, qp, zero), jnp.where(low, zero, qp)], axis=0)


def _pair_heads_t(ref):
    return jnp.concatenate([ref[0], ref[1]], axis=0)


def _with_ones(v):
    return jnp.concatenate([v, jnp.ones_like(v)], axis=1)


def _softmax_pv(scores, values):
    m = scores[0].max(axis=-1, keepdims=True)
    for s in scores[1:]:
        m = jnp.maximum(m, s.max(axis=-1, keepdims=True))
    acc = None
    for s, v in zip(scores, values):
        o = _dot(jnp.exp(s - m).astype(BF16), v)
        acc = o if acc is None else acc + o
    return acc[:, 0:LANES] / acc[:, LANES:2 * LANES]


def _merge_pair(o):
    m = o.shape[0] // 2
    return jnp.where(_low_half(m), o[0:m], o[m:2 * m])


def _short_conv(z, cw_ref, cb_ref, seq_len):
    rows = z.shape[0]
    pos = lax.broadcasted_iota(jnp.int32, z.shape, 0) & (seq_len - 1)
    z_prev = jnp.where(pos == 0, 0.0, pltpu.roll(z, 1, 0))
    z_next = jnp.where(pos == seq_len - 1, 0.0, pltpu.roll(z, rows - 1, 0))
    y = z_prev * cw_ref[0:1, :]
    y = y + z * cw_ref[1:2, :]
    y = y + z_next * cw_ref[2:3, :]
    return y + cb_ref[...]


def _pool_mix(uc, pw_ref, ps_ref, seq_len):
    pos = lax.broadcasted_iota(jnp.int32, (seq_len, POOL_C), 0)
    pad = jnp.zeros((POOL_PAD, POOL_C), F32)
    n_ext = seq_len + 2 * POOL_PAD
    outs = []
    for gi, win in enumerate(POOL_WINDOWS):
        ug = uc[:, gi * POOL_C:(gi + 1) * POOL_C]
        run = jnp.concatenate([pad, ug, pad], axis=0)
        span = 1
        while span < win:
            run = run + pltpu.roll(run, span, 0)
            span *= 2
        back = win // 2 - 1
        if back:
            run = pltpu.roll(run, n_ext - back, 0)
        wsum = run[POOL_PAD:POOL_PAD + seq_len, :]
        cnt = jnp.minimum(pos + (win - win // 2), seq_len) - jnp.maximum(pos - win // 2, 0)
        pooled = (wsum / cnt.astype(F32) - ug).astype(BF16)
        outs.append(_dot(pooled, pw_ref[gi]))
    return jnp.concatenate(outs, axis=-1) * ps_ref[...]


def _even_prompt_kernel(q_ref, k_ref, v_ref, bg_ref, z_ref, cw_ref, cb_ref, fa_ref, fb_ref):
    for b in range(SEQ_PER_TILE):
        rows = slice(b * SEQ, (b + 1) * SEQ)
        pairs = [slice(p * LANES, (p + 1) * LANES) for p in range(H_A // 2)]
        scores = [_dot_t(_split_pair(q_ref[rows, sl]), k_ref[rows, sl]) for sl in pairs]
        outs = [_merge_pair(_softmax_pv([s], [_with_ones(v_ref[rows, sl])])) for s, sl in zip(scores, pairs)]
        fa_ref[rows, :] = jnp.concatenate(outs, axis=-1).astype(BF16)
    fb_ref[...] = (bg_ref[...] * _short_conv(z_ref[...], cw_ref, cb_ref, SEQ)).astype(BF16)


def _even_prompt(q, k, v, bg, z, conv_w, conv_b, idx):
    tile = lambda w: pl.BlockSpec((TM, w), lambda t: (t, 0))
    return pl.pallas_call(
        _even_prompt_kernel,
        grid=(PROMPT_TILES,),
        in_specs=[tile(HD_A), tile(HD_A), tile(HD_A), tile(C_B), tile(C_B),
                  pl.BlockSpec((None, CONV_W, C_B), lambda t: (idx, 0, 0)),
                  pl.BlockSpec((None, 1, C_B), lambda t: (idx, 0, 0))],
        out_specs=[tile(HD_A), tile(C_B)],
        out_shape=[
            jax.ShapeDtypeStruct((N_PROMPT, HD_A), BF16),
            jax.ShapeDtypeStruct((N_PROMPT, C_B), BF16),
        ],
        compiler_params=_params(1),
        name="even_prompt",
    )(q, k, v, bg, z, conv_w, conv_b)


def _na_bias_tiles(rpb_h):
    c = lax.broadcasted_iota(jnp.int32, (GRID_W, GRID_W), 0)
    kc = lax.broadcasted_iota(jnp.int32, (GRID_W, GRID_W), 1)
    cs = jnp.clip(c - NA_COLS // 2, 0, GRID_W - NA_COLS)
    valid = (kc >= cs) & (kc < cs + NA_COLS)
    tiles = []
    for ro in range(2 * NA_ROWS - 1):
        g = jnp.broadcast_to(rpb_h[ro:ro + 1, :], (GRID_W, LANES))
        skew = pltpu.roll(g, LANES - (NA_COLS - 1), 1, stride=1, stride_axis=0)
        tiles.append(jnp.where(valid, skew[:, 0:GRID_W], NEG_INF))
    return tiles


def _even_sample_kernel(*refs, n_cast, cast_mixers):
    n_mix_in, n_mix_out = (4, 3) if cast_mixers else (0, 0)
    q_ref, k_ref, v_ref, bg_ref, z_ref, ck_ref, cv_ref, rpb_ref, cw_ref, cb_ref = refs[0:10]
    out0 = 10 + n_cast + n_mix_in
    fa_ref, fb_ref = refs[out0:out0 + 2]
    bias_ref = refs[out0 + 2 + n_cast + n_mix_out]
    _cast_weights(refs[10:10 + n_cast], refs[out0 + 2:out0 + 2 + n_cast])
    if cast_mixers:
        _cast_mixer_weights(*refs[10 + n_cast:out0], *refs[out0 + 2 + n_cast:out0 + 2 + n_cast + n_mix_out])
    win = NA_ROWS * GRID_W
    for h in range(2):
        tiles = _na_bias_tiles(rpb_ref[h])
        for var in range(NA_ROWS):
            bias_ref[var, h * GRID_W:(h + 1) * GRID_W, :] = jnp.concatenate(
                [tiles[i - var + NA_ROWS - 1] for i in range(NA_ROWS)], axis=1)
    ck_t = _pair_heads_t(ck_ref).astype(BF16)
    cv_ext = _with_ones(_pair_heads_t(cv_ref).T.astype(BF16))

    def rows(it, carry):
        staged = []
        for j in range(ROW_UNROLL):
            r = it * ROW_UNROLL + j
            rs = jnp.clip(r - NA_ROWS // 2, 0, GRID_ROWS - NA_ROWS)
            q0 = pl.multiple_of(r * GRID_W, GRID_W)
            k0 = pl.multiple_of(rs * GRID_W, GRID_W)
            q2 = _split_pair(q_ref[pl.ds(q0, GRID_W), :])
            s_loc = _dot_t(q2, k_ref[pl.ds(k0, win), :]) + bias_ref[r - rs]
            s_ctx = _dot(q2, ck_t)
            staged.append((q0, k0, s_loc, s_ctx))
        for q0, k0, s_loc, s_ctx in staged:
            o = _softmax_pv([s_loc, s_ctx], [_with_ones(v_ref[pl.ds(k0, win), :]), cv_ext])
            fa_ref[pl.ds(q0, GRID_W), :] = _merge_pair(o).astype(BF16)
        return carry

    lax.fori_loop(0, GRID_ROWS // ROW_UNROLL, rows, 0)
    fb_ref[...] = (bg_ref[...] * _short_conv(z_ref[...], cw_ref, cb_ref, DEC_SEQ)).astype(BF16)


def _even_sample(q, k, v, bg, z, ctx_k, ctx_v, rpb, conv_w, conv_b, idx, ffn_w1, ffn_w2, cast_jobs, mixer_w=None):
    first = N_PROMPT // DEC_SEQ
    n_pair = H_A // 2
    n_steps = DEC_BATCH * n_pair
    step_of = lambda b, p: b * n_pair + p
    seq_in = pl.BlockSpec((DEC_SEQ, LANES), lambda b, p: (first + b, p))
    seq_out = pl.BlockSpec((DEC_SEQ, LANES), lambda b, p: (b, p))
    ctx = pl.BlockSpec((None, None, 2, HEAD_DIM, PAST_LEN), lambda b, p: (b, idx, p, 0, 0))
    cast_in, cast_out, cast_shape = _cast_specs(cast_jobs, n_steps, step_of)
    n_cast = len(cast_in)
    cast_args = [ffn_w1, ffn_w2] * len(cast_jobs)
    if mixer_w is not None:
        assert n_steps == MIX_CAST_STEPS
        mix_in, mix_out, mix_shape = _mixer_cast_specs(step_of)
        cast_in, cast_out, cast_shape = cast_in + mix_in, cast_out + mix_out, cast_shape + mix_shape
        cast_args += [mixer_w[0], mixer_w[1], mixer_w[2], mixer_w[2]]
    return pl.pallas_call(
        functools.partial(_even_sample_kernel, n_cast=n_cast, cast_mixers=mixer_w is not None),
        grid=(DEC_BATCH, n_pair),
        in_specs=[seq_in, seq_in, seq_in, seq_in, seq_in, ctx, ctx,
                  pl.BlockSpec((None, 2, 2 * NA_ROWS - 1, LANES), lambda b, p: (idx, p, 0, 0)),
                  pl.BlockSpec((None, CONV_W, LANES), lambda b, p: (idx, 0, p)),
                  pl.BlockSpec((None, 1, LANES), lambda b, p: (idx, 0, p))] + cast_in,
        out_specs=[seq_out, seq_out] + cast_out,
        out_shape=[
            jax.ShapeDtypeStruct((N_SAMPLE, HD_A), BF16),
            jax.ShapeDtypeStruct((N_SAMPLE, C_B), BF16),
        ] + cast_shape,
        scratch_shapes=[pltpu.VMEM((NA_ROWS, 2 * GRID_W, NA_ROWS * GRID_W), F32)],
        compiler_params=_params(2),
        name="even_sample",
    )(q, k, v, bg, z, ctx_k, ctx_v, rpb, conv_w, conv_b, *cast_args)


def _split_groups(q):
    low = _low_half(q.shape[0])
    zero = jnp.zeros((q.shape[0], LANES), BF16)
    pairs = [q[:, j * LANES:(j + 1) * LANES] for j in range(GQA_GROUP)]
    return jnp.concatenate([jnp.where(low, p, zero) for p in pairs] + [jnp.where(low, zero, p) for p in pairs],
                           axis=0)


def _merge_groups(o, m):
    low = _low_half(m)
    half = GQA_GROUP * m
    return jnp.concatenate([jnp.where(low, o[j * m:(j + 1) * m], o[half + j * m:half + (j + 1) * m])
                            for j in range(GQA_GROUP)], axis=-1)


def _odd_prompt_kernel(uc_ref, q_ref, k_ref, v_ref, pw_ref, ps_ref, fc_ref, fd_ref):
    seqs = [slice(b * SEQ, (b + 1) * SEQ) for b in range(SEQ_PER_TILE)]
    scores = [_dot_t(_split_groups(q_ref[rows, :]), k_ref[rows, :]) for rows in seqs]
    for s, rows in zip(scores, seqs):
        o = _softmax_pv([s], [_with_ones(v_ref[rows, :])])
        fd_ref[rows, :] = _merge_groups(o, SEQ).astype(BF16)
    for rows in seqs:
        fc_ref[rows, :] = _pool_mix(uc_ref[rows, :], pw_ref, ps_ref, SEQ).astype(BF16)


def _odd_prompt(uc, q, k, v, pool_w, pool_scale, idx):
    seq = lambda w: pl.BlockSpec((TM, w), lambda t: (t, 0))
    return pl.pallas_call(
        _odd_prompt_kernel,
        grid=(PROMPT_TILES,),
        in_specs=[seq(C_POOL), seq(HD_Q), seq(HD_KV), seq(HD_KV),
                  pl.BlockSpec((None, N_POOL, POOL_C, POOL_C), lambda t: (idx, 0, 0, 0)),
                  pl.BlockSpec((None, 1, C_POOL), lambda t: (idx, 0, 0))],
        out_specs=[seq(C_POOL), seq(HD_Q)],
        out_shape=[
            jax.ShapeDtypeStruct((N_PROMPT, C_POOL), BF16),
            jax.ShapeDtypeStruct((N_PROMPT, HD_Q), BF16),
        ],
        compiler_params=_params(1),
        name="odd_prompt",
    )(uc, q, k, v, pool_w, pool_scale)


def _odd_sample_kernel(*refs, n_cast):
    uc_ref, q_ref, k_ref, v_ref, ck_ref, cv_ref, pw_ref, ps_ref = refs[0:8]
    fc_ref, fd_ref = refs[8 + n_cast:10 + n_cast]
    kt_ref, vext_ref, ckpair_ref, cvext_ref = refs[10 + 2 * n_cast:]
    _cast_weights(refs[8:8 + n_cast], refs[10 + n_cast:10 + 2 * n_cast])

    @pl.when(pl.program_id(1) == 0)
    def _():
        fc_ref[...] = _pool_mix(uc_ref[...], pw_ref, ps_ref, DEC_SEQ).astype(BF16)
        kt_ref[...] = k_ref[...].astype(F32).T.astype(BF16)
        vext_ref[...] = _with_ones(v_ref[...])
        ckpair_ref[...] = _pair_heads_t(ck_ref).astype(BF16)
        cvext_ref[...] = _with_ones(_pair_heads_t(cv_ref).T.astype(BF16))

    k_t = kt_ref[...]
    ck_t = ckpair_ref[...]
    low = _low_half(Q_BLK)
    scores = []
    for j in range(GQA_GROUP):
        q2 = _split_pair(q_ref[:, j * LANES:(j + 1) * LANES])
        scores.append([_dot(q2, k_t), _dot(q2, ck_t)])
    outs = []
    for s in scores:
        o = _softmax_pv(s, [vext_ref[...], cvext_ref[...]])
        outs.append(jnp.where(low, o[0:Q_BLK], o[Q_BLK:2 * Q_BLK]))
    fd_ref[...] = jnp.concatenate(outs, axis=-1).astype(BF16)


def _odd_sample(uc, q, k, v, ctx_k, ctx_v, pool_w, pool_scale, idx, ffn_w1, ffn_w2, cast_jobs):
    first = N_PROMPT // DEC_SEQ
    n_qb = DEC_SEQ // Q_BLK
    first_q = N_PROMPT // Q_BLK
    whole_in = lambda w: pl.BlockSpec((DEC_SEQ, w), lambda b, i: (first + b, 0))
    ctx = pl.BlockSpec((None, None, HKV_D, HEAD_DIM, PAST_LEN), lambda b, i: (b, idx, 0, 0, 0))
    cast_in, cast_out, cast_shape = _cast_specs(cast_jobs, DEC_BATCH * n_qb, lambda b, i: b * n_qb + i)
    return pl.pallas_call(
        functools.partial(_odd_sample_kernel, n_cast=len(cast_in)),
        grid=(DEC_BATCH, n_qb),
        in_specs=[whole_in(C_POOL),
                  pl.BlockSpec((Q_BLK, HD_Q), lambda b, i: (first_q + b * n_qb + i, 0)),
                  whole_in(HD_KV), whole_in(HD_KV), ctx, ctx,
                  pl.BlockSpec((None, N_POOL, POOL_C, POOL_C), lambda b, i: (idx, 0, 0, 0)),
                  pl.BlockSpec((None, 1, C_POOL), lambda b, i: (idx, 0, 0))] + cast_in,
        out_specs=[pl.BlockSpec((DEC_SEQ, C_POOL), lambda b, i: (b, 0)),
                   pl.BlockSpec((Q_BLK, HD_Q), lambda b, i: (b * n_qb + i, 0))] + cast_out,
        out_shape=[
            jax.ShapeDtypeStruct((N_SAMPLE, C_POOL), BF16),
            jax.ShapeDtypeStruct((N_SAMPLE, HD_Q), BF16),
        ] + cast_shape,
        scratch_shapes=[pltpu.VMEM((LANES, DEC_SEQ), BF16), pltpu.VMEM((DEC_SEQ, 2 * LANES), BF16),
                        pltpu.VMEM((LANES, PAST_LEN), BF16), pltpu.VMEM((PAST_LEN, 2 * LANES), BF16)],
        compiler_params=_params(2),
        name="odd_sample",
    )(uc, q, k, v, ctx_k, ctx_v, pool_w, pool_scale, *([ffn_w1, ffn_w2] * len(cast_jobs)))


def _rope_tables():
    t = np.arange(DEC_SEQ)
    n_freq = HEAD_DIM // 4
    inv = ROPE_BASE ** (-np.arange(n_freq, dtype=np.float64) / n_freq)
    ang = np.concatenate([(t // GRID_W)[:, None] * inv, (t % GRID_W)[:, None] * inv], axis=-1)
    cos = np.repeat(np.cos(ang), 2, axis=-1)
    sin = np.repeat(np.sin(ang), 2, axis=-1) * np.tile([-1.0, 1.0], HEAD_DIM // 2)
    n_heads = HQ_D + HKV_D
    cos = np.concatenate([np.ones((TM, QK_W)), np.tile(cos, (1, n_heads))], axis=0)
    sin = np.concatenate([np.zeros((TM, QK_W)), np.tile(sin, (1, n_heads))], axis=0)
    return jnp.asarray(cos, F32), jnp.asarray(sin, F32)


def _head_ones():
    head = np.arange(MXU_DIM) // HEAD_DIM
    return jnp.asarray(head[:, None] == head[None, :], BF16)


def kernel(x_prompt, x_sample, cache_a_k, cache_a_v, cache_d_k, cache_d_v, c, c_ctx, mod_w, mod_b, norm_w,
           ffn_w1, ffn_w2, ev_w_in, ev_rpb, ev_conv_w, ev_conv_b, ev_w_out, od_w_in, od_pool_w,
           od_pool_scale, od_q_norm, od_k_norm, od_w_out):
    xs = (x_prompt.reshape(N_PROMPT, D_MODEL), x_sample.reshape(N_SAMPLE, D_MODEL))
    cond = jnp.concatenate([c_ctx[None, :], c, jnp.zeros((COND_PAD - N_COND, D_MODEL), F32)], axis=0)
    mod_b = mod_b.reshape(DEPTH, 1, N_MOD * D_MODEL)
    mods, w1_a, w2_a = _first_modulation(cond, mod_w, mod_b, ffn_w1, ffn_w2)
    cache_a_k, cache_a_v, cache_d_k, cache_d_v = (jnp.swapaxes(t, -1, -2)
                                                  for t in (cache_a_k, cache_a_v, cache_d_k, cache_d_v))
    rpb = jnp.pad(ev_rpb, ((0, 0), (0, 0), (0, 0), (0, LANES - ev_rpb.shape[-1])))
    conv_b = ev_conv_b[:, None, :]
    pool_w, pool_scale = od_pool_w.astype(BF16), od_pool_scale[:, None, :]
    cos_t, sin_t = _rope_tables()
    ones_bd = _head_ones()
    states = []
    for l in range(DEPTH):
        i = l // 2
        last = l == DEPTH - 1
        mods = mods.reshape(COND_PAD, N_MOD, D_MODEL)
        if l == 0:
            x, ev_w_in_b = _ffn(xs, mods, norm_w, w1_a, w2_a, l, 0, ev_w_in=ev_w_in)
        else:
            x, = _ffn(xs, mods, norm_w, w1_a, w2_a, l, 0)
        cast_jobs = [(l, 1)] + ([] if last else [(l + 1, 0)])
        if l % 2 == 0:
            q, k, v, bg, z, s_k, s_v = _even_in(x, mods, norm_w, ev_w_in_b, l)
            fa_p, fb_p = _even_prompt(q, k, v, bg, z, ev_conv_w, conv_b, i)
            fa_s, fb_s, *cast = _even_sample(q, k, v, bg, z, cache_a_k, cache_a_v, rpb, ev_conv_w, conv_b, i,
                                             ffn_w1, ffn_w2, cast_jobs, mixer_w=(ev_w_out, od_w_in, od_w_out))
            ev_w_out_b, od_w_in_b, od_w_out_b = cast[2 * len(cast_jobs):]
            w_out = ev_w_out_b
        else:
            gain = jnp.concatenate([jnp.tile(od_q_norm[i], HQ_D), jnp.tile(od_k_norm[i], HKV_D)])[None, :]
            uc, q, k, v, s_k, s_v = _odd_in(x, mods, norm_w, od_w_in_b, gain, cos_t, sin_t, ones_bd, l)
            fa_p, fb_p = _odd_prompt(uc, q, k, v, pool_w, pool_scale, i)
            fa_s, fb_s, *cast = _odd_sample(uc, q, k, v, cache_d_k, cache_d_v, pool_w, pool_scale, i,
                                            ffn_w1, ffn_w2, cast_jobs)
            w_out = od_w_out_b
        w1_b, w2_b = cast[0:2]
        if not last:
            w1_a, w2_a = cast[2:4]
        states.append((jnp.swapaxes(s_k, -1, -2), jnp.swapaxes(s_v, -1, -2)))
        outs = _ffn((x,), mods, norm_w, w1_b, w2_b, l, 2, feats=(fa_p, fb_p, fa_s, fb_s), w_out=w_out,
                    split_out=last, next_mod=None if last else (cond, mod_w, mod_b))
        if last:
            xs = tuple(outs)
        else:
            xs, mods = (outs[0],), outs[1]
    y_prompt = xs[0].reshape(BATCH, SEQ, D_MODEL)
    y_sample = xs[1].reshape(DEC_BATCH, DEC_SEQ, D_MODEL)
    return (y_prompt, y_sample, states[0][0], states[0][1], states[1][0], states[1][1])
```

```python
import functools

import jax
import jax.numpy as jnp
import numpy as np
from jax import lax
from jax.experimental import pallas as pl
from jax.experimental.pallas import tpu as pltpu

D_MODEL = 1024
BATCH = 32
SEQ = 256
DEPTH = 2
DEC_BATCH = 2
DEC_SEQ = 2048
PAST_LEN = 256
GRID_W = 64
HEAD_DIM = 64
N_MOD = 9
D_FF = 2816
FFN_RES = 0.5
H_A = 8
NA_ROWS = 8
NA_COLS = 16
C_B = 512
CONV_W = 3
C_POOL = 512
POOL_WINDOWS = (2, 4, 8, 16)
N_POOL = 4
POOL_C = C_POOL // N_POOL
HQ_D = 8
HKV_D = 2
GQA_GROUP = HQ_D // HKV_D
ROPE_BASE = 10000.0
EVEN_IN = 3 * H_A * HEAD_DIM + 3 * C_B
ODD_IN = C_POOL + (HQ_D + 2 * HKV_D) * HEAD_DIM
RMS_EPS = 1e-6
NEG_INF = -1e30
ATT_SCALE = HEAD_DIM ** -0.5

LANES = 128
GRID_ROWS = DEC_SEQ // GRID_W
N_PROMPT = BATCH * SEQ
N_SAMPLE = DEC_BATCH * DEC_SEQ
N_TOK = N_PROMPT + N_SAMPLE
N_COND = 1 + DEC_BATCH
COND_PAD = 8
HD_A = H_A * HEAD_DIM
HD_Q = HQ_D * HEAD_DIM
HD_KV = HKV_D * HEAD_DIM
QK_W = HD_Q + HD_KV
POOL_PAD = 8

TM = 512
N_TILES = N_TOK // TM
PROMPT_TILES = N_PROMPT // TM
TILES_PER_SAMPLE = DEC_SEQ // TM
SEQ_PER_TILE = TM // SEQ
TM_FFN = 1024
FFN_HALF = TM_FFN // 2
MIX_CAST_STEPS = N_PROMPT // TM_FFN
MIX_CAST_ROWS = D_MODEL // MIX_CAST_STEPS
assert MIX_CAST_ROWS == 2 * HEAD_DIM and (DEPTH + 1) // 2 == 1 and DEPTH // 2 == 1
MXU_DIM = 256
FF_CHUNKS = tuple((lo, min(lo + 3 * MXU_DIM, D_FF)) for lo in range(0, D_FF, 3 * MXU_DIM))
MOD_TN = 1152
Q_BLK = 256
ROW_UNROLL = 8
VMEM_LIMIT = 60 * 1024 * 1024

F32 = jnp.float32
BF16 = jnp.bfloat16


def _params(n_grid):
    return pltpu.CompilerParams(dimension_semantics=("arbitrary",) * n_grid, vmem_limit_bytes=VMEM_LIMIT)


def _cond_of_tile(i, tm=TM):
    n_prompt = N_PROMPT // tm
    return jnp.where(i < n_prompt, 0, 1 + (i - n_prompt) // (DEC_SEQ // tm))


def _prompt_tile(i, tm=TM):
    return jnp.minimum(i, N_PROMPT // tm - 1)


def _sample_tile(i, tm=TM):
    return jnp.maximum(i - N_PROMPT // tm, 0)


def _rms(x, g):
    return x * lax.rsqrt(jnp.mean(x * x, axis=-1, keepdims=True) + RMS_EPS) * g


def _sigmoid(x):
    return 1.0 / (1.0 + jnp.exp(-x))


def _dot(a, b):
    return jnp.dot(a, b, preferred_element_type=F32)


def _dot_t(a, b):
    return lax.dot_general(a, b, (((1,), (1,)), ((), ())), preferred_element_type=F32)


def _mod_kernel(c_ref, w_ref, b_ref, w1f_ref, w2f_ref, o_ref, w1b_ref, w2b_ref):
    _cast_weights([w1f_ref, w2f_ref], [w1b_ref, w2b_ref])
    c = c_ref[...]
    sc = (c * _sigmoid(c)).astype(BF16)
    o_ref[...] = _dot(sc, w_ref[...].astype(BF16)) + b_ref[...]


def _modulation(cond, mod_w, mod_b, ffn_w1, ffn_w2):
    n_col = N_MOD * D_MODEL
    n_blk = n_col // MOD_TN
    cast_in, cast_out, cast_shape = _cast_specs([(0, 0)], DEPTH * n_blk, lambda l, j: l * n_blk + j)
    return pl.pallas_call(
        _mod_kernel,
        grid=(DEPTH, n_blk),
        in_specs=[
            pl.BlockSpec((COND_PAD, D_MODEL), lambda l, j: (0, 0)),
            pl.BlockSpec((None, D_MODEL, MOD_TN), lambda l, j: (l, 0, j)),
            pl.BlockSpec((None, 1, MOD_TN), lambda l, j: (l, 0, j)),
        ] + cast_in,
        out_specs=[pl.BlockSpec((None, COND_PAD, MOD_TN), lambda l, j: (l, 0, j))] + cast_out,
        out_shape=[jax.ShapeDtypeStruct((DEPTH, COND_PAD, n_col), F32)] + cast_shape,
        compiler_params=_params(2),
        name="modulation",
    )(cond, mod_w, mod_b.reshape(DEPTH, 1, n_col), ffn_w1, ffn_w2)


def _mod_spec(layer, tm=TM):
    return pl.BlockSpec((None, None, N_MOD, D_MODEL), lambda i: (layer, _cond_of_tile(i, tm), 0, 0))


def _gain_spec(layer):
    return pl.BlockSpec((None, 6, D_MODEL), lambda i: (layer, 0, 0))


def _cast_specs(jobs, n_steps, step_of):
    r1, r2 = D_MODEL // n_steps, D_FF // n_steps
    in_specs, out_specs, out_shape = [], [], []
    for layer, which in jobs:
        in_specs += [pl.BlockSpec((None, None, r1, 2 * D_FF), lambda *g, lw=(layer, which): (*lw, step_of(*g), 0)),
                     pl.BlockSpec((None, None, r2, D_MODEL), lambda *g, lw=(layer, which): (*lw, step_of(*g), 0))]
        out_specs += [pl.BlockSpec((r1, 2 * D_FF), lambda *g: (step_of(*g), 0)),
                      pl.BlockSpec((r2, D_MODEL), lambda *g: (step_of(*g), 0))]
        out_shape += [jax.ShapeDtypeStruct((D_MODEL, 2 * D_FF), BF16), jax.ShapeDtypeStruct((D_FF, D_MODEL), BF16)]
    return in_specs, out_specs, out_shape


def _cast_weights(f32_refs, bf16_refs):
    for src, dst in zip(f32_refs, bf16_refs):
        dst[...] = src[...].astype(BF16)


def _ev_in_cast_specs():
    step = lambda i: jnp.minimum(i, MIX_CAST_STEPS - 1)
    return ([pl.BlockSpec((None, MIX_CAST_ROWS, EVEN_IN), lambda i: (0, step(i), 0))],
            [pl.BlockSpec((MIX_CAST_ROWS, EVEN_IN), lambda i: (step(i), 0))],
            [jax.ShapeDtypeStruct((D_MODEL, EVEN_IN), BF16)])


def _mixer_cast_specs(step_of):
    rows = MIX_CAST_ROWS
    n_pool_blk = C_POOL // rows

    def od_out_block(s, g):
        return jnp.where(s < n_pool_blk, 2 * s + g, C_POOL // HEAD_DIM + g * GQA_GROUP + s - n_pool_blk)

    widths = (D_MODEL, ODD_IN, D_MODEL)
    in_specs = [pl.BlockSpec((None, rows, D_MODEL), lambda *g: (0, step_of(*g), 0)),
                pl.BlockSpec((None, rows, ODD_IN), lambda *g: (0, step_of(*g), 0)),
                pl.BlockSpec((None, HEAD_DIM, D_MODEL), lambda *g: (0, od_out_block(step_of(*g), 0), 0)),
                pl.BlockSpec((None, HEAD_DIM, D_MODEL), lambda *g: (0, od_out_block(step_of(*g), 1), 0))]
    out_specs = [pl.BlockSpec((rows, w), lambda *g: (step_of(*g), 0)) for w in widths]
    out_shape = [jax.ShapeDtypeStruct((D_MODEL, w), BF16) for w in widths]
    return in_specs, out_specs, out_shape


def _cast_mixer_weights(ev_out_f, od_in_f, od_out_lo_f, od_out_hi_f, ev_out_b, od_in_b, od_out_b):
    ev_out_b[...] = ev_out_f[...].astype(BF16)
    w = od_in_f[...]
    q = w[:, C_POOL:C_POOL + HD_Q]
    heads = [q[:, (g * GQA_GROUP + j) * HEAD_DIM:(g * GQA_GROUP + j + 1) * HEAD_DIM]
             for j in range(GQA_GROUP) for g in range(HKV_D)]
    od_in_b[...] = jnp.concatenate([w[:, 0:C_POOL]] + heads + [w[:, C_POOL + HD_Q:]], axis=1).astype(BF16)
    od_out_b[...] = jnp.concatenate([od_out_lo_f[...], od_out_hi_f[...]], axis=0).astype(BF16)


def _ffn_kernel(*refs, sub, split_in, mix, split_out, cast_ev_in):
    refs = list(refs)
    x_refs = [refs.pop(0) for _ in range(2 if split_in else 1)]
    feat_refs = [refs.pop(0) for _ in range(4 if mix else 0)]
    m_ref, g_ref = refs.pop(0), refs.pop(0)
    wo_ref = refs.pop(0) if mix else None
    w1_ref, w2_ref = refs.pop(0), refs.pop(0)
    if cast_ev_in:
        @pl.when(pl.program_id(0) < MIX_CAST_STEPS)
        def _():
            refs[-1][...] = refs[0][...].astype(BF16)
        refs = refs[1:-1]
    out_refs = refs
    is_prompt = pl.program_id(0) < N_PROMPT // TM_FFN
    shift = m_ref[3 * sub:3 * sub + 1, :]
    scale = m_ref[3 * sub + 1:3 * sub + 2, :]
    gate = m_ref[3 * sub + 2:3 * sub + 3, :]
    for half in range(TM_FFN // FFN_HALF):
        rows = slice(half * FFN_HALF, (half + 1) * FFN_HALF)
        if split_in:
            x = jnp.where(is_prompt, x_refs[0][rows, :], x_refs[1][rows, :])
        else:
            x = x_refs[0][rows, :]
        if mix:
            fa = jnp.where(is_prompt, feat_refs[0][rows, :], feat_refs[2][rows, :])
            fb = jnp.where(is_prompt, feat_refs[1][rows, :], feat_refs[3][rows, :])
            mixed = _dot(jnp.concatenate([fa, fb], axis=1), wo_ref[...])
            x = x + m_ref[5:6, :] * _rms(mixed, g_ref[3:4, :])
        h = (_rms(x, g_ref[2 * sub:2 * sub + 1, :]) * (1.0 + scale) + shift).astype(BF16)
        acc = jnp.zeros((FFN_HALF, D_MODEL), F32)
        for lo, hi in FF_CHUNKS:
            a = _dot(h, w1_ref[:, lo:hi])
            u = _dot(h, w1_ref[:, D_FF + lo:D_FF + hi])
            act = (a * _sigmoid(a) * u).astype(BF16)
            acc = acc + _dot(act, w2_ref[lo:hi, :])
        y = x + FFN_RES * gate * _rms(acc, g_ref[2 * sub + 1:2 * sub + 2, :])
        if split_out:
            @pl.when(is_prompt)
            def _():
                out_refs[0][rows, :] = y

            @pl.when(jnp.logical_not(is_prompt))
            def _():
                out_refs[1][rows, :] = y
        else:
            out_refs[0][rows, :] = y


def _ffn(xs, mods, gains, w1, w2, layer, sub, feats=None, w_out=None, split_out=False, ev_w_in=None):
    split_in = len(xs) == 2
    mix = feats is not None
    cast_ev_in = ev_w_in is not None
    resident = pl.Buffered(1)
    half_w = D_MODEL // 2
    feat_specs, mix_w_spec = [], []
    if mix:
        feat_p = pl.BlockSpec((TM_FFN, half_w), lambda i: (_prompt_tile(i, TM_FFN), 0))
        feat_s = pl.BlockSpec((TM_FFN, half_w), lambda i: (_sample_tile(i, TM_FFN), 0))
        feat_specs = [feat_p, feat_p, feat_s, feat_s]
        mix_w_spec = [pl.BlockSpec((D_MODEL, D_MODEL), lambda i: (0, 0), pipeline_mode=resident)]
    cast_in, cast_out, cast_shape, cast_args = [], [], [], ()
    if cast_ev_in:
        cast_in, cast_out, cast_shape = _ev_in_cast_specs()
        cast_args = (ev_w_in,)
    tok =pl.BlockSpec((TM_FFN, D_MODEL), lambda i: (i, 0))
    prompt_tok = pl.BlockSpec((TM_FFN, D_MODEL), lambda i: (_prompt_tile(i, TM_FFN), 0))
    sample_tok = pl.BlockSpec((TM_FFN, D_MODEL), lambda i: (_sample_tile(i, TM_FFN), 0))
    if split_out:
        out_specs = [prompt_tok, sample_tok]
        out_shape = [jax.ShapeDtypeStruct((N_PROMPT, D_MODEL), F32),
                     jax.ShapeDtypeStruct((N_SAMPLE, D_MODEL), F32)]
    else:
        out_specs = [tok]
        out_shape = [jax.ShapeDtypeStruct((N_TOK, D_MODEL), F32)]
    return pl.pallas_call(
        functools.partial(_ffn_kernel, sub=sub, split_in=split_in, mix=mix, split_out=split_out,
                          cast_ev_in=cast_ev_in),
        grid=(N_TOK // TM_FFN,),
        in_specs=([prompt_tok, sample_tok] if split_in else [tok]) + feat_specs + [
            _mod_spec(layer, TM_FFN),
            _gain_spec(layer),
        ] + mix_w_spec + [
            pl.BlockSpec((D_MODEL, 2 * D_FF), lambda i: (0, 0), pipeline_mode=resident),
            pl.BlockSpec((D_FF, D_MODEL), lambda i: (0, 0), pipeline_mode=resident),
        ] + cast_in,
        out_specs=out_specs + cast_out,
        out_shape=out_shape + cast_shape,
        compiler_params=_params(1),
        name=f"ffn{sub}",
    )(*xs, *(feats or ()), mods, gains, *((w_out,) if mix else ()), w1, w2, *cast_args)


def _mixer_norm(x, m_ref, g_ref):
    return (_rms(x, g_ref[2:3, :]) * (1.0 + m_ref[4:5, :]) + m_ref[3:4, :]).astype(BF16)


def _store_heads(state_ref, b, x, n_heads):
    xt = x.T
    for h in range(n_heads):
        state_ref[b, 0, h] = xt[h * HEAD_DIM:(h + 1) * HEAD_DIM, :]


def _even_in_kernel(x_ref, m_ref, g_ref, w_ref, q_ref, k_ref, v_ref, bg_ref, z_ref, sk_ref, sv_ref):
    kv = []
    for b in range(SEQ_PER_TILE):
        rows = slice(b * SEQ, (b + 1) * SEQ)
        u = _dot(_mixer_norm(x_ref[rows, :], m_ref, g_ref), w_ref[...])
        k = u[:, HD_A:2 * HD_A]
        v = u[:, 2 * HD_A:3 * HD_A]
        q_ref[rows, :] = (u[:, 0:HD_A] * ATT_SCALE).astype(BF16)
        k_ref[rows, :] = k.astype(BF16)
        v_ref[rows, :] = v.astype(BF16)
        bg_ref[rows, :] = u[:, 3 * HD_A:3 * HD_A + C_B]
        z_ref[rows, :] = u[:, 3 * HD_A + C_B:3 * HD_A + 2 * C_B] * u[:, 3 * HD_A + 2 * C_B:3 * HD_A + 3 * C_B]
        kv.append((k, v))

    @pl.when(pl.program_id(0) < PROMPT_TILES)
    def _():
        for b, (k, v) in enumerate(kv):
            _store_heads(sk_ref, b, k, H_A)
            _store_heads(sv_ref, b, v, H_A)


def _state_spec(n_heads):
    return pl.BlockSpec((SEQ_PER_TILE, 1, n_heads, HEAD_DIM, SEQ), lambda i: (_prompt_tile(i), 0, 0, 0, 0))


def _even_in(x, mods, gains, w_in, layer):
    tok = lambda w: pl.BlockSpec((TM, w), lambda i: (i, 0))
    state = jax.ShapeDtypeStruct((BATCH, 1, H_A, HEAD_DIM, SEQ), F32)
    return pl.pallas_call(
        _even_in_kernel,
        grid=(N_TILES,),
        in_specs=[
            tok(D_MODEL), _mod_spec(layer), _gain_spec(layer),
            pl.BlockSpec((D_MODEL, EVEN_IN), lambda i: (0, 0)),
        ],
        out_specs=[tok(HD_A), tok(HD_A), tok(HD_A), tok(C_B), tok(C_B), _state_spec(H_A), _state_spec(H_A)],
        out_shape=[
            jax.ShapeDtypeStruct((N_TOK, HD_A), BF16),
            jax.ShapeDtypeStruct((N_TOK, HD_A), BF16),
            jax.ShapeDtypeStruct((N_TOK, HD_A), BF16),
            jax.ShapeDtypeStruct((N_TOK, C_B), F32),
            jax.ShapeDtypeStruct((N_TOK, C_B), F32),
            state, state,
        ],
        compiler_params=_params(1),
        name="even_in",
    )(x, mods, gains, w_in)


def _swap_pairs(x):
    n = x.shape[-1]
    lane = lax.broadcasted_iota(jnp.int32, x.shape, x.ndim - 1)
    return jnp.where(lane % 2 == 0, pltpu.roll(x, n - 1, x.ndim - 1), pltpu.roll(x, 1, x.ndim - 1))


def _odd_in_kernel(x_ref, m_ref, g_ref, w_ref, ng_ref, cos_ref, sin_ref, ones_ref,
                   uc_ref, q_ref, k_ref, v_ref, sk_ref, sv_ref):
    ones = ones_ref[...]
    kv = []
    for b in range(SEQ_PER_TILE):
        rows = slice(b * SEQ, (b + 1) * SEQ)
        u = _dot(_mixer_norm(x_ref[rows, :], m_ref, g_ref), w_ref[...])
        uc_ref[rows, :] = u[:, 0:C_POOL]
        qk = u[:, C_POOL:C_POOL + QK_W]
        sq = qk * qk
        hi = sq.astype(BF16)
        lo = (sq - hi.astype(F32)).astype(BF16)
        sums = []
        for c0 in range(0, QK_W, MXU_DIM):
            c1 = min(c0 + MXU_DIM, QK_W)
            tile = ones[0:c1 - c0, 0:c1 - c0]
            sums.append(_dot(hi[:, c0:c1], tile) + _dot(lo[:, c0:c1], tile))
        ms = jnp.concatenate(sums, axis=-1) * (1.0 / HEAD_DIM)
        n = qk * lax.rsqrt(ms + RMS_EPS) * ng_ref[...]
        r = n * cos_ref[rows, :] + _swap_pairs(n) * sin_ref[rows, :]
        k = r[:, HD_Q:QK_W]
        v = u[:, C_POOL + QK_W:ODD_IN]
        q_ref[rows, :] = (r[:, 0:HD_Q] * ATT_SCALE).astype(BF16)
        k_ref[rows, :] = k.astype(BF16)
        v_ref[rows, :] = v.astype(BF16)
        kv.append((k, v))

    @pl.when(pl.program_id(0) < PROMPT_TILES)
    def _():
        for b, (k, v) in enumerate(kv):
            _store_heads(sk_ref, b, k, HKV_D)
            _store_heads(sv_ref, b, v, HKV_D)


def _rope_tile_index(i):
    return jnp.where(i < PROMPT_TILES, 0, 1 + (i - PROMPT_TILES) % TILES_PER_SAMPLE)


def _odd_in(x, mods, gains, w_in, qk_gain, cos_t, sin_t, ones_bd, layer):
    tok = lambda w: pl.BlockSpec((TM, w), lambda i: (i, 0))
    state = jax.ShapeDtypeStruct((BATCH, 1, HKV_D, HEAD_DIM, SEQ), F32)
    return pl.pallas_call(
        _odd_in_kernel,
        grid=(N_TILES,),
        in_specs=[
            tok(D_MODEL), _mod_spec(layer), _gain_spec(layer),
            pl.BlockSpec((D_MODEL, ODD_IN), lambda i: (0, 0)),
            pl.BlockSpec((1, QK_W), lambda i: (0, 0)),
            pl.BlockSpec((TM, QK_W), lambda i: (_rope_tile_index(i), 0)),
            pl.BlockSpec((TM, QK_W), lambda i: (_rope_tile_index(i), 0)),
            pl.BlockSpec((MXU_DIM, MXU_DIM), lambda i: (0, 0)),
        ],
        out_specs=[tok(C_POOL), tok(HD_Q), tok(HD_KV), tok(HD_KV), _state_spec(HKV_D), _state_spec(HKV_D)],
        out_shape=[
            jax.ShapeDtypeStruct((N_TOK, C_POOL), F32),
            jax.ShapeDtypeStruct((N_TOK, HD_Q), BF16),
            jax.ShapeDtypeStruct((N_TOK, HD_KV), BF16),
            jax.ShapeDtypeStruct((N_TOK, HD_KV), BF16),
            state, state,
        ],
        compiler_params=_params(1),
        name="odd_in",
    )(x, mods, gains, w_in, qk_gain, cos_t, sin_t, ones_bd)


def _low_half(rows):
    return lax.broadcasted_iota(jnp.int32, (rows, LANES), 1) < HEAD_DIM


def _split_pair(qp):
    low = _low_half(qp.shape[0])
    zero = jnp.zeros_like(qp)
    return jnp.concatenate([jnp.where(low, qp, zero), jnp.where(low, zero, qp)], axis=0)


def _pair_heads_t(ref):
    return jnp.concatenate([ref[0], ref[1]], axis=0)


def _with_ones(v):
    return jnp.concatenate([v, jnp.ones_like(v)], axis=1)


def _softmax_pv(scores, values):
    m = scores[0].max(axis=-1, keepdims=True)
    for s in scores[1:]:
        m = jnp.maximum(m, s.max(axis=-1, keepdims=True))
    acc = None
    for s, v in zip(scores, values):
        o = _dot(jnp.exp(s - m).astype(BF16), v)
        acc = o if acc is None else acc + o
    return acc[:, 0:LANES] / acc[:, LANES:2 * LANES]


def _merge_pair(o):
    m = o.shape[0] // 2
    return jnp.where(_low_half(m), o[0:m], o[m:2 * m])


def _short_conv(z, cw_ref, cb_ref, seq_len):
    rows = z.shape[0]
    pos = lax.broadcasted_iota(jnp.int32, z.shape, 0) & (seq_len - 1)
    z_prev = jnp.where(pos == 0, 0.0, pltpu.roll(z, 1, 0))
    z_next = jnp.where(pos == seq_len - 1, 0.0, pltpu.roll(z, rows - 1, 0))
    y = z_prev * cw_ref[0:1, :]
    y = y + z * cw_ref[1:2, :]
    y = y + z_next * cw_ref[2:3, :]
    return y + cb_ref[...]


def _pool_mix(uc, pw_ref, ps_ref, seq_len):
    pos = lax.broadcasted_iota(jnp.int32, (seq_len, POOL_C), 0)
    pad = jnp.zeros((POOL_PAD, POOL_C), F32)
    n_ext = seq_len + 2 * POOL_PAD
    outs = []
    for gi, win in enumerate(POOL_WINDOWS):
        ug = uc[:, gi * POOL_C:(gi + 1) * POOL_C]
        run = jnp.concatenate([pad, ug, pad], axis=0)
        span = 1
        while span < win:
            run = run + pltpu.roll(run, span, 0)
            span *= 2
        back = win // 2 - 1
        if back:
            run = pltpu.roll(run, n_ext - back, 0)
        wsum = run[POOL_PAD:POOL_PAD + seq_len, :]
        cnt = jnp.minimum(pos + (win - win // 2), seq_len) - jnp.maximum(pos - win // 2, 0)
        pooled = (wsum / cnt.astype(F32) - ug).astype(BF16)
        outs.append(_dot(pooled, pw_ref[gi]))
    return jnp.concatenate(outs, axis=-1) * ps_ref[...]


def _even_prompt_kernel(q_ref, k_ref, v_ref, bg_ref, z_ref, cw_ref, cb_ref, fa_ref, fb_ref):
    for b in range(SEQ_PER_TILE):
        rows = slice(b * SEQ, (b + 1) * SEQ)
        pairs = [slice(p * LANES, (p + 1) * LANES) for p in range(H_A // 2)]
        scores = [_dot_t(_split_pair(q_ref[rows, sl]), k_ref[rows, sl]) for sl in pairs]
        outs = [_merge_pair(_softmax_pv([s], [_with_ones(v_ref[rows, sl])])) for s, sl in zip(scores, pairs)]
        fa_ref[rows, :] = jnp.concatenate(outs, axis=-1).astype(BF16)
    fb_ref[...] = (bg_ref[...] * _short_conv(z_ref[...], cw_ref, cb_ref, SEQ)).astype(BF16)


def _even_prompt(q, k, v, bg, z, conv_w, conv_b, idx):
    tile = lambda w: pl.BlockSpec((TM, w), lambda t: (t, 0))
    return pl.pallas_call(
        _even_prompt_kernel,
        grid=(PROMPT_TILES,),
        in_specs=[tile(HD_A), tile(HD_A), tile(HD_A), tile(C_B), tile(C_B),
                  pl.BlockSpec((None, CONV_W, C_B), lambda t: (idx, 0, 0)),
                  pl.BlockSpec((None, 1, C_B), lambda t: (idx, 0, 0))],
        out_specs=[tile(HD_A), tile(C_B)],
        out_shape=[
            jax.ShapeDtypeStruct((N_PROMPT, HD_A), BF16),
            jax.ShapeDtypeStruct((N_PROMPT, C_B), BF16),
        ],
        compiler_params=_params(1),
        name="even_prompt",
    )(q, k, v, bg, z, conv_w, conv_b)


def _na_bias_tiles(rpb_h):
    c = lax.broadcasted_iota(jnp.int32, (GRID_W, GRID_W), 0)
    kc = lax.broadcasted_iota(jnp.int32, (GRID_W, GRID_W), 1)
    cs = jnp.clip(c - NA_COLS // 2, 0, GRID_W - NA_COLS)
    valid = (kc >= cs) & (kc < cs + NA_COLS)
    tiles = []
    for ro in range(2 * NA_ROWS - 1):
        g = jnp.broadcast_to(rpb_h[ro:ro + 1, :], (GRID_W, LANES))
        skew = pltpu.roll(g, LANES - (NA_COLS - 1), 1, stride=1, stride_axis=0)
        tiles.append(jnp.where(valid, skew[:, 0:GRID_W], NEG_INF))
    return tiles


def _even_sample_kernel(*refs, n_cast, cast_mixers):
    n_mix_in, n_mix_out = (4, 3) if cast_mixers else (0, 0)
    q_ref, k_ref, v_ref, bg_ref, z_ref, ck_ref, cv_ref, rpb_ref, cw_ref, cb_ref = refs[0:10]
    out0 = 10 + n_cast + n_mix_in
    fa_ref, fb_ref = refs[out0:out0 + 2]
    bias_ref = refs[out0 + 2 + n_cast + n_mix_out]
    _cast_weights(refs[10:10 + n_cast], refs[out0 + 2:out0 + 2 + n_cast])
    if cast_mixers:
        _cast_mixer_weights(*refs[10 + n_cast:out0], *refs[out0 + 2 + n_cast:out0 + 2 + n_cast + n_mix_out])
    win = NA_ROWS * GRID_W
    for h in range(2):
        tiles = _na_bias_tiles(rpb_ref[h])
        for var in range(NA_ROWS):
            bias_ref[var, h * GRID_W:(h + 1) * GRID_W, :] = jnp.concatenate(
                [tiles[i - var + NA_ROWS - 1] for i in range(NA_ROWS)], axis=1)
    ck_t = _pair_heads_t(ck_ref).astype(BF16)
    cv_ext = _with_ones(_pair_heads_t(cv_ref).T.astype(BF16))

    def rows(it, carry):
        staged = []
        for j in range(ROW_UNROLL):
            r = it * ROW_UNROLL + j
            rs = jnp.clip(r - NA_ROWS // 2, 0, GRID_ROWS - NA_ROWS)
            q0 = pl.multiple_of(r * GRID_W, GRID_W)
            k0 = pl.multiple_of(rs * GRID_W, GRID_W)
            q2 = _split_pair(q_ref[pl.ds(q0, GRID_W), :])
            s_loc = _dot_t(q2, k_ref[pl.ds(k0, win), :]) + bias_ref[r - rs]
            s_ctx = _dot(q2, ck_t)
            staged.append((q0, k0, s_loc, s_ctx))
        for q0, k0, s_loc, s_ctx in staged:
            o = _softmax_pv([s_loc, s_ctx], [_with_ones(v_ref[pl.ds(k0, win), :]), cv_ext])
            fa_ref[pl.ds(q0, GRID_W), :] = _merge_pair(o).astype(BF16)
        return carry

    lax.fori_loop(0, GRID_ROWS // ROW_UNROLL, rows, 0)
    fb_ref[...] = (bg_ref[...] * _short_conv(z_ref[...], cw_ref, cb_ref, DEC_SEQ)).astype(BF16)


def _even_sample(q, k, v, bg, z, ctx_k, ctx_v, rpb, conv_w, conv_b, idx, ffn_w1, ffn_w2, cast_jobs, mixer_w=None):
    first = N_PROMPT // DEC_SEQ
    n_pair = H_A // 2
    n_steps = DEC_BATCH * n_pair
    step_of = lambda b, p: b * n_pair + p
    seq_in = pl.BlockSpec((DEC_SEQ, LANES), lambda b, p: (first + b, p))
    seq_out = pl.BlockSpec((DEC_SEQ, LANES), lambda b, p: (b, p))
    ctx = pl.BlockSpec((None, None, 2, HEAD_DIM, PAST_LEN), lambda b, p: (b, idx, p, 0, 0))
    cast_in, cast_out, cast_shape = _cast_specs(cast_jobs, n_steps, step_of)
    n_cast = len(cast_in)
    cast_args = [ffn_w1, ffn_w2] * len(cast_jobs)
    if mixer_w is not None:
        assert n_steps == MIX_CAST_STEPS
        mix_in, mix_out, mix_shape = _mixer_cast_specs(step_of)
        cast_in, cast_out, cast_shape = cast_in + mix_in, cast_out + mix_out, cast_shape + mix_shape
        cast_args += [mixer_w[0], mixer_w[1], mixer_w[2], mixer_w[2]]
    return pl.pallas_call(
        functools.partial(_even_sample_kernel, n_cast=n_cast, cast_mixers=mixer_w is not None),
        grid=(DEC_BATCH, n_pair),
        in_specs=[seq_in, seq_in, seq_in, seq_in, seq_in, ctx, ctx,
                  pl.BlockSpec((None, 2, 2 * NA_ROWS - 1, LANES), lambda b, p: (idx, p, 0, 0)),
                  pl.BlockSpec((None, CONV_W, LANES), lambda b, p: (idx, 0, p)),
                  pl.BlockSpec((None, 1, LANES), lambda b, p: (idx, 0, p))] + cast_in,
        out_specs=[seq_out, seq_out] + cast_out,
        out_shape=[
            jax.ShapeDtypeStruct((N_SAMPLE, HD_A), BF16),
            jax.ShapeDtypeStruct((N_SAMPLE, C_B), BF16),
        ] + cast_shape,
        scratch_shapes=[pltpu.VMEM((NA_ROWS, 2 * GRID_W, NA_ROWS * GRID_W), F32)],
        compiler_params=_params(2),
        name="even_sample",
    )(q, k, v, bg, z, ctx_k, ctx_v, rpb, conv_w, conv_b, *cast_args)


def _split_groups(q):
    low = _low_half(q.shape[0])
    zero = jnp.zeros((q.shape[0], LANES), BF16)
    pairs = [q[:, j * LANES:(j + 1) * LANES] for j in range(GQA_GROUP)]
    return jnp.concatenate([jnp.where(low, p, zero) for p in pairs] + [jnp.where(low, zero, p) for p in pairs],
                           axis=0)


def _merge_groups(o, m):
    low = _low_half(m)
    half = GQA_GROUP * m
    return jnp.concatenate([jnp.where(low, o[j * m:(j + 1) * m], o[half + j * m:half + (j + 1) * m])
                            for j in range(GQA_GROUP)], axis=-1)


def _odd_prompt_kernel(uc_ref, q_ref, k_ref, v_ref, pw_ref, ps_ref, fc_ref, fd_ref):
    seqs = [slice(b * SEQ, (b + 1) * SEQ) for b in range(SEQ_PER_TILE)]
    scores = [_dot_t(_split_groups(q_ref[rows, :]), k_ref[rows, :]) for rows in seqs]
    for s, rows in zip(scores, seqs):
        o = _softmax_pv([s], [_with_ones(v_ref[rows, :])])
        fd_ref[rows, :] = _merge_groups(o, SEQ).astype(BF16)
    for rows in seqs:
        fc_ref[rows, :] = _pool_mix(uc_ref[rows, :], pw_ref, ps_ref, SEQ).astype(BF16)


def _odd_prompt(uc, q, k, v, pool_w, pool_scale, idx):
    seq = lambda w: pl.BlockSpec((TM, w), lambda t: (t, 0))
    return pl.pallas_call(
        _odd_prompt_kernel,
        grid=(PROMPT_TILES,),
        in_specs=[seq(C_POOL), seq(HD_Q), seq(HD_KV), seq(HD_KV),
                  pl.BlockSpec((None, N_POOL, POOL_C, POOL_C), lambda t: (idx, 0, 0, 0)),
                  pl.BlockSpec((None, 1, C_POOL), lambda t: (idx, 0, 0))],
        out_specs=[seq(C_POOL), seq(HD_Q)],
        out_shape=[
            jax.ShapeDtypeStruct((N_PROMPT, C_POOL), BF16),
            jax.ShapeDtypeStruct((N_PROMPT, HD_Q), BF16),
        ],
        compiler_params=_params(1),
        name="odd_prompt",
    )(uc, q, k, v, pool_w, pool_scale)


def _odd_sample_kernel(*refs, n_cast):
    uc_ref, q_ref, k_ref, v_ref, ck_ref, cv_ref, pw_ref, ps_ref = refs[0:8]
    fc_ref, fd_ref = refs[8 + n_cast:10 + n_cast]
    kt_ref, vext_ref, ckpair_ref, cvext_ref = refs[10 + 2 * n_cast:]
    _cast_weights(refs[8:8 + n_cast], refs[10 + n_cast:10 + 2 * n_cast])

    @pl.when(pl.program_id(1) == 0)
    def _():
        fc_ref[...] = _pool_mix(uc_ref[...], pw_ref, ps_ref, DEC_SEQ).astype(BF16)
        kt_ref[...] = k_ref[...].astype(F32).T.astype(BF16)
        vext_ref[...] = _with_ones(v_ref[...])
        ckpair_ref[...] = _pair_heads_t(ck_ref).astype(BF16)
        cvext_ref[...] = _with_ones(_pair_heads_t(cv_ref).T.astype(BF16))

    k_t = kt_ref[...]
    ck_t = ckpair_ref[...]
    low = _low_half(Q_BLK)
    scores = []
    for j in range(GQA_GROUP):
        q2 = _split_pair(q_ref[:, j * LANES:(j + 1) * LANES])
        scores.append([_dot(q2, k_t), _dot(q2, ck_t)])
    outs = []
    for s in scores:
        o = _softmax_pv(s, [vext_ref[...], cvext_ref[...]])
        outs.append(jnp.where(low, o[0:Q_BLK], o[Q_BLK:2 * Q_BLK]))
    fd_ref[...] = jnp.concatenate(outs, axis=-1).astype(BF16)


def _odd_sample(uc, q, k, v, ctx_k, ctx_v, pool_w, pool_scale, idx, ffn_w1, ffn_w2, cast_jobs):
    first = N_PROMPT // DEC_SEQ
    n_qb = DEC_SEQ // Q_BLK
    first_q = N_PROMPT // Q_BLK
    whole_in = lambda w: pl.BlockSpec((DEC_SEQ, w), lambda b, i: (first + b, 0))
    ctx = pl.BlockSpec((None, None, HKV_D, HEAD_DIM, PAST_LEN), lambda b, i: (b, idx, 0, 0, 0))
    cast_in, cast_out, cast_shape = _cast_specs(cast_jobs, DEC_BATCH * n_qb, lambda b, i: b * n_qb + i)
    return pl.pallas_call(
        functools.partial(_odd_sample_kernel, n_cast=len(cast_in)),
        grid=(DEC_BATCH, n_qb),
        in_specs=[whole_in(C_POOL),
                  pl.BlockSpec((Q_BLK, HD_Q), lambda b, i: (first_q + b * n_qb + i, 0)),
                  whole_in(HD_KV), whole_in(HD_KV), ctx, ctx,
                  pl.BlockSpec((None, N_POOL, POOL_C, POOL_C), lambda b, i: (idx, 0, 0, 0)),
                  pl.BlockSpec((None, 1, C_POOL), lambda b, i: (idx, 0, 0))] + cast_in,
        out_specs=[pl.BlockSpec((DEC_SEQ, C_POOL), lambda b, i: (b, 0)),
                   pl.BlockSpec((Q_BLK, HD_Q), lambda b, i: (b * n_qb + i, 0))] + cast_out,
        out_shape=[
            jax.ShapeDtypeStruct((N_SAMPLE, C_POOL), BF16),
            jax.ShapeDtypeStruct((N_SAMPLE, HD_Q), BF16),
        ] + cast_shape,
        scratch_shapes=[pltpu.VMEM((LANES, DEC_SEQ), BF16), pltpu.VMEM((DEC_SEQ, 2 * LANES), BF16),
                        pltpu.VMEM((LANES, PAST_LEN), BF16), pltpu.VMEM((PAST_LEN, 2 * LANES), BF16)],
        compiler_params=_params(2),
        name="odd_sample",
    )(uc, q, k, v, ctx_k, ctx_v, pool_w, pool_scale, *([ffn_w1, ffn_w2] * len(cast_jobs)))


def _rope_tables():
    t = np.arange(DEC_SEQ)
    n_freq = HEAD_DIM // 4
    inv = ROPE_BASE ** (-np.arange(n_freq, dtype=np.float64) / n_freq)
    ang = np.concatenate([(t // GRID_W)[:, None] * inv, (t % GRID_W)[:, None] * inv], axis=-1)
    cos = np.repeat(np.cos(ang), 2, axis=-1)
    sin = np.repeat(np.sin(ang), 2, axis=-1) * np.tile([-1.0, 1.0], HEAD_DIM // 2)
    n_heads = HQ_D + HKV_D
    cos = np.concatenate([np.ones((TM, QK_W)), np.tile(cos, (1, n_heads))], axis=0)
    sin = np.concatenate([np.zeros((TM, QK_W)), np.tile(sin, (1, n_heads))], axis=0)
    return jnp.asarray(cos, F32), jnp.asarray(sin, F32)


def _head_ones():
    head = np.arange(MXU_DIM) // HEAD_DIM
    return jnp.asarray(head[:, None] == head[None, :], BF16)


def kernel(x_prompt, x_sample, cache_a_k, cache_a_v, cache_d_k, cache_d_v, c, c_ctx, mod_w, mod_b, norm_w,
           ffn_w1, ffn_w2, ev_w_in, ev_rpb, ev_conv_w, ev_conv_b, ev_w_out, od_w_in, od_pool_w,
           od_pool_scale, od_q_norm, od_k_norm, od_w_out):
    xs = (x_prompt.reshape(N_PROMPT, D_MODEL), x_sample.reshape(N_SAMPLE, D_MODEL))
    cond = jnp.concatenate([c_ctx[None, :], c, jnp.zeros((COND_PAD - N_COND, D_MODEL), F32)], axis=0)
    mods, w1_a, w2_a = _modulation(cond, mod_w, mod_b, ffn_w1, ffn_w2)
    mods = mods.reshape(DEPTH, COND_PAD, N_MOD, D_MODEL)
    cache_a_k, cache_a_v, cache_d_k, cache_d_v = (jnp.swapaxes(t, -1, -2)
                                                  for t in (cache_a_k, cache_a_v, cache_d_k, cache_d_v))
    rpb = jnp.pad(ev_rpb, ((0, 0), (0, 0), (0, 0), (0, LANES - ev_rpb.shape[-1])))
    conv_b = ev_conv_b[:, None, :]
    pool_w, pool_scale = od_pool_w.astype(BF16), od_pool_scale[:, None, :]
    cos_t, sin_t = _rope_tables()
    ones_bd = _head_ones()
    states = []
    for l in range(DEPTH):
        i = l // 2
        last = l == DEPTH - 1
        if l == 0:
            x, ev_w_in_b = _ffn(xs, mods, norm_w, w1_a, w2_a, l, 0, ev_w_in=ev_w_in)
        else:
            x, = _ffn(xs, mods, norm_w, w1_a, w2_a, l, 0)
        cast_jobs = [(l, 1)] + ([] if last else [(l + 1, 0)])
        if l % 2 == 0:
            q, k, v, bg, z, s_k, s_v = _even_in(x, mods, norm_w, ev_w_in_b, l)
            fa_p, fb_p = _even_prompt(q, k, v, bg, z, ev_conv_w, conv_b, i)
            fa_s, fb_s, *cast = _even_sample(q, k, v, bg, z, cache_a_k, cache_a_v, rpb, ev_conv_w, conv_b, i,
                                             ffn_w1, ffn_w2, cast_jobs, mixer_w=(ev_w_out, od_w_in, od_w_out))
            ev_w_out_b, od_w_in_b, od_w_out_b = cast[2 * len(cast_jobs):]
            w_out = ev_w_out_b
        else:
            gain = jnp.concatenate([jnp.tile(od_q_norm[i], HQ_D), jnp.tile(od_k_norm[i], HKV_D)])[None, :]
            uc, q, k, v, s_k, s_v = _odd_in(x, mods, norm_w, od_w_in_b, gain, cos_t, sin_t, ones_bd, l)
            fa_p, fb_p = _odd_prompt(uc, q, k, v, pool_w, pool_scale, i)
            fa_s, fb_s, *cast = _odd_sample(uc, q, k, v, cache_d_k, cache_d_v, pool_w, pool_scale, i,
                                            ffn_w1, ffn_w2, cast_jobs)
            w_out = od_w_out_b
        w1_b, w2_b = cast[0:2]
        if not last:
            w1_a, w2_a = cast[2:4]
        states.append((jnp.swapaxes(s_k, -1, -2), jnp.swapaxes(s_v, -1, -2)))
        xs = tuple(_ffn((x,), mods, norm_w, w1_b, w2_b, l, 2, feats=(fa_p, fb_p, fa_s, fb_s), w_out=w_out,
                        split_out=last))
    y_prompt = xs[0].reshape(BATCH, SEQ, D_MODEL)
    y_sample = xs[1].reshape(DEC_BATCH, DEC_SEQ, D_MODEL)
    return (y_prompt, y_sample, states[0][0], states[0][1], states[1][0], states[1][1])
```

```python
import functools

import jax
import jax.numpy as jnp
import numpy as np
from jax import lax
from jax.experimental import pallas as pl
from jax.experimental.pallas import tpu as pltpu

D_MODEL = 1024
BATCH = 32
SEQ = 256
DEPTH = 2
DEC_BATCH = 2
DEC_SEQ = 2048
PAST_LEN = 256
GRID_W = 64
HEAD_DIM = 64
N_MOD = 9
D_FF = 2816
FFN_RES = 0.5
H_A = 8
NA_ROWS = 8
NA_COLS = 16
C_B = 512
CONV_W = 3
C_POOL = 512
POOL_WINDOWS = (2, 4, 8, 16)
N_POOL = 4
POOL_C = C_POOL // N_POOL
HQ_D = 8
HKV_D = 2
GQA_GROUP = HQ_D // HKV_D
ROPE_BASE = 10000.0
EVEN_IN = 3 * H_A * HEAD_DIM + 3 * C_B
ODD_IN = C_POOL + (HQ_D + 2 * HKV_D) * HEAD_DIM
RMS_EPS = 1e-6
NEG_INF = -1e30
ATT_SCALE = HEAD_DIM ** -0.5

LANES = 128
GRID_ROWS = DEC_SEQ // GRID_W
N_PROMPT = BATCH * SEQ
N_SAMPLE = DEC_BATCH * DEC_SEQ
N_TOK = N_PROMPT + N_SAMPLE
N_COND = 1 + DEC_BATCH
COND_PAD = 8
HD_A = H_A * HEAD_DIM
HD_Q = HQ_D * HEAD_DIM
HD_KV = HKV_D * HEAD_DIM
QK_W = HD_Q + HD_KV
POOL_PAD = 8

TM = 1024
N_TILES = N_TOK // TM
PROMPT_TILES = N_PROMPT // TM
TILES_PER_SAMPLE = DEC_SEQ // TM
SEQ_PER_TILE = TM // SEQ
TM_FFN = 1024
FFN_HALF = TM_FFN // 2
MIX_CAST_STEPS = N_PROMPT // TM_FFN
MIX_CAST_ROWS = D_MODEL // MIX_CAST_STEPS
assert MIX_CAST_ROWS == 2 * HEAD_DIM and (DEPTH + 1) // 2 == 1 and DEPTH // 2 == 1
MXU_DIM = 256
FF_CHUNKS = tuple((lo, min(lo + 3 * MXU_DIM, D_FF)) for lo in range(0, D_FF, 3 * MXU_DIM))
MOD_TN = 1152
Q_BLK = 256
ROW_UNROLL = 8
VMEM_LIMIT = 60 * 1024 * 1024

F32 = jnp.float32
BF16 = jnp.bfloat16


def _params(n_grid):
    return pltpu.CompilerParams(dimension_semantics=("arbitrary",) * n_grid, vmem_limit_bytes=VMEM_LIMIT)


def _cond_of_tile(i, tm=TM):
    n_prompt = N_PROMPT // tm
    return jnp.where(i < n_prompt, 0, 1 + (i - n_prompt) // (DEC_SEQ // tm))


def _prompt_tile(i, tm=TM):
    return jnp.minimum(i, N_PROMPT // tm - 1)


def _sample_tile(i, tm=TM):
    return jnp.maximum(i - N_PROMPT // tm, 0)


def _rms(x, g):
    return x * lax.rsqrt(jnp.mean(x * x, axis=-1, keepdims=True) + RMS_EPS) * g


def _sigmoid(x):
    return 1.0 / (1.0 + jnp.exp(-x))


def _dot(a, b):
    return jnp.dot(a, b, preferred_element_type=F32)


def _dot_t(a, b):
    return lax.dot_general(a, b, (((1,), (1,)), ((), ())), preferred_element_type=F32)


def _mod_kernel(c_ref, w_ref, b_ref, w1f_ref, w2f_ref, o_ref, w1b_ref, w2b_ref):
    _cast_weights([w1f_ref, w2f_ref], [w1b_ref, w2b_ref])
    c = c_ref[...]
    sc = (c * _sigmoid(c)).astype(BF16)
    o_ref[...] = _dot(sc, w_ref[...].astype(BF16)) + b_ref[...]


def _modulation(cond, mod_w, mod_b, ffn_w1, ffn_w2):
    n_col = N_MOD * D_MODEL
    n_blk = n_col // MOD_TN
    cast_in, cast_out, cast_shape = _cast_specs([(0, 0)], DEPTH * n_blk, lambda l, j: l * n_blk + j)
    return pl.pallas_call(
        _mod_kernel,
        grid=(DEPTH, n_blk),
        in_specs=[
            pl.BlockSpec((COND_PAD, D_MODEL), lambda l, j: (0, 0)),
            pl.BlockSpec((None, D_MODEL, MOD_TN), lambda l, j: (l, 0, j)),
            pl.BlockSpec((None, 1, MOD_TN), lambda l, j: (l, 0, j)),
        ] + cast_in,
        out_specs=[pl.BlockSpec((None, COND_PAD, MOD_TN), lambda l, j: (l, 0, j))] + cast_out,
        out_shape=[jax.ShapeDtypeStruct((DEPTH, COND_PAD, n_col), F32)] + cast_shape,
        compiler_params=_params(2),
        name="modulation",
    )(cond, mod_w, mod_b.reshape(DEPTH, 1, n_col), ffn_w1, ffn_w2)


def _mod_spec(layer, tm=TM):
    return pl.BlockSpec((None, None, N_MOD, D_MODEL), lambda i: (layer, _cond_of_tile(i, tm), 0, 0))


def _gain_spec(layer):
    return pl.BlockSpec((None, 6, D_MODEL), lambda i: (layer, 0, 0))


def _cast_specs(jobs, n_steps, step_of):
    r1, r2 = D_MODEL // n_steps, D_FF // n_steps
    in_specs, out_specs, out_shape = [], [], []
    for layer, which in jobs:
        in_specs += [pl.BlockSpec((None, None, r1, 2 * D_FF), lambda *g, lw=(layer, which): (*lw, step_of(*g), 0)),
                     pl.BlockSpec((None, None, r2, D_MODEL), lambda *g, lw=(layer, which): (*lw, step_of(*g), 0))]
        out_specs += [pl.BlockSpec((r1, 2 * D_FF), lambda *g: (step_of(*g), 0)),
                      pl.BlockSpec((r2, D_MODEL), lambda *g: (step_of(*g), 0))]
        out_shape += [jax.ShapeDtypeStruct((D_MODEL, 2 * D_FF), BF16), jax.ShapeDtypeStruct((D_FF, D_MODEL), BF16)]
    return in_specs, out_specs, out_shape


def _cast_weights(f32_refs, bf16_refs):
    for src, dst in zip(f32_refs, bf16_refs):
        dst[...] = src[...].astype(BF16)


def _ev_in_cast_specs():
    step = lambda i: jnp.minimum(i, MIX_CAST_STEPS - 1)
    return ([pl.BlockSpec((None, MIX_CAST_ROWS, EVEN_IN), lambda i: (0, step(i), 0))],
            [pl.BlockSpec((MIX_CAST_ROWS, EVEN_IN), lambda i: (step(i), 0))],
            [jax.ShapeDtypeStruct((D_MODEL, EVEN_IN), BF16)])


def _mixer_cast_specs(step_of):
    rows = MIX_CAST_ROWS
    n_pool_blk = C_POOL // rows

    def od_out_block(s, g):
        return jnp.where(s < n_pool_blk, 2 * s + g, C_POOL // HEAD_DIM + g * GQA_GROUP + s - n_pool_blk)

    widths = (D_MODEL, ODD_IN, D_MODEL)
    in_specs = [pl.BlockSpec((None, rows, D_MODEL), lambda *g: (0, step_of(*g), 0)),
                pl.BlockSpec((None, rows, ODD_IN), lambda *g: (0, step_of(*g), 0)),
                pl.BlockSpec((None, HEAD_DIM, D_MODEL), lambda *g: (0, od_out_block(step_of(*g), 0), 0)),
                pl.BlockSpec((None, HEAD_DIM, D_MODEL), lambda *g: (0, od_out_block(step_of(*g), 1), 0))]
    out_specs = [pl.BlockSpec((rows, w), lambda *g: (step_of(*g), 0)) for w in widths]
    out_shape = [jax.ShapeDtypeStruct((D_MODEL, w), BF16) for w in widths]
    return in_specs, out_specs, out_shape


def _cast_mixer_weights(ev_out_f, od_in_f, od_out_lo_f, od_out_hi_f, ev_out_b, od_in_b, od_out_b):
    ev_out_b[...] = ev_out_f[...].astype(BF16)
    w = od_in_f[...]
    q = w[:, C_POOL:C_POOL + HD_Q]
    heads = [q[:, (g * GQA_GROUP + j) * HEAD_DIM:(g * GQA_GROUP + j + 1) * HEAD_DIM]
             for j in range(GQA_GROUP) for g in range(HKV_D)]
    od_in_b[...] = jnp.concatenate([w[:, 0:C_POOL]] + heads + [w[:, C_POOL + HD_Q:]], axis=1).astype(BF16)
    od_out_b[...] = jnp.concatenate([od_out_lo_f[...], od_out_hi_f[...]], axis=0).astype(BF16)


def _ffn_kernel(*refs, sub, split_in, mix, split_out, cast_ev_in):
    refs = list(refs)
    x_refs = [refs.pop(0) for _ in range(2 if split_in else 1)]
    feat_refs = [refs.pop(0) for _ in range(4 if mix else 0)]
    m_ref, g_ref = refs.pop(0), refs.pop(0)
    wo_ref = refs.pop(0) if mix else None
    w1_ref, w2_ref = refs.pop(0), refs.pop(0)
    if cast_ev_in:
        @pl.when(pl.program_id(0) < MIX_CAST_STEPS)
        def _():
            refs[-1][...] = refs[0][...].astype(BF16)
        refs = refs[1:-1]
    out_refs = refs
    is_prompt = pl.program_id(0) < N_PROMPT // TM_FFN
    shift = m_ref[3 * sub:3 * sub + 1, :]
    scale = m_ref[3 * sub + 1:3 * sub + 2, :]
    gate = m_ref[3 * sub + 2:3 * sub + 3, :]
    for half in range(TM_FFN // FFN_HALF):
        rows = slice(half * FFN_HALF, (half + 1) * FFN_HALF)
        if split_in:
            x = jnp.where(is_prompt, x_refs[0][rows, :], x_refs[1][rows, :])
        else:
            x = x_refs[0][rows, :]
        if mix:
            fa = jnp.where(is_prompt, feat_refs[0][rows, :], feat_refs[2][rows, :])
            fb = jnp.where(is_prompt, feat_refs[1][rows, :], feat_refs[3][rows, :])
            mixed = _dot(jnp.concatenate([fa, fb], axis=1), wo_ref[...])
            x = x + m_ref[5:6, :] * _rms(mixed, g_ref[3:4, :])
        h = (_rms(x, g_ref[2 * sub:2 * sub + 1, :]) * (1.0 + scale) + shift).astype(BF16)
        acc = jnp.zeros((FFN_HALF, D_MODEL), F32)
        for lo, hi in FF_CHUNKS:
            a = _dot(h, w1_ref[:, lo:hi])
            u = _dot(h, w1_ref[:, D_FF + lo:D_FF + hi])
            act = (a * _sigmoid(a) * u).astype(BF16)
            acc = acc + _dot(act, w2_ref[lo:hi, :])
        y = x + FFN_RES * gate * _rms(acc, g_ref[2 * sub + 1:2 * sub + 2, :])
        if split_out:
            @pl.when(is_prompt)
            def _():
                out_refs[0][rows, :] = y

            @pl.when(jnp.logical_not(is_prompt))
            def _():
                out_refs[1][rows, :] = y
        else:
            out_refs[0][rows, :] = y


def _ffn(xs, mods, gains, w1, w2, layer, sub, feats=None, w_out=None, split_out=False, ev_w_in=None):
    split_in = len(xs) == 2
    mix = feats is not None
    cast_ev_in = ev_w_in is not None
    resident = pl.Buffered(1)
    half_w = D_MODEL // 2
    feat_specs, mix_w_spec = [], []
    if mix:
        feat_p = pl.BlockSpec((TM_FFN, half_w), lambda i: (_prompt_tile(i, TM_FFN), 0))
        feat_s = pl.BlockSpec((TM_FFN, half_w), lambda i: (_sample_tile(i, TM_FFN), 0))
        feat_specs = [feat_p, feat_p, feat_s, feat_s]
        mix_w_spec = [pl.BlockSpec((D_MODEL, D_MODEL), lambda i: (0, 0), pipeline_mode=resident)]
    cast_in, cast_out, cast_shape, cast_args = [], [], [], ()
    if cast_ev_in:
        cast_in, cast_out, cast_shape = _ev_in_cast_specs()
        cast_args = (ev_w_in,)
    tok =pl.BlockSpec((TM_FFN, D_MODEL), lambda i: (i, 0))
    prompt_tok = pl.BlockSpec((TM_FFN, D_MODEL), lambda i: (_prompt_tile(i, TM_FFN), 0))
    sample_tok = pl.BlockSpec((TM_FFN, D_MODEL), lambda i: (_sample_tile(i, TM_FFN), 0))
    if split_out:
        out_specs = [prompt_tok, sample_tok]
        out_shape = [jax.ShapeDtypeStruct((N_PROMPT, D_MODEL), F32),
                     jax.ShapeDtypeStruct((N_SAMPLE, D_MODEL), F32)]
    else:
        out_specs = [tok]
        out_shape = [jax.ShapeDtypeStruct((N_TOK, D_MODEL), F32)]
    return pl.pallas_call(
        functools.partial(_ffn_kernel, sub=sub, split_in=split_in, mix=mix, split_out=split_out,
                          cast_ev_in=cast_ev_in),
        grid=(N_TOK // TM_FFN,),
        in_specs=([prompt_tok, sample_tok] if split_in else [tok]) + feat_specs + [
            _mod_spec(layer, TM_FFN),
            _gain_spec(layer),
        ] + mix_w_spec + [
            pl.BlockSpec((D_MODEL, 2 * D_FF), lambda i: (0, 0), pipeline_mode=resident),
            pl.BlockSpec((D_FF, D_MODEL), lambda i: (0, 0), pipeline_mode=resident),
        ] + cast_in,
        out_specs=out_specs + cast_out,
        out_shape=out_shape + cast_shape,
        compiler_params=_params(1),
        name=f"ffn{sub}",
    )(*xs, *(feats or ()), mods, gains, *((w_out,) if mix else ()), w1, w2, *cast_args)


def _mixer_norm(x, m_ref, g_ref):
    return (_rms(x, g_ref[2:3, :]) * (1.0 + m_ref[4:5, :]) + m_ref[3:4, :]).astype(BF16)


def _store_heads(state_ref, b, x, n_heads):
    xt = x.T
    for h in range(n_heads):
        state_ref[b, 0, h] = xt[h * HEAD_DIM:(h + 1) * HEAD_DIM, :]


def _even_in_kernel(x_ref, m_ref, g_ref, w_ref, q_ref, k_ref, v_ref, bg_ref, z_ref, sk_ref, sv_ref):
    kv = []
    for b in range(SEQ_PER_TILE):
        rows = slice(b * SEQ, (b + 1) * SEQ)
        u = _dot(_mixer_norm(x_ref[rows, :], m_ref, g_ref), w_ref[...])
        k = u[:, HD_A:2 * HD_A]
        v = u[:, 2 * HD_A:3 * HD_A]
        q_ref[rows, :] = (u[:, 0:HD_A] * ATT_SCALE).astype(BF16)
        k_ref[rows, :] = k.astype(BF16)
        v_ref[rows, :] = v.astype(BF16)
        bg_ref[rows, :] = u[:, 3 * HD_A:3 * HD_A + C_B]
        z_ref[rows, :] = u[:, 3 * HD_A + C_B:3 * HD_A + 2 * C_B] * u[:, 3 * HD_A + 2 * C_B:3 * HD_A + 3 * C_B]
        kv.append((k, v))

    @pl.when(pl.program_id(0) < PROMPT_TILES)
    def _():
        for b, (k, v) in enumerate(kv):
            _store_heads(sk_ref, b, k, H_A)
            _store_heads(sv_ref, b, v, H_A)


def _state_spec(n_heads):
    return pl.BlockSpec((SEQ_PER_TILE, 1, n_heads, HEAD_DIM, SEQ), lambda i: (_prompt_tile(i), 0, 0, 0, 0))


def _even_in(x, mods, gains, w_in, layer):
    tok = lambda w: pl.BlockSpec((TM, w), lambda i: (i, 0))
    state = jax.ShapeDtypeStruct((BATCH, 1, H_A, HEAD_DIM, SEQ), F32)
    return pl.pallas_call(
        _even_in_kernel,
        grid=(N_TILES,),
        in_specs=[
            tok(D_MODEL), _mod_spec(layer), _gain_spec(layer),
            pl.BlockSpec((D_MODEL, EVEN_IN), lambda i: (0, 0)),
        ],
        out_specs=[tok(HD_A), tok(HD_A), tok(HD_A), tok(C_B), tok(C_B), _state_spec(H_A), _state_spec(H_A)],
        out_shape=[
            jax.ShapeDtypeStruct((N_TOK, HD_A), BF16),
            jax.ShapeDtypeStruct((N_TOK, HD_A), BF16),
            jax.ShapeDtypeStruct((N_TOK, HD_A), BF16),
            jax.ShapeDtypeStruct((N_TOK, C_B), F32),
            jax.ShapeDtypeStruct((N_TOK, C_B), F32),
            state, state,
        ],
        compiler_params=_params(1),
        name="even_in",
    )(x, mods, gains, w_in)


def _swap_pairs(x):
    n = x.shape[-1]
    lane = lax.broadcasted_iota(jnp.int32, x.shape, x.ndim - 1)
    return jnp.where(lane % 2 == 0, pltpu.roll(x, n - 1, x.ndim - 1), pltpu.roll(x, 1, x.ndim - 1))


def _odd_in_kernel(x_ref, m_ref, g_ref, w_ref, ng_ref, cos_ref, sin_ref, ones_ref,
                   uc_ref, q_ref, k_ref, v_ref, sk_ref, sv_ref):
    ones = ones_ref[...]
    kv = []
    for b in range(SEQ_PER_TILE):
        rows = slice(b * SEQ, (b + 1) * SEQ)
        u = _dot(_mixer_norm(x_ref[rows, :], m_ref, g_ref), w_ref[...])
        uc_ref[rows, :] = u[:, 0:C_POOL]
        qk = u[:, C_POOL:C_POOL + QK_W]
        sq = qk * qk
        hi = sq.astype(BF16)
        lo = (sq - hi.astype(F32)).astype(BF16)
        sums = []
        for c0 in range(0, QK_W, MXU_DIM):
            c1 = min(c0 + MXU_DIM, QK_W)
            tile = ones[0:c1 - c0, 0:c1 - c0]
            sums.append(_dot(hi[:, c0:c1], tile) + _dot(lo[:, c0:c1], tile))
        ms = jnp.concatenate(sums, axis=-1) * (1.0 / HEAD_DIM)
        n = qk * lax.rsqrt(ms + RMS_EPS) * ng_ref[...]
        r = n * cos_ref[rows, :] + _swap_pairs(n) * sin_ref[rows, :]
        k = r[:, HD_Q:QK_W]
        v = u[:, C_POOL + QK_W:ODD_IN]
        q_ref[rows, :] = (r[:, 0:HD_Q] * ATT_SCALE).astype(BF16)
        k_ref[rows, :] = k.astype(BF16)
        v_ref[rows, :] = v.astype(BF16)
        kv.append((k, v))

    @pl.when(pl.program_id(0) < PROMPT_TILES)
    def _():
        for b, (k, v) in enumerate(kv):
            _store_heads(sk_ref, b, k, HKV_D)
            _store_heads(sv_ref, b, v, HKV_D)


def _rope_tile_index(i):
    return jnp.where(i < PROMPT_TILES, 0, 1 + (i - PROMPT_TILES) % TILES_PER_SAMPLE)


def _odd_in(x, mods, gains, w_in, qk_gain, cos_t, sin_t, ones_bd, layer):
    tok = lambda w: pl.BlockSpec((TM, w), lambda i: (i, 0))
    state = jax.ShapeDtypeStruct((BATCH, 1, HKV_D, HEAD_DIM, SEQ), F32)
    return pl.pallas_call(
        _odd_in_kernel,
        grid=(N_TILES,),
        in_specs=[
            tok(D_MODEL), _mod_spec(layer), _gain_spec(layer),
            pl.BlockSpec((D_MODEL, ODD_IN), lambda i: (0, 0)),
            pl.BlockSpec((1, QK_W), lambda i: (0, 0)),
            pl.BlockSpec((TM, QK_W), lambda i: (_rope_tile_index(i), 0)),
            pl.BlockSpec((TM, QK_W), lambda i: (_rope_tile_index(i), 0)),
            pl.BlockSpec((MXU_DIM, MXU_DIM), lambda i: (0, 0)),
        ],
        out_specs=[tok(C_POOL), tok(HD_Q), tok(HD_KV), tok(HD_KV), _state_spec(HKV_D), _state_spec(HKV_D)],
        out_shape=[
            jax.ShapeDtypeStruct((N_TOK, C_POOL), F32),
            jax.ShapeDtypeStruct((N_TOK, HD_Q), BF16),
            jax.ShapeDtypeStruct((N_TOK, HD_KV), BF16),
            jax.ShapeDtypeStruct((N_TOK, HD_KV), BF16),
            state, state,
        ],
        compiler_params=_params(1),
        name="odd_in",
    )(x, mods, gains, w_in, qk_gain, cos_t, sin_t, ones_bd)


def _low_half(rows):
    return lax.broadcasted_iota(jnp.int32, (rows, LANES), 1) < HEAD_DIM


def _split_pair(qp):
    low = _low_half(qp.shape[0])
    zero = jnp.zeros_like(qp)
    return jnp.concatenate([jnp.where(low, qp, zero), jnp.where(low, zero, qp)], axis=0)


def _pair_heads_t(ref):
    return jnp.concatenate([ref[0], ref[1]], axis=0)


def _with_ones(v):
    return jnp.concatenate([v, jnp.ones_like(v)], axis=1)


def _softmax_pv(scores, values):
    m = scores[0].max(axis=-1, keepdims=True)
    for s in scores[1:]:
        m = jnp.maximum(m, s.max(axis=-1, keepdims=True))
    acc = None
    for s, v in zip(scores, values):
        o = _dot(jnp.exp(s - m).astype(BF16), v)
        acc = o if acc is None else acc + o
    return acc[:, 0:LANES] / acc[:, LANES:2 * LANES]


def _merge_pair(o):
    m = o.shape[0] // 2
    return jnp.where(_low_half(m), o[0:m], o[m:2 * m])


def _short_conv(z, cw_ref, cb_ref, seq_len):
    rows = z.shape[0]
    pos = lax.broadcasted_iota(jnp.int32, z.shape, 0) & (seq_len - 1)
    z_prev = jnp.where(pos == 0, 0.0, pltpu.roll(z, 1, 0))
    z_next = jnp.where(pos == seq_len - 1, 0.0, pltpu.roll(z, rows - 1, 0))
    y = z_prev * cw_ref[0:1, :]
    y = y + z * cw_ref[1:2, :]
    y = y + z_next * cw_ref[2:3, :]
    return y + cb_ref[...]


def _pool_mix(uc, pw_ref, ps_ref, seq_len):
    pos = lax.broadcasted_iota(jnp.int32, (seq_len, POOL_C), 0)
    pad = jnp.zeros((POOL_PAD, POOL_C), F32)
    n_ext = seq_len + 2 * POOL_PAD
    outs = []
    for gi, win in enumerate(POOL_WINDOWS):
        ug = uc[:, gi * POOL_C:(gi + 1) * POOL_C]
        run = jnp.concatenate([pad, ug, pad], axis=0)
        span = 1
        while span < win:
            run = run + pltpu.roll(run, span, 0)
            span *= 2
        back = win // 2 - 1
        if back:
            run = pltpu.roll(run, n_ext - back, 0)
        wsum = run[POOL_PAD:POOL_PAD + seq_len, :]
        cnt = jnp.minimum(pos + (win - win // 2), seq_len) - jnp.maximum(pos - win // 2, 0)
        pooled = (wsum / cnt.astype(F32) - ug).astype(BF16)
        outs.append(_dot(pooled, pw_ref[gi]))
    return jnp.concatenate(outs, axis=-1) * ps_ref[...]


def _even_prompt_kernel(q_ref, k_ref, v_ref, bg_ref, z_ref, cw_ref, cb_ref, fa_ref, fb_ref):
    for b in range(SEQ_PER_TILE):
        rows = slice(b * SEQ, (b + 1) * SEQ)
        pairs = [slice(p * LANES, (p + 1) * LANES) for p in range(H_A // 2)]
        scores = [_dot_t(_split_pair(q_ref[rows, sl]), k_ref[rows, sl]) for sl in pairs]
        outs = [_merge_pair(_softmax_pv([s], [_with_ones(v_ref[rows, sl])])) for s, sl in zip(scores, pairs)]
        fa_ref[rows, :] = jnp.concatenate(outs, axis=-1).astype(BF16)
    fb_ref[...] = (bg_ref[...] * _short_conv(z_ref[...], cw_ref, cb_ref, SEQ)).astype(BF16)


def _even_prompt(q, k, v, bg, z, conv_w, conv_b, idx):
    tile = lambda w: pl.BlockSpec((TM, w), lambda t: (t, 0))
    return pl.pallas_call(
        _even_prompt_kernel,
        grid=(PROMPT_TILES,),
        in_specs=[tile(HD_A), tile(HD_A), tile(HD_A), tile(C_B), tile(C_B),
                  pl.BlockSpec((None, CONV_W, C_B), lambda t: (idx, 0, 0)),
                  pl.BlockSpec((None, 1, C_B), lambda t: (idx, 0, 0))],
        out_specs=[tile(HD_A), tile(C_B)],
        out_shape=[
            jax.ShapeDtypeStruct((N_PROMPT, HD_A), BF16),
            jax.ShapeDtypeStruct((N_PROMPT, C_B), BF16),
        ],
        compiler_params=_params(1),
        name="even_prompt",
    )(q, k, v, bg, z, conv_w, conv_b)


def _na_bias_tiles(rpb_h):
    c = lax.broadcasted_iota(jnp.int32, (GRID_W, GRID_W), 0)
    kc = lax.broadcasted_iota(jnp.int32, (GRID_W, GRID_W), 1)
    cs = jnp.clip(c - NA_COLS // 2, 0, GRID_W - NA_COLS)
    valid = (kc >= cs) & (kc < cs + NA_COLS)
    tiles = []
    for ro in range(2 * NA_ROWS - 1):
        g = jnp.broadcast_to(rpb_h[ro:ro + 1, :], (GRID_W, LANES))
        skew = pltpu.roll(g, LANES - (NA_COLS - 1), 1, stride=1, stride_axis=0)
        tiles.append(jnp.where(valid, skew[:, 0:GRID_W], NEG_INF))
    return tiles


def _even_sample_kernel(*refs, n_cast, cast_mixers):
    n_mix_in, n_mix_out = (4, 3) if cast_mixers else (0, 0)
    q_ref, k_ref, v_ref, bg_ref, z_ref, ck_ref, cv_ref, rpb_ref, cw_ref, cb_ref = refs[0:10]
    out0 = 10 + n_cast + n_mix_in
    fa_ref, fb_ref = refs[out0:out0 + 2]
    bias_ref = refs[out0 + 2 + n_cast + n_mix_out]
    _cast_weights(refs[10:10 + n_cast], refs[out0 + 2:out0 + 2 + n_cast])
    if cast_mixers:
        _cast_mixer_weights(*refs[10 + n_cast:out0], *refs[out0 + 2 + n_cast:out0 + 2 + n_cast + n_mix_out])
    win = NA_ROWS * GRID_W
    for h in range(2):
        tiles = _na_bias_tiles(rpb_ref[h])
        for var in range(NA_ROWS):
            bias_ref[var, h * GRID_W:(h + 1) * GRID_W, :] = jnp.concatenate(
                [tiles[i - var + NA_ROWS - 1] for i in range(NA_ROWS)], axis=1)
    ck_t = _pair_heads_t(ck_ref).astype(BF16)
    cv_ext = _with_ones(_pair_heads_t(cv_ref).T.astype(BF16))

    def rows(it, carry):
        staged = []
        for j in range(ROW_UNROLL):
            r = it * ROW_UNROLL + j
            rs = jnp.clip(r - NA_ROWS // 2, 0, GRID_ROWS - NA_ROWS)
            q0 = pl.multiple_of(r * GRID_W, GRID_W)
            k0 = pl.multiple_of(rs * GRID_W, GRID_W)
            q2 = _split_pair(q_ref[pl.ds(q0, GRID_W), :])
            s_loc = _dot_t(q2, k_ref[pl.ds(k0, win), :]) + bias_ref[r - rs]
            s_ctx = _dot(q2, ck_t)
            staged.append((q0, k0, s_loc, s_ctx))
        for q0, k0, s_loc, s_ctx in staged:
            o = _softmax_pv([s_loc, s_ctx], [_with_ones(v_ref[pl.ds(k0, win), :]), cv_ext])
            fa_ref[pl.ds(q0, GRID_W), :] = _merge_pair(o).astype(BF16)
        return carry

    lax.fori_loop(0, GRID_ROWS // ROW_UNROLL, rows, 0)
    fb_ref[...] = (bg_ref[...] * _short_conv(z_ref[...], cw_ref, cb_ref, DEC_SEQ)).astype(BF16)


def _even_sample(q, k, v, bg, z, ctx_k, ctx_v, rpb, conv_w, conv_b, idx, ffn_w1, ffn_w2, cast_jobs, mixer_w=None):
    first = N_PROMPT // DEC_SEQ
    n_pair = H_A // 2
    n_steps = DEC_BATCH * n_pair
    step_of = lambda b, p: b * n_pair + p
    seq_in = pl.BlockSpec((DEC_SEQ, LANES), lambda b, p: (first + b, p))
    seq_out = pl.BlockSpec((DEC_SEQ, LANES), lambda b, p: (b, p))
    ctx = pl.BlockSpec((None, None, 2, HEAD_DIM, PAST_LEN), lambda b, p: (b, idx, p, 0, 0))
    cast_in, cast_out, cast_shape = _cast_specs(cast_jobs, n_steps, step_of)
    n_cast = len(cast_in)
    cast_args = [ffn_w1, ffn_w2] * len(cast_jobs)
    if mixer_w is not None:
        assert n_steps == MIX_CAST_STEPS
        mix_in, mix_out, mix_shape = _mixer_cast_specs(step_of)
        cast_in, cast_out, cast_shape = cast_in + mix_in, cast_out + mix_out, cast_shape + mix_shape
        cast_args += [mixer_w[0], mixer_w[1], mixer_w[2], mixer_w[2]]
    return pl.pallas_call(
        functools.partial(_even_sample_kernel, n_cast=n_cast, cast_mixers=mixer_w is not None),
        grid=(DEC_BATCH, n_pair),
        in_specs=[seq_in, seq_in, seq_in, seq_in, seq_in, ctx, ctx,
                  pl.BlockSpec((None, 2, 2 * NA_ROWS - 1, LANES), lambda b, p: (idx, p, 0, 0)),
                  pl.BlockSpec((None, CONV_W, LANES), lambda b, p: (idx, 0, p)),
                  pl.BlockSpec((None, 1, LANES), lambda b, p: (idx, 0, p))] + cast_in,
        out_specs=[seq_out, seq_out] + cast_out,
        out_shape=[
            jax.ShapeDtypeStruct((N_SAMPLE, HD_A), BF16),
            jax.ShapeDtypeStruct((N_SAMPLE, C_B), BF16),
        ] + cast_shape,
        scratch_shapes=[pltpu.VMEM((NA_ROWS, 2 * GRID_W, NA_ROWS * GRID_W), F32)],
        compiler_params=_params(2),
        name="even_sample",
    )(q, k, v, bg, z, ctx_k, ctx_v, rpb, conv_w, conv_b, *cast_args)


def _split_groups(q):
    low = _low_half(q.shape[0])
    zero = jnp.zeros((q.shape[0], LANES), BF16)
    pairs = [q[:, j * LANES:(j + 1) * LANES] for j in range(GQA_GROUP)]
    return jnp.concatenate([jnp.where(low, p, zero) for p in pairs] + [jnp.where(low, zero, p) for p in pairs],
                           axis=0)


def _merge_groups(o, m):
    low = _low_half(m)
    half = GQA_GROUP * m
    return jnp.concatenate([jnp.where(low, o[j * m:(j + 1) * m], o[half + j * m:half + (j + 1) * m])
                            for j in range(GQA_GROUP)], axis=-1)


def _odd_prompt_kernel(uc_ref, q_ref, k_ref, v_ref, pw_ref, ps_ref, fc_ref, fd_ref):
    seqs = [slice(b * SEQ, (b + 1) * SEQ) for b in range(SEQ_PER_TILE)]
    scores = [_dot_t(_split_groups(q_ref[rows, :]), k_ref[rows, :]) for rows in seqs]
    for s, rows in zip(scores, seqs):
        o = _softmax_pv([s], [_with_ones(v_ref[rows, :])])
        fd_ref[rows, :] = _merge_groups(o, SEQ).astype(BF16)
    for rows in seqs:
        fc_ref[rows, :] = _pool_mix(uc_ref[rows, :], pw_ref, ps_ref, SEQ).astype(BF16)


def _odd_prompt(uc, q, k, v, pool_w, pool_scale, idx):
    seq = lambda w: pl.BlockSpec((TM, w), lambda t: (t, 0))
    return pl.pallas_call(
        _odd_prompt_kernel,
        grid=(PROMPT_TILES,),
        in_specs=[seq(C_POOL), seq(HD_Q), seq(HD_KV), seq(HD_KV),
                  pl.BlockSpec((None, N_POOL, POOL_C, POOL_C), lambda t: (idx, 0, 0, 0)),
                  pl.BlockSpec((None, 1, C_POOL), lambda t: (idx, 0, 0))],
        out_specs=[seq(C_POOL), seq(HD_Q)],
        out_shape=[
            jax.ShapeDtypeStruct((N_PROMPT, C_POOL), BF16),
            jax.ShapeDtypeStruct((N_PROMPT, HD_Q), BF16),
        ],
        compiler_params=_params(1),
        name="odd_prompt",
    )(uc, q, k, v, pool_w, pool_scale)


def _odd_sample_kernel(*refs, n_cast):
    uc_ref, q_ref, k_ref, v_ref, ck_ref, cv_ref, pw_ref, ps_ref = refs[0:8]
    fc_ref, fd_ref = refs[8 + n_cast:10 + n_cast]
    kt_ref, vext_ref, ckpair_ref, cvext_ref = refs[10 + 2 * n_cast:]
    _cast_weights(refs[8:8 + n_cast], refs[10 + n_cast:10 + 2 * n_cast])

    @pl.when(pl.program_id(1) == 0)
    def _():
        fc_ref[...] = _pool_mix(uc_ref[...], pw_ref, ps_ref, DEC_SEQ).astype(BF16)
        kt_ref[...] = k_ref[...].astype(F32).T.astype(BF16)
        vext_ref[...] = _with_ones(v_ref[...])
        ckpair_ref[...] = _pair_heads_t(ck_ref).astype(BF16)
        cvext_ref[...] = _with_ones(_pair_heads_t(cv_ref).T.astype(BF16))

    k_t = kt_ref[...]
    ck_t = ckpair_ref[...]
    low = _low_half(Q_BLK)
    scores = []
    for j in range(GQA_GROUP):
        q2 = _split_pair(q_ref[:, j * LANES:(j + 1) * LANES])
        scores.append([_dot(q2, k_t), _dot(q2, ck_t)])
    outs = []
    for s in scores:
        o = _softmax_pv(s, [vext_ref[...], cvext_ref[...]])
        outs.append(jnp.where(low, o[0:Q_BLK], o[Q_BLK:2 * Q_BLK]))
    fd_ref[...] = jnp.concatenate(outs, axis=-1).astype(BF16)


def _odd_sample(uc, q, k, v, ctx_k, ctx_v, pool_w, pool_scale, idx, ffn_w1, ffn_w2, cast_jobs):
    first = N_PROMPT // DEC_SEQ
    n_qb = DEC_SEQ // Q_BLK
    first_q = N_PROMPT // Q_BLK
    whole_in = lambda w: pl.BlockSpec((DEC_SEQ, w), lambda b, i: (first + b, 0))
    ctx = pl.BlockSpec((None, None, HKV_D, HEAD_DIM, PAST_LEN), lambda b, i: (b, idx, 0, 0, 0))
    cast_in, cast_out, cast_shape = _cast_specs(cast_jobs, DEC_BATCH * n_qb, lambda b, i: b * n_qb + i)
    return pl.pallas_call(
        functools.partial(_odd_sample_kernel, n_cast=len(cast_in)),
        grid=(DEC_BATCH, n_qb),
        in_specs=[whole_in(C_POOL),
                  pl.BlockSpec((Q_BLK, HD_Q), lambda b, i: (first_q + b * n_qb + i, 0)),
                  whole_in(HD_KV), whole_in(HD_KV), ctx, ctx,
                  pl.BlockSpec((None, N_POOL, POOL_C, POOL_C), lambda b, i: (idx, 0, 0, 0)),
                  pl.BlockSpec((None, 1, C_POOL), lambda b, i: (idx, 0, 0))] + cast_in,
        out_specs=[pl.BlockSpec((DEC_SEQ, C_POOL), lambda b, i: (b, 0)),
                   pl.BlockSpec((Q_BLK, HD_Q), lambda b, i: (b * n_qb + i, 0))] + cast_out,
        out_shape=[
            jax.ShapeDtypeStruct((N_SAMPLE, C_POOL), BF16),
            jax.ShapeDtypeStruct((N_SAMPLE, HD_Q), BF16),
        ] + cast_shape,
        scratch_shapes=[pltpu.VMEM((LANES, DEC_SEQ), BF16), pltpu.VMEM((DEC_SEQ, 2 * LANES), BF16),
                        pltpu.VMEM((LANES, PAST_LEN), BF16), pltpu.VMEM((PAST_LEN, 2 * LANES), BF16)],
        compiler_params=_params(2),
        name="odd_sample",
    )(uc, q, k, v, ctx_k, ctx_v, pool_w, pool_scale, *([ffn_w1, ffn_w2] * len(cast_jobs)))


def _rope_tables():
    t = np.arange(DEC_SEQ)
    n_freq = HEAD_DIM // 4
    inv = ROPE_BASE ** (-np.arange(n_freq, dtype=np.float64) / n_freq)
    ang = np.concatenate([(t // GRID_W)[:, None] * inv, (t % GRID_W)[:, None] * inv], axis=-1)
    cos = np.repeat(np.cos(ang), 2, axis=-1)
    sin = np.repeat(np.sin(ang), 2, axis=-1) * np.tile([-1.0, 1.0], HEAD_DIM // 2)
    n_heads = HQ_D + HKV_D
    cos = np.concatenate([np.ones((TM, QK_W)), np.tile(cos, (1, n_heads))], axis=0)
    sin = np.concatenate([np.zeros((TM, QK_W)), np.tile(sin, (1, n_heads))], axis=0)
    return jnp.asarray(cos, F32), jnp.asarray(sin, F32)


def _head_ones():
    head = np.arange(MXU_DIM) // HEAD_DIM
    return jnp.asarray(head[:, None] == head[None, :], BF16)


def kernel(x_prompt, x_sample, cache_a_k, cache_a_v, cache_d_k, cache_d_v, c, c_ctx, mod_w, mod_b, norm_w,
           ffn_w1, ffn_w2, ev_w_in, ev_rpb, ev_conv_w, ev_conv_b, ev_w_out, od_w_in, od_pool_w,
           od_pool_scale, od_q_norm, od_k_norm, od_w_out):
    xs = (x_prompt.reshape(N_PROMPT, D_MODEL), x_sample.reshape(N_SAMPLE, D_MODEL))
    cond = jnp.concatenate([c_ctx[None, :], c, jnp.zeros((COND_PAD - N_COND, D_MODEL), F32)], axis=0)
    mods, w1_a, w2_a = _modulation(cond, mod_w, mod_b, ffn_w1, ffn_w2)
    mods = mods.reshape(DEPTH, COND_PAD, N_MOD, D_MODEL)
    cache_a_k, cache_a_v, cache_d_k, cache_d_v = (jnp.swapaxes(t, -1, -2)
                                                  for t in (cache_a_k, cache_a_v, cache_d_k, cache_d_v))
    rpb = jnp.pad(ev_rpb, ((0, 0), (0, 0), (0, 0), (0, LANES - ev_rpb.shape[-1])))
    conv_b = ev_conv_b[:, None, :]
    pool_w, pool_scale = od_pool_w.astype(BF16), od_pool_scale[:, None, :]
    cos_t, sin_t = _rope_tables()
    ones_bd = _head_ones()
    states = []
    for l in range(DEPTH):
        i = l // 2
        last = l == DEPTH - 1
        if l == 0:
            x, ev_w_in_b = _ffn(xs, mods, norm_w, w1_a, w2_a, l, 0, ev_w_in=ev_w_in)
        else:
            x, = _ffn(xs, mods, norm_w, w1_a, w2_a, l, 0)
        cast_jobs = [(l, 1)] + ([] if last else [(l + 1, 0)])
        if l % 2 == 0:
            q, k, v, bg, z, s_k, s_v = _even_in(x, mods, norm_w, ev_w_in_b, l)
            fa_p, fb_p = _even_prompt(q, k, v, bg, z, ev_conv_w, conv_b, i)
            fa_s, fb_s, *cast = _even_sample(q, k, v, bg, z, cache_a_k, cache_a_v, rpb, ev_conv_w, conv_b, i,
                                             ffn_w1, ffn_w2, cast_jobs, mixer_w=(ev_w_out, od_w_in, od_w_out))
            ev_w_out_b, od_w_in_b, od_w_out_b = cast[2 * len(cast_jobs):]
            w_out = ev_w_out_b
        else:
            gain = jnp.concatenate([jnp.tile(od_q_norm[i], HQ_D), jnp.tile(od_k_norm[i], HKV_D)])[None, :]
            uc, q, k, v, s_k, s_v = _odd_in(x, mods, norm_w, od_w_in_b, gain, cos_t, sin_t, ones_bd, l)
            fa_p, fb_p = _odd_prompt(uc, q, k, v, pool_w, pool_scale, i)
            fa_s, fb_s, *cast = _odd_sample(uc, q, k, v, cache_d_k, cache_d_v, pool_w, pool_scale, i,
                                            ffn_w1, ffn_w2, cast_jobs)
            w_out = od_w_out_b
        w1_b, w2_b = cast[0:2]
        if not last:
            w1_a, w2_a = cast[2:4]
        states.append((jnp.swapaxes(s_k, -1, -2), jnp.swapaxes(s_v, -1, -2)))
        xs = tuple(_ffn((x,), mods, norm_w, w1_b, w2_b, l, 2, feats=(fa_p, fb_p, fa_s, fb_s), w_out=w_out,
                        split_out=last))
    y_prompt = xs[0].reshape(BATCH, SEQ, D_MODEL)
    y_sample = xs[1].reshape(DEC_BATCH, DEC_SEQ, D_MODEL)
    return (y_prompt, y_sample, states[0][0], states[0][1], states[1][0], states[1][1])
```

```python
import functools

import jax
import jax.numpy as jnp
import numpy as np
from jax import lax
from jax.experimental import pallas as pl
from jax.experimental.pallas import tpu as pltpu

D_MODEL = 1024
BATCH = 32
SEQ = 256
DEPTH = 2
DEC_BATCH = 2
DEC_SEQ = 2048
PAST_LEN = 256
GRID_W = 64
HEAD_DIM = 64
N_MOD = 9
N_NORM = 6
D_FF = 2816
FFN_RES = 0.5
H_A = 8
NA_ROWS = 8
NA_COLS = 16
C_B = 512
CONV_W = 3
C_POOL = 512
POOL_WINDOWS = (2, 4, 8, 16)
N_POOL = 4
POOL_C = C_POOL // N_POOL
HQ_D = 8
HKV_D = 2
GQA_GROUP = HQ_D // HKV_D
ROPE_BASE = 10000.0
EVEN_IN = 3 * H_A * HEAD_DIM + 3 * C_B
ODD_IN = C_POOL + (HQ_D + 2 * HKV_D) * HEAD_DIM
RMS_EPS = 1e-6
NEG_INF = -1e30
ATT_SCALE = HEAD_DIM ** -0.5

LANES = 128
GRID_ROWS = DEC_SEQ // GRID_W
N_PROMPT = BATCH * SEQ
N_SAMPLE = DEC_BATCH * DEC_SEQ
N_TOK = N_PROMPT + N_SAMPLE
N_COND = 1 + DEC_BATCH
COND_PAD = 8
HD_A = H_A * HEAD_DIM
HD_Q = HQ_D * HEAD_DIM
HD_KV = HKV_D * HEAD_DIM
QK_W = HD_Q + HD_KV
POOL_PAD = 8

TM = 1024
N_TILES = N_TOK // TM
PROMPT_TILES = N_PROMPT // TM
TILES_PER_SAMPLE = DEC_SEQ // TM
SEQ_PER_TILE = TM // SEQ
TM_FFN = 1024
FFN_HALF = TM_FFN // 2
MIX_CAST_STEPS = N_PROMPT // TM_FFN
MIX_CAST_ROWS = D_MODEL // MIX_CAST_STEPS
assert MIX_CAST_ROWS == 2 * HEAD_DIM and (DEPTH + 1) // 2 == 1 and DEPTH // 2 == 1
MXU_DIM = 256
FF_CHUNKS = tuple((lo, min(lo + 3 * MXU_DIM, D_FF)) for lo in range(0, D_FF, 3 * MXU_DIM))
MOD_TN = 1152
Q_BLK = 256
ROW_UNROLL = 8
VMEM_LIMIT = 60 * 1024 * 1024

F32 = jnp.float32
BF16 = jnp.bfloat16


def _params(n_grid):
    return pltpu.CompilerParams(dimension_semantics=("arbitrary",) * n_grid, vmem_limit_bytes=VMEM_LIMIT)


def _cond_of_tile(i, tm=TM):
    n_prompt = N_PROMPT // tm
    return jnp.where(i < n_prompt, 0, 1 + (i - n_prompt) // (DEC_SEQ // tm))


def _prompt_tile(i, tm=TM):
    return jnp.minimum(i, N_PROMPT // tm - 1)


def _sample_tile(i, tm=TM):
    return jnp.maximum(i - N_PROMPT // tm, 0)


def _rms(x, g):
    return x * lax.rsqrt(jnp.mean(x * x, axis=-1, keepdims=True) + RMS_EPS) * g


def _sigmoid(x):
    return 1.0 / (1.0 + jnp.exp(-x))


def _dot(a, b):
    return jnp.dot(a, b, preferred_element_type=F32)


def _dot_t(a, b):
    return lax.dot_general(a, b, (((1,), (1,)), ((), ())), preferred_element_type=F32)


def _mod_kernel(c_ref, w_ref, b_ref, w1f_ref, w2f_ref, o_ref, w1b_ref, w2b_ref):
    _cast_weights([w1f_ref, w2f_ref], [w1b_ref, w2b_ref])
    c = c_ref[...]
    sc = (c * _sigmoid(c)).astype(BF16)
    o_ref[...] = _dot(sc, w_ref[...].astype(BF16)) + b_ref[...]


def _modulation(cond, mod_w, mod_b, ffn_w1, ffn_w2):
    n_col = N_MOD * D_MODEL
    n_blk = n_col // MOD_TN
    cast_in, cast_out, cast_shape = _cast_specs([(0, 0)], DEPTH * n_blk, lambda l, j: l * n_blk + j)
    return pl.pallas_call(
        _mod_kernel,
        grid=(DEPTH, n_blk),
        in_specs=[
            pl.BlockSpec((COND_PAD, D_MODEL), lambda l, j: (0, 0)),
            pl.BlockSpec((None, D_MODEL, MOD_TN), lambda l, j: (l, 0, j)),
            pl.BlockSpec((None, 1, MOD_TN), lambda l, j: (l, 0, j)),
        ] + cast_in,
        out_specs=[pl.BlockSpec((None, COND_PAD, MOD_TN), lambda l, j: (l, 0, j))] + cast_out,
        out_shape=[jax.ShapeDtypeStruct((DEPTH, COND_PAD, n_col), F32)] + cast_shape,
        compiler_params=_params(2),
        name="modulation",
    )(cond, mod_w, mod_b.reshape(DEPTH, 1, n_col), ffn_w1, ffn_w2)


def _mod_spec(layer, tm=TM):
    return pl.BlockSpec((None, None, N_MOD, D_MODEL), lambda i: (layer, _cond_of_tile(i, tm), 0, 0))


def _gain_spec(layer):
    return pl.BlockSpec((None, N_NORM, D_MODEL), lambda i: (layer, 0, 0))


def _cast_specs(jobs, n_steps, step_of):
    r1, r2 = D_MODEL // n_steps, D_FF // n_steps
    in_specs, out_specs, out_shape = [], [], []
    for layer, which in jobs:
        in_specs += [pl.BlockSpec((None, None, r1, 2 * D_FF), lambda *g, lw=(layer, which): (*lw, step_of(*g), 0)),
                     pl.BlockSpec((None, None, r2, D_MODEL), lambda *g, lw=(layer, which): (*lw, step_of(*g), 0))]
        out_specs += [pl.BlockSpec((r1, 2 * D_FF), lambda *g: (step_of(*g), 0)),
                      pl.BlockSpec((r2, D_MODEL), lambda *g: (step_of(*g), 0))]
        out_shape += [jax.ShapeDtypeStruct((D_MODEL, 2 * D_FF), BF16), jax.ShapeDtypeStruct((D_FF, D_MODEL), BF16)]
    return in_specs, out_specs, out_shape


def _cast_weights(f32_refs, bf16_refs):
    for src, dst in zip(f32_refs, bf16_refs):
        dst[...] = src[...].astype(BF16)


def _ev_in_cast_specs():
    step = lambda i: jnp.minimum(i, MIX_CAST_STEPS - 1)
    return ([pl.BlockSpec((None, MIX_CAST_ROWS, EVEN_IN), lambda i: (0, step(i), 0))],
            [pl.BlockSpec((MIX_CAST_ROWS, EVEN_IN), lambda i: (step(i), 0))],
            [jax.ShapeDtypeStruct((D_MODEL, EVEN_IN), BF16)])


def _mixer_cast_specs(step_of):
    rows = MIX_CAST_ROWS
    n_pool_blk = C_POOL // rows

    def od_out_block(s, g):
        return jnp.where(s < n_pool_blk, 2 * s + g, C_POOL // HEAD_DIM + g * GQA_GROUP + s - n_pool_blk)

    widths = (D_MODEL, ODD_IN, D_MODEL)
    in_specs = [pl.BlockSpec((None, rows, D_MODEL), lambda *g: (0, step_of(*g), 0)),
                pl.BlockSpec((None, rows, ODD_IN), lambda *g: (0, step_of(*g), 0)),
                pl.BlockSpec((None, HEAD_DIM, D_MODEL), lambda *g: (0, od_out_block(step_of(*g), 0), 0)),
                pl.BlockSpec((None, HEAD_DIM, D_MODEL), lambda *g: (0, od_out_block(step_of(*g), 1), 0))]
    out_specs = [pl.BlockSpec((rows, w), lambda *g: (step_of(*g), 0)) for w in widths]
    out_shape = [jax.ShapeDtypeStruct((D_MODEL, w), BF16) for w in widths]
    return in_specs, out_specs, out_shape


def _cast_mixer_weights(ev_out_f, od_in_f, od_out_lo_f, od_out_hi_f, ev_out_b, od_in_b, od_out_b):
    ev_out_b[...] = ev_out_f[...].astype(BF16)
    w = od_in_f[...]
    q = w[:, C_POOL:C_POOL + HD_Q]
    heads = [q[:, (g * GQA_GROUP + j) * HEAD_DIM:(g * GQA_GROUP + j + 1) * HEAD_DIM]
             for j in range(GQA_GROUP) for g in range(HKV_D)]
    od_in_b[...] = jnp.concatenate([w[:, 0:C_POOL]] + heads + [w[:, C_POOL + HD_Q:]], axis=1).astype(BF16)
    od_out_b[...] = jnp.concatenate([od_out_lo_f[...], od_out_hi_f[...]], axis=0).astype(BF16)


def _ffn_kernel(*refs, sub, split_in, mix, split_out, cast_ev_in):
    refs = list(refs)
    x_refs = [refs.pop(0) for _ in range(2 if split_in else 1)]
    feat_refs = [refs.pop(0) for _ in range(4 if mix else 0)]
    m_ref, g_ref = refs.pop(0), refs.pop(0)
    wo_ref = refs.pop(0) if mix else None
    w1_ref, w2_ref = refs.pop(0), refs.pop(0)
    if cast_ev_in:
        @pl.when(pl.program_id(0) < MIX_CAST_STEPS)
        def _():
            refs[-1][...] = refs[0][...].astype(BF16)
        refs = refs[1:-1]
    out_refs = refs
    is_prompt = pl.program_id(0) < N_PROMPT // TM_FFN
    shift = m_ref[3 * sub:3 * sub + 1, :]
    scale = m_ref[3 * sub + 1:3 * sub + 2, :]
    gate = m_ref[3 * sub + 2:3 * sub + 3, :]
    halves = [slice(hf * FFN_HALF, (hf + 1) * FFN_HALF) for hf in range(TM_FFN // FFN_HALF)]
    mixed = []
    if mix:
        for rows in halves:
            fa = jnp.where(is_prompt, feat_refs[0][rows, :], feat_refs[2][rows, :])
            fb = jnp.where(is_prompt, feat_refs[1][rows, :], feat_refs[3][rows, :])
            mixed.append(_dot(jnp.concatenate([fa, fb], axis=1), wo_ref[...]))
    for half, rows in enumerate(halves):
        if split_in:
            x = jnp.where(is_prompt, x_refs[0][rows, :], x_refs[1][rows, :])
        else:
            x = x_refs[0][rows, :]
        if mix:
            x = x + m_ref[5:6, :] * _rms(mixed[half], g_ref[3:4, :])
        h = (_rms(x, g_ref[2 * sub:2 * sub + 1, :]) * (1.0 + scale) + shift).astype(BF16)
        acc = jnp.zeros((FFN_HALF, D_MODEL), F32)
        for lo, hi in FF_CHUNKS:
            a = _dot(h, w1_ref[:, lo:hi])
            u = _dot(h, w1_ref[:, D_FF + lo:D_FF + hi])
            act = (a * _sigmoid(a) * u).astype(BF16)
            acc = acc + _dot(act, w2_ref[lo:hi, :])
        y = x + FFN_RES * gate * _rms(acc, g_ref[2 * sub + 1:2 * sub + 2, :])
        if split_out:
            @pl.when(is_prompt)
            def _():
                out_refs[0][rows, :] = y

            @pl.when(jnp.logical_not(is_prompt))
            def _():
                out_refs[1][rows, :] = y
        else:
            out_refs[0][rows, :] = y


def _ffn(xs, mods, gains, w1, w2, layer, sub, feats=None, w_out=None, split_out=False, ev_w_in=None):
    split_in = len(xs) == 2
    mix = feats is not None
    cast_ev_in = ev_w_in is not None
    resident = pl.Buffered(1)
    half_w = D_MODEL // 2
    feat_specs, mix_w_spec = [], []
    if mix:
        feat_p = pl.BlockSpec((TM_FFN, half_w), lambda i: (_prompt_tile(i, TM_FFN), 0))
        feat_s = pl.BlockSpec((TM_FFN, half_w), lambda i: (_sample_tile(i, TM_FFN), 0))
        feat_specs = [feat_p, feat_p, feat_s, feat_s]
        mix_w_spec = [pl.BlockSpec((D_MODEL, D_MODEL), lambda i: (0, 0), pipeline_mode=resident)]
    cast_in, cast_out, cast_shape, cast_args = [], [], [], ()
    if cast_ev_in:
        cast_in, cast_out, cast_shape = _ev_in_cast_specs()
        cast_args = (ev_w_in,)
    tok = pl.BlockSpec((TM_FFN, D_MODEL), lambda i: (i, 0))
    prompt_tok = pl.BlockSpec((TM_FFN, D_MODEL), lambda i: (_prompt_tile(i, TM_FFN), 0))
    sample_tok = pl.BlockSpec((TM_FFN, D_MODEL), lambda i: (_sample_tile(i, TM_FFN), 0))
    if split_out:
        out_specs = [prompt_tok, sample_tok]
        out_shape = [jax.ShapeDtypeStruct((N_PROMPT, D_MODEL), F32),
                     jax.ShapeDtypeStruct((N_SAMPLE, D_MODEL), F32)]
    else:
        out_specs = [tok]
        out_shape = [jax.ShapeDtypeStruct((N_TOK, D_MODEL), F32)]
    return pl.pallas_call(
        functools.partial(_ffn_kernel, sub=sub, split_in=split_in, mix=mix, split_out=split_out,
                          cast_ev_in=cast_ev_in),
        grid=(N_TOK // TM_FFN,),
        in_specs=([prompt_tok, sample_tok] if split_in else [tok]) + feat_specs + [
            _mod_spec(layer, TM_FFN),
            _gain_spec(layer),
        ] + mix_w_spec + [
            pl.BlockSpec((D_MODEL, 2 * D_FF), lambda i: (0, 0), pipeline_mode=resident),
            pl.BlockSpec((D_FF, D_MODEL), lambda i: (0, 0), pipeline_mode=resident),
        ] + cast_in,
        out_specs=out_specs + cast_out,
        out_shape=out_shape + cast_shape,
        compiler_params=_params(1),
        name=f"ffn{sub}",
    )(*xs, *(feats or ()), mods, gains, *((w_out,) if mix else ()), w1, w2, *cast_args)


def _mixer_norm(x, m_ref, g_ref):
    return (_rms(x, g_ref[2:3, :]) * (1.0 + m_ref[4:5, :]) + m_ref[3:4, :]).astype(BF16)


def _store_heads(state_ref, b, x, n_heads):
    xt = x.T
    for h in range(n_heads):
        state_ref[b, 0, h] = xt[h * HEAD_DIM:(h + 1) * HEAD_DIM, :]


def _even_in_kernel(x_ref, m_ref, g_ref, w_ref, q_ref, k_ref, v_ref, bg_ref, z_ref, sk_ref, sv_ref):
    kv = []
    for b in range(SEQ_PER_TILE):
        rows = slice(b * SEQ, (b + 1) * SEQ)
        u = _dot(_mixer_norm(x_ref[rows, :], m_ref, g_ref), w_ref[...])
        k = u[:, HD_A:2 * HD_A]
        v = u[:, 2 * HD_A:3 * HD_A]
        q_ref[rows, :] = (u[:, 0:HD_A] * ATT_SCALE).astype(BF16)
        k_ref[rows, :] = k.astype(BF16)
        v_ref[rows, :] = v.astype(BF16)
        bg_ref[rows, :] = u[:, 3 * HD_A:3 * HD_A + C_B]
        z_ref[rows, :] = u[:, 3 * HD_A + C_B:3 * HD_A + 2 * C_B] * u[:, 3 * HD_A + 2 * C_B:3 * HD_A + 3 * C_B]
        kv.append((k, v))

    @pl.when(pl.program_id(0) < PROMPT_TILES)
    def _():
        for b, (k, v) in enumerate(kv):
            _store_heads(sk_ref, b, k, H_A)
            _store_heads(sv_ref, b, v, H_A)


def _state_spec(n_heads):
    return pl.BlockSpec((SEQ_PER_TILE, 1, n_heads, HEAD_DIM, SEQ), lambda i: (_prompt_tile(i), 0, 0, 0, 0))


def _even_in(x, mods, gains, w_in, layer):
    tok = lambda w: pl.BlockSpec((TM, w), lambda i: (i, 0))
    state = jax.ShapeDtypeStruct((BATCH, 1, H_A, HEAD_DIM, SEQ), F32)
    return pl.pallas_call(
        _even_in_kernel,
        grid=(N_TILES,),
        in_specs=[
            tok(D_MODEL), _mod_spec(layer), _gain_spec(layer),
            pl.BlockSpec((D_MODEL, EVEN_IN), lambda i: (0, 0)),
        ],
        out_specs=[tok(HD_A), tok(HD_A), tok(HD_A), tok(C_B), tok(C_B), _state_spec(H_A), _state_spec(H_A)],
        out_shape=[
            jax.ShapeDtypeStruct((N_TOK, HD_A), BF16),
            jax.ShapeDtypeStruct((N_TOK, HD_A), BF16),
            jax.ShapeDtypeStruct((N_TOK, HD_A), BF16),
            jax.ShapeDtypeStruct((N_TOK, C_B), F32),
            jax.ShapeDtypeStruct((N_TOK, C_B), F32),
            state, state,
        ],
        compiler_params=_params(1),
        name="even_in",
    )(x, mods, gains, w_in)


def _swap_pairs(x):
    n = x.shape[-1]
    lane = lax.broadcasted_iota(jnp.int32, x.shape, x.ndim - 1)
    return jnp.where(lane % 2 == 0, pltpu.roll(x, n - 1, x.ndim - 1), pltpu.roll(x, 1, x.ndim - 1))


def _odd_in_kernel(x_ref, m_ref, g_ref, w_ref, ng_ref, cos_ref, sin_ref, ones_ref,
                   uc_ref, q_ref, k_ref, v_ref, sk_ref, sv_ref):
    ones = ones_ref[...]
    kv = []
    for b in range(SEQ_PER_TILE):
        rows = slice(b * SEQ, (b + 1) * SEQ)
        u = _dot(_mixer_norm(x_ref[rows, :], m_ref, g_ref), w_ref[...])
        uc_ref[rows, :] = u[:, 0:C_POOL]
        qk = u[:, C_POOL:C_POOL + QK_W]
        sq = qk * qk
        hi = sq.astype(BF16)
        lo = (sq - hi.astype(F32)).astype(BF16)
        sums = []
        for c0 in range(0, QK_W, MXU_DIM):
            c1 = min(c0 + MXU_DIM, QK_W)
            tile = ones[0:c1 - c0, 0:c1 - c0]
            sums.append(_dot(hi[:, c0:c1], tile) + _dot(lo[:, c0:c1], tile))
        ms = jnp.concatenate(sums, axis=-1) * (1.0 / HEAD_DIM)
        n = qk * lax.rsqrt(ms + RMS_EPS) * ng_ref[...]
        r = n * cos_ref[rows, :] + _swap_pairs(n) * sin_ref[rows, :]
        k = r[:, HD_Q:QK_W]
        v = u[:, C_POOL + QK_W:ODD_IN]
        q_ref[rows, :] = (r[:, 0:HD_Q] * ATT_SCALE).astype(BF16)
        k_ref[rows, :] = k.astype(BF16)
        v_ref[rows, :] = v.astype(BF16)
        kv.append((k, v))

    @pl.when(pl.program_id(0) < PROMPT_TILES)
    def _():
        for b, (k, v) in enumerate(kv):
            _store_heads(sk_ref, b, k, HKV_D)
            _store_heads(sv_ref, b, v, HKV_D)


def _rope_tile_index(i):
    return jnp.where(i < PROMPT_TILES, 0, 1 + (i - PROMPT_TILES) % TILES_PER_SAMPLE)


def _odd_in(x, mods, gains, w_in, qk_gain, cos_t, sin_t, ones_bd, layer):
    tok = lambda w: pl.BlockSpec((TM, w), lambda i: (i, 0))
    state = jax.ShapeDtypeStruct((BATCH, 1, HKV_D, HEAD_DIM, SEQ), F32)
    return pl.pallas_call(
        _odd_in_kernel,
        grid=(N_TILES,),
        in_specs=[
            tok(D_MODEL), _mod_spec(layer), _gain_spec(layer),
            pl.BlockSpec((D_MODEL, ODD_IN), lambda i: (0, 0)),
            pl.BlockSpec((1, QK_W), lambda i: (0, 0)),
            pl.BlockSpec((TM, QK_W), lambda i: (_rope_tile_index(i), 0)),
            pl.BlockSpec((TM, QK_W), lambda i: (_rope_tile_index(i), 0)),
            pl.BlockSpec((MXU_DIM, MXU_DIM), lambda i: (0, 0)),
        ],
        out_specs=[tok(C_POOL), tok(HD_Q), tok(HD_KV), tok(HD_KV), _state_spec(HKV_D), _state_spec(HKV_D)],
        out_shape=[
            jax.ShapeDtypeStruct((N_TOK, C_POOL), F32),
            jax.ShapeDtypeStruct((N_TOK, HD_Q), BF16),
            jax.ShapeDtypeStruct((N_TOK, HD_KV), BF16),
            jax.ShapeDtypeStruct((N_TOK, HD_KV), BF16),
            state, state,
        ],
        compiler_params=_params(1),
        name="odd_in",
    )(x, mods, gains, w_in, qk_gain, cos_t, sin_t, ones_bd)


def _low_half(rows):
    return lax.broadcasted_iota(jnp.int32, (rows, LANES), 1) < HEAD_DIM


def _split_pair(qp):
    low = _low_half(qp.shape[0])
    zero = jnp.zeros_like(qp)
    return jnp.concatenate([jnp.where(low, qp, zero), jnp.where(low, zero, qp)], axis=0)


def _pair_heads_t(ref):
    return jnp.concatenate([ref[0], ref[1]], axis=0)


def _with_ones(v):
    return jnp.concatenate([v, jnp.ones_like(v)], axis=1)


def _softmax_pv(scores, values):
    m = scores[0].max(axis=-1, keepdims=True)
    for s in scores[1:]:
        m = jnp.maximum(m, s.max(axis=-1, keepdims=True))
    acc = None
    for s, v in zip(scores, values):
        o = _dot(jnp.exp(s - m).astype(BF16), v)
        acc = o if acc is None else acc + o
    return acc[:, 0:LANES] / acc[:, LANES:2 * LANES]


def _merge_pair(o):
    m = o.shape[0] // 2
    return jnp.where(_low_half(m), o[0:m], o[m:2 * m])


def _short_conv(z, cw_ref, cb_ref, seq_len):
    rows = z.shape[0]
    pos = lax.broadcasted_iota(jnp.int32, z.shape, 0) & (seq_len - 1)
    z_prev = jnp.where(pos == 0, 0.0, pltpu.roll(z, 1, 0))
    z_next = jnp.where(pos == seq_len - 1, 0.0, pltpu.roll(z, rows - 1, 0))
    y = z_prev * cw_ref[0:1, :]
    y = y + z * cw_ref[1:2, :]
    y = y + z_next * cw_ref[2:3, :]
    return y + cb_ref[...]


def _pool_mix(uc, pw_ref, ps_ref, seq_len):
    pos = lax.broadcasted_iota(jnp.int32, (seq_len, POOL_C), 0)
    pad = jnp.zeros((POOL_PAD, POOL_C), F32)
    n_ext = seq_len + 2 * POOL_PAD
    outs = []
    for gi, win in enumerate(POOL_WINDOWS):
        ug = uc[:, gi * POOL_C:(gi + 1) * POOL_C]
        run = jnp.concatenate([pad, ug, pad], axis=0)
        span = 1
        while span < win:
            run = run + pltpu.roll(run, span, 0)
            span *= 2
        back = win // 2 - 1
        if back:
            run = pltpu.roll(run, n_ext - back, 0)
        wsum = run[POOL_PAD:POOL_PAD + seq_len, :]
        cnt = jnp.minimum(pos + (win - win // 2), seq_len) - jnp.maximum(pos - win // 2, 0)
        pooled = (wsum / cnt.astype(F32) - ug).astype(BF16)
        outs.append(_dot(pooled, pw_ref[gi]))
    return jnp.concatenate(outs, axis=-1) * ps_ref[...]


def _even_prompt_kernel(q_ref, k_ref, v_ref, bg_ref, z_ref, cw_ref, cb_ref, fa_ref, fb_ref):
    for b in range(SEQ_PER_TILE):
        rows = slice(b * SEQ, (b + 1) * SEQ)
        pairs = [slice(p * LANES, (p + 1) * LANES) for p in range(H_A // 2)]
        scores = [_dot_t(_split_pair(q_ref[rows, sl]), k_ref[rows, sl]) for sl in pairs]
        outs = [_merge_pair(_softmax_pv([s], [_with_ones(v_ref[rows, sl])])) for s, sl in zip(scores, pairs)]
        fa_ref[rows, :] = jnp.concatenate(outs, axis=-1).astype(BF16)
    fb_ref[...] = (bg_ref[...] * _short_conv(z_ref[...], cw_ref, cb_ref, SEQ)).astype(BF16)


def _even_prompt(q, k, v, bg, z, conv_w, conv_b, idx):
    tile = lambda w: pl.BlockSpec((TM, w), lambda t: (t, 0))
    return pl.pallas_call(
        _even_prompt_kernel,
        grid=(PROMPT_TILES,),
        in_specs=[tile(HD_A), tile(HD_A), tile(HD_A), tile(C_B), tile(C_B),
                  pl.BlockSpec((None, CONV_W, C_B), lambda t: (idx, 0, 0)),
                  pl.BlockSpec((None, 1, C_B), lambda t: (idx, 0, 0))],
        out_specs=[tile(HD_A), tile(C_B)],
        out_shape=[
            jax.ShapeDtypeStruct((N_PROMPT, HD_A), BF16),
            jax.ShapeDtypeStruct((N_PROMPT, C_B), BF16),
        ],
        compiler_params=_params(1),
        name="even_prompt",
    )(q, k, v, bg, z, conv_w, conv_b)


def _na_bias_tiles(rpb_h):
    c = lax.broadcasted_iota(jnp.int32, (GRID_W, GRID_W), 0)
    kc = lax.broadcasted_iota(jnp.int32, (GRID_W, GRID_W), 1)
    cs = jnp.clip(c - NA_COLS // 2, 0, GRID_W - NA_COLS)
    valid = (kc >= cs) & (kc < cs + NA_COLS)
    tiles = []
    for ro in range(2 * NA_ROWS - 1):
        g = jnp.broadcast_to(rpb_h[ro:ro + 1, :], (GRID_W, LANES))
        skew = pltpu.roll(g, LANES - (NA_COLS - 1), 1, stride=1, stride_axis=0)
        tiles.append(jnp.where(valid, skew[:, 0:GRID_W], NEG_INF))
    return tiles


def _even_sample_kernel(*refs, n_cast, cast_mixers):
    n_mix_in, n_mix_out = (4, 3) if cast_mixers else (0, 0)
    q_ref, k_ref, v_ref, bg_ref, z_ref, ck_ref, cv_ref, rpb_ref, cw_ref, cb_ref = refs[0:10]
    out0 = 10 + n_cast + n_mix_in
    fa_ref, fb_ref = refs[out0:out0 + 2]
    bias_ref = refs[out0 + 2 + n_cast + n_mix_out]
    _cast_weights(refs[10:10 + n_cast], refs[out0 + 2:out0 + 2 + n_cast])
    if cast_mixers:
        _cast_mixer_weights(*refs[10 + n_cast:out0], *refs[out0 + 2 + n_cast:out0 + 2 + n_cast + n_mix_out])
    win = NA_ROWS * GRID_W
    for h in range(2):
        tiles = _na_bias_tiles(rpb_ref[h])
        for var in range(NA_ROWS):
            bias_ref[var, h * GRID_W:(h + 1) * GRID_W, :] = jnp.concatenate(
                [tiles[i - var + NA_ROWS - 1] for i in range(NA_ROWS)], axis=1)
    ck_t = _pair_heads_t(ck_ref).astype(BF16)
    cv_ext = _with_ones(_pair_heads_t(cv_ref).T.astype(BF16))

    def rows(it, carry):
        staged = []
        for j in range(ROW_UNROLL):
            r = it * ROW_UNROLL + j
            rs = jnp.clip(r - NA_ROWS // 2, 0, GRID_ROWS - NA_ROWS)
            q0 = pl.multiple_of(r * GRID_W, GRID_W)
            k0 = pl.multiple_of(rs * GRID_W, GRID_W)
            q2 = _split_pair(q_ref[pl.ds(q0, GRID_W), :])
            s_loc = _dot_t(q2, k_ref[pl.ds(k0, win), :]) + bias_ref[r - rs]
            s_ctx = _dot(q2, ck_t)
            staged.append((q0, k0, s_loc, s_ctx))
        for q0, k0, s_loc, s_ctx in staged:
            o = _softmax_pv([s_loc, s_ctx], [_with_ones(v_ref[pl.ds(k0, win), :]), cv_ext])
            fa_ref[pl.ds(q0, GRID_W), :] = _merge_pair(o).astype(BF16)
        return carry

    lax.fori_loop(0, GRID_ROWS // ROW_UNROLL, rows, 0)
    fb_ref[...] = (bg_ref[...] * _short_conv(z_ref[...], cw_ref, cb_ref, DEC_SEQ)).astype(BF16)


def _even_sample(q, k, v, bg, z, ctx_k, ctx_v, rpb, conv_w, conv_b, idx, ffn_w1, ffn_w2, cast_jobs, mixer_w=None):
    first = N_PROMPT // DEC_SEQ
    n_pair = H_A // 2
    n_steps = DEC_BATCH * n_pair
    step_of = lambda b, p: b * n_pair + p
    seq_in = pl.BlockSpec((DEC_SEQ, LANES), lambda b, p: (first + b, p))
    seq_out = pl.BlockSpec((DEC_SEQ, LANES), lambda b, p: (b, p))
    ctx = pl.BlockSpec((None, None, 2, HEAD_DIM, PAST_LEN), lambda b, p: (b, idx, p, 0, 0))
    cast_in, cast_out, cast_shape = _cast_specs(cast_jobs, n_steps, step_of)
    n_cast = len(cast_in)
    cast_args = [ffn_w1, ffn_w2] * len(cast_jobs)
    if mixer_w is not None:
        assert n_steps == MIX_CAST_STEPS
        mix_in, mix_out, mix_shape = _mixer_cast_specs(step_of)
        cast_in, cast_out, cast_shape = cast_in + mix_in, cast_out + mix_out, cast_shape + mix_shape
        cast_args += [mixer_w[0], mixer_w[1], mixer_w[2], mixer_w[2]]
    return pl.pallas_call(
        functools.partial(_even_sample_kernel, n_cast=n_cast, cast_mixers=mixer_w is not None),
        grid=(DEC_BATCH, n_pair),
        in_specs=[seq_in, seq_in, seq_in, seq_in, seq_in, ctx, ctx,
                  pl.BlockSpec((None, 2, 2 * NA_ROWS - 1, LANES), lambda b, p: (idx, p, 0, 0)),
                  pl.BlockSpec((None, CONV_W, LANES), lambda b, p: (idx, 0, p)),
                  pl.BlockSpec((None, 1, LANES), lambda b, p: (idx, 0, p))] + cast_in,
        out_specs=[seq_out, seq_out] + cast_out,
        out_shape=[
            jax.ShapeDtypeStruct((N_SAMPLE, HD_A), BF16),
            jax.ShapeDtypeStruct((N_SAMPLE, C_B), BF16),
        ] + cast_shape,
        scratch_shapes=[pltpu.VMEM((NA_ROWS, 2 * GRID_W, NA_ROWS * GRID_W), F32)],
        compiler_params=_params(2),
        name="even_sample",
    )(q, k, v, bg, z, ctx_k, ctx_v, rpb, conv_w, conv_b, *cast_args)


def _split_groups(q):
    low = _low_half(q.shape[0])
    zero = jnp.zeros((q.shape[0], LANES), BF16)
    pairs = [q[:, j * LANES:(j + 1) * LANES] for j in range(GQA_GROUP)]
    return jnp.concatenate([jnp.where(low, p, zero) for p in pairs] + [jnp.where(low, zero, p) for p in pairs],
                           axis=0)


def _merge_groups(o, m):
    low = _low_half(m)
    half = GQA_GROUP * m
    return jnp.concatenate([jnp.where(low, o[j * m:(j + 1) * m], o[half + j * m:half + (j + 1) * m])
                            for j in range(GQA_GROUP)], axis=-1)


def _odd_prompt_kernel(uc_ref, q_ref, k_ref, v_ref, pw_ref, ps_ref, fc_ref, fd_ref):
    seqs = [slice(b * SEQ, (b + 1) * SEQ) for b in range(SEQ_PER_TILE)]
    scores = [_dot_t(_split_groups(q_ref[rows, :]), k_ref[rows, :]) for rows in seqs]
    for s, rows in zip(scores, seqs):
        o = _softmax_pv([s], [_with_ones(v_ref[rows, :])])
        fd_ref[rows, :] = _merge_groups(o, SEQ).astype(BF16)
    for rows in seqs:
        fc_ref[rows, :] = _pool_mix(uc_ref[rows, :], pw_ref, ps_ref, SEQ).astype(BF16)


def _odd_prompt(uc, q, k, v, pool_w, pool_scale, idx):
    seq = lambda w: pl.BlockSpec((TM, w), lambda t: (t, 0))
    return pl.pallas_call(
        _odd_prompt_kernel,
        grid=(PROMPT_TILES,),
        in_specs=[seq(C_POOL), seq(HD_Q), seq(HD_KV), seq(HD_KV),
                  pl.BlockSpec((None, N_POOL, POOL_C, POOL_C), lambda t: (idx, 0, 0, 0)),
                  pl.BlockSpec((None, 1, C_POOL), lambda t: (idx, 0, 0))],
        out_specs=[seq(C_POOL), seq(HD_Q)],
        out_shape=[
            jax.ShapeDtypeStruct((N_PROMPT, C_POOL), BF16),
            jax.ShapeDtypeStruct((N_PROMPT, HD_Q), BF16),
        ],
        compiler_params=_params(1),
        name="odd_prompt",
    )(uc, q, k, v, pool_w, pool_scale)


def _odd_sample_kernel(*refs, n_cast):
    uc_ref, q_ref, k_ref, v_ref, ck_ref, cv_ref, pw_ref, ps_ref = refs[0:8]
    fc_ref, fd_ref = refs[8 + n_cast:10 + n_cast]
    kt_ref, vext_ref, ckpair_ref, cvext_ref = refs[10 + 2 * n_cast:]
    _cast_weights(refs[8:8 + n_cast], refs[10 + n_cast:10 + 2 * n_cast])

    @pl.when(pl.program_id(1) == 0)
    def _():
        fc_ref[...] = _pool_mix(uc_ref[...], pw_ref, ps_ref, DEC_SEQ).astype(BF16)
        kt_ref[...] = k_ref[...].astype(F32).T.astype(BF16)
        vext_ref[...] = _with_ones(v_ref[...])
        ckpair_ref[...] = _pair_heads_t(ck_ref).astype(BF16)
        cvext_ref[...] = _with_ones(_pair_heads_t(cv_ref).T.astype(BF16))

    k_t = kt_ref[...]
    ck_t = ckpair_ref[...]
    low = _low_half(Q_BLK)
    scores = []
    for j in range(GQA_GROUP):
        q2 = _split_pair(q_ref[:, j * LANES:(j + 1) * LANES])
        scores.append([_dot(q2, k_t), _dot(q2, ck_t)])
    outs = []
    for s in scores:
        o = _softmax_pv(s, [vext_ref[...], cvext_ref[...]])
        outs.append(jnp.where(low, o[0:Q_BLK], o[Q_BLK:2 * Q_BLK]))
    fd_ref[...] = jnp.concatenate(outs, axis=-1).astype(BF16)


def _odd_sample(uc, q, k, v, ctx_k, ctx_v, pool_w, pool_scale, idx, ffn_w1, ffn_w2, cast_jobs):
    first = N_PROMPT // DEC_SEQ
    n_qb = DEC_SEQ // Q_BLK
    first_q = N_PROMPT // Q_BLK
    whole_in = lambda w: pl.BlockSpec((DEC_SEQ, w), lambda b, i: (first + b, 0))
    ctx = pl.BlockSpec((None, None, HKV_D, HEAD_DIM, PAST_LEN), lambda b, i: (b, idx, 0, 0, 0))
    cast_in, cast_out, cast_shape = _cast_specs(cast_jobs, DEC_BATCH * n_qb, lambda b, i: b * n_qb + i)
    return pl.pallas_call(
        functools.partial(_odd_sample_kernel, n_cast=len(cast_in)),
        grid=(DEC_BATCH, n_qb),
        in_specs=[whole_in(C_POOL),
                  pl.BlockSpec((Q_BLK, HD_Q), lambda b, i: (first_q + b * n_qb + i, 0)),
                  whole_in(HD_KV), whole_in(HD_KV), ctx, ctx,
                  pl.BlockSpec((None, N_POOL, POOL_C, POOL_C), lambda b, i: (idx, 0, 0, 0)),
                  pl.BlockSpec((None, 1, C_POOL), lambda b, i: (idx, 0, 0))] + cast_in,
        out_specs=[pl.BlockSpec((DEC_SEQ, C_POOL), lambda b, i: (b, 0)),
                   pl.BlockSpec((Q_BLK, HD_Q), lambda b, i: (b * n_qb + i, 0))] + cast_out,
        out_shape=[
            jax.ShapeDtypeStruct((N_SAMPLE, C_POOL), BF16),
            jax.ShapeDtypeStruct((N_SAMPLE, HD_Q), BF16),
        ] + cast_shape,
        scratch_shapes=[pltpu.VMEM((LANES, DEC_SEQ), BF16), pltpu.VMEM((DEC_SEQ, 2 * LANES), BF16),
                        pltpu.VMEM((LANES, PAST_LEN), BF16), pltpu.VMEM((PAST_LEN, 2 * LANES), BF16)],
        compiler_params=_params(2),
        name="odd_sample",
    )(uc, q, k, v, ctx_k, ctx_v, pool_w, pool_scale, *([ffn_w1, ffn_w2] * len(cast_jobs)))


def _rope_tables():
    t = np.arange(DEC_SEQ)
    n_freq = HEAD_DIM // 4
    inv = ROPE_BASE ** (-np.arange(n_freq, dtype=np.float64) / n_freq)
    ang = np.concatenate([(t // GRID_W)[:, None] * inv, (t % GRID_W)[:, None] * inv], axis=-1)
    cos = np.repeat(np.cos(ang), 2, axis=-1)
    sin = np.repeat(np.sin(ang), 2, axis=-1) * np.tile([-1.0, 1.0], HEAD_DIM // 2)
    n_heads = HQ_D + HKV_D
    cos = np.concatenate([np.ones((TM, QK_W)), np.tile(cos, (1, n_heads))], axis=0)
    sin = np.concatenate([np.zeros((TM, QK_W)), np.tile(sin, (1, n_heads))], axis=0)
    return jnp.asarray(cos, F32), jnp.asarray(sin, F32)


def _head_ones():
    head = np.arange(MXU_DIM) // HEAD_DIM
    return jnp.asarray(head[:, None] == head[None, :], BF16)


def kernel(x_prompt, x_sample, cache_a_k, cache_a_v, cache_d_k, cache_d_v, c, c_ctx, mod_w, mod_b, norm_w,
           ffn_w1, ffn_w2, ev_w_in, ev_rpb, ev_conv_w, ev_conv_b, ev_w_out, od_w_in, od_pool_w,
           od_pool_scale, od_q_norm, od_k_norm, od_w_out):
    xs = (x_prompt.reshape(N_PROMPT, D_MODEL), x_sample.reshape(N_SAMPLE, D_MODEL))
    cond = jnp.concatenate([c_ctx[None, :], c, jnp.zeros((COND_PAD - N_COND, D_MODEL), F32)], axis=0)
    mods, w1_a, w2_a = _modulation(cond, mod_w, mod_b, ffn_w1, ffn_w2)
    mods = mods.reshape(DEPTH, COND_PAD, N_MOD, D_MODEL)
    cache_a_k, cache_a_v, cache_d_k, cache_d_v = (jnp.swapaxes(t, -1, -2)
                                                  for t in (cache_a_k, cache_a_v, cache_d_k, cache_d_v))
    rpb = jnp.pad(ev_rpb, ((0, 0), (0, 0), (0, 0), (0, LANES - ev_rpb.shape[-1])))
    conv_b = ev_conv_b[:, None, :]
    pool_w, pool_scale = od_pool_w.astype(BF16), od_pool_scale[:, None, :]
    cos_t, sin_t = _rope_tables()
    ones_bd = _head_ones()
    states = []
    for l in range(DEPTH):
        i = l // 2
        last = l == DEPTH - 1
        if l == 0:
            x, ev_w_in_b = _ffn(xs, mods, norm_w, w1_a, w2_a, l, 0, ev_w_in=ev_w_in)
        else:
            x, = _ffn(xs, mods, norm_w, w1_a, w2_a, l, 0)
        cast_jobs = [(l, 1)] + ([] if last else [(l + 1, 0)])
        if l % 2 == 0:
            q, k, v, bg, z, s_k, s_v = _even_in(x, mods, norm_w, ev_w_in_b, l)
            fa_p, fb_p = _even_prompt(q, k, v, bg, z, ev_conv_w, conv_b, i)
            fa_s, fb_s, *cast = _even_sample(q, k, v, bg, z, cache_a_k, cache_a_v, rpb, ev_conv_w, conv_b, i,
                                             ffn_w1, ffn_w2, cast_jobs, mixer_w=(ev_w_out, od_w_in, od_w_out))
            ev_w_out_b, od_w_in_b, od_w_out_b = cast[2 * len(cast_jobs):]
            w_out = ev_w_out_b
        else:
            gain = jnp.concatenate([jnp.tile(od_q_norm[i], HQ_D), jnp.tile(od_k_norm[i], HKV_D)])[None, :]
            uc, q, k, v, s_k, s_v = _odd_in(x, mods, norm_w, od_w_in_b, gain, cos_t, sin_t, ones_bd, l)
            fa_p, fb_p = _odd_prompt(uc, q, k, v, pool_w, pool_scale, i)
            fa_s, fb_s, *cast = _odd_sample(uc, q, k, v, cache_d_k, cache_d_v, pool_w, pool_scale, i,
                                            ffn_w1, ffn_w2, cast_jobs)
            w_out = od_w_out_b
        w1_b, w2_b = cast[0:2]
        if not last:
            w1_a, w2_a = cast[2:4]
        states.append((jnp.swapaxes(s_k, -1, -2), jnp.swapaxes(s_v, -1, -2)))
        xs = tuple(_ffn((x,), mods, norm_w, w1_b, w2_b, l, 2, feats=(fa_p, fb_p, fa_s, fb_s), w_out=w_out,
                        split_out=last))
    y_prompt = xs[0].reshape(BATCH, SEQ, D_MODEL)
    y_sample = xs[1].reshape(DEC_BATCH, DEC_SEQ, D_MODEL)
    return (y_prompt, y_sample, states[0][0], states[0][1], states[1][0], states[1][1])
```

```python
import functools

import jax
import jax.numpy as jnp
import numpy as np
from jax import lax
from jax.experimental import pallas as pl
from jax.experimental.pallas import tpu as pltpu

D_MODEL = 1024
BATCH = 32
SEQ = 256
DEPTH = 2
DEC_BATCH = 2
DEC_SEQ = 2048
PAST_LEN = 256
GRID_W = 64
HEAD_DIM = 64
N_MOD = 9
N_NORM = 6
D_FF = 2816
FFN_RES = 0.5
H_A = 8
NA_ROWS = 8
NA_COLS = 16
C_B = 512
CONV_W = 3
C_POOL = 512
POOL_WINDOWS = (2, 4, 8, 16)
N_POOL = 4
POOL_C = C_POOL // N_POOL
HQ_D = 8
HKV_D = 2
GQA_GROUP = HQ_D // HKV_D
ROPE_BASE = 10000.0
EVEN_IN = 3 * H_A * HEAD_DIM + 3 * C_B
ODD_IN = C_POOL + (HQ_D + 2 * HKV_D) * HEAD_DIM
RMS_EPS = 1e-6
NEG_INF = -1e30
ATT_SCALE = HEAD_DIM ** -0.5

LANES = 128
GRID_ROWS = DEC_SEQ // GRID_W
N_PROMPT = BATCH * SEQ
N_SAMPLE = DEC_BATCH * DEC_SEQ
N_TOK = N_PROMPT + N_SAMPLE
N_COND = 1 + DEC_BATCH
COND_PAD = 8
HD_A = H_A * HEAD_DIM
HD_Q = HQ_D * HEAD_DIM
HD_KV = HKV_D * HEAD_DIM
QK_W = HD_Q + HD_KV
POOL_PAD = 8

TM = 1024
N_TILES = N_TOK // TM
PROMPT_TILES = N_PROMPT // TM
TILES_PER_SAMPLE = DEC_SEQ // TM
SEQ_PER_TILE = TM // SEQ
TM_FFN = 1024
FFN_HALF = TM_FFN // 2
MIX_CAST_STEPS = N_PROMPT // TM_FFN
MIX_CAST_ROWS = D_MODEL // MIX_CAST_STEPS
assert MIX_CAST_ROWS == 2 * HEAD_DIM and (DEPTH + 1) // 2 == 1 and DEPTH // 2 == 1
MXU_DIM = 256
FF_CHUNKS = tuple((lo, min(lo + 3 * MXU_DIM, D_FF)) for lo in range(0, D_FF, 3 * MXU_DIM))
MOD_TN = 1152
Q_BLK = 256
ROW_UNROLL = 8
VMEM_LIMIT = 60 * 1024 * 1024

F32 = jnp.float32
BF16 = jnp.bfloat16


def _params(n_grid):
    return pltpu.CompilerParams(dimension_semantics=("arbitrary",) * n_grid, vmem_limit_bytes=VMEM_LIMIT)


def _cond_of_tile(i, tm=TM):
    n_prompt = N_PROMPT // tm
    return jnp.where(i < n_prompt, 0, 1 + (i - n_prompt) // (DEC_SEQ // tm))


def _prompt_tile(i, tm=TM):
    return jnp.minimum(i, N_PROMPT // tm - 1)


def _sample_tile(i, tm=TM):
    return jnp.maximum(i - N_PROMPT // tm, 0)


def _rms(x, g):
    return x * lax.rsqrt(jnp.mean(x * x, axis=-1, keepdims=True) + RMS_EPS) * g


def _sigmoid(x):
    return 1.0 / (1.0 + jnp.exp(-x))


def _dot(a, b):
    return jnp.dot(a, b, preferred_element_type=F32)


def _dot_t(a, b):
    return lax.dot_general(a, b, (((1,), (1,)), ((), ())), preferred_element_type=F32)


def _mod_kernel(c_ref, w_ref, b_ref, w1f_ref, w2f_ref, o_ref, w1b_ref, w2b_ref):
    _cast_weights([w1f_ref, w2f_ref], [w1b_ref, w2b_ref])
    c = c_ref[...]
    sc = (c * _sigmoid(c)).astype(BF16)
    o_ref[...] = _dot(sc, w_ref[...].astype(BF16)) + b_ref[...]


def _modulation(cond, mod_w, mod_b, ffn_w1, ffn_w2):
    n_col = N_MOD * D_MODEL
    n_blk = n_col // MOD_TN
    cast_in, cast_out, cast_shape = _cast_specs([(0, 0)], DEPTH * n_blk, lambda l, j: l * n_blk + j)
    return pl.pallas_call(
        _mod_kernel,
        grid=(DEPTH, n_blk),
        in_specs=[
            pl.BlockSpec((COND_PAD, D_MODEL), lambda l, j: (0, 0)),
            pl.BlockSpec((None, D_MODEL, MOD_TN), lambda l, j: (l, 0, j)),
            pl.BlockSpec((None, 1, MOD_TN), lambda l, j: (l, 0, j)),
        ] + cast_in,
        out_specs=[pl.BlockSpec((None, COND_PAD, MOD_TN), lambda l, j: (l, 0, j))] + cast_out,
        out_shape=[jax.ShapeDtypeStruct((DEPTH, COND_PAD, n_col), F32)] + cast_shape,
        compiler_params=_params(2),
        name="modulation",
    )(cond, mod_w, mod_b.reshape(DEPTH, 1, n_col), ffn_w1, ffn_w2)


def _mod_spec(layer, tm=TM):
    return pl.BlockSpec((None, None, N_MOD, D_MODEL), lambda i: (layer, _cond_of_tile(i, tm), 0, 0))


def _gain_spec(layer):
    return pl.BlockSpec((None, N_NORM, D_MODEL), lambda i: (layer, 0, 0))


def _cast_specs(jobs, n_steps, step_of):
    r1, r2 = D_MODEL // n_steps, D_FF // n_steps
    in_specs, out_specs, out_shape = [], [], []
    for layer, which in jobs:
        in_specs += [pl.BlockSpec((None, None, r1, 2 * D_FF), lambda *g, lw=(layer, which): (*lw, step_of(*g), 0)),
                     pl.BlockSpec((None, None, r2, D_MODEL), lambda *g, lw=(layer, which): (*lw, step_of(*g), 0))]
        out_specs += [pl.BlockSpec((r1, 2 * D_FF), lambda *g: (step_of(*g), 0)),
                      pl.BlockSpec((r2, D_MODEL), lambda *g: (step_of(*g), 0))]
        out_shape += [jax.ShapeDtypeStruct((D_MODEL, 2 * D_FF), BF16), jax.ShapeDtypeStruct((D_FF, D_MODEL), BF16)]
    return in_specs, out_specs, out_shape


def _cast_weights(f32_refs, bf16_refs):
    for src, dst in zip(f32_refs, bf16_refs):
        dst[...] = src[...].astype(BF16)


def _ev_in_cast_specs():
    step = lambda i: jnp.minimum(i, MIX_CAST_STEPS - 1)
    return ([pl.BlockSpec((None, MIX_CAST_ROWS, EVEN_IN), lambda i: (0, step(i), 0))],
            [pl.BlockSpec((MIX_CAST_ROWS, EVEN_IN), lambda i: (step(i), 0))],
            [jax.ShapeDtypeStruct((D_MODEL, EVEN_IN), BF16)])


def _mixer_cast_specs(step_of):
    rows = MIX_CAST_ROWS
    n_pool_blk = C_POOL // rows

    def od_out_block(s, g):
        return jnp.where(s < n_pool_blk, 2 * s + g, C_POOL // HEAD_DIM + g * GQA_GROUP + s - n_pool_blk)

    widths = (D_MODEL, ODD_IN, D_MODEL)
    in_specs = [pl.BlockSpec((None, rows, D_MODEL), lambda *g: (0, step_of(*g), 0)),
                pl.BlockSpec((None, rows, ODD_IN), lambda *g: (0, step_of(*g), 0)),
                pl.BlockSpec((None, HEAD_DIM, D_MODEL), lambda *g: (0, od_out_block(step_of(*g), 0), 0)),
                pl.BlockSpec((None, HEAD_DIM, D_MODEL), lambda *g: (0, od_out_block(step_of(*g), 1), 0))]
    out_specs = [pl.BlockSpec((rows, w), lambda *g: (step_of(*g), 0)) for w in widths]
    out_shape = [jax.ShapeDtypeStruct((D_MODEL, w), BF16) for w in widths]
    return in_specs, out_specs, out_shape


def _cast_mixer_weights(ev_out_f, od_in_f, od_out_lo_f, od_out_hi_f, ev_out_b, od_in_b, od_out_b):
    ev_out_b[...] = ev_out_f[...].astype(BF16)
    w = od_in_f[...]
    q = w[:, C_POOL:C_POOL + HD_Q]
    heads = [q[:, (g * GQA_GROUP + j) * HEAD_DIM:(g * GQA_GROUP + j + 1) * HEAD_DIM]
             for j in range(GQA_GROUP) for g in range(HKV_D)]
    od_in_b[...] = jnp.concatenate([w[:, 0:C_POOL]] + heads + [w[:, C_POOL + HD_Q:]], axis=1).astype(BF16)
    od_out_b[...] = jnp.concatenate([od_out_lo_f[...], od_out_hi_f[...]], axis=0).astype(BF16)


def _ffn_kernel(*refs, sub, split_in, mix, split_out, cast_ev_in):
    refs = list(refs)
    x_refs = [refs.pop(0) for _ in range(2 if split_in else 1)]
    feat_refs = [refs.pop(0) for _ in range(2 if mix else 0)]
    m_ref, g_ref = refs.pop(0), refs.pop(0)
    wo_ref = refs.pop(0) if mix else None
    w1_ref, w2_ref = refs.pop(0), refs.pop(0)
    if cast_ev_in:
        @pl.when(pl.program_id(0) < MIX_CAST_STEPS)
        def _():
            refs[-1][...] = refs[0][...].astype(BF16)
        refs = refs[1:-1]
    out_refs = refs
    is_prompt = pl.program_id(0) < N_PROMPT // TM_FFN
    shift = m_ref[3 * sub:3 * sub + 1, :]
    scale = m_ref[3 * sub + 1:3 * sub + 2, :]
    gate = m_ref[3 * sub + 2:3 * sub + 3, :]
    halves = [slice(hf * FFN_HALF, (hf + 1) * FFN_HALF) for hf in range(TM_FFN // FFN_HALF)]
    mixed = []
    if mix:
        for rows in halves:
            feat = jnp.concatenate([feat_refs[0][rows, :], feat_refs[1][rows, :]], axis=1)
            mixed.append(_dot(feat, wo_ref[...]))
    for half, rows in enumerate(halves):
        if split_in:
            x = jnp.where(is_prompt, x_refs[0][rows, :], x_refs[1][rows, :])
        else:
            x = x_refs[0][rows, :]
        if mix:
            x = x + m_ref[5:6, :] * _rms(mixed[half], g_ref[3:4, :])
        h = (_rms(x, g_ref[2 * sub:2 * sub + 1, :]) * (1.0 + scale) + shift).astype(BF16)
        acc = jnp.zeros((FFN_HALF, D_MODEL), F32)
        for lo, hi in FF_CHUNKS:
            a = _dot(h, w1_ref[:, lo:hi])
            u = _dot(h, w1_ref[:, D_FF + lo:D_FF + hi])
            act = (a * _sigmoid(a) * u).astype(BF16)
            acc = acc + _dot(act, w2_ref[lo:hi, :])
        y = x + FFN_RES * gate * _rms(acc, g_ref[2 * sub + 1:2 * sub + 2, :])
        if split_out:
            @pl.when(is_prompt)
            def _():
                out_refs[0][rows, :] = y

            @pl.when(jnp.logical_not(is_prompt))
            def _():
                out_refs[1][rows, :] = y
        else:
            out_refs[0][rows, :] = y


def _ffn(xs, mods, gains, w1, w2, layer, sub, feats=None, w_out=None, split_out=False, ev_w_in=None):
    split_in = len(xs) == 2
    mix = feats is not None
    cast_ev_in = ev_w_in is not None
    resident = pl.Buffered(1)
    half_w = D_MODEL // 2
    feat_specs, mix_w_spec = [], []
    if mix:
        feat_specs = [pl.BlockSpec((TM_FFN, half_w), lambda i: (i, 0))] * 2
        mix_w_spec = [pl.BlockSpec((D_MODEL, D_MODEL), lambda i: (0, 0), pipeline_mode=resident)]
    cast_in, cast_out, cast_shape, cast_args = [], [], [], ()
    if cast_ev_in:
        cast_in, cast_out, cast_shape = _ev_in_cast_specs()
        cast_args = (ev_w_in,)
    tok = pl.BlockSpec((TM_FFN, D_MODEL), lambda i: (i, 0))
    prompt_tok = pl.BlockSpec((TM_FFN, D_MODEL), lambda i: (_prompt_tile(i, TM_FFN), 0))
    sample_tok = pl.BlockSpec((TM_FFN, D_MODEL), lambda i: (_sample_tile(i, TM_FFN), 0))
    if split_out:
        out_specs = [prompt_tok, sample_tok]
        out_shape = [jax.ShapeDtypeStruct((N_PROMPT, D_MODEL), F32),
                     jax.ShapeDtypeStruct((N_SAMPLE, D_MODEL), F32)]
    else:
        out_specs = [tok]
        out_shape = [jax.ShapeDtypeStruct((N_TOK, D_MODEL), F32)]
    return pl.pallas_call(
        functools.partial(_ffn_kernel, sub=sub, split_in=split_in, mix=mix, split_out=split_out,
                          cast_ev_in=cast_ev_in),
        grid=(N_TOK // TM_FFN,),
        in_specs=([prompt_tok, sample_tok] if split_in else [tok]) + feat_specs + [
            _mod_spec(layer, TM_FFN),
            _gain_spec(layer),
        ] + mix_w_spec + [
            pl.BlockSpec((D_MODEL, 2 * D_FF), lambda i: (0, 0), pipeline_mode=resident),
            pl.BlockSpec((D_FF, D_MODEL), lambda i: (0, 0), pipeline_mode=resident),
        ] + cast_in,
        out_specs=out_specs + cast_out,
        out_shape=out_shape + cast_shape,
        compiler_params=_params(1),
        name=f"ffn{sub}",
    )(*xs, *(feats or ()), mods, gains, *((w_out,) if mix else ()), w1, w2, *cast_args)


def _mixer_norm(x, m_ref, g_ref):
    return (_rms(x, g_ref[2:3, :]) * (1.0 + m_ref[4:5, :]) + m_ref[3:4, :]).astype(BF16)


def _store_heads(state_ref, b, x, n_heads):
    xt = x.T
    for h in range(n_heads):
        state_ref[b, 0, h] = xt[h * HEAD_DIM:(h + 1) * HEAD_DIM, :]


def _even_in_kernel(x_ref, m_ref, g_ref, w_ref, q_ref, k_ref, v_ref, bg_ref, z_ref, sk_ref, sv_ref):
    kv = []
    for b in range(SEQ_PER_TILE):
        rows = slice(b * SEQ, (b + 1) * SEQ)
        u = _dot(_mixer_norm(x_ref[rows, :], m_ref, g_ref), w_ref[...])
        k = u[:, HD_A:2 * HD_A]
        v = u[:, 2 * HD_A:3 * HD_A]
        q_ref[rows, :] = (u[:, 0:HD_A] * ATT_SCALE).astype(BF16)
        k_ref[rows, :] = k.astype(BF16)
        v_ref[rows, :] = v.astype(BF16)
        bg_ref[rows, :] = u[:, 3 * HD_A:3 * HD_A + C_B]
        z_ref[rows, :] = u[:, 3 * HD_A + C_B:3 * HD_A + 2 * C_B] * u[:, 3 * HD_A + 2 * C_B:3 * HD_A + 3 * C_B]
        kv.append((k, v))

    @pl.when(pl.program_id(0) < PROMPT_TILES)
    def _():
        for b, (k, v) in enumerate(kv):
            _store_heads(sk_ref, b, k, H_A)
            _store_heads(sv_ref, b, v, H_A)


def _state_spec(n_heads):
    return pl.BlockSpec((SEQ_PER_TILE, 1, n_heads, HEAD_DIM, SEQ), lambda i: (_prompt_tile(i), 0, 0, 0, 0))


def _even_in(x, mods, gains, w_in, layer):
    tok = lambda w: pl.BlockSpec((TM, w), lambda i: (i, 0))
    state = jax.ShapeDtypeStruct((BATCH, 1, H_A, HEAD_DIM, SEQ), F32)
    return pl.pallas_call(
        _even_in_kernel,
        grid=(N_TILES,),
        in_specs=[
            tok(D_MODEL), _mod_spec(layer), _gain_spec(layer),
            pl.BlockSpec((D_MODEL, EVEN_IN), lambda i: (0, 0)),
        ],
        out_specs=[tok(HD_A), tok(HD_A), tok(HD_A), tok(C_B), tok(C_B), _state_spec(H_A), _state_spec(H_A)],
        out_shape=[
            jax.ShapeDtypeStruct((N_TOK, HD_A), BF16),
            jax.ShapeDtypeStruct((N_TOK, HD_A), BF16),
            jax.ShapeDtypeStruct((N_TOK, HD_A), BF16),
            jax.ShapeDtypeStruct((N_TOK, C_B), F32),
            jax.ShapeDtypeStruct((N_TOK, C_B), F32),
            state, state,
        ],
        compiler_params=_params(1),
        name="even_in",
    )(x, mods, gains, w_in)


def _swap_pairs(x):
    n = x.shape[-1]
    lane = lax.broadcasted_iota(jnp.int32, x.shape, x.ndim - 1)
    return jnp.where(lane % 2 == 0, pltpu.roll(x, n - 1, x.ndim - 1), pltpu.roll(x, 1, x.ndim - 1))


def _odd_in_kernel(x_ref, m_ref, g_ref, w_ref, ng_ref, cos_ref, sin_ref, ones_ref,
                   uc_ref, q_ref, k_ref, v_ref, sk_ref, sv_ref):
    ones = ones_ref[...]
    kv = []
    for b in range(SEQ_PER_TILE):
        rows = slice(b * SEQ, (b + 1) * SEQ)
        u = _dot(_mixer_norm(x_ref[rows, :], m_ref, g_ref), w_ref[...])
        uc_ref[rows, :] = u[:, 0:C_POOL]
        qk = u[:, C_POOL:C_POOL + QK_W]
        sq = qk * qk
        hi = sq.astype(BF16)
        lo = (sq - hi.astype(F32)).astype(BF16)
        sums = []
        for c0 in range(0, QK_W, MXU_DIM):
            c1 = min(c0 + MXU_DIM, QK_W)
            tile = ones[0:c1 - c0, 0:c1 - c0]
            sums.append(_dot(hi[:, c0:c1], tile) + _dot(lo[:, c0:c1], tile))
        ms = jnp.concatenate(sums, axis=-1) * (1.0 / HEAD_DIM)
        n = qk * lax.rsqrt(ms + RMS_EPS) * ng_ref[...]
        r = n * cos_ref[rows, :] + _swap_pairs(n) * sin_ref[rows, :]
        k = r[:, HD_Q:QK_W]
        v = u[:, C_POOL + QK_W:ODD_IN]
        q_ref[rows, :] = (r[:, 0:HD_Q] * ATT_SCALE).astype(BF16)
        k_ref[rows, :] = k.astype(BF16)
        v_ref[rows, :] = v.astype(BF16)
        kv.append((k, v))

    @pl.when(pl.program_id(0) < PROMPT_TILES)
    def _():
        for b, (k, v) in enumerate(kv):
            _store_heads(sk_ref, b, k, HKV_D)
            _store_heads(sv_ref, b, v, HKV_D)


def _rope_tile_index(i):
    return jnp.where(i < PROMPT_TILES, 0, 1 + (i - PROMPT_TILES) % TILES_PER_SAMPLE)


def _odd_in(x, mods, gains, w_in, qk_gain, cos_t, sin_t, ones_bd, layer):
    tok = lambda w: pl.BlockSpec((TM, w), lambda i: (i, 0))
    state = jax.ShapeDtypeStruct((BATCH, 1, HKV_D, HEAD_DIM, SEQ), F32)
    return pl.pallas_call(
        _odd_in_kernel,
        grid=(N_TILES,),
        in_specs=[
            tok(D_MODEL), _mod_spec(layer), _gain_spec(layer),
            pl.BlockSpec((D_MODEL, ODD_IN), lambda i: (0, 0)),
            pl.BlockSpec((1, QK_W), lambda i: (0, 0)),
            pl.BlockSpec((TM, QK_W), lambda i: (_rope_tile_index(i), 0)),
            pl.BlockSpec((TM, QK_W), lambda i: (_rope_tile_index(i), 0)),
            pl.BlockSpec((MXU_DIM, MXU_DIM), lambda i: (0, 0)),
        ],
        out_specs=[tok(C_POOL), tok(HD_Q), tok(HD_KV), tok(HD_KV), _state_spec(HKV_D), _state_spec(HKV_D)],
        out_shape=[
            jax.ShapeDtypeStruct((N_TOK, C_POOL), F32),
            jax.ShapeDtypeStruct((N_TOK, HD_Q), BF16),
            jax.ShapeDtypeStruct((N_TOK, HD_KV), BF16),
            jax.ShapeDtypeStruct((N_TOK, HD_KV), BF16),
            state, state,
        ],
        compiler_params=_params(1),
        name="odd_in",
    )(x, mods, gains, w_in, qk_gain, cos_t, sin_t, ones_bd)


def _low_half(rows):
    return lax.broadcasted_iota(jnp.int32, (rows, LANES), 1) < HEAD_DIM


def _split_pair(qp):
    low = _low_half(qp.shape[0])
    zero = jnp.zeros_like(qp)
    return jnp.concatenate([jnp.where(low, qp, zero), jnp.where(low, zero, qp)], axis=0)


def _pair_heads_t(ref):
    return jnp.concatenate([ref[0], ref[1]], axis=0)


def _with_ones(v):
    return jnp.concatenate([v, jnp.ones_like(v)], axis=1)


def _softmax_pv(scores, values):
    m = scores[0].max(axis=-1, keepdims=True)
    for s in scores[1:]:
        m = jnp.maximum(m, s.max(axis=-1, keepdims=True))
    acc = None
    for s, v in zip(scores, values):
        o = _dot(jnp.exp(s - m).astype(BF16), v)
        acc = o if acc is None else acc + o
    return acc[:, 0:LANES] / acc[:, LANES:2 * LANES]


def _merge_pair(o):
    m = o.shape[0] // 2
    return jnp.where(_low_half(m), o[0:m], o[m:2 * m])


def _short_conv(z, cw_ref, cb_ref, seq_len):
    rows = z.shape[0]
    pos = lax.broadcasted_iota(jnp.int32, z.shape, 0) & (seq_len - 1)
    z_prev = jnp.where(pos == 0, 0.0, pltpu.roll(z, 1, 0))
    z_next = jnp.where(pos == seq_len - 1, 0.0, pltpu.roll(z, rows - 1, 0))
    y = z_prev * cw_ref[0:1, :]
    y = y + z * cw_ref[1:2, :]
    y = y + z_next * cw_ref[2:3, :]
    return y + cb_ref[...]


def _pool_mix(uc, pw_ref, ps_ref, seq_len):
    pos = lax.broadcasted_iota(jnp.int32, (seq_len, POOL_C), 0)
    pad = jnp.zeros((POOL_PAD, POOL_C), F32)
    n_ext = seq_len + 2 * POOL_PAD
    outs = []
    for gi, win in enumerate(POOL_WINDOWS):
        ug = uc[:, gi * POOL_C:(gi + 1) * POOL_C]
        run = jnp.concatenate([pad, ug, pad], axis=0)
        span = 1
        while span < win:
            run = run + pltpu.roll(run, span, 0)
            span *= 2
        back = win // 2 - 1
        if back:
            run = pltpu.roll(run, n_ext - back, 0)
        wsum = run[POOL_PAD:POOL_PAD + seq_len, :]
        cnt = jnp.minimum(pos + (win - win // 2), seq_len) - jnp.maximum(pos - win // 2, 0)
        pooled = (wsum / cnt.astype(F32) - ug).astype(BF16)
        outs.append(_dot(pooled, pw_ref[gi]))
    return jnp.concatenate(outs, axis=-1) * ps_ref[...]


def _even_prompt_kernel(q_ref, k_ref, v_ref, bg_ref, z_ref, cw_ref, cb_ref, fa_ref, fb_ref):
    for b in range(SEQ_PER_TILE):
        rows = slice(b * SEQ, (b + 1) * SEQ)
        pairs = [slice(p * LANES, (p + 1) * LANES) for p in range(H_A // 2)]
        scores = [_dot_t(_split_pair(q_ref[rows, sl]), k_ref[rows, sl]) for sl in pairs]
        outs = [_merge_pair(_softmax_pv([s], [_with_ones(v_ref[rows, sl])])) for s, sl in zip(scores, pairs)]
        fa_ref[rows, :] = jnp.concatenate(outs, axis=-1).astype(BF16)
    fb_ref[...] = (bg_ref[...] * _short_conv(z_ref[...], cw_ref, cb_ref, SEQ)).astype(BF16)


def _even_prompt(q, k, v, bg, z, conv_w, conv_b, idx):
    tile = lambda w: pl.BlockSpec((TM, w), lambda t: (t, 0))
    return pl.pallas_call(
        _even_prompt_kernel,
        grid=(PROMPT_TILES,),
        in_specs=[tile(HD_A), tile(HD_A), tile(HD_A), tile(C_B), tile(C_B),
                  pl.BlockSpec((None, CONV_W, C_B), lambda t: (idx, 0, 0)),
                  pl.BlockSpec((None, 1, C_B), lambda t: (idx, 0, 0))],
        out_specs=[tile(HD_A), tile(C_B)],
        out_shape=[
            jax.ShapeDtypeStruct((N_TOK, HD_A), BF16),
            jax.ShapeDtypeStruct((N_TOK, C_B), BF16),
        ],
        compiler_params=_params(1),
        name="even_prompt",
    )(q, k, v, bg, z, conv_w, conv_b)


def _na_bias_tiles(rpb_h):
    c = lax.broadcasted_iota(jnp.int32, (GRID_W, GRID_W), 0)
    kc = lax.broadcasted_iota(jnp.int32, (GRID_W, GRID_W), 1)
    cs = jnp.clip(c - NA_COLS // 2, 0, GRID_W - NA_COLS)
    valid = (kc >= cs) & (kc < cs + NA_COLS)
    tiles = []
    for ro in range(2 * NA_ROWS - 1):
        g = jnp.broadcast_to(rpb_h[ro:ro + 1, :], (GRID_W, LANES))
        skew = pltpu.roll(g, LANES - (NA_COLS - 1), 1, stride=1, stride_axis=0)
        tiles.append(jnp.where(valid, skew[:, 0:GRID_W], NEG_INF))
    return tiles


def _even_sample_kernel(*refs, n_cast, cast_mixers):
    n_mix_in, n_mix_out = (4, 3) if cast_mixers else (0, 0)
    q_ref, k_ref, v_ref, bg_ref, z_ref, ck_ref, cv_ref, rpb_ref, cw_ref, cb_ref = refs[0:10]
    cast0 = 12
    out0 = cast0 + n_cast + n_mix_in
    fa_ref, fb_ref = refs[out0:out0 + 2]
    bias_ref = refs[out0 + 2 + n_cast + n_mix_out]
    _cast_weights(refs[cast0:cast0 + n_cast], refs[out0 + 2:out0 + 2 + n_cast])
    if cast_mixers:
        _cast_mixer_weights(*refs[cast0 + n_cast:out0], *refs[out0 + 2 + n_cast:out0 + 2 + n_cast + n_mix_out])
    win = NA_ROWS * GRID_W
    for h in range(2):
        tiles = _na_bias_tiles(rpb_ref[h])
        for var in range(NA_ROWS):
            bias_ref[var, h * GRID_W:(h + 1) * GRID_W, :] = jnp.concatenate(
                [tiles[i - var + NA_ROWS - 1] for i in range(NA_ROWS)], axis=1)
    ck_t = _pair_heads_t(ck_ref).astype(BF16)
    cv_ext = _with_ones(_pair_heads_t(cv_ref).T.astype(BF16))

    def rows(it, carry):
        staged = []
        for j in range(ROW_UNROLL):
            r = it * ROW_UNROLL + j
            rs = jnp.clip(r - NA_ROWS // 2, 0, GRID_ROWS - NA_ROWS)
            q0 = pl.multiple_of(r * GRID_W, GRID_W)
            k0 = pl.multiple_of(rs * GRID_W, GRID_W)
            q2 = _split_pair(q_ref[pl.ds(q0, GRID_W), :])
            s_loc = _dot_t(q2, k_ref[pl.ds(k0, win), :]) + bias_ref[r - rs]
            s_ctx = _dot(q2, ck_t)
            staged.append((q0, k0, s_loc, s_ctx))
        for q0, k0, s_loc, s_ctx in staged:
            o = _softmax_pv([s_loc, s_ctx], [_with_ones(v_ref[pl.ds(k0, win), :]), cv_ext])
            fa_ref[pl.ds(q0, GRID_W), :] = _merge_pair(o).astype(BF16)
        return carry

    lax.fori_loop(0, GRID_ROWS // ROW_UNROLL, rows, 0)
    fb_ref[...] = (bg_ref[...] * _short_conv(z_ref[...], cw_ref, cb_ref, DEC_SEQ)).astype(BF16)


def _even_sample(q, k, v, bg, z, ctx_k, ctx_v, rpb, conv_w, conv_b, fa, fb, idx, ffn_w1, ffn_w2, cast_jobs,
                 mixer_w=None):
    first = N_PROMPT // DEC_SEQ
    n_pair = H_A // 2
    n_steps = DEC_BATCH * n_pair
    step_of = lambda b, p: b * n_pair + p
    seq_in = pl.BlockSpec((DEC_SEQ, LANES), lambda b, p: (first + b, p))
    anywhere = pl.BlockSpec(memory_space=pl.ANY)
    ctx = pl.BlockSpec((None, None, 2, HEAD_DIM, PAST_LEN), lambda b, p: (b, idx, p, 0, 0))
    cast_in, cast_out, cast_shape = _cast_specs(cast_jobs, n_steps, step_of)
    n_cast = len(cast_in)
    cast_args = [ffn_w1, ffn_w2] * len(cast_jobs)
    if mixer_w is not None:
        assert n_steps == MIX_CAST_STEPS
        mix_in, mix_out, mix_shape = _mixer_cast_specs(step_of)
        cast_in, cast_out, cast_shape = cast_in + mix_in, cast_out + mix_out, cast_shape + mix_shape
        cast_args += [mixer_w[0], mixer_w[1], mixer_w[2], mixer_w[2]]
    return pl.pallas_call(
        functools.partial(_even_sample_kernel, n_cast=n_cast, cast_mixers=mixer_w is not None),
        grid=(DEC_BATCH, n_pair),
        in_specs=[seq_in, seq_in, seq_in, seq_in, seq_in, ctx, ctx,
                  pl.BlockSpec((None, 2, 2 * NA_ROWS - 1, LANES), lambda b, p: (idx, p, 0, 0)),
                  pl.BlockSpec((None, CONV_W, LANES), lambda b, p: (idx, 0, p)),
                  pl.BlockSpec((None, 1, LANES), lambda b, p: (idx, 0, p)), anywhere, anywhere] + cast_in,
        out_specs=[seq_in, seq_in] + cast_out,
        out_shape=[
            jax.ShapeDtypeStruct((N_TOK, HD_A), BF16),
            jax.ShapeDtypeStruct((N_TOK, C_B), BF16),
        ] + cast_shape,
        input_output_aliases={10: 0, 11: 1},
        scratch_shapes=[pltpu.VMEM((NA_ROWS, 2 * GRID_W, NA_ROWS * GRID_W), F32)],
        compiler_params=_params(2),
        name="even_sample",
    )(q, k, v, bg, z, ctx_k, ctx_v, rpb, conv_w, conv_b, fa, fb, *cast_args)


def _split_groups(q):
    low = _low_half(q.shape[0])
    zero = jnp.zeros((q.shape[0], LANES), BF16)
    pairs = [q[:, j * LANES:(j + 1) * LANES] for j in range(GQA_GROUP)]
    return jnp.concatenate([jnp.where(low, p, zero) for p in pairs] + [jnp.where(low, zero, p) for p in pairs],
                           axis=0)


def _merge_groups(o, m):
    low = _low_half(m)
    half = GQA_GROUP * m
    return jnp.concatenate([jnp.where(low, o[j * m:(j + 1) * m], o[half + j * m:half + (j + 1) * m])
                            for j in range(GQA_GROUP)], axis=-1)


def _odd_prompt_kernel(uc_ref, q_ref, k_ref, v_ref, pw_ref, ps_ref, fc_ref, fd_ref):
    seqs = [slice(b * SEQ, (b + 1) * SEQ) for b in range(SEQ_PER_TILE)]
    scores = [_dot_t(_split_groups(q_ref[rows, :]), k_ref[rows, :]) for rows in seqs]
    for s, rows in zip(scores, seqs):
        o = _softmax_pv([s], [_with_ones(v_ref[rows, :])])
        fd_ref[rows, :] = _merge_groups(o, SEQ).astype(BF16)
    for rows in seqs:
        fc_ref[rows, :] = _pool_mix(uc_ref[rows, :], pw_ref, ps_ref, SEQ).astype(BF16)


def _odd_prompt(uc, q, k, v, pool_w, pool_scale, idx):
    seq = lambda w: pl.BlockSpec((TM, w), lambda t: (t, 0))
    return pl.pallas_call(
        _odd_prompt_kernel,
        grid=(PROMPT_TILES,),
        in_specs=[seq(C_POOL), seq(HD_Q), seq(HD_KV), seq(HD_KV),
                  pl.BlockSpec((None, N_POOL, POOL_C, POOL_C), lambda t: (idx, 0, 0, 0)),
                  pl.BlockSpec((None, 1, C_POOL), lambda t: (idx, 0, 0))],
        out_specs=[seq(C_POOL), seq(HD_Q)],
        out_shape=[
            jax.ShapeDtypeStruct((N_TOK, C_POOL), BF16),
            jax.ShapeDtypeStruct((N_TOK, HD_Q), BF16),
        ],
        compiler_params=_params(1),
        name="odd_prompt",
    )(uc, q, k, v, pool_w, pool_scale)


def _odd_sample_kernel(*refs, n_cast):
    uc_ref, q_ref, k_ref, v_ref, ck_ref, cv_ref, pw_ref, ps_ref = refs[0:8]
    fc_ref, fd_ref = refs[10 + n_cast:12 + n_cast]
    kt_ref, vext_ref, ckpair_ref, cvext_ref = refs[12 + 2 * n_cast:]
    _cast_weights(refs[10:10 + n_cast], refs[12 + n_cast:12 + 2 * n_cast])

    @pl.when(pl.program_id(1) == 0)
    def _():
        fc_ref[...] = _pool_mix(uc_ref[...], pw_ref, ps_ref, DEC_SEQ).astype(BF16)
        kt_ref[...] = k_ref[...].astype(F32).T.astype(BF16)
        vext_ref[...] = _with_ones(v_ref[...])
        ckpair_ref[...] = _pair_heads_t(ck_ref).astype(BF16)
        cvext_ref[...] = _with_ones(_pair_heads_t(cv_ref).T.astype(BF16))

    k_t = kt_ref[...]
    ck_t = ckpair_ref[...]
    low = _low_half(Q_BLK)
    scores = []
    for j in range(GQA_GROUP):
        q2 = _split_pair(q_ref[:, j * LANES:(j + 1) * LANES])
        scores.append([_dot(q2, k_t), _dot(q2, ck_t)])
    outs = []
    for s in scores:
        o = _softmax_pv(s, [vext_ref[...], cvext_ref[...]])
        outs.append(jnp.where(low, o[0:Q_BLK], o[Q_BLK:2 * Q_BLK]))
    fd_ref[...] = jnp.concatenate(outs, axis=-1).astype(BF16)


def _odd_sample(uc, q, k, v, ctx_k, ctx_v, pool_w, pool_scale, fc, fd, idx, ffn_w1, ffn_w2, cast_jobs):
    first = N_PROMPT // DEC_SEQ
    n_qb = DEC_SEQ // Q_BLK
    first_q = N_PROMPT // Q_BLK
    whole_in = lambda w: pl.BlockSpec((DEC_SEQ, w), lambda b, i: (first + b, 0))
    ctx = pl.BlockSpec((None, None, HKV_D, HEAD_DIM, PAST_LEN), lambda b, i: (b, idx, 0, 0, 0))
    cast_in, cast_out, cast_shape = _cast_specs(cast_jobs, DEC_BATCH * n_qb, lambda b, i: b * n_qb + i)
    return pl.pallas_call(
        functools.partial(_odd_sample_kernel, n_cast=len(cast_in)),
        grid=(DEC_BATCH, n_qb),
        in_specs=[whole_in(C_POOL),
                  pl.BlockSpec((Q_BLK, HD_Q), lambda b, i: (first_q + b * n_qb + i, 0)),
                  whole_in(HD_KV), whole_in(HD_KV), ctx, ctx,
                  pl.BlockSpec((None, N_POOL, POOL_C, POOL_C), lambda b, i: (idx, 0, 0, 0)),
                  pl.BlockSpec((None, 1, C_POOL), lambda b, i: (idx, 0, 0)),
                  pl.BlockSpec(memory_space=pl.ANY), pl.BlockSpec(memory_space=pl.ANY)] + cast_in,
        out_specs=[pl.BlockSpec((DEC_SEQ, C_POOL), lambda b, i: (first + b, 0)),
                   pl.BlockSpec((Q_BLK, HD_Q), lambda b, i: (first_q + b * n_qb + i, 0))] + cast_out,
        out_shape=[
            jax.ShapeDtypeStruct((N_TOK, C_POOL), BF16),
            jax.ShapeDtypeStruct((N_TOK, HD_Q), BF16),
        ] + cast_shape,
        input_output_aliases={8: 0, 9: 1},
        scratch_shapes=[pltpu.VMEM((LANES, DEC_SEQ), BF16), pltpu.VMEM((DEC_SEQ, 2 * LANES), BF16),
                        pltpu.VMEM((LANES, PAST_LEN), BF16), pltpu.VMEM((PAST_LEN, 2 * LANES), BF16)],
        compiler_params=_params(2),
        name="odd_sample",
    )(uc, q, k, v, ctx_k, ctx_v, pool_w, pool_scale, fc, fd, *([ffn_w1, ffn_w2] * len(cast_jobs)))


def _rope_tables():
    t = np.arange(DEC_SEQ)
    n_freq = HEAD_DIM // 4
    inv = ROPE_BASE ** (-np.arange(n_freq, dtype=np.float64) / n_freq)
    ang = np.concatenate([(t // GRID_W)[:, None] * inv, (t % GRID_W)[:, None] * inv], axis=-1)
    cos = np.repeat(np.cos(ang), 2, axis=-1)
    sin = np.repeat(np.sin(ang), 2, axis=-1) * np.tile([-1.0, 1.0], HEAD_DIM // 2)
    n_heads = HQ_D + HKV_D
    cos = np.concatenate([np.ones((TM, QK_W)), np.tile(cos, (1, n_heads))], axis=0)
    sin = np.concatenate([np.zeros((TM, QK_W)), np.tile(sin, (1, n_heads))], axis=0)
    return jnp.asarray(cos, F32), jnp.asarray(sin, F32)


def _head_ones():
    head = np.arange(MXU_DIM) // HEAD_DIM
    return jnp.asarray(head[:, None] == head[None, :], BF16)


def kernel(x_prompt, x_sample, cache_a_k, cache_a_v, cache_d_k, cache_d_v, c, c_ctx, mod_w, mod_b, norm_w,
           ffn_w1, ffn_w2, ev_w_in, ev_rpb, ev_conv_w, ev_conv_b, ev_w_out, od_w_in, od_pool_w,
           od_pool_scale, od_q_norm, od_k_norm, od_w_out):
    xs = (x_prompt.reshape(N_PROMPT, D_MODEL), x_sample.reshape(N_SAMPLE, D_MODEL))
    cond = jnp.concatenate([c_ctx[None, :], c, jnp.zeros((COND_PAD - N_COND, D_MODEL), F32)], axis=0)
    mods, w1_a, w2_a = _modulation(cond, mod_w, mod_b, ffn_w1, ffn_w2)
    mods = mods.reshape(DEPTH, COND_PAD, N_MOD, D_MODEL)
    cache_a_k, cache_a_v, cache_d_k, cache_d_v = (jnp.swapaxes(t, -1, -2)
                                                  for t in (cache_a_k, cache_a_v, cache_d_k, cache_d_v))
    rpb = jnp.pad(ev_rpb, ((0, 0), (0, 0), (0, 0), (0, LANES - ev_rpb.shape[-1])))
    conv_b = ev_conv_b[:, None, :]
    pool_w, pool_scale = od_pool_w.astype(BF16), od_pool_scale[:, None, :]
    cos_t, sin_t = _rope_tables()
    ones_bd = _head_ones()
    states = []
    for l in range(DEPTH):
        i = l // 2
        last = l == DEPTH - 1
        if l == 0:
            x, ev_w_in_b = _ffn(xs, mods, norm_w, w1_a, w2_a, l, 0, ev_w_in=ev_w_in)
        else:
            x, = _ffn(xs, mods, norm_w, w1_a, w2_a, l, 0)
        cast_jobs = [(l, 1)] + ([] if last else [(l + 1, 0)])
        if l % 2 == 0:
            q, k, v, bg, z, s_k, s_v = _even_in(x, mods, norm_w, ev_w_in_b, l)
            fa, fb = _even_prompt(q, k, v, bg, z, ev_conv_w, conv_b, i)
            fa, fb, *cast = _even_sample(q, k, v, bg, z, cache_a_k, cache_a_v, rpb, ev_conv_w, conv_b, fa, fb, i,
                                         ffn_w1, ffn_w2, cast_jobs, mixer_w=(ev_w_out, od_w_in, od_w_out))
            ev_w_out_b, od_w_in_b, od_w_out_b = cast[2 * len(cast_jobs):]
            w_out = ev_w_out_b
        else:
            gain = jnp.concatenate([jnp.tile(od_q_norm[i], HQ_D), jnp.tile(od_k_norm[i], HKV_D)])[None, :]
            uc, q, k, v, s_k, s_v = _odd_in(x, mods, norm_w, od_w_in_b, gain, cos_t, sin_t, ones_bd, l)
            fa, fb = _odd_prompt(uc, q, k, v, pool_w, pool_scale, i)
            fa, fb, *cast = _odd_sample(uc, q, k, v, cache_d_k, cache_d_v, pool_w, pool_scale, fa, fb, i,
                                        ffn_w1, ffn_w2, cast_jobs)
            w_out = od_w_out_b
        w1_b, w2_b = cast[0:2]
        if not last:
            w1_a, w2_a = cast[2:4]
        states.append((jnp.swapaxes(s_k, -1, -2), jnp.swapaxes(s_v, -1, -2)))
        xs = tuple(_ffn((x,), mods, norm_w, w1_b, w2_b, l, 2, feats=(fa, fb), w_out=w_out,
                        split_out=last))
    y_prompt = xs[0].reshape(BATCH, SEQ, D_MODEL)
    y_sample = xs[1].reshape(DEC_BATCH, DEC_SEQ, D_MODEL)
    return (y_prompt, y_sample, states[0][0], states[0][1], states[1][0], states[1][1])
```

```python
import functools

import jax
import jax.numpy as jnp
import numpy as np
from jax import lax
from jax.experimental import pallas as pl
from jax.experimental.pallas import tpu as pltpu

D_MODEL = 1024
BATCH = 32
SEQ = 256
DEPTH = 2
DEC_BATCH = 2
DEC_SEQ = 2048
PAST_LEN = 256
GRID_W = 64
HEAD_DIM = 64
N_MOD = 9
N_NORM = 6
D_FF = 2816
FFN_RES = 0.5
H_A = 8
NA_ROWS = 8
NA_COLS = 16
C_B = 512
CONV_W = 3
C_POOL = 512
POOL_WINDOWS = (2, 4, 8, 16)
N_POOL = 4
POOL_C = C_POOL // N_POOL
HQ_D = 8
HKV_D = 2
GQA_GROUP = HQ_D // HKV_D
ROPE_BASE = 10000.0
EVEN_IN = 3 * H_A * HEAD_DIM + 3 * C_B
ODD_IN = C_POOL + (HQ_D + 2 * HKV_D) * HEAD_DIM
RMS_EPS = 1e-6
NEG_INF = -1e30
ATT_SCALE = HEAD_DIM ** -0.5

LANES = 128
GRID_ROWS = DEC_SEQ // GRID_W
N_PROMPT = BATCH * SEQ
N_SAMPLE = DEC_BATCH * DEC_SEQ
N_TOK = N_PROMPT + N_SAMPLE
N_COND = 1 + DEC_BATCH
COND_PAD = 8
HD_A = H_A * HEAD_DIM
HD_Q = HQ_D * HEAD_DIM
HD_KV = HKV_D * HEAD_DIM
QK_W = HD_Q + HD_KV
POOL_PAD = 8

TM = 1024
N_TILES = N_TOK // TM
PROMPT_TILES = N_PROMPT // TM
TILES_PER_SAMPLE = DEC_SEQ // TM
SEQ_PER_TILE = TM // SEQ
TM_FFN = 1024
FFN_HALF = TM_FFN // 2
MIX_CAST_STEPS = N_PROMPT // TM_FFN
MIX_CAST_ROWS = D_MODEL // MIX_CAST_STEPS
assert MIX_CAST_ROWS == 2 * HEAD_DIM and (DEPTH + 1) // 2 == 1 and DEPTH // 2 == 1
MXU_DIM = 256
FF_CHUNKS = tuple((lo, min(lo + 3 * MXU_DIM, D_FF)) for lo in range(0, D_FF, 3 * MXU_DIM))
MOD_TN = 1152
Q_BLK = 256
ROW_UNROLL = 8
VMEM_LIMIT = 60 * 1024 * 1024

F32 = jnp.float32
BF16 = jnp.bfloat16


def _params(n_grid):
    return pltpu.CompilerParams(dimension_semantics=("arbitrary",) * n_grid, vmem_limit_bytes=VMEM_LIMIT)


def _cond_of_tile(i, tm=TM):
    n_prompt = N_PROMPT // tm
    return jnp.where(i < n_prompt, 0, 1 + (i - n_prompt) // (DEC_SEQ // tm))


def _prompt_tile(i, tm=TM):
    return jnp.minimum(i, N_PROMPT // tm - 1)


def _sample_tile(i, tm=TM):
    return jnp.maximum(i - N_PROMPT // tm, 0)


def _rms(x, g):
    return x * lax.rsqrt(jnp.mean(x * x, axis=-1, keepdims=True) + RMS_EPS) * g


def _sigmoid(x):
    return 1.0 / (1.0 + jnp.exp(-x))


def _dot(a, b):
    return jnp.dot(a, b, preferred_element_type=F32)


def _dot_t(a, b):
    return lax.dot_general(a, b, (((1,), (1,)), ((), ())), preferred_element_type=F32)


def _mod_kernel(c_ref, w_ref, b_ref, w1f_ref, w2f_ref, o_ref, w1b_ref, w2b_ref):
    _cast_weights([w1f_ref, w2f_ref], [w1b_ref, w2b_ref])
    c = c_ref[...]
    sc = (c * _sigmoid(c)).astype(BF16)
    o_ref[...] = _dot(sc, w_ref[...].astype(BF16)) + b_ref[...]


def _modulation(cond, mod_w, mod_b, ffn_w1, ffn_w2):
    n_col = N_MOD * D_MODEL
    n_blk = n_col // MOD_TN
    cast_in, cast_out, cast_shape = _cast_specs([(0, 0)], DEPTH * n_blk, lambda l, j: l * n_blk + j)
    return pl.pallas_call(
        _mod_kernel,
        grid=(DEPTH, n_blk),
        in_specs=[
            pl.BlockSpec((COND_PAD, D_MODEL), lambda l, j: (0, 0)),
            pl.BlockSpec((None, D_MODEL, MOD_TN), lambda l, j: (l, 0, j)),
            pl.BlockSpec((None, 1, MOD_TN), lambda l, j: (l, 0, j)),
        ] + cast_in,
        out_specs=[pl.BlockSpec((None, COND_PAD, MOD_TN), lambda l, j: (l, 0, j))] + cast_out,
        out_shape=[jax.ShapeDtypeStruct((DEPTH, COND_PAD, n_col), F32)] + cast_shape,
        compiler_params=_params(2),
        name="modulation",
    )(cond, mod_w, mod_b.reshape(DEPTH, 1, n_col), ffn_w1, ffn_w2)


def _mod_spec(layer, tm=TM):
    return pl.BlockSpec((None, None, N_MOD, D_MODEL), lambda i: (layer, _cond_of_tile(i, tm), 0, 0))


def _gain_spec(layer):
    return pl.BlockSpec((None, N_NORM, D_MODEL), lambda i: (layer, 0, 0))


def _cast_specs(jobs, n_steps, step_of):
    r1, r2 = D_MODEL // n_steps, D_FF // n_steps
    in_specs, out_specs, out_shape = [], [], []
    for layer, which in jobs:
        in_specs += [pl.BlockSpec((None, None, r1, 2 * D_FF), lambda *g, lw=(layer, which): (*lw, step_of(*g), 0)),
                     pl.BlockSpec((None, None, r2, D_MODEL), lambda *g, lw=(layer, which): (*lw, step_of(*g), 0))]
        out_specs += [pl.BlockSpec((r1, 2 * D_FF), lambda *g: (step_of(*g), 0)),
                      pl.BlockSpec((r2, D_MODEL), lambda *g: (step_of(*g), 0))]
        out_shape += [jax.ShapeDtypeStruct((D_MODEL, 2 * D_FF), BF16), jax.ShapeDtypeStruct((D_FF, D_MODEL), BF16)]
    return in_specs, out_specs, out_shape


def _cast_weights(f32_refs, bf16_refs):
    for src, dst in zip(f32_refs, bf16_refs):
        dst[...] = src[...].astype(BF16)


def _ev_in_cast_specs():
    step = lambda i: jnp.minimum(i, MIX_CAST_STEPS - 1)
    return ([pl.BlockSpec((None, MIX_CAST_ROWS, EVEN_IN), lambda i: (0, step(i), 0))],
            [pl.BlockSpec((MIX_CAST_ROWS, EVEN_IN), lambda i: (step(i), 0))],
            [jax.ShapeDtypeStruct((D_MODEL, EVEN_IN), BF16)])


def _mixer_cast_specs(step_of):
    rows = MIX_CAST_ROWS
    n_pool_blk = C_POOL // rows

    def od_out_block(s, g):
        return jnp.where(s < n_pool_blk, 2 * s + g, C_POOL // HEAD_DIM + g * GQA_GROUP + s - n_pool_blk)

    widths = (D_MODEL, ODD_IN, D_MODEL)
    in_specs = [pl.BlockSpec((None, rows, D_MODEL), lambda *g: (0, step_of(*g), 0)),
                pl.BlockSpec((None, rows, ODD_IN), lambda *g: (0, step_of(*g), 0)),
                pl.BlockSpec((None, HEAD_DIM, D_MODEL), lambda *g: (0, od_out_block(step_of(*g), 0), 0)),
                pl.BlockSpec((None, HEAD_DIM, D_MODEL), lambda *g: (0, od_out_block(step_of(*g), 1), 0))]
    out_specs = [pl.BlockSpec((rows, w), lambda *g: (step_of(*g), 0)) for w in widths]
    out_shape = [jax.ShapeDtypeStruct((D_MODEL, w), BF16) for w in widths]
    return in_specs, out_specs, out_shape


def _cast_mixer_weights(ev_out_f, od_in_f, od_out_lo_f, od_out_hi_f, ev_out_b, od_in_b, od_out_b):
    ev_out_b[...] = ev_out_f[...].astype(BF16)
    w = od_in_f[...]
    q = w[:, C_POOL:C_POOL + HD_Q]
    heads = [q[:, (g * GQA_GROUP + j) * HEAD_DIM:(g * GQA_GROUP + j + 1) * HEAD_DIM]
             for j in range(GQA_GROUP) for g in range(HKV_D)]
    od_in_b[...] = jnp.concatenate([w[:, 0:C_POOL]] + heads + [w[:, C_POOL + HD_Q:]], axis=1).astype(BF16)
    od_out_b[...] = jnp.concatenate([od_out_lo_f[...], od_out_hi_f[...]], axis=0).astype(BF16)


def _ffn_kernel(*refs, sub, split_in, mix, split_out, cast_ev_in):
    refs = list(refs)
    x_refs = [refs.pop(0) for _ in range(2 if split_in else 1)]
    feat_refs = [refs.pop(0) for _ in range(2 if mix else 0)]
    m_ref, g_ref = refs.pop(0), refs.pop(0)
    wo_ref = refs.pop(0) if mix else None
    w1_ref, w2_ref = refs.pop(0), refs.pop(0)
    if cast_ev_in:
        @pl.when(pl.program_id(0) < MIX_CAST_STEPS)
        def _():
            refs[-1][...] = refs[0][...].astype(BF16)
        refs = refs[1:-1]
    out_refs = refs
    is_prompt = pl.program_id(0) < N_PROMPT // TM_FFN
    shift = m_ref[3 * sub:3 * sub + 1, :]
    scale = m_ref[3 * sub + 1:3 * sub + 2, :]
    gate = m_ref[3 * sub + 2:3 * sub + 3, :]
    halves = [slice(hf * FFN_HALF, (hf + 1) * FFN_HALF) for hf in range(TM_FFN // FFN_HALF)]
    mixed, ys = [], []
    if mix:
        for rows in halves:
            feat = jnp.concatenate([feat_refs[0][rows, :], feat_refs[1][rows, :]], axis=1)
            mixed.append(_dot(feat, wo_ref[...]))
    for half, rows in enumerate(halves):
        if split_in:
            x = jnp.where(is_prompt, x_refs[0][rows, :], x_refs[1][rows, :])
        else:
            x = x_refs[0][rows, :]
        if mix:
            x = x + m_ref[5:6, :] * _rms(mixed[half], g_ref[3:4, :])
        h = (_rms(x, g_ref[2 * sub:2 * sub + 1, :]) * (1.0 + scale) + shift).astype(BF16)
        acc = jnp.zeros((FFN_HALF, D_MODEL), F32)
        for lo, hi in FF_CHUNKS:
            a = _dot(h, w1_ref[:, lo:hi])
            u = _dot(h, w1_ref[:, D_FF + lo:D_FF + hi])
            act = (a * _sigmoid(a) * u).astype(BF16)
            acc = acc + _dot(act, w2_ref[lo:hi, :])
        y = x + FFN_RES * gate * _rms(acc, g_ref[2 * sub + 1:2 * sub + 2, :])
        if split_out:
            ys.append(y)
        else:
            out_refs[0][rows, :] = y
    if split_out:
        @pl.when(is_prompt)
        def _():
            for rows, y in zip(halves, ys):
                out_refs[0][rows, :] = y

        @pl.when(jnp.logical_not(is_prompt))
        def _():
            for rows, y in zip(halves, ys):
                out_refs[1][rows, :] = y


def _ffn(xs, mods, gains, w1, w2, layer, sub, feats=None, w_out=None, split_out=False, ev_w_in=None):
    split_in = len(xs) == 2
    mix = feats is not None
    cast_ev_in = ev_w_in is not None
    resident = pl.Buffered(1)
    half_w = D_MODEL // 2
    feat_specs, mix_w_spec = [], []
    if mix:
        feat_specs = [pl.BlockSpec((TM_FFN, half_w), lambda i: (i, 0))] * 2
        mix_w_spec = [pl.BlockSpec((D_MODEL, D_MODEL), lambda i: (0, 0), pipeline_mode=resident)]
    cast_in, cast_out, cast_shape, cast_args = [], [], [], ()
    if cast_ev_in:
        cast_in, cast_out, cast_shape = _ev_in_cast_specs()
        cast_args = (ev_w_in,)
    tok = pl.BlockSpec((TM_FFN, D_MODEL), lambda i: (i, 0))
    prompt_tok = pl.BlockSpec((TM_FFN, D_MODEL), lambda i: (_prompt_tile(i, TM_FFN), 0))
    sample_tok = pl.BlockSpec((TM_FFN, D_MODEL), lambda i: (_sample_tile(i, TM_FFN), 0))
    if split_out:
        out_specs = [prompt_tok, sample_tok]
        out_shape = [jax.ShapeDtypeStruct((N_PROMPT, D_MODEL), F32),
                     jax.ShapeDtypeStruct((N_SAMPLE, D_MODEL), F32)]
    else:
        out_specs = [tok]
        out_shape = [jax.ShapeDtypeStruct((N_TOK, D_MODEL), F32)]
    return pl.pallas_call(
        functools.partial(_ffn_kernel, sub=sub, split_in=split_in, mix=mix, split_out=split_out,
                          cast_ev_in=cast_ev_in),
        grid=(N_TOK // TM_FFN,),
        in_specs=([prompt_tok, sample_tok] if split_in else [tok]) + feat_specs + [
            _mod_spec(layer, TM_FFN),
            _gain_spec(layer),
        ] + mix_w_spec + [
            pl.BlockSpec((D_MODEL, 2 * D_FF), lambda i: (0, 0), pipeline_mode=resident),
            pl.BlockSpec((D_FF, D_MODEL), lambda i: (0, 0), pipeline_mode=resident),
        ] + cast_in,
        out_specs=out_specs + cast_out,
        out_shape=out_shape + cast_shape,
        compiler_params=_params(1),
        name=f"ffn{sub}",
    )(*xs, *(feats or ()), mods, gains, *((w_out,) if mix else ()), w1, w2, *cast_args)


def _mixer_norm(x, m_ref, g_ref):
    return (_rms(x, g_ref[2:3, :]) * (1.0 + m_ref[4:5, :]) + m_ref[3:4, :]).astype(BF16)


def _store_heads(state_ref, b, x, n_heads):
    xt = x.T
    for h in range(n_heads):
        state_ref[b, 0, h] = xt[h * HEAD_DIM:(h + 1) * HEAD_DIM, :]


def _even_in_kernel(x_ref, m_ref, g_ref, w_ref, q_ref, k_ref, v_ref, bg_ref, z_ref, sk_ref, sv_ref):
    kv = []
    for b in range(SEQ_PER_TILE):
        rows = slice(b * SEQ, (b + 1) * SEQ)
        u = _dot(_mixer_norm(x_ref[rows, :], m_ref, g_ref), w_ref[...])
        k = u[:, HD_A:2 * HD_A]
        v = u[:, 2 * HD_A:3 * HD_A]
        q_ref[rows, :] = (u[:, 0:HD_A] * ATT_SCALE).astype(BF16)
        k_ref[rows, :] = k.astype(BF16)
        v_ref[rows, :] = v.astype(BF16)
        bg_ref[rows, :] = u[:, 3 * HD_A:3 * HD_A + C_B]
        z_ref[rows, :] = u[:, 3 * HD_A + C_B:3 * HD_A + 2 * C_B] * u[:, 3 * HD_A + 2 * C_B:3 * HD_A + 3 * C_B]
        kv.append((k, v))

    @pl.when(pl.program_id(0) < PROMPT_TILES)
    def _():
        for b, (k, v) in enumerate(kv):
            _store_heads(sk_ref, b, k, H_A)
            _store_heads(sv_ref, b, v, H_A)


def _state_spec(n_heads):
    return pl.BlockSpec((SEQ_PER_TILE, 1, n_heads, HEAD_DIM, SEQ), lambda i: (_prompt_tile(i), 0, 0, 0, 0))


def _even_in(x, mods, gains, w_in, layer):
    tok = lambda w: pl.BlockSpec((TM, w), lambda i: (i, 0))
    state = jax.ShapeDtypeStruct((BATCH, 1, H_A, HEAD_DIM, SEQ), F32)
    return pl.pallas_call(
        _even_in_kernel,
        grid=(N_TILES,),
        in_specs=[
            tok(D_MODEL), _mod_spec(layer), _gain_spec(layer),
            pl.BlockSpec((D_MODEL, EVEN_IN), lambda i: (0, 0)),
        ],
        out_specs=[tok(HD_A), tok(HD_A), tok(HD_A), tok(C_B), tok(C_B), _state_spec(H_A), _state_spec(H_A)],
        out_shape=[
            jax.ShapeDtypeStruct((N_TOK, HD_A), BF16),
            jax.ShapeDtypeStruct((N_TOK, HD_A), BF16),
            jax.ShapeDtypeStruct((N_TOK, HD_A), BF16),
            jax.ShapeDtypeStruct((N_TOK, C_B), F32),
            jax.ShapeDtypeStruct((N_TOK, C_B), F32),
            state, state,
        ],
        compiler_params=_params(1),
        name="even_in",
    )(x, mods, gains, w_in)


def _swap_pairs(x):
    n = x.shape[-1]
    lane = lax.broadcasted_iota(jnp.int32, x.shape, x.ndim - 1)
    return jnp.where(lane % 2 == 0, pltpu.roll(x, n - 1, x.ndim - 1), pltpu.roll(x, 1, x.ndim - 1))


def _odd_in_kernel(x_ref, m_ref, g_ref, w_ref, ng_ref, cos_ref, sin_ref, ones_ref,
                   uc_ref, q_ref, k_ref, v_ref, sk_ref, sv_ref):
    ones = ones_ref[...]
    kv = []
    for b in range(SEQ_PER_TILE):
        rows = slice(b * SEQ, (b + 1) * SEQ)
        u = _dot(_mixer_norm(x_ref[rows, :], m_ref, g_ref), w_ref[...])
        uc_ref[rows, :] = u[:, 0:C_POOL]
        qk = u[:, C_POOL:C_POOL + QK_W]
        sq = qk * qk
        hi = sq.astype(BF16)
        lo = (sq - hi.astype(F32)).astype(BF16)
        sums = []
        for c0 in range(0, QK_W, MXU_DIM):
            c1 = min(c0 + MXU_DIM, QK_W)
            tile = ones[0:c1 - c0, 0:c1 - c0]
            sums.append(_dot(hi[:, c0:c1], tile) + _dot(lo[:, c0:c1], tile))
        ms = jnp.concatenate(sums, axis=-1) * (1.0 / HEAD_DIM)
        n = qk * lax.rsqrt(ms + RMS_EPS) * ng_ref[...]
        r = n * cos_ref[rows, :] + _swap_pairs(n) * sin_ref[rows, :]
        k = r[:, HD_Q:QK_W]
        v = u[:, C_POOL + QK_W:ODD_IN]
        q_ref[rows, :] = (r[:, 0:HD_Q] * ATT_SCALE).astype(BF16)
        k_ref[rows, :] = k.astype(BF16)
        v_ref[rows, :] = v.astype(BF16)
        kv.append((k, v))

    @pl.when(pl.program_id(0) < PROMPT_TILES)
    def _():
        for b, (k, v) in enumerate(kv):
            _store_heads(sk_ref, b, k, HKV_D)
            _store_heads(sv_ref, b, v, HKV_D)


def _rope_tile_index(i):
    return jnp.where(i < PROMPT_TILES, 0, 1 + (i - PROMPT_TILES) % TILES_PER_SAMPLE)


def _odd_in(x, mods, gains, w_in, qk_gain, cos_t, sin_t, ones_bd, layer):
    tok = lambda w: pl.BlockSpec((TM, w), lambda i: (i, 0))
    state = jax.ShapeDtypeStruct((BATCH, 1, HKV_D, HEAD_DIM, SEQ), F32)
    return pl.pallas_call(
        _odd_in_kernel,
        grid=(N_TILES,),
        in_specs=[
            tok(D_MODEL), _mod_spec(layer), _gain_spec(layer),
            pl.BlockSpec((D_MODEL, ODD_IN), lambda i: (0, 0)),
            pl.BlockSpec((1, QK_W), lambda i: (0, 0)),
            pl.BlockSpec((TM, QK_W), lambda i: (_rope_tile_index(i), 0)),
            pl.BlockSpec((TM, QK_W), lambda i: (_rope_tile_index(i), 0)),
            pl.BlockSpec((MXU_DIM, MXU_DIM), lambda i: (0, 0)),
        ],
        out_specs=[tok(C_POOL), tok(HD_Q), tok(HD_KV), tok(HD_KV), _state_spec(HKV_D), _state_spec(HKV_D)],
        out_shape=[
            jax.ShapeDtypeStruct((N_TOK, C_POOL), F32),
            jax.ShapeDtypeStruct((N_TOK, HD_Q), BF16),
            jax.ShapeDtypeStruct((N_TOK, HD_KV), BF16),
            jax.ShapeDtypeStruct((N_TOK, HD_KV), BF16),
            state, state,
        ],
        compiler_params=_params(1),
        name="odd_in",
    )(x, mods, gains, w_in, qk_gain, cos_t, sin_t, ones_bd)


def _low_half(rows):
    return lax.broadcasted_iota(jnp.int32, (rows, LANES), 1) < HEAD_DIM


def _split_pair(qp):
    low = _low_half(qp.shape[0])
    zero = jnp.zeros_like(qp)
    return jnp.concatenate([jnp.where(low, qp, zero), jnp.where(low, zero, qp)], axis=0)


def _pair_heads_t(ref):
    return jnp.concatenate([ref[0], ref[1]], axis=0)


def _with_ones(v):
    return jnp.concatenate([v, jnp.ones_like(v)], axis=1)


def _softmax_pv(scores, values):
    m = scores[0].max(axis=-1, keepdims=True)
    for s in scores[1:]:
        m = jnp.maximum(m, s.max(axis=-1, keepdims=True))
    acc = None
    for s, v in zip(scores, values):
        o = _dot(jnp.exp(s - m).astype(BF16), v)
        acc = o if acc is None else acc + o
    return acc[:, 0:LANES] / acc[:, LANES:2 * LANES]


def _merge_pair(o):
    m = o.shape[0] // 2
    return jnp.where(_low_half(m), o[0:m], o[m:2 * m])


def _short_conv(z, cw_ref, cb_ref, seq_len):
    rows = z.shape[0]
    pos = lax.broadcasted_iota(jnp.int32, z.shape, 0) & (seq_len - 1)
    z_prev = jnp.where(pos == 0, 0.0, pltpu.roll(z, 1, 0))
    z_next = jnp.where(pos == seq_len - 1, 0.0, pltpu.roll(z, rows - 1, 0))
    y = z_prev * cw_ref[0:1, :]
    y = y + z * cw_ref[1:2, :]
    y = y + z_next * cw_ref[2:3, :]
    return y + cb_ref[...]


def _pool_mix(uc, pw_ref, ps_ref, seq_len):
    pos = lax.broadcasted_iota(jnp.int32, (seq_len, POOL_C), 0)
    pad = jnp.zeros((POOL_PAD, POOL_C), F32)
    n_ext = seq_len + 2 * POOL_PAD
    outs = []
    for gi, win in enumerate(POOL_WINDOWS):
        ug = uc[:, gi * POOL_C:(gi + 1) * POOL_C]
        run = jnp.concatenate([pad, ug, pad], axis=0)
        span = 1
        while span < win:
            run = run + pltpu.roll(run, span, 0)
            span *= 2
        back = win // 2 - 1
        if back:
            run = pltpu.roll(run, n_ext - back, 0)
        wsum = run[POOL_PAD:POOL_PAD + seq_len, :]
        cnt = jnp.minimum(pos + (win - win // 2), seq_len) - jnp.maximum(pos - win // 2, 0)
        pooled = (wsum / cnt.astype(F32) - ug).astype(BF16)
        outs.append(_dot(pooled, pw_ref[gi]))
    return jnp.concatenate(outs, axis=-1) * ps_ref[...]


def _even_prompt_kernel(q_ref, k_ref, v_ref, bg_ref, z_ref, cw_ref, cb_ref, fa_ref, fb_ref):
    for b in range(SEQ_PER_TILE):
        rows = slice(b * SEQ, (b + 1) * SEQ)
        pairs = [slice(p * LANES, (p + 1) * LANES) for p in range(H_A // 2)]
        scores = [_dot_t(_split_pair(q_ref[rows, sl]), k_ref[rows, sl]) for sl in pairs]
        outs = [_merge_pair(_softmax_pv([s], [_with_ones(v_ref[rows, sl])])) for s, sl in zip(scores, pairs)]
        fa_ref[rows, :] = jnp.concatenate(outs, axis=-1).astype(BF16)
    fb_ref[...] = (bg_ref[...] * _short_conv(z_ref[...], cw_ref, cb_ref, SEQ)).astype(BF16)


def _even_prompt(q, k, v, bg, z, conv_w, conv_b, idx):
    tile = lambda w: pl.BlockSpec((TM, w), lambda t: (t, 0))
    return pl.pallas_call(
        _even_prompt_kernel,
        grid=(PROMPT_TILES,),
        in_specs=[tile(HD_A), tile(HD_A), tile(HD_A), tile(C_B), tile(C_B),
                  pl.BlockSpec((None, CONV_W, C_B), lambda t: (idx, 0, 0)),
                  pl.BlockSpec((None, 1, C_B), lambda t: (idx, 0, 0))],
        out_specs=[tile(HD_A), tile(C_B)],
        out_shape=[
            jax.ShapeDtypeStruct((N_TOK, HD_A), BF16),
            jax.ShapeDtypeStruct((N_TOK, C_B), BF16),
        ],
        compiler_params=_params(1),
        name="even_prompt",
    )(q, k, v, bg, z, conv_w, conv_b)


def _na_bias_tiles(rpb_h):
    c = lax.broadcasted_iota(jnp.int32, (GRID_W, GRID_W), 0)
    kc = lax.broadcasted_iota(jnp.int32, (GRID_W, GRID_W), 1)
    cs = jnp.clip(c - NA_COLS // 2, 0, GRID_W - NA_COLS)
    valid = (kc >= cs) & (kc < cs + NA_COLS)
    tiles = []
    for ro in range(2 * NA_ROWS - 1):
        g = jnp.broadcast_to(rpb_h[ro:ro + 1, :], (GRID_W, LANES))
        skew = pltpu.roll(g, LANES - (NA_COLS - 1), 1, stride=1, stride_axis=0)
        tiles.append(jnp.where(valid, skew[:, 0:GRID_W], NEG_INF))
    return tiles


def _even_sample_kernel(*refs, n_cast, cast_mixers):
    n_mix_in, n_mix_out = (4, 3) if cast_mixers else (0, 0)
    q_ref, k_ref, v_ref, bg_ref, z_ref, ck_ref, cv_ref, rpb_ref, cw_ref, cb_ref = refs[0:10]
    cast0 = 12
    out0 = cast0 + n_cast + n_mix_in
    fa_ref, fb_ref = refs[out0:out0 + 2]
    bias_ref = refs[out0 + 2 + n_cast + n_mix_out]
    _cast_weights(refs[cast0:cast0 + n_cast], refs[out0 + 2:out0 + 2 + n_cast])
    if cast_mixers:
        _cast_mixer_weights(*refs[cast0 + n_cast:out0], *refs[out0 + 2 + n_cast:out0 + 2 + n_cast + n_mix_out])
    win = NA_ROWS * GRID_W
    for h in range(2):
        tiles = _na_bias_tiles(rpb_ref[h])
        for var in range(NA_ROWS):
            bias_ref[var, h * GRID_W:(h + 1) * GRID_W, :] = jnp.concatenate(
                [tiles[i - var + NA_ROWS - 1] for i in range(NA_ROWS)], axis=1)
    ck_t = _pair_heads_t(ck_ref).astype(BF16)
    cv_ext = _with_ones(_pair_heads_t(cv_ref).T.astype(BF16))

    def rows(it, carry):
        staged = []
        for j in range(ROW_UNROLL):
            r = it * ROW_UNROLL + j
            rs = jnp.clip(r - NA_ROWS // 2, 0, GRID_ROWS - NA_ROWS)
            q0 = pl.multiple_of(r * GRID_W, GRID_W)
            k0 = pl.multiple_of(rs * GRID_W, GRID_W)
            q2 = _split_pair(q_ref[pl.ds(q0, GRID_W), :])
            s_loc = _dot_t(q2, k_ref[pl.ds(k0, win), :]) + bias_ref[r - rs]
            s_ctx = _dot(q2, ck_t)
            staged.append((q0, k0, s_loc, s_ctx))
        for q0, k0, s_loc, s_ctx in staged:
            o = _softmax_pv([s_loc, s_ctx], [_with_ones(v_ref[pl.ds(k0, win), :]), cv_ext])
            fa_ref[pl.ds(q0, GRID_W), :] = _merge_pair(o).astype(BF16)
        return carry

    lax.fori_loop(0, GRID_ROWS // ROW_UNROLL, rows, 0)
    fb_ref[...] = (bg_ref[...] * _short_conv(z_ref[...], cw_ref, cb_ref, DEC_SEQ)).astype(BF16)


def _even_sample(q, k, v, bg, z, ctx_k, ctx_v, rpb, conv_w, conv_b, fa, fb, idx, ffn_w1, ffn_w2, cast_jobs,
                 mixer_w=None):
    first = N_PROMPT // DEC_SEQ
    n_pair = H_A // 2
    n_steps = DEC_BATCH * n_pair
    step_of = lambda b, p: b * n_pair + p
    seq_in = pl.BlockSpec((DEC_SEQ, LANES), lambda b, p: (first + b, p))
    anywhere = pl.BlockSpec(memory_space=pl.ANY)
    ctx = pl.BlockSpec((None, None, 2, HEAD_DIM, PAST_LEN), lambda b, p: (b, idx, p, 0, 0))
    cast_in, cast_out, cast_shape = _cast_specs(cast_jobs, n_steps, step_of)
    n_cast = len(cast_in)
    cast_args = [ffn_w1, ffn_w2] * len(cast_jobs)
    if mixer_w is not None:
        assert n_steps == MIX_CAST_STEPS
        mix_in, mix_out, mix_shape = _mixer_cast_specs(step_of)
        cast_in, cast_out, cast_shape = cast_in + mix_in, cast_out + mix_out, cast_shape + mix_shape
        cast_args += [mixer_w[0], mixer_w[1], mixer_w[2], mixer_w[2]]
    return pl.pallas_call(
        functools.partial(_even_sample_kernel, n_cast=n_cast, cast_mixers=mixer_w is not None),
        grid=(DEC_BATCH, n_pair),
        in_specs=[seq_in, seq_in, seq_in, seq_in, seq_in, ctx, ctx,
                  pl.BlockSpec((None, 2, 2 * NA_ROWS - 1, LANES), lambda b, p: (idx, p, 0, 0)),
                  pl.BlockSpec((None, CONV_W, LANES), lambda b, p: (idx, 0, p)),
                  pl.BlockSpec((None, 1, LANES), lambda b, p: (idx, 0, p)), anywhere, anywhere] + cast_in,
        out_specs=[seq_in, seq_in] + cast_out,
        out_shape=[
            jax.ShapeDtypeStruct((N_TOK, HD_A), BF16),
            jax.ShapeDtypeStruct((N_TOK, C_B), BF16),
        ] + cast_shape,
        input_output_aliases={10: 0, 11: 1},
        scratch_shapes=[pltpu.VMEM((NA_ROWS, 2 * GRID_W, NA_ROWS * GRID_W), F32)],
        compiler_params=_params(2),
        name="even_sample",
    )(q, k, v, bg, z, ctx_k, ctx_v, rpb, conv_w, conv_b, fa, fb, *cast_args)


def _split_groups(q):
    low = _low_half(q.shape[0])
    zero = jnp.zeros((q.shape[0], LANES), BF16)
    pairs = [q[:, j * LANES:(j + 1) * LANES] for j in range(GQA_GROUP)]
    return jnp.concatenate([jnp.where(low, p, zero) for p in pairs] + [jnp.where(low, zero, p) for p in pairs],
                           axis=0)


def _merge_groups(o, m):
    low = _low_half(m)
    half = GQA_GROUP * m
    return jnp.concatenate([jnp.where(low, o[j * m:(j + 1) * m], o[half + j * m:half + (j + 1) * m])
                            for j in range(GQA_GROUP)], axis=-1)


def _odd_prompt_kernel(uc_ref, q_ref, k_ref, v_ref, pw_ref, ps_ref, fc_ref, fd_ref):
    seqs = [slice(b * SEQ, (b + 1) * SEQ) for b in range(SEQ_PER_TILE)]
    scores = [_dot_t(_split_groups(q_ref[rows, :]), k_ref[rows, :]) for rows in seqs]
    for s, rows in zip(scores, seqs):
        o = _softmax_pv([s], [_with_ones(v_ref[rows, :])])
        fd_ref[rows, :] = _merge_groups(o, SEQ).astype(BF16)
    for rows in seqs:
        fc_ref[rows, :] = _pool_mix(uc_ref[rows, :], pw_ref, ps_ref, SEQ).astype(BF16)


def _odd_prompt(uc, q, k, v, pool_w, pool_scale, idx):
    seq = lambda w: pl.BlockSpec((TM, w), lambda t: (t, 0))
    return pl.pallas_call(
        _odd_prompt_kernel,
        grid=(PROMPT_TILES,),
        in_specs=[seq(C_POOL), seq(HD_Q), seq(HD_KV), seq(HD_KV),
                  pl.BlockSpec((None, N_POOL, POOL_C, POOL_C), lambda t: (idx, 0, 0, 0)),
                  pl.BlockSpec((None, 1, C_POOL), lambda t: (idx, 0, 0))],
        out_specs=[seq(C_POOL), seq(HD_Q)],
        out_shape=[
            jax.ShapeDtypeStruct((N_TOK, C_POOL), BF16),
            jax.ShapeDtypeStruct((N_TOK, HD_Q), BF16),
        ],
        compiler_params=_params(1),
        name="odd_prompt",
    )(uc, q, k, v, pool_w, pool_scale)


def _odd_sample_kernel(*refs, n_cast):
    uc_ref, q_ref, k_ref, v_ref, ck_ref, cv_ref, pw_ref, ps_ref = refs[0:8]
    fc_ref, fd_ref = refs[10 + n_cast:12 + n_cast]
    kt_ref, vext_ref, ckpair_ref, cvext_ref = refs[12 + 2 * n_cast:]
    _cast_weights(refs[10:10 + n_cast], refs[12 + n_cast:12 + 2 * n_cast])

    @pl.when(pl.program_id(1) == 0)
    def _():
        fc_ref[...] = _pool_mix(uc_ref[...], pw_ref, ps_ref, DEC_SEQ).astype(BF16)
        kt_ref[...] = k_ref[...].astype(F32).T.astype(BF16)
        vext_ref[...] = _with_ones(v_ref[...])
        ckpair_ref[...] = _pair_heads_t(ck_ref).astype(BF16)
        cvext_ref[...] = _with_ones(_pair_heads_t(cv_ref).T.astype(BF16))

    k_t = kt_ref[...]
    ck_t = ckpair_ref[...]
    low = _low_half(Q_BLK)
    scores = []
    for j in range(GQA_GROUP):
        q2 = _split_pair(q_ref[:, j * LANES:(j + 1) * LANES])
        scores.append([_dot(q2, k_t), _dot(q2, ck_t)])
    outs = []
    for s in scores:
        o = _softmax_pv(s, [vext_ref[...], cvext_ref[...]])
        outs.append(jnp.where(low, o[0:Q_BLK], o[Q_BLK:2 * Q_BLK]))
    fd_ref[...] = jnp.concatenate(outs, axis=-1).astype(BF16)


def _odd_sample(uc, q, k, v, ctx_k, ctx_v, pool_w, pool_scale, fc, fd, idx, ffn_w1, ffn_w2, cast_jobs):
    first = N_PROMPT // DEC_SEQ
    n_qb = DEC_SEQ // Q_BLK
    first_q = N_PROMPT // Q_BLK
    whole_in = lambda w: pl.BlockSpec((DEC_SEQ, w), lambda b, i: (first + b, 0))
    ctx = pl.BlockSpec((None, None, HKV_D, HEAD_DIM, PAST_LEN), lambda b, i: (b, idx, 0, 0, 0))
    cast_in, cast_out, cast_shape = _cast_specs(cast_jobs, DEC_BATCH * n_qb, lambda b, i: b * n_qb + i)
    return pl.pallas_call(
        functools.partial(_odd_sample_kernel, n_cast=len(cast_in)),
        grid=(DEC_BATCH, n_qb),
        in_specs=[whole_in(C_POOL),
                  pl.BlockSpec((Q_BLK, HD_Q), lambda b, i: (first_q + b * n_qb + i, 0)),
                  whole_in(HD_KV), whole_in(HD_KV), ctx, ctx,
                  pl.BlockSpec((None, N_POOL, POOL_C, POOL_C), lambda b, i: (idx, 0, 0, 0)),
                  pl.BlockSpec((None, 1, C_POOL), lambda b, i: (idx, 0, 0)),
                  pl.BlockSpec(memory_space=pl.ANY), pl.BlockSpec(memory_space=pl.ANY)] + cast_in,
        out_specs=[pl.BlockSpec((DEC_SEQ, C_POOL), lambda b, i: (first + b, 0)),
                   pl.BlockSpec((Q_BLK, HD_Q), lambda b, i: (first_q + b * n_qb + i, 0))] + cast_out,
        out_shape=[
            jax.ShapeDtypeStruct((N_TOK, C_POOL), BF16),
            jax.ShapeDtypeStruct((N_TOK, HD_Q), BF16),
        ] + cast_shape,
        input_output_aliases={8: 0, 9: 1},
        scratch_shapes=[pltpu.VMEM((LANES, DEC_SEQ), BF16), pltpu.VMEM((DEC_SEQ, 2 * LANES), BF16),
                        pltpu.VMEM((LANES, PAST_LEN), BF16), pltpu.VMEM((PAST_LEN, 2 * LANES), BF16)],
        compiler_params=_params(2),
        name="odd_sample",
    )(uc, q, k, v, ctx_k, ctx_v, pool_w, pool_scale, fc, fd, *([ffn_w1, ffn_w2] * len(cast_jobs)))


def _rope_tables():
    t = np.arange(DEC_SEQ)
    n_freq = HEAD_DIM // 4
    inv = ROPE_BASE ** (-np.arange(n_freq, dtype=np.float64) / n_freq)
    ang = np.concatenate([(t // GRID_W)[:, None] * inv, (t % GRID_W)[:, None] * inv], axis=-1)
    cos = np.repeat(np.cos(ang), 2, axis=-1)
    sin = np.repeat(np.sin(ang), 2, axis=-1) * np.tile([-1.0, 1.0], HEAD_DIM // 2)
    n_heads = HQ_D + HKV_D
    cos = np.concatenate([np.ones((TM, QK_W)), np.tile(cos, (1, n_heads))], axis=0)
    sin = np.concatenate([np.zeros((TM, QK_W)), np.tile(sin, (1, n_heads))], axis=0)
    return jnp.asarray(cos, F32), jnp.asarray(sin, F32)


def _head_ones():
    head = np.arange(MXU_DIM) // HEAD_DIM
    return jnp.asarray(head[:, None] == head[None, :], BF16)


def kernel(x_prompt, x_sample, cache_a_k, cache_a_v, cache_d_k, cache_d_v, c, c_ctx, mod_w, mod_b, norm_w,
           ffn_w1, ffn_w2, ev_w_in, ev_rpb, ev_conv_w, ev_conv_b, ev_w_out, od_w_in, od_pool_w,
           od_pool_scale, od_q_norm, od_k_norm, od_w_out):
    xs = (x_prompt.reshape(N_PROMPT, D_MODEL), x_sample.reshape(N_SAMPLE, D_MODEL))
    cond = jnp.concatenate([c_ctx[None, :], c, jnp.zeros((COND_PAD - N_COND, D_MODEL), F32)], axis=0)
    mods, w1_a, w2_a = _modulation(cond, mod_w, mod_b, ffn_w1, ffn_w2)
    mods = mods.reshape(DEPTH, COND_PAD, N_MOD, D_MODEL)
    cache_a_k, cache_a_v, cache_d_k, cache_d_v = (jnp.swapaxes(t, -1, -2)
                                                  for t in (cache_a_k, cache_a_v, cache_d_k, cache_d_v))
    rpb = jnp.pad(ev_rpb, ((0, 0), (0, 0), (0, 0), (0, LANES - ev_rpb.shape[-1])))
    conv_b = ev_conv_b[:, None, :]
    pool_w, pool_scale = od_pool_w.astype(BF16), od_pool_scale[:, None, :]
    cos_t, sin_t = _rope_tables()
    ones_bd = _head_ones()
    states = []
    for l in range(DEPTH):
        i = l // 2
        last = l == DEPTH - 1
        if l == 0:
            x, ev_w_in_b = _ffn(xs, mods, norm_w, w1_a, w2_a, l, 0, ev_w_in=ev_w_in)
        else:
            x, = _ffn(xs, mods, norm_w, w1_a, w2_a, l, 0)
        cast_jobs = [(l, 1)] + ([] if last else [(l + 1, 0)])
        if l % 2 == 0:
            q, k, v, bg, z, s_k, s_v = _even_in(x, mods, norm_w, ev_w_in_b, l)
            fa, fb = _even_prompt(q, k, v, bg, z, ev_conv_w, conv_b, i)
            fa, fb, *cast = _even_sample(q, k, v, bg, z, cache_a_k, cache_a_v, rpb, ev_conv_w, conv_b, fa, fb, i,
                                         ffn_w1, ffn_w2, cast_jobs, mixer_w=(ev_w_out, od_w_in, od_w_out))
            ev_w_out_b, od_w_in_b, od_w_out_b = cast[2 * len(cast_jobs):]
            w_out = ev_w_out_b
        else:
            gain = jnp.concatenate([jnp.tile(od_q_norm[i], HQ_D), jnp.tile(od_k_norm[i], HKV_D)])[None, :]
            uc, q, k, v, s_k, s_v = _odd_in(x, mods, norm_w, od_w_in_b, gain, cos_t, sin_t, ones_bd, l)
            fa, fb = _odd_prompt(uc, q, k, v, pool_w, pool_scale, i)
            fa, fb, *cast = _odd_sample(uc, q, k, v, cache_d_k, cache_d_v, pool_w, pool_scale, fa, fb, i,
                                        ffn_w1, ffn_w2, cast_jobs)
            w_out = od_w_out_b
        w1_b, w2_b = cast[0:2]
        if not last:
            w1_a, w2_a = cast[2:4]
        states.append((jnp.swapaxes(s_k, -1, -2), jnp.swapaxes(s_v, -1, -2)))
        xs = tuple(_ffn((x,), mods, norm_w, w1_b, w2_b, l, 2, feats=(fa, fb), w_out=w_out,
                        split_out=last))
    y_prompt = xs[0].reshape(BATCH, SEQ, D_MODEL)
    y_sample = xs[1].reshape(DEC_BATCH, DEC_SEQ, D_MODEL)
    return (y_prompt, y_sample, states[0][0], states[0][1], states[1][0], states[1][1])
```

```python
import functools

import jax
import jax.numpy as jnp
import numpy as np
from jax import lax
from jax.experimental import pallas as pl
from jax.experimental.pallas import tpu as pltpu

D_MODEL = 1024
BATCH = 32
SEQ = 256
DEPTH = 2
DEC_BATCH = 2
DEC_SEQ = 2048
PAST_LEN = 256
GRID_W = 64
HEAD_DIM = 64
N_MOD = 9
N_NORM = 6
D_FF = 2816
FFN_RES = 0.5
H_A = 8
NA_ROWS = 8
NA_COLS = 16
C_B = 512
CONV_W = 3
C_POOL = 512
POOL_WINDOWS = (2, 4, 8, 16)
N_POOL = 4
POOL_C = C_POOL // N_POOL
HQ_D = 8
HKV_D = 2
GQA_GROUP = HQ_D // HKV_D
ROPE_BASE = 10000.0
EVEN_IN = 3 * H_A * HEAD_DIM + 3 * C_B
ODD_IN = C_POOL + (HQ_D + 2 * HKV_D) * HEAD_DIM
RMS_EPS = 1e-6
NEG_INF = -1e30
ATT_SCALE = HEAD_DIM ** -0.5

LANES = 128
GRID_ROWS = DEC_SEQ // GRID_W
N_PROMPT = BATCH * SEQ
N_SAMPLE = DEC_BATCH * DEC_SEQ
N_TOK = N_PROMPT + N_SAMPLE
N_COND = 1 + DEC_BATCH
COND_PAD = 8
HD_A = H_A * HEAD_DIM
HD_Q = HQ_D * HEAD_DIM
HD_KV = HKV_D * HEAD_DIM
QK_W = HD_Q + HD_KV
POOL_PAD = 8

TM = 1024
N_TILES = N_TOK // TM
PROMPT_TILES = N_PROMPT // TM
TILES_PER_SAMPLE = DEC_SEQ // TM
SEQ_PER_TILE = TM // SEQ
TM_FFN = 1024
FFN_HALF = TM_FFN // 2
MIX_CAST_STEPS = N_PROMPT // TM_FFN
MIX_CAST_ROWS = D_MODEL // MIX_CAST_STEPS
assert MIX_CAST_ROWS == 2 * HEAD_DIM and (DEPTH + 1) // 2 == 1 and DEPTH // 2 == 1
MXU_DIM = 256
FF_CHUNKS = tuple((lo, min(lo + 3 * MXU_DIM, D_FF)) for lo in range(0, D_FF, 3 * MXU_DIM))
MOD_TN = 1152
Q_BLK = 256
ROW_UNROLL = 8
VMEM_LIMIT = 60 * 1024 * 1024

F32 = jnp.float32
BF16 = jnp.bfloat16


def _params(n_grid):
    return pltpu.CompilerParams(dimension_semantics=("arbitrary",) * n_grid, vmem_limit_bytes=VMEM_LIMIT)


def _cond_of_tile(i, tm=TM):
    n_prompt = N_PROMPT // tm
    return jnp.where(i < n_prompt, 0, 1 + (i - n_prompt) // (DEC_SEQ // tm))


def _prompt_tile(i, tm=TM):
    return jnp.minimum(i, N_PROMPT // tm - 1)


def _sample_tile(i, tm=TM):
    return jnp.maximum(i - N_PROMPT // tm, 0)


def _rms(x, g):
    return x * lax.rsqrt(jnp.mean(x * x, axis=-1, keepdims=True) + RMS_EPS) * g


def _sigmoid(x):
    return 1.0 / (1.0 + jnp.exp(-x))


def _dot(a, b):
    return jnp.dot(a, b, preferred_element_type=F32)


def _dot_t(a, b):
    return lax.dot_general(a, b, (((1,), (1,)), ((), ())), preferred_element_type=F32)


def _mod_kernel(c_ref, w_ref, b_ref, w1f_ref, w2f_ref, o_ref, w1b_ref, w2b_ref):
    _cast_weights([w1f_ref, w2f_ref], [w1b_ref, w2b_ref])
    c = c_ref[...]
    sc = (c * _sigmoid(c)).astype(BF16)
    o_ref[...] = _dot(sc, w_ref[...].astype(BF16)) + b_ref[...]


def _modulation(cond, mod_w, mod_b, ffn_w1, ffn_w2):
    n_col = N_MOD * D_MODEL
    n_blk = n_col // MOD_TN
    cast_in, cast_out, cast_shape = _cast_specs([(0, 0)], DEPTH * n_blk, lambda l, j: l * n_blk + j)
    return pl.pallas_call(
        _mod_kernel,
        grid=(DEPTH, n_blk),
        in_specs=[
            pl.BlockSpec((COND_PAD, D_MODEL), lambda l, j: (0, 0)),
            pl.BlockSpec((None, D_MODEL, MOD_TN), lambda l, j: (l, 0, j)),
            pl.BlockSpec((None, 1, MOD_TN), lambda l, j: (l, 0, j)),
        ] + cast_in,
        out_specs=[pl.BlockSpec((None, COND_PAD, MOD_TN), lambda l, j: (l, 0, j))] + cast_out,
        out_shape=[jax.ShapeDtypeStruct((DEPTH, COND_PAD, n_col), F32)] + cast_shape,
        compiler_params=_params(2),
        name="modulation",
    )(cond, mod_w, mod_b.reshape(DEPTH, 1, n_col), ffn_w1, ffn_w2)


def _mod_spec(layer, tm=TM):
    return pl.BlockSpec((None, None, N_MOD, D_MODEL), lambda i: (layer, _cond_of_tile(i, tm), 0, 0))


def _gain_spec(layer):
    return pl.BlockSpec((None, N_NORM, D_MODEL), lambda i: (layer, 0, 0))


def _cast_specs(jobs, n_steps, step_of):
    r1, r2 = D_MODEL // n_steps, D_FF // n_steps
    in_specs, out_specs, out_shape = [], [], []
    for layer, which in jobs:
        in_specs += [pl.BlockSpec((None, None, r1, 2 * D_FF), lambda *g, lw=(layer, which): (*lw, step_of(*g), 0)),
                     pl.BlockSpec((None, None, r2, D_MODEL), lambda *g, lw=(layer, which): (*lw, step_of(*g), 0))]
        out_specs += [pl.BlockSpec((r1, 2 * D_FF), lambda *g: (step_of(*g), 0)),
                      pl.BlockSpec((r2, D_MODEL), lambda *g: (step_of(*g), 0))]
        out_shape += [jax.ShapeDtypeStruct((D_MODEL, 2 * D_FF), BF16), jax.ShapeDtypeStruct((D_FF, D_MODEL), BF16)]
    return in_specs, out_specs, out_shape


def _cast_weights(f32_refs, bf16_refs):
    for src, dst in zip(f32_refs, bf16_refs):
        dst[...] = src[...].astype(BF16)


def _ev_in_cast_specs():
    step = lambda i: jnp.minimum(i, MIX_CAST_STEPS - 1)
    return ([pl.BlockSpec((None, MIX_CAST_ROWS, EVEN_IN), lambda i: (0, step(i), 0))],
            [pl.BlockSpec((MIX_CAST_ROWS, EVEN_IN), lambda i: (step(i), 0))],
            [jax.ShapeDtypeStruct((D_MODEL, EVEN_IN), BF16)])


def _mixer_cast_specs(step_of):
    rows = MIX_CAST_ROWS
    n_pool_blk = C_POOL // rows

    def od_out_block(s, g):
        return jnp.where(s < n_pool_blk, 2 * s + g, C_POOL // HEAD_DIM + g * GQA_GROUP + s - n_pool_blk)

    widths = (D_MODEL, ODD_IN, D_MODEL)
    in_specs = [pl.BlockSpec((None, rows, D_MODEL), lambda *g: (0, step_of(*g), 0)),
                pl.BlockSpec((None, rows, ODD_IN), lambda *g: (0, step_of(*g), 0)),
                pl.BlockSpec((None, HEAD_DIM, D_MODEL), lambda *g: (0, od_out_block(step_of(*g), 0), 0)),
                pl.BlockSpec((None, HEAD_DIM, D_MODEL), lambda *g: (0, od_out_block(step_of(*g), 1), 0))]
    out_specs = [pl.BlockSpec((rows, w), lambda *g: (step_of(*g), 0)) for w in widths]
    out_shape = [jax.ShapeDtypeStruct((D_MODEL, w), BF16) for w in widths]
    return in_specs, out_specs, out_shape


def _cast_mixer_weights(ev_out_f, od_in_f, od_out_lo_f, od_out_hi_f, ev_out_b, od_in_b, od_out_b):
    ev_out_b[...] = ev_out_f[...].astype(BF16)
    w = od_in_f[...]
    q = w[:, C_POOL:C_POOL + HD_Q]
    heads = [q[:, (g * GQA_GROUP + j) * HEAD_DIM:(g * GQA_GROUP + j + 1) * HEAD_DIM]
             for j in range(GQA_GROUP) for g in range(HKV_D)]
    od_in_b[...] = jnp.concatenate([w[:, 0:C_POOL]] + heads + [w[:, C_POOL + HD_Q:]], axis=1).astype(BF16)
    od_out_b[...] = jnp.concatenate([od_out_lo_f[...], od_out_hi_f[...]], axis=0).astype(BF16)


def _ffn_kernel(*refs, sub, split_in, mix, split_out, cast_ev_in):
    refs = list(refs)
    x_refs = [refs.pop(0) for _ in range(2 if split_in else 1)]
    feat_refs = [refs.pop(0) for _ in range(2 if mix else 0)]
    m_ref, g_ref = refs.pop(0), refs.pop(0)
    wo_ref = refs.pop(0) if mix else None
    w1_ref, w2_ref = refs.pop(0), refs.pop(0)
    if cast_ev_in:
        @pl.when(pl.program_id(0) < MIX_CAST_STEPS)
        def _():
            refs[-1][...] = refs[0][...].astype(BF16)
        refs = refs[1:-1]
    out_refs = refs
    is_prompt = pl.program_id(0) < N_PROMPT // TM_FFN
    shift = m_ref[3 * sub:3 * sub + 1, :]
    scale = m_ref[3 * sub + 1:3 * sub + 2, :]
    gate = m_ref[3 * sub + 2:3 * sub + 3, :]
    halves = [slice(hf * FFN_HALF, (hf + 1) * FFN_HALF) for hf in range(TM_FFN // FFN_HALF)]
    mixed, ys = [], []
    if mix:
        for rows in halves:
            feat = jnp.concatenate([feat_refs[0][rows, :], feat_refs[1][rows, :]], axis=1)
            mixed.append(_dot(feat, wo_ref[...]))
    for half, rows in enumerate(halves):
        if split_in:
            x = jnp.where(is_prompt, x_refs[0][rows, :], x_refs[1][rows, :])
        else:
            x = x_refs[0][rows, :]
        if mix:
            x = x + m_ref[5:6, :] * _rms(mixed[half], g_ref[3:4, :])
        h = (_rms(x, g_ref[2 * sub:2 * sub + 1, :]) * (1.0 + scale) + shift).astype(BF16)
        acc = jnp.zeros((FFN_HALF, D_MODEL), F32)
        for lo, hi in FF_CHUNKS:
            a = _dot(h, w1_ref[:, lo:hi])
            u = _dot(h, w1_ref[:, D_FF + lo:D_FF + hi])
            act = (a * _sigmoid(a) * u).astype(BF16)
            acc = acc + _dot(act, w2_ref[lo:hi, :])
        y = x + FFN_RES * gate * _rms(acc, g_ref[2 * sub + 1:2 * sub + 2, :])
        if split_out:
            ys.append(y)
        else:
            out_refs[0][rows, :] = y
    if split_out:
        @pl.when(is_prompt)
        def _():
            for rows, y in zip(halves, ys):
                out_refs[0][rows, :] = y

        @pl.when(jnp.logical_not(is_prompt))
        def _():
            for rows, y in zip(halves, ys):
                out_refs[1][rows, :] = y


def _ffn(xs, mods, gains, w1, w2, layer, sub, feats=None, w_out=None, split_out=False, ev_w_in=None):
    split_in = len(xs) == 2
    mix = feats is not None
    cast_ev_in = ev_w_in is not None
    resident = pl.Buffered(1)
    half_w = D_MODEL // 2
    feat_specs, mix_w_spec = [], []
    if mix:
        feat_specs = [pl.BlockSpec((TM_FFN, half_w), lambda i: (i, 0))] * 2
        mix_w_spec = [pl.BlockSpec((D_MODEL, D_MODEL), lambda i: (0, 0), pipeline_mode=resident)]
    cast_in, cast_out, cast_shape, cast_args = [], [], [], ()
    if cast_ev_in:
        cast_in, cast_out, cast_shape = _ev_in_cast_specs()
        cast_args = (ev_w_in,)
    tok = pl.BlockSpec((TM_FFN, D_MODEL), lambda i: (i, 0))
    prompt_tok = pl.BlockSpec((TM_FFN, D_MODEL), lambda i: (_prompt_tile(i, TM_FFN), 0))
    sample_tok = pl.BlockSpec((TM_FFN, D_MODEL), lambda i: (_sample_tile(i, TM_FFN), 0))
    if split_out:
        out_specs = [prompt_tok, sample_tok]
        out_shape = [jax.ShapeDtypeStruct((N_PROMPT, D_MODEL), F32),
                     jax.ShapeDtypeStruct((N_SAMPLE, D_MODEL), F32)]
    else:
        out_specs = [tok]
        out_shape = [jax.ShapeDtypeStruct((N_TOK, D_MODEL), F32)]
    return pl.pallas_call(
        functools.partial(_ffn_kernel, sub=sub, split_in=split_in, mix=mix, split_out=split_out,
                          cast_ev_in=cast_ev_in),
        grid=(N_TOK // TM_FFN,),
        in_specs=([prompt_tok, sample_tok] if split_in else [tok]) + feat_specs + [
            _mod_spec(layer, TM_FFN),
            _gain_spec(layer),
        ] + mix_w_spec + [
            pl.BlockSpec((D_MODEL, 2 * D_FF), lambda i: (0, 0), pipeline_mode=resident),
            pl.BlockSpec((D_FF, D_MODEL), lambda i: (0, 0), pipeline_mode=resident),
        ] + cast_in,
        out_specs=out_specs + cast_out,
        out_shape=out_shape + cast_shape,
        input_output_aliases={} if (split_in or split_out) else {0: 0},
        compiler_params=_params(1),
        name=f"ffn{sub}",
    )(*xs, *(feats or ()), mods, gains, *((w_out,) if mix else ()), w1, w2, *cast_args)


def _mixer_norm(x, m_ref, g_ref):
    return (_rms(x, g_ref[2:3, :]) * (1.0 + m_ref[4:5, :]) + m_ref[3:4, :]).astype(BF16)


def _store_heads(state_ref, b, x, n_heads):
    xt = x.T
    for h in range(n_heads):
        state_ref[b, 0, h] = xt[h * HEAD_DIM:(h + 1) * HEAD_DIM, :]


def _even_in_kernel(x_ref, m_ref, g_ref, w_ref, q_ref, k_ref, v_ref, bg_ref, z_ref, sk_ref, sv_ref):
    kv = []
    for b in range(SEQ_PER_TILE):
        rows = slice(b * SEQ, (b + 1) * SEQ)
        u = _dot(_mixer_norm(x_ref[rows, :], m_ref, g_ref), w_ref[...])
        k = u[:, HD_A:2 * HD_A]
        v = u[:, 2 * HD_A:3 * HD_A]
        q_ref[rows, :] = (u[:, 0:HD_A] * ATT_SCALE).astype(BF16)
        k_ref[rows, :] = k.astype(BF16)
        v_ref[rows, :] = v.astype(BF16)
        bg_ref[rows, :] = u[:, 3 * HD_A:3 * HD_A + C_B]
        z_ref[rows, :] = u[:, 3 * HD_A + C_B:3 * HD_A + 2 * C_B] * u[:, 3 * HD_A + 2 * C_B:3 * HD_A + 3 * C_B]
        kv.append((k, v))

    @pl.when(pl.program_id(0) < PROMPT_TILES)
    def _():
        for b, (k, v) in enumerate(kv):
            _store_heads(sk_ref, b, k, H_A)
            _store_heads(sv_ref, b, v, H_A)


def _state_spec(n_heads):
    return pl.BlockSpec((SEQ_PER_TILE, 1, n_heads, HEAD_DIM, SEQ), lambda i: (_prompt_tile(i), 0, 0, 0, 0))


def _even_in(x, mods, gains, w_in, layer):
    tok = lambda w: pl.BlockSpec((TM, w), lambda i: (i, 0))
    state = jax.ShapeDtypeStruct((BATCH, 1, H_A, HEAD_DIM, SEQ), F32)
    return pl.pallas_call(
        _even_in_kernel,
        grid=(N_TILES,),
        in_specs=[
            tok(D_MODEL), _mod_spec(layer), _gain_spec(layer),
            pl.BlockSpec((D_MODEL, EVEN_IN), lambda i: (0, 0)),
        ],
        out_specs=[tok(HD_A), tok(HD_A), tok(HD_A), tok(C_B), tok(C_B), _state_spec(H_A), _state_spec(H_A)],
        out_shape=[
            jax.ShapeDtypeStruct((N_TOK, HD_A), BF16),
            jax.ShapeDtypeStruct((N_TOK, HD_A), BF16),
            jax.ShapeDtypeStruct((N_TOK, HD_A), BF16),
            jax.ShapeDtypeStruct((N_TOK, C_B), F32),
            jax.ShapeDtypeStruct((N_TOK, C_B), F32),
            state, state,
        ],
        compiler_params=_params(1),
        name="even_in",
    )(x, mods, gains, w_in)


def _swap_pairs(x):
    n = x.shape[-1]
    lane = lax.broadcasted_iota(jnp.int32, x.shape, x.ndim - 1)
    return jnp.where(lane % 2 == 0, pltpu.roll(x, n - 1, x.ndim - 1), pltpu.roll(x, 1, x.ndim - 1))


def _odd_in_kernel(x_ref, m_ref, g_ref, w_ref, ng_ref, cos_ref, sin_ref, ones_ref,
                   uc_ref, q_ref, k_ref, v_ref, sk_ref, sv_ref):
    ones = ones_ref[...]
    kv = []
    for b in range(SEQ_PER_TILE):
        rows = slice(b * SEQ, (b + 1) * SEQ)
        u = _dot(_mixer_norm(x_ref[rows, :], m_ref, g_ref), w_ref[...])
        uc_ref[rows, :] = u[:, 0:C_POOL]
        qk = u[:, C_POOL:C_POOL + QK_W]
        sq = qk * qk
        hi = sq.astype(BF16)
        lo = (sq - hi.astype(F32)).astype(BF16)
        sums = []
        for c0 in range(0, QK_W, MXU_DIM):
            c1 = min(c0 + MXU_DIM, QK_W)
            tile = ones[0:c1 - c0, 0:c1 - c0]
            sums.append(_dot(hi[:, c0:c1], tile) + _dot(lo[:, c0:c1], tile))
        ms = jnp.concatenate(sums, axis=-1) * (1.0 / HEAD_DIM)
        n = qk * lax.rsqrt(ms + RMS_EPS) * ng_ref[...]
        r = n * cos_ref[rows, :] + _swap_pairs(n) * sin_ref[rows, :]
        k = r[:, HD_Q:QK_W]
        v = u[:, C_POOL + QK_W:ODD_IN]
        q_ref[rows, :] = (r[:, 0:HD_Q] * ATT_SCALE).astype(BF16)
        k_ref[rows, :] = k.astype(BF16)
        v_ref[rows, :] = v.astype(BF16)
        kv.append((k, v))

    @pl.when(pl.program_id(0) < PROMPT_TILES)
    def _():
        for b, (k, v) in enumerate(kv):
            _store_heads(sk_ref, b, k, HKV_D)
            _store_heads(sv_ref, b, v, HKV_D)


def _rope_tile_index(i):
    return jnp.where(i < PROMPT_TILES, 0, 1 + (i - PROMPT_TILES) % TILES_PER_SAMPLE)


def _odd_in(x, mods, gains, w_in, qk_gain, cos_t, sin_t, ones_bd, layer):
    tok = lambda w: pl.BlockSpec((TM, w), lambda i: (i, 0))
    state = jax.ShapeDtypeStruct((BATCH, 1, HKV_D, HEAD_DIM, SEQ), F32)
    return pl.pallas_call(
        _odd_in_kernel,
        grid=(N_TILES,),
        in_specs=[
            tok(D_MODEL), _mod_spec(layer), _gain_spec(layer),
            pl.BlockSpec((D_MODEL, ODD_IN), lambda i: (0, 0)),
            pl.BlockSpec((1, QK_W), lambda i: (0, 0)),
            pl.BlockSpec((TM, QK_W), lambda i: (_rope_tile_index(i), 0)),
            pl.BlockSpec((TM, QK_W), lambda i: (_rope_tile_index(i), 0)),
            pl.BlockSpec((MXU_DIM, MXU_DIM), lambda i: (0, 0)),
        ],
        out_specs=[tok(C_POOL), tok(HD_Q), tok(HD_KV), tok(HD_KV), _state_spec(HKV_D), _state_spec(HKV_D)],
        out_shape=[
            jax.ShapeDtypeStruct((N_TOK, C_POOL), F32),
            jax.ShapeDtypeStruct((N_TOK, HD_Q), BF16),
            jax.ShapeDtypeStruct((N_TOK, HD_KV), BF16),
            jax.ShapeDtypeStruct((N_TOK, HD_KV), BF16),
            state, state,
        ],
        compiler_params=_params(1),
        name="odd_in",
    )(x, mods, gains, w_in, qk_gain, cos_t, sin_t, ones_bd)


def _low_half(rows):
    return lax.broadcasted_iota(jnp.int32, (rows, LANES), 1) < HEAD_DIM


def _split_pair(qp):
    low = _low_half(qp.shape[0])
    zero = jnp.zeros_like(qp)
    return jnp.concatenate([jnp.where(low, qp, zero), jnp.where(low, zero, qp)], axis=0)


def _pair_heads_t(ref):
    return jnp.concatenate([ref[0], ref[1]], axis=0)


def _with_ones(v):
    return jnp.concatenate([v, jnp.ones_like(v)], axis=1)


def _softmax_pv(scores, values):
    m = scores[0].max(axis=-1, keepdims=True)
    for s in scores[1:]:
        m = jnp.maximum(m, s.max(axis=-1, keepdims=True))
    acc = None
    for s, v in zip(scores, values):
        o = _dot(jnp.exp(s - m).astype(BF16), v)
        acc = o if acc is None else acc + o
    return acc[:, 0:LANES] / acc[:, LANES:2 * LANES]


def _merge_pair(o):
    m = o.shape[0] // 2
    return jnp.where(_low_half(m), o[0:m], o[m:2 * m])


def _short_conv(z, cw_ref, cb_ref, seq_len):
    rows = z.shape[0]
    pos = lax.broadcasted_iota(jnp.int32, z.shape, 0) & (seq_len - 1)
    z_prev = jnp.where(pos == 0, 0.0, pltpu.roll(z, 1, 0))
    z_next = jnp.where(pos == seq_len - 1, 0.0, pltpu.roll(z, rows - 1, 0))
    y = z_prev * cw_ref[0:1, :]
    y = y + z * cw_ref[1:2, :]
    y = y + z_next * cw_ref[2:3, :]
    return y + cb_ref[...]


def _pool_mix(uc, pw_ref, ps_ref, seq_len):
    pos = lax.broadcasted_iota(jnp.int32, (seq_len, POOL_C), 0)
    pad = jnp.zeros((POOL_PAD, POOL_C), F32)
    n_ext = seq_len + 2 * POOL_PAD
    outs = []
    for gi, win in enumerate(POOL_WINDOWS):
        ug = uc[:, gi * POOL_C:(gi + 1) * POOL_C]
        run = jnp.concatenate([pad, ug, pad], axis=0)
        span = 1
        while span < win:
            run = run + pltpu.roll(run, span, 0)
            span *= 2
        back = win // 2 - 1
        if back:
            run = pltpu.roll(run, n_ext - back, 0)
        wsum = run[POOL_PAD:POOL_PAD + seq_len, :]
        cnt = jnp.minimum(pos + (win - win // 2), seq_len) - jnp.maximum(pos - win // 2, 0)
        pooled = (wsum / cnt.astype(F32) - ug).astype(BF16)
        outs.append(_dot(pooled, pw_ref[gi]))
    return jnp.concatenate(outs, axis=-1) * ps_ref[...]


def _even_prompt_kernel(q_ref, k_ref, v_ref, bg_ref, z_ref, cw_ref, cb_ref, fa_ref, fb_ref):
    for b in range(SEQ_PER_TILE):
        rows = slice(b * SEQ, (b + 1) * SEQ)
        pairs = [slice(p * LANES, (p + 1) * LANES) for p in range(H_A // 2)]
        scores = [_dot_t(_split_pair(q_ref[rows, sl]), k_ref[rows, sl]) for sl in pairs]
        outs = [_merge_pair(_softmax_pv([s], [_with_ones(v_ref[rows, sl])])) for s, sl in zip(scores, pairs)]
        fa_ref[rows, :] = jnp.concatenate(outs, axis=-1).astype(BF16)
    fb_ref[...] = (bg_ref[...] * _short_conv(z_ref[...], cw_ref, cb_ref, SEQ)).astype(BF16)


def _even_prompt(q, k, v, bg, z, conv_w, conv_b, idx):
    tile = lambda w: pl.BlockSpec((TM, w), lambda t: (t, 0))
    return pl.pallas_call(
        _even_prompt_kernel,
        grid=(PROMPT_TILES,),
        in_specs=[tile(HD_A), tile(HD_A), tile(HD_A), tile(C_B), tile(C_B),
                  pl.BlockSpec((None, CONV_W, C_B), lambda t: (idx, 0, 0)),
                  pl.BlockSpec((None, 1, C_B), lambda t: (idx, 0, 0))],
        out_specs=[tile(HD_A), tile(C_B)],
        out_shape=[
            jax.ShapeDtypeStruct((N_TOK, HD_A), BF16),
            jax.ShapeDtypeStruct((N_TOK, C_B), BF16),
        ],
        compiler_params=_params(1),
        name="even_prompt",
    )(q, k, v, bg, z, conv_w, conv_b)


def _na_bias_tiles(rpb_h):
    c = lax.broadcasted_iota(jnp.int32, (GRID_W, GRID_W), 0)
    kc = lax.broadcasted_iota(jnp.int32, (GRID_W, GRID_W), 1)
    cs = jnp.clip(c - NA_COLS // 2, 0, GRID_W - NA_COLS)
    valid = (kc >= cs) & (kc < cs + NA_COLS)
    tiles = []
    for ro in range(2 * NA_ROWS - 1):
        g = jnp.broadcast_to(rpb_h[ro:ro + 1, :], (GRID_W, LANES))
        skew = pltpu.roll(g, LANES - (NA_COLS - 1), 1, stride=1, stride_axis=0)
        tiles.append(jnp.where(valid, skew[:, 0:GRID_W], NEG_INF))
    return tiles


def _even_sample_kernel(*refs, n_cast, cast_mixers):
    n_mix_in, n_mix_out = (4, 3) if cast_mixers else (0, 0)
    q_ref, k_ref, v_ref, bg_ref, z_ref, ck_ref, cv_ref, rpb_ref, cw_ref, cb_ref = refs[0:10]
    cast0 = 12
    out0 = cast0 + n_cast + n_mix_in
    fa_ref, fb_ref = refs[out0:out0 + 2]
    bias_ref = refs[out0 + 2 + n_cast + n_mix_out]
    _cast_weights(refs[cast0:cast0 + n_cast], refs[out0 + 2:out0 + 2 + n_cast])
    if cast_mixers:
        _cast_mixer_weights(*refs[cast0 + n_cast:out0], *refs[out0 + 2 + n_cast:out0 + 2 + n_cast + n_mix_out])
    win = NA_ROWS * GRID_W
    for h in range(2):
        tiles = _na_bias_tiles(rpb_ref[h])
        for var in range(NA_ROWS):
            bias_ref[var, h * GRID_W:(h + 1) * GRID_W, :] = jnp.concatenate(
                [tiles[i - var + NA_ROWS - 1] for i in range(NA_ROWS)], axis=1)
    ck_t = _pair_heads_t(ck_ref).astype(BF16)
    cv_ext = _with_ones(_pair_heads_t(cv_ref).T.astype(BF16))

    def rows(it, carry):
        staged = []
        for j in range(ROW_UNROLL):
            r = it * ROW_UNROLL + j
            rs = jnp.clip(r - NA_ROWS // 2, 0, GRID_ROWS - NA_ROWS)
            q0 = pl.multiple_of(r * GRID_W, GRID_W)
            k0 = pl.multiple_of(rs * GRID_W, GRID_W)
            q2 = _split_pair(q_ref[pl.ds(q0, GRID_W), :])
            s_loc = _dot_t(q2, k_ref[pl.ds(k0, win), :]) + bias_ref[r - rs]
            s_ctx = _dot(q2, ck_t)
            staged.append((q0, k0, s_loc, s_ctx))
        for q0, k0, s_loc, s_ctx in staged:
            o = _softmax_pv([s_loc, s_ctx], [_with_ones(v_ref[pl.ds(k0, win), :]), cv_ext])
            fa_ref[pl.ds(q0, GRID_W), :] = _merge_pair(o).astype(BF16)
        return carry

    lax.fori_loop(0, GRID_ROWS // ROW_UNROLL, rows, 0)
    fb_ref[...] = (bg_ref[...] * _short_conv(z_ref[...], cw_ref, cb_ref, DEC_SEQ)).astype(BF16)


def _even_sample(q, k, v, bg, z, ctx_k, ctx_v, rpb, conv_w, conv_b, fa, fb, idx, ffn_w1, ffn_w2, cast_jobs,
                 mixer_w=None):
    first = N_PROMPT // DEC_SEQ
    n_pair = H_A // 2
    n_steps = DEC_BATCH * n_pair
    step_of = lambda b, p: b * n_pair + p
    seq_in = pl.BlockSpec((DEC_SEQ, LANES), lambda b, p: (first + b, p))
    anywhere = pl.BlockSpec(memory_space=pl.ANY)
    ctx = pl.BlockSpec((None, None, 2, HEAD_DIM, PAST_LEN), lambda b, p: (b, idx, p, 0, 0))
    cast_in, cast_out, cast_shape = _cast_specs(cast_jobs, n_steps, step_of)
    n_cast = len(cast_in)
    cast_args = [ffn_w1, ffn_w2] * len(cast_jobs)
    if mixer_w is not None:
        assert n_steps == MIX_CAST_STEPS
        mix_in, mix_out, mix_shape = _mixer_cast_specs(step_of)
        cast_in, cast_out, cast_shape = cast_in + mix_in, cast_out + mix_out, cast_shape + mix_shape
        cast_args += [mixer_w[0], mixer_w[1], mixer_w[2], mixer_w[2]]
    return pl.pallas_call(
        functools.partial(_even_sample_kernel, n_cast=n_cast, cast_mixers=mixer_w is not None),
        grid=(DEC_BATCH, n_pair),
        in_specs=[seq_in, seq_in, seq_in, seq_in, seq_in, ctx, ctx,
                  pl.BlockSpec((None, 2, 2 * NA_ROWS - 1, LANES), lambda b, p: (idx, p, 0, 0)),
                  pl.BlockSpec((None, CONV_W, LANES), lambda b, p: (idx, 0, p)),
                  pl.BlockSpec((None, 1, LANES), lambda b, p: (idx, 0, p)), anywhere, anywhere] + cast_in,
        out_specs=[seq_in, seq_in] + cast_out,
        out_shape=[
            jax.ShapeDtypeStruct((N_TOK, HD_A), BF16),
            jax.ShapeDtypeStruct((N_TOK, C_B), BF16),
        ] + cast_shape,
        input_output_aliases={10: 0, 11: 1},
        scratch_shapes=[pltpu.VMEM((NA_ROWS, 2 * GRID_W, NA_ROWS * GRID_W), F32)],
        compiler_params=_params(2),
        name="even_sample",
    )(q, k, v, bg, z, ctx_k, ctx_v, rpb, conv_w, conv_b, fa, fb, *cast_args)


def _split_groups(q):
    low = _low_half(q.shape[0])
    zero = jnp.zeros((q.shape[0], LANES), BF16)
    pairs = [q[:, j * LANES:(j + 1) * LANES] for j in range(GQA_GROUP)]
    return jnp.concatenate([jnp.where(low, p, zero) for p in pairs] + [jnp.where(low, zero, p) for p in pairs],
                           axis=0)


def _merge_groups(o, m):
    low = _low_half(m)
    half = GQA_GROUP * m
    return jnp.concatenate([jnp.where(low, o[j * m:(j + 1) * m], o[half + j * m:half + (j + 1) * m])
                            for j in range(GQA_GROUP)], axis=-1)


def _odd_prompt_kernel(uc_ref, q_ref, k_ref, v_ref, pw_ref, ps_ref, fc_ref, fd_ref):
    seqs = [slice(b * SEQ, (b + 1) * SEQ) for b in range(SEQ_PER_TILE)]
    scores = [_dot_t(_split_groups(q_ref[rows, :]), k_ref[rows, :]) for rows in seqs]
    for s, rows in zip(scores, seqs):
        o = _softmax_pv([s], [_with_ones(v_ref[rows, :])])
        fd_ref[rows, :] = _merge_groups(o, SEQ).astype(BF16)
    for rows in seqs:
        fc_ref[rows, :] = _pool_mix(uc_ref[rows, :], pw_ref, ps_ref, SEQ).astype(BF16)


def _odd_prompt(uc, q, k, v, pool_w, pool_scale, idx):
    seq = lambda w: pl.BlockSpec((TM, w), lambda t: (t, 0))
    return pl.pallas_call(
        _odd_prompt_kernel,
        grid=(PROMPT_TILES,),
        in_specs=[seq(C_POOL), seq(HD_Q), seq(HD_KV), seq(HD_KV),
                  pl.BlockSpec((None, N_POOL, POOL_C, POOL_C), lambda t: (idx, 0, 0, 0)),
                  pl.BlockSpec((None, 1, C_POOL), lambda t: (idx, 0, 0))],
        out_specs=[seq(C_POOL), seq(HD_Q)],
        out_shape=[
            jax.ShapeDtypeStruct((N_TOK, C_POOL), BF16),
            jax.ShapeDtypeStruct((N_TOK, HD_Q), BF16),
        ],
        compiler_params=_params(1),
        name="odd_prompt",
    )(uc, q, k, v, pool_w, pool_scale)


def _odd_sample_kernel(*refs, n_cast):
    uc_ref, q_ref, k_ref, v_ref, ck_ref, cv_ref, pw_ref, ps_ref = refs[0:8]
    fc_ref, fd_ref = refs[10 + n_cast:12 + n_cast]
    kt_ref, vext_ref, ckpair_ref, cvext_ref = refs[12 + 2 * n_cast:]
    _cast_weights(refs[10:10 + n_cast], refs[12 + n_cast:12 + 2 * n_cast])

    @pl.when(pl.program_id(1) == 0)
    def _():
        fc_ref[...] = _pool_mix(uc_ref[...], pw_ref, ps_ref, DEC_SEQ).astype(BF16)
        kt_ref[...] = k_ref[...].astype(F32).T.astype(BF16)
        vext_ref[...] = _with_ones(v_ref[...])
        ckpair_ref[...] = _pair_heads_t(ck_ref).astype(BF16)
        cvext_ref[...] = _with_ones(_pair_heads_t(cv_ref).T.astype(BF16))

    k_t = kt_ref[...]
    ck_t = ckpair_ref[...]
    low = _low_half(Q_BLK)
    scores = []
    for j in range(GQA_GROUP):
        q2 = _split_pair(q_ref[:, j * LANES:(j + 1) * LANES])
        scores.append([_dot(q2, k_t), _dot(q2, ck_t)])
    outs = []
    for s in scores:
        o = _softmax_pv(s, [vext_ref[...], cvext_ref[...]])
        outs.append(jnp.where(low, o[0:Q_BLK], o[Q_BLK:2 * Q_BLK]))
    fd_ref[...] = jnp.concatenate(outs, axis=-1).astype(BF16)


def _odd_sample(uc, q, k, v, ctx_k, ctx_v, pool_w, pool_scale, fc, fd, idx, ffn_w1, ffn_w2, cast_jobs):
    first = N_PROMPT // DEC_SEQ
    n_qb = DEC_SEQ // Q_BLK
    first_q = N_PROMPT // Q_BLK
    whole_in = lambda w: pl.BlockSpec((DEC_SEQ, w), lambda b, i: (first + b, 0))
    ctx = pl.BlockSpec((None, None, HKV_D, HEAD_DIM, PAST_LEN), lambda b, i: (b, idx, 0, 0, 0))
    cast_in, cast_out, cast_shape = _cast_specs(cast_jobs, DEC_BATCH * n_qb, lambda b, i: b * n_qb + i)
    return pl.pallas_call(
        functools.partial(_odd_sample_kernel, n_cast=len(cast_in)),
        grid=(DEC_BATCH, n_qb),
        in_specs=[whole_in(C_POOL),
                  pl.BlockSpec((Q_BLK, HD_Q), lambda b, i: (first_q + b * n_qb + i, 0)),
                  whole_in(HD_KV), whole_in(HD_KV), ctx, ctx,
                  pl.BlockSpec((None, N_POOL, POOL_C, POOL_C), lambda b, i: (idx, 0, 0, 0)),
                  pl.BlockSpec((None, 1, C_POOL), lambda b, i: (idx, 0, 0)),
                  pl.BlockSpec(memory_space=pl.ANY), pl.BlockSpec(memory_space=pl.ANY)] + cast_in,
        out_specs=[pl.BlockSpec((DEC_SEQ, C_POOL), lambda b, i: (first + b, 0)),
                   pl.BlockSpec((Q_BLK, HD_Q), lambda b, i: (first_q + b * n_qb + i, 0))] + cast_out,
        out_shape=[
            jax.ShapeDtypeStruct((N_TOK, C_POOL), BF16),
            jax.ShapeDtypeStruct((N_TOK, HD_Q), BF16),
        ] + cast_shape,
        input_output_aliases={8: 0, 9: 1},
        scratch_shapes=[pltpu.VMEM((LANES, DEC_SEQ), BF16), pltpu.VMEM((DEC_SEQ, 2 * LANES), BF16),
                        pltpu.VMEM((LANES, PAST_LEN), BF16), pltpu.VMEM((PAST_LEN, 2 * LANES), BF16)],
        compiler_params=_params(2),
        name="odd_sample",
    )(uc, q, k, v, ctx_k, ctx_v, pool_w, pool_scale, fc, fd, *([ffn_w1, ffn_w2] * len(cast_jobs)))


def _rope_tables():
    t = np.arange(DEC_SEQ)
    n_freq = HEAD_DIM // 4
    inv = ROPE_BASE ** (-np.arange(n_freq, dtype=np.float64) / n_freq)
    ang = np.concatenate([(t // GRID_W)[:, None] * inv, (t % GRID_W)[:, None] * inv], axis=-1)
    cos = np.repeat(np.cos(ang), 2, axis=-1)
    sin = np.repeat(np.sin(ang), 2, axis=-1) * np.tile([-1.0, 1.0], HEAD_DIM // 2)
    n_heads = HQ_D + HKV_D
    cos = np.concatenate([np.ones((TM, QK_W)), np.tile(cos, (1, n_heads))], axis=0)
    sin = np.concatenate([np.zeros((TM, QK_W)), np.tile(sin, (1, n_heads))], axis=0)
    return jnp.asarray(cos, F32), jnp.asarray(sin, F32)


def _head_ones():
    head = np.arange(MXU_DIM) // HEAD_DIM
    return jnp.asarray(head[:, None] == head[None, :], BF16)


def kernel(x_prompt, x_sample, cache_a_k, cache_a_v, cache_d_k, cache_d_v, c, c_ctx, mod_w, mod_b, norm_w,
           ffn_w1, ffn_w2, ev_w_in, ev_rpb, ev_conv_w, ev_conv_b, ev_w_out, od_w_in, od_pool_w,
           od_pool_scale, od_q_norm, od_k_norm, od_w_out):
    xs = (x_prompt.reshape(N_PROMPT, D_MODEL), x_sample.reshape(N_SAMPLE, D_MODEL))
    cond = jnp.concatenate([c_ctx[None, :], c, jnp.zeros((COND_PAD - N_COND, D_MODEL), F32)], axis=0)
    mods, w1_a, w2_a = _modulation(cond, mod_w, mod_b, ffn_w1, ffn_w2)
    mods = mods.reshape(DEPTH, COND_PAD, N_MOD, D_MODEL)
    cache_a_k, cache_a_v, cache_d_k, cache_d_v = (jnp.swapaxes(t, -1, -2)
                                                  for t in (cache_a_k, cache_a_v, cache_d_k, cache_d_v))
    rpb = jnp.pad(ev_rpb, ((0, 0), (0, 0), (0, 0), (0, LANES - ev_rpb.shape[-1])))
    conv_b = ev_conv_b[:, None, :]
    pool_w, pool_scale = od_pool_w.astype(BF16), od_pool_scale[:, None, :]
    cos_t, sin_t = _rope_tables()
    ones_bd = _head_ones()
    states = []
    for l in range(DEPTH):
        i = l // 2
        last = l == DEPTH - 1
        if l == 0:
            x, ev_w_in_b = _ffn(xs, mods, norm_w, w1_a, w2_a, l, 0, ev_w_in=ev_w_in)
        else:
            x, = _ffn(xs, mods, norm_w, w1_a, w2_a, l, 0)
        cast_jobs = [(l, 1)] + ([] if last else [(l + 1, 0)])
        if l % 2 == 0:
            q, k, v, bg, z, s_k, s_v = _even_in(x, mods, norm_w, ev_w_in_b, l)
            fa, fb = _even_prompt(q, k, v, bg, z, ev_conv_w, conv_b, i)
            fa, fb, *cast = _even_sample(q, k, v, bg, z, cache_a_k, cache_a_v, rpb, ev_conv_w, conv_b, fa, fb, i,
                                         ffn_w1, ffn_w2, cast_jobs, mixer_w=(ev_w_out, od_w_in, od_w_out))
            ev_w_out_b, od_w_in_b, od_w_out_b = cast[2 * len(cast_jobs):]
            w_out = ev_w_out_b
        else:
            gain = jnp.concatenate([jnp.tile(od_q_norm[i], HQ_D), jnp.tile(od_k_norm[i], HKV_D)])[None, :]
            uc, q, k, v, s_k, s_v = _odd_in(x, mods, norm_w, od_w_in_b, gain, cos_t, sin_t, ones_bd, l)
            fa, fb = _odd_prompt(uc, q, k, v, pool_w, pool_scale, i)
            fa, fb, *cast = _odd_sample(uc, q, k, v, cache_d_k, cache_d_v, pool_w, pool_scale, fa, fb, i,
                                        ffn_w1, ffn_w2, cast_jobs)
            w_out = od_w_out_b
        w1_b, w2_b = cast[0:2]
        if not last:
            w1_a, w2_a = cast[2:4]
        states.append((jnp.swapaxes(s_k, -1, -2), jnp.swapaxes(s_v, -1, -2)))
        xs = tuple(_ffn((x,), mods, norm_w, w1_b, w2_b, l, 2, feats=(fa, fb), w_out=w_out,
                        split_out=last))
    y_prompt = xs[0].reshape(BATCH, SEQ, D_MODEL)
    y_sample = xs[1].reshape(DEC_BATCH, DEC_SEQ, D_MODEL)
    return (y_prompt, y_sample, states[0][0], states[0][1], states[1][0], states[1][1])
```

```python
import functools

import jax
import jax.numpy as jnp
import numpy as np
from jax import lax
from jax.experimental import pallas as pl
from jax.experimental.pallas import tpu as pltpu

D_MODEL = 1024
BATCH = 32
SEQ = 256
DEPTH = 2
DEC_BATCH = 2
DEC_SEQ = 2048
PAST_LEN = 256
GRID_W = 64
HEAD_DIM = 64
N_MOD = 9
N_NORM = 6
D_FF = 2816
FFN_RES = 0.5
H_A = 8
NA_ROWS = 8
NA_COLS = 16
C_B = 512
CONV_W = 3
C_POOL = 512
POOL_WINDOWS = (2, 4, 8, 16)
N_POOL = 4
POOL_C = C_POOL // N_POOL
HQ_D = 8
HKV_D = 2
GQA_GROUP = HQ_D // HKV_D
ROPE_BASE = 10000.0
EVEN_IN = 3 * H_A * HEAD_DIM + 3 * C_B
ODD_IN = C_POOL + (HQ_D + 2 * HKV_D) * HEAD_DIM
RMS_EPS = 1e-6
NEG_INF = -1e30
ATT_SCALE = HEAD_DIM ** -0.5

LANES = 128
GRID_ROWS = DEC_SEQ // GRID_W
N_PROMPT = BATCH * SEQ
N_SAMPLE = DEC_BATCH * DEC_SEQ
N_TOK = N_PROMPT + N_SAMPLE
N_COND = 1 + DEC_BATCH
COND_PAD = 8
HD_A = H_A * HEAD_DIM
HD_Q = HQ_D * HEAD_DIM
HD_KV = HKV_D * HEAD_DIM
QK_W = HD_Q + HD_KV
POOL_PAD = 8

TM = 1024
N_TILES = N_TOK // TM
PROMPT_TILES = N_PROMPT // TM
TILES_PER_SAMPLE = DEC_SEQ // TM
SEQ_PER_TILE = TM // SEQ
TM_FFN = 1024
FFN_HALF = TM_FFN // 2
MIX_CAST_STEPS = N_PROMPT // TM_FFN
MIX_CAST_ROWS = D_MODEL // MIX_CAST_STEPS
assert MIX_CAST_ROWS == 2 * HEAD_DIM and (DEPTH + 1) // 2 == 1 and DEPTH // 2 == 1
MXU_DIM = 256
FF_CHUNKS = tuple((lo, min(lo + 3 * MXU_DIM, D_FF)) for lo in range(0, D_FF, 3 * MXU_DIM))
MOD_TN = 1152
Q_BLK = 256
ROW_UNROLL = 16
VMEM_LIMIT = 60 * 1024 * 1024

F32 = jnp.float32
BF16 = jnp.bfloat16


def _params(n_grid):
    return pltpu.CompilerParams(dimension_semantics=("arbitrary",) * n_grid, vmem_limit_bytes=VMEM_LIMIT)


def _cond_of_tile(i, tm=TM):
    n_prompt = N_PROMPT // tm
    return jnp.where(i < n_prompt, 0, 1 + (i - n_prompt) // (DEC_SEQ // tm))


def _prompt_tile(i, tm=TM):
    return jnp.minimum(i, N_PROMPT // tm - 1)


def _sample_tile(i, tm=TM):
    return jnp.maximum(i - N_PROMPT // tm, 0)


def _rms(x, g):
    return x * lax.rsqrt(jnp.mean(x * x, axis=-1, keepdims=True) + RMS_EPS) * g


def _sigmoid(x):
    return 1.0 / (1.0 + jnp.exp(-x))


def _dot(a, b):
    return jnp.dot(a, b, preferred_element_type=F32)


def _dot_t(a, b):
    return lax.dot_general(a, b, (((1,), (1,)), ((), ())), preferred_element_type=F32)


def _mod_kernel(c_ref, w_ref, b_ref, w1f_ref, w2f_ref, o_ref, w1b_ref, w2b_ref):
    _cast_weights([w1f_ref, w2f_ref], [w1b_ref, w2b_ref])
    c = c_ref[...]
    sc = (c * _sigmoid(c)).astype(BF16)
    o_ref[...] = _dot(sc, w_ref[...].astype(BF16)) + b_ref[...]


def _modulation(cond, mod_w, mod_b, ffn_w1, ffn_w2):
    n_col = N_MOD * D_MODEL
    n_blk = n_col // MOD_TN
    cast_in, cast_out, cast_shape = _cast_specs([(0, 0)], DEPTH * n_blk, lambda l, j: l * n_blk + j)
    return pl.pallas_call(
        _mod_kernel,
        grid=(DEPTH, n_blk),
        in_specs=[
            pl.BlockSpec((COND_PAD, D_MODEL), lambda l, j: (0, 0)),
            pl.BlockSpec((None, D_MODEL, MOD_TN), lambda l, j: (l, 0, j)),
            pl.BlockSpec((None, 1, MOD_TN), lambda l, j: (l, 0, j)),
        ] + cast_in,
        out_specs=[pl.BlockSpec((None, COND_PAD, MOD_TN), lambda l, j: (l, 0, j))] + cast_out,
        out_shape=[jax.ShapeDtypeStruct((DEPTH, COND_PAD, n_col), F32)] + cast_shape,
        compiler_params=_params(2),
        name="modulation",
    )(cond, mod_w, mod_b.reshape(DEPTH, 1, n_col), ffn_w1, ffn_w2)


def _mod_spec(layer, tm=TM):
    return pl.BlockSpec((None, None, N_MOD, D_MODEL), lambda i: (layer, _cond_of_tile(i, tm), 0, 0))


def _gain_spec(layer):
    return pl.BlockSpec((None, N_NORM, D_MODEL), lambda i: (layer, 0, 0))


def _cast_specs(jobs, n_steps, step_of):
    r1, r2 = D_MODEL // n_steps, D_FF // n_steps
    in_specs, out_specs, out_shape = [], [], []
    for layer, which in jobs:
        in_specs += [pl.BlockSpec((None, None, r1, 2 * D_FF), lambda *g, lw=(layer, which): (*lw, step_of(*g), 0)),
                     pl.BlockSpec((None, None, r2, D_MODEL), lambda *g, lw=(layer, which): (*lw, step_of(*g), 0))]
        out_specs += [pl.BlockSpec((r1, 2 * D_FF), lambda *g: (step_of(*g), 0)),
                      pl.BlockSpec((r2, D_MODEL), lambda *g: (step_of(*g), 0))]
        out_shape += [jax.ShapeDtypeStruct((D_MODEL, 2 * D_FF), BF16), jax.ShapeDtypeStruct((D_FF, D_MODEL), BF16)]
    return in_specs, out_specs, out_shape


def _cast_weights(f32_refs, bf16_refs):
    for src, dst in zip(f32_refs, bf16_refs):
        dst[...] = src[...].astype(BF16)


def _ev_in_cast_specs():
    step = lambda i: jnp.minimum(i, MIX_CAST_STEPS - 1)
    return ([pl.BlockSpec((None, MIX_CAST_ROWS, EVEN_IN), lambda i: (0, step(i), 0))],
            [pl.BlockSpec((MIX_CAST_ROWS, EVEN_IN), lambda i: (step(i), 0))],
            [jax.ShapeDtypeStruct((D_MODEL, EVEN_IN), BF16)])


def _mixer_cast_specs(step_of):
    rows = MIX_CAST_ROWS
    n_pool_blk = C_POOL // rows

    def od_out_block(s, g):
        return jnp.where(s < n_pool_blk, 2 * s + g, C_POOL // HEAD_DIM + g * GQA_GROUP + s - n_pool_blk)

    widths = (D_MODEL, ODD_IN, D_MODEL)
    in_specs = [pl.BlockSpec((None, rows, D_MODEL), lambda *g: (0, step_of(*g), 0)),
                pl.BlockSpec((None, rows, ODD_IN), lambda *g: (0, step_of(*g), 0)),
                pl.BlockSpec((None, HEAD_DIM, D_MODEL), lambda *g: (0, od_out_block(step_of(*g), 0), 0)),
                pl.BlockSpec((None, HEAD_DIM, D_MODEL), lambda *g: (0, od_out_block(step_of(*g), 1), 0))]
    out_specs = [pl.BlockSpec((rows, w), lambda *g: (step_of(*g), 0)) for w in widths]
    out_shape = [jax.ShapeDtypeStruct((D_MODEL, w), BF16) for w in widths]
    return in_specs, out_specs, out_shape


def _cast_mixer_weights(ev_out_f, od_in_f, od_out_lo_f, od_out_hi_f, ev_out_b, od_in_b, od_out_b):
    ev_out_b[...] = ev_out_f[...].astype(BF16)
    w = od_in_f[...]
    q = w[:, C_POOL:C_POOL + HD_Q]
    heads = [q[:, (g * GQA_GROUP + j) * HEAD_DIM:(g * GQA_GROUP + j + 1) * HEAD_DIM]
             for j in range(GQA_GROUP) for g in range(HKV_D)]
    od_in_b[...] = jnp.concatenate([w[:, 0:C_POOL]] + heads + [w[:, C_POOL + HD_Q:]], axis=1).astype(BF16)
    od_out_b[...] = jnp.concatenate([od_out_lo_f[...], od_out_hi_f[...]], axis=0).astype(BF16)


def _ffn_kernel(*refs, sub, split_in, mix, split_out, cast_ev_in):
    refs = list(refs)
    x_refs = [refs.pop(0) for _ in range(2 if split_in else 1)]
    feat_refs = [refs.pop(0) for _ in range(2 if mix else 0)]
    m_ref, g_ref = refs.pop(0), refs.pop(0)
    wo_ref = refs.pop(0) if mix else None
    w1_ref, w2_ref = refs.pop(0), refs.pop(0)
    if cast_ev_in:
        @pl.when(pl.program_id(0) < MIX_CAST_STEPS)
        def _():
            refs[-1][...] = refs[0][...].astype(BF16)
        refs = refs[1:-1]
    out_refs = refs
    is_prompt = pl.program_id(0) < N_PROMPT // TM_FFN
    shift = m_ref[3 * sub:3 * sub + 1, :]
    scale = m_ref[3 * sub + 1:3 * sub + 2, :]
    gate = m_ref[3 * sub + 2:3 * sub + 3, :]
    halves = [slice(hf * FFN_HALF, (hf + 1) * FFN_HALF) for hf in range(TM_FFN // FFN_HALF)]
    mixed, ys = [], []
    if mix:
        for rows in halves:
            feat = jnp.concatenate([feat_refs[0][rows, :], feat_refs[1][rows, :]], axis=1)
            mixed.append(_dot(feat, wo_ref[...]))
    for half, rows in enumerate(halves):
        if split_in:
            x = jnp.where(is_prompt, x_refs[0][rows, :], x_refs[1][rows, :])
        else:
            x = x_refs[0][rows, :]
        if mix:
            x = x + m_ref[5:6, :] * _rms(mixed[half], g_ref[3:4, :])
        h = (_rms(x, g_ref[2 * sub:2 * sub + 1, :]) * (1.0 + scale) + shift).astype(BF16)
        acc = jnp.zeros((FFN_HALF, D_MODEL), F32)
        for lo, hi in FF_CHUNKS:
            a = _dot(h, w1_ref[:, lo:hi])
            u = _dot(h, w1_ref[:, D_FF + lo:D_FF + hi])
            act = (a * _sigmoid(a) * u).astype(BF16)
            acc = acc + _dot(act, w2_ref[lo:hi, :])
        y = x + FFN_RES * gate * _rms(acc, g_ref[2 * sub + 1:2 * sub + 2, :])
        if split_out:
            ys.append(y)
        else:
            out_refs[0][rows, :] = y
    if split_out:
        @pl.when(is_prompt)
        def _():
            for rows, y in zip(halves, ys):
                out_refs[0][rows, :] = y

        @pl.when(jnp.logical_not(is_prompt))
        def _():
            for rows, y in zip(halves, ys):
                out_refs[1][rows, :] = y


def _ffn(xs, mods, gains, w1, w2, layer, sub, feats=None, w_out=None, split_out=False, ev_w_in=None):
    split_in = len(xs) == 2
    mix = feats is not None
    cast_ev_in = ev_w_in is not None
    resident = pl.Buffered(1)
    half_w = D_MODEL // 2
    feat_specs, mix_w_spec = [], []
    if mix:
        feat_specs = [pl.BlockSpec((TM_FFN, half_w), lambda i: (i, 0))] * 2
        mix_w_spec = [pl.BlockSpec((D_MODEL, D_MODEL), lambda i: (0, 0), pipeline_mode=resident)]
    cast_in, cast_out, cast_shape, cast_args = [], [], [], ()
    if cast_ev_in:
        cast_in, cast_out, cast_shape = _ev_in_cast_specs()
        cast_args = (ev_w_in,)
    tok = pl.BlockSpec((TM_FFN, D_MODEL), lambda i: (i, 0))
    prompt_tok = pl.BlockSpec((TM_FFN, D_MODEL), lambda i: (_prompt_tile(i, TM_FFN), 0))
    sample_tok = pl.BlockSpec((TM_FFN, D_MODEL), lambda i: (_sample_tile(i, TM_FFN), 0))
    if split_out:
        out_specs = [prompt_tok, sample_tok]
        out_shape = [jax.ShapeDtypeStruct((N_PROMPT, D_MODEL), F32),
                     jax.ShapeDtypeStruct((N_SAMPLE, D_MODEL), F32)]
    else:
        out_specs = [tok]
        out_shape = [jax.ShapeDtypeStruct((N_TOK, D_MODEL), F32)]
    return pl.pallas_call(
        functools.partial(_ffn_kernel, sub=sub, split_in=split_in, mix=mix, split_out=split_out,
                          cast_ev_in=cast_ev_in),
        grid=(N_TOK // TM_FFN,),
        in_specs=([prompt_tok, sample_tok] if split_in else [tok]) + feat_specs + [
            _mod_spec(layer, TM_FFN),
            _gain_spec(layer),
        ] + mix_w_spec + [
            pl.BlockSpec((D_MODEL, 2 * D_FF), lambda i: (0, 0), pipeline_mode=resident),
            pl.BlockSpec((D_FF, D_MODEL), lambda i: (0, 0), pipeline_mode=resident),
        ] + cast_in,
        out_specs=out_specs + cast_out,
        out_shape=out_shape + cast_shape,
        compiler_params=_params(1),
        name=f"ffn{sub}",
    )(*xs, *(feats or ()), mods, gains, *((w_out,) if mix else ()), w1, w2, *cast_args)


def _mixer_norm(x, m_ref, g_ref):
    return (_rms(x, g_ref[2:3, :]) * (1.0 + m_ref[4:5, :]) + m_ref[3:4, :]).astype(BF16)


def _store_heads(state_ref, b, x, n_heads):
    xt = x.T
    for h in range(n_heads):
        state_ref[b, 0, h] = xt[h * HEAD_DIM:(h + 1) * HEAD_DIM, :]


def _even_in_kernel(x_ref, m_ref, g_ref, w_ref, q_ref, k_ref, v_ref, bg_ref, z_ref, sk_ref, sv_ref):
    kv = []
    for b in range(SEQ_PER_TILE):
        rows = slice(b * SEQ, (b + 1) * SEQ)
        u = _dot(_mixer_norm(x_ref[rows, :], m_ref, g_ref), w_ref[...])
        k = u[:, HD_A:2 * HD_A]
        v = u[:, 2 * HD_A:3 * HD_A]
        q_ref[rows, :] = (u[:, 0:HD_A] * ATT_SCALE).astype(BF16)
        k_ref[rows, :] = k.astype(BF16)
        v_ref[rows, :] = v.astype(BF16)
        bg_ref[rows, :] = u[:, 3 * HD_A:3 * HD_A + C_B]
        z_ref[rows, :] = u[:, 3 * HD_A + C_B:3 * HD_A + 2 * C_B] * u[:, 3 * HD_A + 2 * C_B:3 * HD_A + 3 * C_B]
        kv.append((k, v))

    @pl.when(pl.program_id(0) < PROMPT_TILES)
    def _():
        for b, (k, v) in enumerate(kv):
            _store_heads(sk_ref, b, k, H_A)
            _store_heads(sv_ref, b, v, H_A)


def _state_spec(n_heads):
    return pl.BlockSpec((SEQ_PER_TILE, 1, n_heads, HEAD_DIM, SEQ), lambda i: (_prompt_tile(i), 0, 0, 0, 0))


def _even_in(x, mods, gains, w_in, layer):
    tok = lambda w: pl.BlockSpec((TM, w), lambda i: (i, 0))
    state = jax.ShapeDtypeStruct((BATCH, 1, H_A, HEAD_DIM, SEQ), F32)
    return pl.pallas_call(
        _even_in_kernel,
        grid=(N_TILES,),
        in_specs=[
            tok(D_MODEL), _mod_spec(layer), _gain_spec(layer),
            pl.BlockSpec((D_MODEL, EVEN_IN), lambda i: (0, 0)),
        ],
        out_specs=[tok(HD_A), tok(HD_A), tok(HD_A), tok(C_B), tok(C_B), _state_spec(H_A), _state_spec(H_A)],
        out_shape=[
            jax.ShapeDtypeStruct((N_TOK, HD_A), BF16),
            jax.ShapeDtypeStruct((N_TOK, HD_A), BF16),
            jax.ShapeDtypeStruct((N_TOK, HD_A), BF16),
            jax.ShapeDtypeStruct((N_TOK, C_B), F32),
            jax.ShapeDtypeStruct((N_TOK, C_B), F32),
            state, state,
        ],
        compiler_params=_params(1),
        name="even_in",
    )(x, mods, gains, w_in)


def _swap_pairs(x):
    n = x.shape[-1]
    lane = lax.broadcasted_iota(jnp.int32, x.shape, x.ndim - 1)
    return jnp.where(lane % 2 == 0, pltpu.roll(x, n - 1, x.ndim - 1), pltpu.roll(x, 1, x.ndim - 1))


def _odd_in_kernel(x_ref, m_ref, g_ref, w_ref, ng_ref, cos_ref, sin_ref, ones_ref,
                   uc_ref, q_ref, k_ref, v_ref, sk_ref, sv_ref):
    ones = ones_ref[...]
    kv = []
    for b in range(SEQ_PER_TILE):
        rows = slice(b * SEQ, (b + 1) * SEQ)
        u = _dot(_mixer_norm(x_ref[rows, :], m_ref, g_ref), w_ref[...])
        uc_ref[rows, :] = u[:, 0:C_POOL]
        qk = u[:, C_POOL:C_POOL + QK_W]
        sq = qk * qk
        hi = sq.astype(BF16)
        lo = (sq - hi.astype(F32)).astype(BF16)
        sums = []
        for c0 in range(0, QK_W, MXU_DIM):
            c1 = min(c0 + MXU_DIM, QK_W)
            tile = ones[0:c1 - c0, 0:c1 - c0]
            sums.append(_dot(hi[:, c0:c1], tile) + _dot(lo[:, c0:c1], tile))
        ms = jnp.concatenate(sums, axis=-1) * (1.0 / HEAD_DIM)
        n = qk * lax.rsqrt(ms + RMS_EPS) * ng_ref[...]
        r = n * cos_ref[rows, :] + _swap_pairs(n) * sin_ref[rows, :]
        k = r[:, HD_Q:QK_W]
        v = u[:, C_POOL + QK_W:ODD_IN]
        q_ref[rows, :] = (r[:, 0:HD_Q] * ATT_SCALE).astype(BF16)
        k_ref[rows, :] = k.astype(BF16)
        v_ref[rows, :] = v.astype(BF16)
        kv.append((k, v))

    @pl.when(pl.program_id(0) < PROMPT_TILES)
    def _():
        for b, (k, v) in enumerate(kv):
            _store_heads(sk_ref, b, k, HKV_D)
            _store_heads(sv_ref, b, v, HKV_D)


def _rope_tile_index(i):
    return jnp.where(i < PROMPT_TILES, 0, 1 + (i - PROMPT_TILES) % TILES_PER_SAMPLE)


def _odd_in(x, mods, gains, w_in, qk_gain, cos_t, sin_t, ones_bd, layer):
    tok = lambda w: pl.BlockSpec((TM, w), lambda i: (i, 0))
    state = jax.ShapeDtypeStruct((BATCH, 1, HKV_D, HEAD_DIM, SEQ), F32)
    return pl.pallas_call(
        _odd_in_kernel,
        grid=(N_TILES,),
        in_specs=[
            tok(D_MODEL), _mod_spec(layer), _gain_spec(layer),
            pl.BlockSpec((D_MODEL, ODD_IN), lambda i: (0, 0)),
            pl.BlockSpec((1, QK_W), lambda i: (0, 0)),
            pl.BlockSpec((TM, QK_W), lambda i: (_rope_tile_index(i), 0)),
            pl.BlockSpec((TM, QK_W), lambda i: (_rope_tile_index(i), 0)),
            pl.BlockSpec((MXU_DIM, MXU_DIM), lambda i: (0, 0)),
        ],
        out_specs=[tok(C_POOL), tok(HD_Q), tok(HD_KV), tok(HD_KV), _state_spec(HKV_D), _state_spec(HKV_D)],
        out_shape=[
            jax.ShapeDtypeStruct((N_TOK, C_POOL), F32),
            jax.ShapeDtypeStruct((N_TOK, HD_Q), BF16),
            jax.ShapeDtypeStruct((N_TOK, HD_KV), BF16),
            jax.ShapeDtypeStruct((N_TOK, HD_KV), BF16),
            state, state,
        ],
        compiler_params=_params(1),
        name="odd_in",
    )(x, mods, gains, w_in, qk_gain, cos_t, sin_t, ones_bd)


def _low_half(rows):
    return lax.broadcasted_iota(jnp.int32, (rows, LANES), 1) < HEAD_DIM


def _split_pair(qp):
    low = _low_half(qp.shape[0])
    zero = jnp.zeros_like(qp)
    return jnp.concatenate([jnp.where(low, qp, zero), jnp.where(low, zero, qp)], axis=0)


def _pair_heads_t(ref):
    return jnp.concatenate([ref[0], ref[1]], axis=0)


def _with_ones(v):
    return jnp.concatenate([v, jnp.ones_like(v)], axis=1)


def _softmax_pv(scores, values):
    m = scores[0].max(axis=-1, keepdims=True)
    for s in scores[1:]:
        m = jnp.maximum(m, s.max(axis=-1, keepdims=True))
    acc = None
    for s, v in zip(scores, values):
        o = _dot(jnp.exp(s - m).astype(BF16), v)
        acc = o if acc is None else acc + o
    return acc[:, 0:LANES] / acc[:, LANES:2 * LANES]


def _merge_pair(o):
    m = o.shape[0] // 2
    return jnp.where(_low_half(m), o[0:m], o[m:2 * m])


def _short_conv(z, cw_ref, cb_ref, seq_len):
    rows = z.shape[0]
    pos = lax.broadcasted_iota(jnp.int32, z.shape, 0) & (seq_len - 1)
    z_prev = jnp.where(pos == 0, 0.0, pltpu.roll(z, 1, 0))
    z_next = jnp.where(pos == seq_len - 1, 0.0, pltpu.roll(z, rows - 1, 0))
    y = z_prev * cw_ref[0:1, :]
    y = y + z * cw_ref[1:2, :]
    y = y + z_next * cw_ref[2:3, :]
    return y + cb_ref[...]


def _pool_mix(uc, pw_ref, ps_ref, seq_len):
    pos = lax.broadcasted_iota(jnp.int32, (seq_len, POOL_C), 0)
    pad = jnp.zeros((POOL_PAD, POOL_C), F32)
    n_ext = seq_len + 2 * POOL_PAD
    outs = []
    for gi, win in enumerate(POOL_WINDOWS):
        ug = uc[:, gi * POOL_C:(gi + 1) * POOL_C]
        run = jnp.concatenate([pad, ug, pad], axis=0)
        span = 1
        while span < win:
            run = run + pltpu.roll(run, span, 0)
            span *= 2
        back = win // 2 - 1
        if back:
            run = pltpu.roll(run, n_ext - back, 0)
        wsum = run[POOL_PAD:POOL_PAD + seq_len, :]
        cnt = jnp.minimum(pos + (win - win // 2), seq_len) - jnp.maximum(pos - win // 2, 0)
        pooled = (wsum / cnt.astype(F32) - ug).astype(BF16)
        outs.append(_dot(pooled, pw_ref[gi]))
    return jnp.concatenate(outs, axis=-1) * ps_ref[...]


def _even_prompt_kernel(q_ref, k_ref, v_ref, bg_ref, z_ref, cw_ref, cb_ref, fa_ref, fb_ref):
    for b in range(SEQ_PER_TILE):
        rows = slice(b * SEQ, (b + 1) * SEQ)
        pairs = [slice(p * LANES, (p + 1) * LANES) for p in range(H_A // 2)]
        scores = [_dot_t(_split_pair(q_ref[rows, sl]), k_ref[rows, sl]) for sl in pairs]
        outs = [_merge_pair(_softmax_pv([s], [_with_ones(v_ref[rows, sl])])) for s, sl in zip(scores, pairs)]
        fa_ref[rows, :] = jnp.concatenate(outs, axis=-1).astype(BF16)
    fb_ref[...] = (bg_ref[...] * _short_conv(z_ref[...], cw_ref, cb_ref, SEQ)).astype(BF16)


def _even_prompt(q, k, v, bg, z, conv_w, conv_b, idx):
    tile = lambda w: pl.BlockSpec((TM, w), lambda t: (t, 0))
    return pl.pallas_call(
        _even_prompt_kernel,
        grid=(PROMPT_TILES,),
        in_specs=[tile(HD_A), tile(HD_A), tile(HD_A), tile(C_B), tile(C_B),
                  pl.BlockSpec((None, CONV_W, C_B), lambda t: (idx, 0, 0)),
                  pl.BlockSpec((None, 1, C_B), lambda t: (idx, 0, 0))],
        out_specs=[tile(HD_A), tile(C_B)],
        out_shape=[
            jax.ShapeDtypeStruct((N_TOK, HD_A), BF16),
            jax.ShapeDtypeStruct((N_TOK, C_B), BF16),
        ],
        compiler_params=_params(1),
        name="even_prompt",
    )(q, k, v, bg, z, conv_w, conv_b)


def _na_bias_tiles(rpb_h):
    c = lax.broadcasted_iota(jnp.int32, (GRID_W, GRID_W), 0)
    kc = lax.broadcasted_iota(jnp.int32, (GRID_W, GRID_W), 1)
    cs = jnp.clip(c - NA_COLS // 2, 0, GRID_W - NA_COLS)
    valid = (kc >= cs) & (kc < cs + NA_COLS)
    tiles = []
    for ro in range(2 * NA_ROWS - 1):
        g = jnp.broadcast_to(rpb_h[ro:ro + 1, :], (GRID_W, LANES))
        skew = pltpu.roll(g, LANES - (NA_COLS - 1), 1, stride=1, stride_axis=0)
        tiles.append(jnp.where(valid, skew[:, 0:GRID_W], NEG_INF))
    return tiles


def _even_sample_kernel(*refs, n_cast, cast_mixers):
    n_mix_in, n_mix_out = (4, 3) if cast_mixers else (0, 0)
    q_ref, k_ref, v_ref, bg_ref, z_ref, ck_ref, cv_ref, rpb_ref, cw_ref, cb_ref = refs[0:10]
    cast0 = 12
    out0 = cast0 + n_cast + n_mix_in
    fa_ref, fb_ref = refs[out0:out0 + 2]
    bias_ref = refs[out0 + 2 + n_cast + n_mix_out]
    _cast_weights(refs[cast0:cast0 + n_cast], refs[out0 + 2:out0 + 2 + n_cast])
    if cast_mixers:
        _cast_mixer_weights(*refs[cast0 + n_cast:out0], *refs[out0 + 2 + n_cast:out0 + 2 + n_cast + n_mix_out])
    win = NA_ROWS * GRID_W
    for h in range(2):
        tiles = _na_bias_tiles(rpb_ref[h])
        for var in range(NA_ROWS):
            bias_ref[var, h * GRID_W:(h + 1) * GRID_W, :] = jnp.concatenate(
                [tiles[i - var + NA_ROWS - 1] for i in range(NA_ROWS)], axis=1)
    ck_t = _pair_heads_t(ck_ref).astype(BF16)
    cv_ext = _with_ones(_pair_heads_t(cv_ref).T.astype(BF16))

    def rows(it, carry):
        staged = []
        for j in range(ROW_UNROLL):
            r = it * ROW_UNROLL + j
            rs = jnp.clip(r - NA_ROWS // 2, 0, GRID_ROWS - NA_ROWS)
            q0 = pl.multiple_of(r * GRID_W, GRID_W)
            k0 = pl.multiple_of(rs * GRID_W, GRID_W)
            q2 = _split_pair(q_ref[pl.ds(q0, GRID_W), :])
            s_loc = _dot_t(q2, k_ref[pl.ds(k0, win), :]) + bias_ref[r - rs]
            s_ctx = _dot(q2, ck_t)
            staged.append((q0, k0, s_loc, s_ctx))
        for q0, k0, s_loc, s_ctx in staged:
            o = _softmax_pv([s_loc, s_ctx], [_with_ones(v_ref[pl.ds(k0, win), :]), cv_ext])
            fa_ref[pl.ds(q0, GRID_W), :] = _merge_pair(o).astype(BF16)
        return carry

    lax.fori_loop(0, GRID_ROWS // ROW_UNROLL, rows, 0)
    fb_ref[...] = (bg_ref[...] * _short_conv(z_ref[...], cw_ref, cb_ref, DEC_SEQ)).astype(BF16)


def _even_sample(q, k, v, bg, z, ctx_k, ctx_v, rpb, conv_w, conv_b, fa, fb, idx, ffn_w1, ffn_w2, cast_jobs,
                 mixer_w=None):
    first = N_PROMPT // DEC_SEQ
    n_pair = H_A // 2
    n_steps = DEC_BATCH * n_pair
    step_of = lambda b, p: b * n_pair + p
    seq_in = pl.BlockSpec((DEC_SEQ, LANES), lambda b, p: (first + b, p))
    anywhere = pl.BlockSpec(memory_space=pl.ANY)
    ctx = pl.BlockSpec((None, None, 2, HEAD_DIM, PAST_LEN), lambda b, p: (b, idx, p, 0, 0))
    cast_in, cast_out, cast_shape = _cast_specs(cast_jobs, n_steps, step_of)
    n_cast = len(cast_in)
    cast_args = [ffn_w1, ffn_w2] * len(cast_jobs)
    if mixer_w is not None:
        assert n_steps == MIX_CAST_STEPS
        mix_in, mix_out, mix_shape = _mixer_cast_specs(step_of)
        cast_in, cast_out, cast_shape = cast_in + mix_in, cast_out + mix_out, cast_shape + mix_shape
        cast_args += [mixer_w[0], mixer_w[1], mixer_w[2], mixer_w[2]]
    return pl.pallas_call(
        functools.partial(_even_sample_kernel, n_cast=n_cast, cast_mixers=mixer_w is not None),
        grid=(DEC_BATCH, n_pair),
        in_specs=[seq_in, seq_in, seq_in, seq_in, seq_in, ctx, ctx,
                  pl.BlockSpec((None, 2, 2 * NA_ROWS - 1, LANES), lambda b, p: (idx, p, 0, 0)),
                  pl.BlockSpec((None, CONV_W, LANES), lambda b, p: (idx, 0, p)),
                  pl.BlockSpec((None, 1, LANES), lambda b, p: (idx, 0, p)), anywhere, anywhere] + cast_in,
        out_specs=[seq_in, seq_in] + cast_out,
        out_shape=[
            jax.ShapeDtypeStruct((N_TOK, HD_A), BF16),
            jax.ShapeDtypeStruct((N_TOK, C_B), BF16),
        ] + cast_shape,
        input_output_aliases={10: 0, 11: 1},
        scratch_shapes=[pltpu.VMEM((NA_ROWS, 2 * GRID_W, NA_ROWS * GRID_W), F32)],
        compiler_params=_params(2),
        name="even_sample",
    )(q, k, v, bg, z, ctx_k, ctx_v, rpb, conv_w, conv_b, fa, fb, *cast_args)


def _split_groups(q):
    low = _low_half(q.shape[0])
    zero = jnp.zeros((q.shape[0], LANES), BF16)
    pairs = [q[:, j * LANES:(j + 1) * LANES] for j in range(GQA_GROUP)]
    return jnp.concatenate([jnp.where(low, p, zero) for p in pairs] + [jnp.where(low, zero, p) for p in pairs],
                           axis=0)


def _merge_groups(o, m):
    low = _low_half(m)
    half = GQA_GROUP * m
    return jnp.concatenate([jnp.where(low, o[j * m:(j + 1) * m], o[half + j * m:half + (j + 1) * m])
                            for j in range(GQA_GROUP)], axis=-1)


def _odd_prompt_kernel(uc_ref, q_ref, k_ref, v_ref, pw_ref, ps_ref, fc_ref, fd_ref):
    seqs = [slice(b * SEQ, (b + 1) * SEQ) for b in range(SEQ_PER_TILE)]
    scores = [_dot_t(_split_groups(q_ref[rows, :]), k_ref[rows, :]) for rows in seqs]
    for s, rows in zip(scores, seqs):
        o = _softmax_pv([s], [_with_ones(v_ref[rows, :])])
        fd_ref[rows, :] = _merge_groups(o, SEQ).astype(BF16)
    for rows in seqs:
        fc_ref[rows, :] = _pool_mix(uc_ref[rows, :], pw_ref, ps_ref, SEQ).astype(BF16)


def _odd_prompt(uc, q, k, v, pool_w, pool_scale, idx):
    seq = lambda w: pl.BlockSpec((TM, w), lambda t: (t, 0))
    return pl.pallas_call(
        _odd_prompt_kernel,
        grid=(PROMPT_TILES,),
        in_specs=[seq(C_POOL), seq(HD_Q), seq(HD_KV), seq(HD_KV),
                  pl.BlockSpec((None, N_POOL, POOL_C, POOL_C), lambda t: (idx, 0, 0, 0)),
                  pl.BlockSpec((None, 1, C_POOL), lambda t: (idx, 0, 0))],
        out_specs=[seq(C_POOL), seq(HD_Q)],
        out_shape=[
            jax.ShapeDtypeStruct((N_TOK, C_POOL), BF16),
            jax.ShapeDtypeStruct((N_TOK, HD_Q), BF16),
        ],
        compiler_params=_params(1),
        name="odd_prompt",
    )(uc, q, k, v, pool_w, pool_scale)


def _odd_sample_kernel(*refs, n_cast):
    uc_ref, q_ref, k_ref, v_ref, ck_ref, cv_ref, pw_ref, ps_ref = refs[0:8]
    fc_ref, fd_ref = refs[10 + n_cast:12 + n_cast]
    kt_ref, vext_ref, ckpair_ref, cvext_ref = refs[12 + 2 * n_cast:]
    _cast_weights(refs[10:10 + n_cast], refs[12 + n_cast:12 + 2 * n_cast])

    @pl.when(pl.program_id(1) == 0)
    def _():
        fc_ref[...] = _pool_mix(uc_ref[...], pw_ref, ps_ref, DEC_SEQ).astype(BF16)
        kt_ref[...] = k_ref[...].astype(F32).T.astype(BF16)
        vext_ref[...] = _with_ones(v_ref[...])
        ckpair_ref[...] = _pair_heads_t(ck_ref).astype(BF16)
        cvext_ref[...] = _with_ones(_pair_heads_t(cv_ref).T.astype(BF16))

    k_t = kt_ref[...]
    ck_t = ckpair_ref[...]
    low = _low_half(Q_BLK)
    scores = []
    for j in range(GQA_GROUP):
        q2 = _split_pair(q_ref[:, j * LANES:(j + 1) * LANES])
        scores.append([_dot(q2, k_t), _dot(q2, ck_t)])
    outs = []
    for s in scores:
        o = _softmax_pv(s, [vext_ref[...], cvext_ref[...]])
        outs.append(jnp.where(low, o[0:Q_BLK], o[Q_BLK:2 * Q_BLK]))
    fd_ref[...] = jnp.concatenate(outs, axis=-1).astype(BF16)


def _odd_sample(uc, q, k, v, ctx_k, ctx_v, pool_w, pool_scale, fc, fd, idx, ffn_w1, ffn_w2, cast_jobs):
    first = N_PROMPT // DEC_SEQ
    n_qb = DEC_SEQ // Q_BLK
    first_q = N_PROMPT // Q_BLK
    whole_in = lambda w: pl.BlockSpec((DEC_SEQ, w), lambda b, i: (first + b, 0))
    ctx = pl.BlockSpec((None, None, HKV_D, HEAD_DIM, PAST_LEN), lambda b, i: (b, idx, 0, 0, 0))
    cast_in, cast_out, cast_shape = _cast_specs(cast_jobs, DEC_BATCH * n_qb, lambda b, i: b * n_qb + i)
    return pl.pallas_call(
        functools.partial(_odd_sample_kernel, n_cast=len(cast_in)),
        grid=(DEC_BATCH, n_qb),
        in_specs=[whole_in(C_POOL),
                  pl.BlockSpec((Q_BLK, HD_Q), lambda b, i: (first_q + b * n_qb + i, 0)),
                  whole_in(HD_KV), whole_in(HD_KV), ctx, ctx,
                  pl.BlockSpec((None, N_POOL, POOL_C, POOL_C), lambda b, i: (idx, 0, 0, 0)),
                  pl.BlockSpec((None, 1, C_POOL), lambda b, i: (idx, 0, 0)),
                  pl.BlockSpec(memory_space=pl.ANY), pl.BlockSpec(memory_space=pl.ANY)] + cast_in,
        out_specs=[pl.BlockSpec((DEC_SEQ, C_POOL), lambda b, i: (first + b, 0)),
                   pl.BlockSpec((Q_BLK, HD_Q), lambda b, i: (first_q + b * n_qb + i, 0))] + cast_out,
        out_shape=[
            jax.ShapeDtypeStruct((N_TOK, C_POOL), BF16),
            jax.ShapeDtypeStruct((N_TOK, HD_Q), BF16),
        ] + cast_shape,
        input_output_aliases={8: 0, 9: 1},
        scratch_shapes=[pltpu.VMEM((LANES, DEC_SEQ), BF16), pltpu.VMEM((DEC_SEQ, 2 * LANES), BF16),
                        pltpu.VMEM((LANES, PAST_LEN), BF16), pltpu.VMEM((PAST_LEN, 2 * LANES), BF16)],
        compiler_params=_params(2),
        name="odd_sample",
    )(uc, q, k, v, ctx_k, ctx_v, pool_w, pool_scale, fc, fd, *([ffn_w1, ffn_w2] * len(cast_jobs)))


def _rope_tables():
    t = np.arange(DEC_SEQ)
    n_freq = HEAD_DIM // 4
    inv = ROPE_BASE ** (-np.arange(n_freq, dtype=np.float64) / n_freq)
    ang = np.concatenate([(t // GRID_W)[:, None] * inv, (t % GRID_W)[:, None] * inv], axis=-1)
    cos = np.repeat(np.cos(ang), 2, axis=-1)
    sin = np.repeat(np.sin(ang), 2, axis=-1) * np.tile([-1.0, 1.0], HEAD_DIM // 2)
    n_heads = HQ_D + HKV_D
    cos = np.concatenate([np.ones((TM, QK_W)), np.tile(cos, (1, n_heads))], axis=0)
    sin = np.concatenate([np.zeros((TM, QK_W)), np.tile(sin, (1, n_heads))], axis=0)
    return jnp.asarray(cos, F32), jnp.asarray(sin, F32)


def _head_ones():
    head = np.arange(MXU_DIM) // HEAD_DIM
    return jnp.asarray(head[:, None] == head[None, :], BF16)


def kernel(x_prompt, x_sample, cache_a_k, cache_a_v, cache_d_k, cache_d_v, c, c_ctx, mod_w, mod_b, norm_w,
           ffn_w1, ffn_w2, ev_w_in, ev_rpb, ev_conv_w, ev_conv_b, ev_w_out, od_w_in, od_pool_w,
           od_pool_scale, od_q_norm, od_k_norm, od_w_out):
    xs = (x_prompt.reshape(N_PROMPT, D_MODEL), x_sample.reshape(N_SAMPLE, D_MODEL))
    cond = jnp.concatenate([c_ctx[None, :], c, jnp.zeros((COND_PAD - N_COND, D_MODEL), F32)], axis=0)
    mods, w1_a, w2_a = _modulation(cond, mod_w, mod_b, ffn_w1, ffn_w2)
    mods = mods.reshape(DEPTH, COND_PAD, N_MOD, D_MODEL)
    cache_a_k, cache_a_v, cache_d_k, cache_d_v = (jnp.swapaxes(t, -1, -2)
                                                  for t in (cache_a_k, cache_a_v, cache_d_k, cache_d_v))
    rpb = jnp.pad(ev_rpb, ((0, 0), (0, 0), (0, 0), (0, LANES - ev_rpb.shape[-1])))
    conv_b = ev_conv_b[:, None, :]
    pool_w, pool_scale = od_pool_w.astype(BF16), od_pool_scale[:, None, :]
    cos_t, sin_t = _rope_tables()
    ones_bd = _head_ones()
    states = []
    for l in range(DEPTH):
        i = l // 2
        last = l == DEPTH - 1
        if l == 0:
            x, ev_w_in_b = _ffn(xs, mods, norm_w, w1_a, w2_a, l, 0, ev_w_in=ev_w_in)
        else:
            x, = _ffn(xs, mods, norm_w, w1_a, w2_a, l, 0)
        cast_jobs = [(l, 1)] + ([] if last else [(l + 1, 0)])
        if l % 2 == 0:
            q, k, v, bg, z, s_k, s_v = _even_in(x, mods, norm_w, ev_w_in_b, l)
            fa, fb = _even_prompt(q, k, v, bg, z, ev_conv_w, conv_b, i)
            fa, fb, *cast = _even_sample(q, k, v, bg, z, cache_a_k, cache_a_v, rpb, ev_conv_w, conv_b, fa, fb, i,
                                         ffn_w1, ffn_w2, cast_jobs, mixer_w=(ev_w_out, od_w_in, od_w_out))
            ev_w_out_b, od_w_in_b, od_w_out_b = cast[2 * len(cast_jobs):]
            w_out = ev_w_out_b
        else:
            gain = jnp.concatenate([jnp.tile(od_q_norm[i], HQ_D), jnp.tile(od_k_norm[i], HKV_D)])[None, :]
            uc, q, k, v, s_k, s_v = _odd_in(x, mods, norm_w, od_w_in_b, gain, cos_t, sin_t, ones_bd, l)
            fa, fb = _odd_prompt(uc, q, k, v, pool_w, pool_scale, i)
            fa, fb, *cast = _odd_sample(uc, q, k, v, cache_d_k, cache_d_v, pool_w, pool_scale, fa, fb, i,
                                        ffn_w1, ffn_w2, cast_jobs)
            w_out = od_w_out_b
        w1_b, w2_b = cast[0:2]
        if not last:
            w1_a, w2_a = cast[2:4]
        states.append((jnp.swapaxes(s_k, -1, -2), jnp.swapaxes(s_v, -1, -2)))
        xs = tuple(_ffn((x,), mods, norm_w, w1_b, w2_b, l, 2, feats=(fa, fb), w_out=w_out,
                        split_out=last))
    y_prompt = xs[0].reshape(BATCH, SEQ, D_MODEL)
    y_sample = xs[1].reshape(DEC_BATCH, DEC_SEQ, D_MODEL)
    return (y_prompt, y_sample, states[0][0], states[0][1], states[1][0], states[1][1])
```

```python
import functools

import jax
import jax.numpy as jnp
import numpy as np
from jax import lax
from jax.experimental import pallas as pl
from jax.experimental.pallas import tpu as pltpu

D_MODEL = 1024
BATCH = 32
SEQ = 256
DEPTH = 2
DEC_BATCH = 2
DEC_SEQ = 2048
PAST_LEN = 256
GRID_W = 64
HEAD_DIM = 64
N_MOD = 9
N_NORM = 6
D_FF = 2816
FFN_RES = 0.5
H_A = 8
NA_ROWS = 8
NA_COLS = 16
C_B = 512
CONV_W = 3
C_POOL = 512
POOL_WINDOWS = (2, 4, 8, 16)
N_POOL = 4
POOL_C = C_POOL // N_POOL
HQ_D = 8
HKV_D = 2
GQA_GROUP = HQ_D // HKV_D
ROPE_BASE = 10000.0
EVEN_IN = 3 * H_A * HEAD_DIM + 3 * C_B
ODD_IN = C_POOL + (HQ_D + 2 * HKV_D) * HEAD_DIM
RMS_EPS = 1e-6
NEG_INF = -1e30
ATT_SCALE = HEAD_DIM ** -0.5

LANES = 128
GRID_ROWS = DEC_SEQ // GRID_W
N_PROMPT = BATCH * SEQ
N_SAMPLE = DEC_BATCH * DEC_SEQ
N_TOK = N_PROMPT + N_SAMPLE
N_COND = 1 + DEC_BATCH
COND_PAD = 8
HD_A = H_A * HEAD_DIM
HD_Q = HQ_D * HEAD_DIM
HD_KV = HKV_D * HEAD_DIM
QK_W = HD_Q + HD_KV
POOL_PAD = 8

TM = 1024
N_TILES = N_TOK // TM
PROMPT_TILES = N_PROMPT // TM
TILES_PER_SAMPLE = DEC_SEQ // TM
SEQ_PER_TILE = TM // SEQ
TM_FFN = 1024
FFN_HALF = TM_FFN // 2
MIX_CAST_STEPS = N_PROMPT // TM_FFN
MIX_CAST_ROWS = D_MODEL // MIX_CAST_STEPS
assert MIX_CAST_ROWS == 2 * HEAD_DIM and (DEPTH + 1) // 2 == 1 and DEPTH // 2 == 1
MXU_DIM = 256
FF_CHUNKS = tuple((lo, min(lo + 3 * MXU_DIM, D_FF)) for lo in range(0, D_FF, 3 * MXU_DIM))
MOD_TN = 2304
Q_BLK = 256
ROW_UNROLL = 16
VMEM_LIMIT = 60 * 1024 * 1024

F32 = jnp.float32
BF16 = jnp.bfloat16


def _params(n_grid):
    return pltpu.CompilerParams(dimension_semantics=("arbitrary",) * n_grid, vmem_limit_bytes=VMEM_LIMIT)


def _cond_of_tile(i, tm=TM):
    n_prompt = N_PROMPT // tm
    return jnp.where(i < n_prompt, 0, 1 + (i - n_prompt) // (DEC_SEQ // tm))


def _prompt_tile(i, tm=TM):
    return jnp.minimum(i, N_PROMPT // tm - 1)


def _sample_tile(i, tm=TM):
    return jnp.maximum(i - N_PROMPT // tm, 0)


def _rms(x, g):
    return x * lax.rsqrt(jnp.mean(x * x, axis=-1, keepdims=True) + RMS_EPS) * g


def _sigmoid(x):
    return 1.0 / (1.0 + jnp.exp(-x))


def _dot(a, b):
    return jnp.dot(a, b, preferred_element_type=F32)


def _dot_t(a, b):
    return lax.dot_general(a, b, (((1,), (1,)), ((), ())), preferred_element_type=F32)


def _mod_kernel(c_ref, w_ref, b_ref, w1f_ref, w2f_ref, o_ref, w1b_ref, w2b_ref):
    _cast_weights([w1f_ref, w2f_ref], [w1b_ref, w2b_ref])
    c = c_ref[...]
    sc = (c * _sigmoid(c)).astype(BF16)
    o_ref[...] = _dot(sc, w_ref[...].astype(BF16)) + b_ref[...]


def _modulation(cond, mod_w, mod_b, ffn_w1, ffn_w2):
    n_col = N_MOD * D_MODEL
    n_blk = n_col // MOD_TN
    cast_in, cast_out, cast_shape = _cast_specs([(0, 0)], DEPTH * n_blk, lambda l, j: l * n_blk + j)
    return pl.pallas_call(
        _mod_kernel,
        grid=(DEPTH, n_blk),
        in_specs=[
            pl.BlockSpec((COND_PAD, D_MODEL), lambda l, j: (0, 0)),
            pl.BlockSpec((None, D_MODEL, MOD_TN), lambda l, j: (l, 0, j)),
            pl.BlockSpec((None, 1, MOD_TN), lambda l, j: (l, 0, j)),
        ] + cast_in,
        out_specs=[pl.BlockSpec((None, COND_PAD, MOD_TN), lambda l, j: (l, 0, j))] + cast_out,
        out_shape=[jax.ShapeDtypeStruct((DEPTH, COND_PAD, n_col), F32)] + cast_shape,
        compiler_params=_params(2),
        name="modulation",
    )(cond, mod_w, mod_b.reshape(DEPTH, 1, n_col), ffn_w1, ffn_w2)


def _mod_spec(layer, tm=TM):
    return pl.BlockSpec((None, None, N_MOD, D_MODEL), lambda i: (layer, _cond_of_tile(i, tm), 0, 0))


def _gain_spec(layer):
    return pl.BlockSpec((None, N_NORM, D_MODEL), lambda i: (layer, 0, 0))


def _cast_specs(jobs, n_steps, step_of):
    r1, r2 = D_MODEL // n_steps, D_FF // n_steps
    in_specs, out_specs, out_shape = [], [], []
    for layer, which in jobs:
        in_specs += [pl.BlockSpec((None, None, r1, 2 * D_FF), lambda *g, lw=(layer, which): (*lw, step_of(*g), 0)),
                     pl.BlockSpec((None, None, r2, D_MODEL), lambda *g, lw=(layer, which): (*lw, step_of(*g), 0))]
        out_specs += [pl.BlockSpec((r1, 2 * D_FF), lambda *g: (step_of(*g), 0)),
                      pl.BlockSpec((r2, D_MODEL), lambda *g: (step_of(*g), 0))]
        out_shape += [jax.ShapeDtypeStruct((D_MODEL, 2 * D_FF), BF16), jax.ShapeDtypeStruct((D_FF, D_MODEL), BF16)]
    return in_specs, out_specs, out_shape


def _cast_weights(f32_refs, bf16_refs):
    for src, dst in zip(f32_refs, bf16_refs):
        dst[...] = src[...].astype(BF16)


def _ev_in_cast_specs():
    step = lambda i: jnp.minimum(i, MIX_CAST_STEPS - 1)
    return ([pl.BlockSpec((None, MIX_CAST_ROWS, EVEN_IN), lambda i: (0, step(i), 0))],
            [pl.BlockSpec((MIX_CAST_ROWS, EVEN_IN), lambda i: (step(i), 0))],
            [jax.ShapeDtypeStruct((D_MODEL, EVEN_IN), BF16)])


def _mixer_cast_specs(step_of):
    rows = MIX_CAST_ROWS
    n_pool_blk = C_POOL // rows

    def od_out_block(s, g):
        return jnp.where(s < n_pool_blk, 2 * s + g, C_POOL // HEAD_DIM + g * GQA_GROUP + s - n_pool_blk)

    widths = (D_MODEL, ODD_IN, D_MODEL)
    in_specs = [pl.BlockSpec((None, rows, D_MODEL), lambda *g: (0, step_of(*g), 0)),
                pl.BlockSpec((None, rows, ODD_IN), lambda *g: (0, step_of(*g), 0)),
                pl.BlockSpec((None, HEAD_DIM, D_MODEL), lambda *g: (0, od_out_block(step_of(*g), 0), 0)),
                pl.BlockSpec((None, HEAD_DIM, D_MODEL), lambda *g: (0, od_out_block(step_of(*g), 1), 0))]
    out_specs = [pl.BlockSpec((rows, w), lambda *g: (step_of(*g), 0)) for w in widths]
    out_shape = [jax.ShapeDtypeStruct((D_MODEL, w), BF16) for w in widths]
    return in_specs, out_specs, out_shape


def _cast_mixer_weights(ev_out_f, od_in_f, od_out_lo_f, od_out_hi_f, ev_out_b, od_in_b, od_out_b):
    ev_out_b[...] = ev_out_f[...].astype(BF16)
    w = od_in_f[...]
    q = w[:, C_POOL:C_POOL + HD_Q]
    heads = [q[:, (g * GQA_GROUP + j) * HEAD_DIM:(g * GQA_GROUP + j + 1) * HEAD_DIM]
             for j in range(GQA_GROUP) for g in range(HKV_D)]
    od_in_b[...] = jnp.concatenate([w[:, 0:C_POOL]] + heads + [w[:, C_POOL + HD_Q:]], axis=1).astype(BF16)
    od_out_b[...] = jnp.concatenate([od_out_lo_f[...], od_out_hi_f[...]], axis=0).astype(BF16)


def _ffn_kernel(*refs, sub, split_in, mix, split_out, cast_ev_in):
    refs = list(refs)
    x_refs = [refs.pop(0) for _ in range(2 if split_in else 1)]
    feat_refs = [refs.pop(0) for _ in range(2 if mix else 0)]
    m_ref, g_ref = refs.pop(0), refs.pop(0)
    wo_ref = refs.pop(0) if mix else None
    w1_ref, w2_ref = refs.pop(0), refs.pop(0)
    if cast_ev_in:
        @pl.when(pl.program_id(0) < MIX_CAST_STEPS)
        def _():
            refs[-1][...] = refs[0][...].astype(BF16)
        refs = refs[1:-1]
    out_refs = refs
    is_prompt = pl.program_id(0) < N_PROMPT // TM_FFN
    shift = m_ref[3 * sub:3 * sub + 1, :]
    scale = m_ref[3 * sub + 1:3 * sub + 2, :]
    gate = m_ref[3 * sub + 2:3 * sub + 3, :]
    halves = [slice(hf * FFN_HALF, (hf + 1) * FFN_HALF) for hf in range(TM_FFN // FFN_HALF)]
    mixed, ys = [], []
    if mix:
        for rows in halves:
            feat = jnp.concatenate([feat_refs[0][rows, :], feat_refs[1][rows, :]], axis=1)
            mixed.append(_dot(feat, wo_ref[...]))
    for half, rows in enumerate(halves):
        if split_in:
            x = jnp.where(is_prompt, x_refs[0][rows, :], x_refs[1][rows, :])
        else:
            x = x_refs[0][rows, :]
        if mix:
            x = x + m_ref[5:6, :] * _rms(mixed[half], g_ref[3:4, :])
        h = (_rms(x, g_ref[2 * sub:2 * sub + 1, :]) * (1.0 + scale) + shift).astype(BF16)
        acc = jnp.zeros((FFN_HALF, D_MODEL), F32)
        for lo, hi in FF_CHUNKS:
            a = _dot(h, w1_ref[:, lo:hi])
            u = _dot(h, w1_ref[:, D_FF + lo:D_FF + hi])
            act = (a * _sigmoid(a) * u).astype(BF16)
            acc = acc + _dot(act, w2_ref[lo:hi, :])
        y = x + FFN_RES * gate * _rms(acc, g_ref[2 * sub + 1:2 * sub + 2, :])
        if split_out:
            ys.append(y)
        else:
            out_refs[0][rows, :] = y
    if split_out:
        @pl.when(is_prompt)
        def _():
            for rows, y in zip(halves, ys):
                out_refs[0][rows, :] = y

        @pl.when(jnp.logical_not(is_prompt))
        def _():
            for rows, y in zip(halves, ys):
                out_refs[1][rows, :] = y


def _ffn(xs, mods, gains, w1, w2, layer, sub, feats=None, w_out=None, split_out=False, ev_w_in=None):
    split_in = len(xs) == 2
    mix = feats is not None
    cast_ev_in = ev_w_in is not None
    resident = pl.Buffered(1)
    half_w = D_MODEL // 2
    feat_specs, mix_w_spec = [], []
    if mix:
        feat_specs = [pl.BlockSpec((TM_FFN, half_w), lambda i: (i, 0))] * 2
        mix_w_spec = [pl.BlockSpec((D_MODEL, D_MODEL), lambda i: (0, 0), pipeline_mode=resident)]
    cast_in, cast_out, cast_shape, cast_args = [], [], [], ()
    if cast_ev_in:
        cast_in, cast_out, cast_shape = _ev_in_cast_specs()
        cast_args = (ev_w_in,)
    tok = pl.BlockSpec((TM_FFN, D_MODEL), lambda i: (i, 0))
    prompt_tok = pl.BlockSpec((TM_FFN, D_MODEL), lambda i: (_prompt_tile(i, TM_FFN), 0))
    sample_tok = pl.BlockSpec((TM_FFN, D_MODEL), lambda i: (_sample_tile(i, TM_FFN), 0))
    if split_out:
        out_specs = [prompt_tok, sample_tok]
        out_shape = [jax.ShapeDtypeStruct((N_PROMPT, D_MODEL), F32),
                     jax.ShapeDtypeStruct((N_SAMPLE, D_MODEL), F32)]
    else:
        out_specs = [tok]
        out_shape = [jax.ShapeDtypeStruct((N_TOK, D_MODEL), F32)]
    return pl.pallas_call(
        functools.partial(_ffn_kernel, sub=sub, split_in=split_in, mix=mix, split_out=split_out,
                          cast_ev_in=cast_ev_in),
        grid=(N_TOK // TM_FFN,),
        in_specs=([prompt_tok, sample_tok] if split_in else [tok]) + feat_specs + [
            _mod_spec(layer, TM_FFN),
            _gain_spec(layer),
        ] + mix_w_spec + [
            pl.BlockSpec((D_MODEL, 2 * D_FF), lambda i: (0, 0), pipeline_mode=resident),
            pl.BlockSpec((D_FF, D_MODEL), lambda i: (0, 0), pipeline_mode=resident),
        ] + cast_in,
        out_specs=out_specs + cast_out,
        out_shape=out_shape + cast_shape,
        compiler_params=_params(1),
        name=f"ffn{sub}",
    )(*xs, *(feats or ()), mods, gains, *((w_out,) if mix else ()), w1, w2, *cast_args)


def _mixer_norm(x, m_ref, g_ref):
    return (_rms(x, g_ref[2:3, :]) * (1.0 + m_ref[4:5, :]) + m_ref[3:4, :]).astype(BF16)


def _store_heads(state_ref, b, x, n_heads):
    xt = x.T
    for h in range(n_heads):
        state_ref[b, 0, h] = xt[h * HEAD_DIM:(h + 1) * HEAD_DIM, :]


def _even_in_kernel(x_ref, m_ref, g_ref, w_ref, q_ref, k_ref, v_ref, bg_ref, z_ref, sk_ref, sv_ref):
    kv = []
    for b in range(SEQ_PER_TILE):
        rows = slice(b * SEQ, (b + 1) * SEQ)
        u = _dot(_mixer_norm(x_ref[rows, :], m_ref, g_ref), w_ref[...])
        k = u[:, HD_A:2 * HD_A]
        v = u[:, 2 * HD_A:3 * HD_A]
        q_ref[rows, :] = (u[:, 0:HD_A] * ATT_SCALE).astype(BF16)
        k_ref[rows, :] = k.astype(BF16)
        v_ref[rows, :] = v.astype(BF16)
        bg_ref[rows, :] = u[:, 3 * HD_A:3 * HD_A + C_B]
        z_ref[rows, :] = u[:, 3 * HD_A + C_B:3 * HD_A + 2 * C_B] * u[:, 3 * HD_A + 2 * C_B:3 * HD_A + 3 * C_B]
        kv.append((k, v))

    @pl.when(pl.program_id(0) < PROMPT_TILES)
    def _():
        for b, (k, v) in enumerate(kv):
            _store_heads(sk_ref, b, k, H_A)
            _store_heads(sv_ref, b, v, H_A)


def _state_spec(n_heads):
    return pl.BlockSpec((SEQ_PER_TILE, 1, n_heads, HEAD_DIM, SEQ), lambda i: (_prompt_tile(i), 0, 0, 0, 0))


def _even_in(x, mods, gains, w_in, layer):
    tok = lambda w: pl.BlockSpec((TM, w), lambda i: (i, 0))
    state = jax.ShapeDtypeStruct((BATCH, 1, H_A, HEAD_DIM, SEQ), F32)
    return pl.pallas_call(
        _even_in_kernel,
        grid=(N_TILES,),
        in_specs=[
            tok(D_MODEL), _mod_spec(layer), _gain_spec(layer),
            pl.BlockSpec((D_MODEL, EVEN_IN), lambda i: (0, 0)),
        ],
        out_specs=[tok(HD_A), tok(HD_A), tok(HD_A), tok(C_B), tok(C_B), _state_spec(H_A), _state_spec(H_A)],
        out_shape=[
            jax.ShapeDtypeStruct((N_TOK, HD_A), BF16),
            jax.ShapeDtypeStruct((N_TOK, HD_A), BF16),
            jax.ShapeDtypeStruct((N_TOK, HD_A), BF16),
            jax.ShapeDtypeStruct((N_TOK, C_B), F32),
            jax.ShapeDtypeStruct((N_TOK, C_B), F32),
            state, state,
        ],
        compiler_params=_params(1),
        name="even_in",
    )(x, mods, gains, w_in)


def _swap_pairs(x):
    n = x.shape[-1]
    lane = lax.broadcasted_iota(jnp.int32, x.shape, x.ndim - 1)
    return jnp.where(lane % 2 == 0, pltpu.roll(x, n - 1, x.ndim - 1), pltpu.roll(x, 1, x.ndim - 1))


def _odd_in_kernel(x_ref, m_ref, g_ref, w_ref, ng_ref, cos_ref, sin_ref, ones_ref,
                   uc_ref, q_ref, k_ref, v_ref, sk_ref, sv_ref):
    ones = ones_ref[...]
    kv = []
    for b in range(SEQ_PER_TILE):
        rows = slice(b * SEQ, (b + 1) * SEQ)
        u = _dot(_mixer_norm(x_ref[rows, :], m_ref, g_ref), w_ref[...])
        uc_ref[rows, :] = u[:, 0:C_POOL]
        qk = u[:, C_POOL:C_POOL + QK_W]
        sq = qk * qk
        hi = sq.astype(BF16)
        lo = (sq - hi.astype(F32)).astype(BF16)
        sums = []
        for c0 in range(0, QK_W, MXU_DIM):
            c1 = min(c0 + MXU_DIM, QK_W)
            tile = ones[0:c1 - c0, 0:c1 - c0]
            sums.append(_dot(hi[:, c0:c1], tile) + _dot(lo[:, c0:c1], tile))
        ms = jnp.concatenate(sums, axis=-1) * (1.0 / HEAD_DIM)
        n = qk * lax.rsqrt(ms + RMS_EPS) * ng_ref[...]
        r = n * cos_ref[rows, :] + _swap_pairs(n) * sin_ref[rows, :]
        k = r[:, HD_Q:QK_W]
        v = u[:, C_POOL + QK_W:ODD_IN]
        q_ref[rows, :] = (r[:, 0:HD_Q] * ATT_SCALE).astype(BF16)
        k_ref[rows, :] = k.astype(BF16)
        v_ref[rows, :] = v.astype(BF16)
        kv.append((k, v))

    @pl.when(pl.program_id(0) < PROMPT_TILES)
    def _():
        for b, (k, v) in enumerate(kv):
            _store_heads(sk_ref, b, k, HKV_D)
            _store_heads(sv_ref, b, v, HKV_D)


def _rope_tile_index(i):
    return jnp.where(i < PROMPT_TILES, 0, 1 + (i - PROMPT_TILES) % TILES_PER_SAMPLE)


def _odd_in(x, mods, gains, w_in, qk_gain, cos_t, sin_t, ones_bd, layer):
    tok = lambda w: pl.BlockSpec((TM, w), lambda i: (i, 0))
    state = jax.ShapeDtypeStruct((BATCH, 1, HKV_D, HEAD_DIM, SEQ), F32)
    return pl.pallas_call(
        _odd_in_kernel,
        grid=(N_TILES,),
        in_specs=[
            tok(D_MODEL), _mod_spec(layer), _gain_spec(layer),
            pl.BlockSpec((D_MODEL, ODD_IN), lambda i: (0, 0)),
            pl.BlockSpec((1, QK_W), lambda i: (0, 0)),
            pl.BlockSpec((TM, QK_W), lambda i: (_rope_tile_index(i), 0)),
            pl.BlockSpec((TM, QK_W), lambda i: (_rope_tile_index(i), 0)),
            pl.BlockSpec((MXU_DIM, MXU_DIM), lambda i: (0, 0)),
        ],
        out_specs=[tok(C_POOL), tok(HD_Q), tok(HD_KV), tok(HD_KV), _state_spec(HKV_D), _state_spec(HKV_D)],
        out_shape=[
            jax.ShapeDtypeStruct((N_TOK, C_POOL), F32),
            jax.ShapeDtypeStruct((N_TOK, HD_Q), BF16),
            jax.ShapeDtypeStruct((N_TOK, HD_KV), BF16),
            jax.ShapeDtypeStruct((N_TOK, HD_KV), BF16),
            state, state,
        ],
        compiler_params=_params(1),
        name="odd_in",
    )(x, mods, gains, w_in, qk_gain, cos_t, sin_t, ones_bd)


def _low_half(rows):
    return lax.broadcasted_iota(jnp.int32, (rows, LANES), 1) < HEAD_DIM


def _split_pair(qp):
    low = _low_half(qp.shape[0])
    zero = jnp.zeros_like(qp)
    return jnp.concatenate([jnp.where(low, qp, zero), jnp.where(low, zero, qp)], axis=0)


def _pair_heads_t(ref):
    return jnp.concatenate([ref[0], ref[1]], axis=0)


def _with_ones(v):
    return jnp.concatenate([v, jnp.ones_like(v)], axis=1)


def _softmax_pv(scores, values):
    m = scores[0].max(axis=-1, keepdims=True)
    for s in scores[1:]:
        m = jnp.maximum(m, s.max(axis=-1, keepdims=True))
    acc = None
    for s, v in zip(scores, values):
        o = _dot(jnp.exp(s - m).astype(BF16), v)
        acc = o if acc is None else acc + o
    return acc[:, 0:LANES] / acc[:, LANES:2 * LANES]


def _merge_pair(o):
    m = o.shape[0] // 2
    return jnp.where(_low_half(m), o[0:m], o[m:2 * m])


def _short_conv(z, cw_ref, cb_ref, seq_len):
    rows = z.shape[0]
    pos = lax.broadcasted_iota(jnp.int32, z.shape, 0) & (seq_len - 1)
    z_prev = jnp.where(pos == 0, 0.0, pltpu.roll(z, 1, 0))
    z_next = jnp.where(pos == seq_len - 1, 0.0, pltpu.roll(z, rows - 1, 0))
    y = z_prev * cw_ref[0:1, :]
    y = y + z * cw_ref[1:2, :]
    y = y + z_next * cw_ref[2:3, :]
    return y + cb_ref[...]


def _pool_mix(uc, pw_ref, ps_ref, seq_len):
    pos = lax.broadcasted_iota(jnp.int32, (seq_len, POOL_C), 0)
    pad = jnp.zeros((POOL_PAD, POOL_C), F32)
    n_ext = seq_len + 2 * POOL_PAD
    outs = []
    for gi, win in enumerate(POOL_WINDOWS):
        ug = uc[:, gi * POOL_C:(gi + 1) * POOL_C]
        run = jnp.concatenate([pad, ug, pad], axis=0)
        span = 1
        while span < win:
            run = run + pltpu.roll(run, span, 0)
            span *= 2
        back = win // 2 - 1
        if back:
            run = pltpu.roll(run, n_ext - back, 0)
        wsum = run[POOL_PAD:POOL_PAD + seq_len, :]
        cnt = jnp.minimum(pos + (win - win // 2), seq_len) - jnp.maximum(pos - win // 2, 0)
        pooled = (wsum / cnt.astype(F32) - ug).astype(BF16)
        outs.append(_dot(pooled, pw_ref[gi]))
    return jnp.concatenate(outs, axis=-1) * ps_ref[...]


def _even_prompt_kernel(q_ref, k_ref, v_ref, bg_ref, z_ref, cw_ref, cb_ref, fa_ref, fb_ref):
    for b in range(SEQ_PER_TILE):
        rows = slice(b * SEQ, (b + 1) * SEQ)
        pairs = [slice(p * LANES, (p + 1) * LANES) for p in range(H_A // 2)]
        scores = [_dot_t(_split_pair(q_ref[rows, sl]), k_ref[rows, sl]) for sl in pairs]
        outs = [_merge_pair(_softmax_pv([s], [_with_ones(v_ref[rows, sl])])) for s, sl in zip(scores, pairs)]
        fa_ref[rows, :] = jnp.concatenate(outs, axis=-1).astype(BF16)
    fb_ref[...] = (bg_ref[...] * _short_conv(z_ref[...], cw_ref, cb_ref, SEQ)).astype(BF16)


def _even_prompt(q, k, v, bg, z, conv_w, conv_b, idx):
    tile = lambda w: pl.BlockSpec((TM, w), lambda t: (t, 0))
    return pl.pallas_call(
        _even_prompt_kernel,
        grid=(PROMPT_TILES,),
        in_specs=[tile(HD_A), tile(HD_A), tile(HD_A), tile(C_B), tile(C_B),
                  pl.BlockSpec((None, CONV_W, C_B), lambda t: (idx, 0, 0)),
                  pl.BlockSpec((None, 1, C_B), lambda t: (idx, 0, 0))],
        out_specs=[tile(HD_A), tile(C_B)],
        out_shape=[
            jax.ShapeDtypeStruct((N_TOK, HD_A), BF16),
            jax.ShapeDtypeStruct((N_TOK, C_B), BF16),
        ],
        compiler_params=_params(1),
        name="even_prompt",
    )(q, k, v, bg, z, conv_w, conv_b)


def _na_bias_tiles(rpb_h):
    c = lax.broadcasted_iota(jnp.int32, (GRID_W, GRID_W), 0)
    kc = lax.broadcasted_iota(jnp.int32, (GRID_W, GRID_W), 1)
    cs = jnp.clip(c - NA_COLS // 2, 0, GRID_W - NA_COLS)
    valid = (kc >= cs) & (kc < cs + NA_COLS)
    tiles = []
    for ro in range(2 * NA_ROWS - 1):
        g = jnp.broadcast_to(rpb_h[ro:ro + 1, :], (GRID_W, LANES))
        skew = pltpu.roll(g, LANES - (NA_COLS - 1), 1, stride=1, stride_axis=0)
        tiles.append(jnp.where(valid, skew[:, 0:GRID_W], NEG_INF))
    return tiles


def _even_sample_kernel(*refs, n_cast, cast_mixers):
    n_mix_in, n_mix_out = (4, 3) if cast_mixers else (0, 0)
    q_ref, k_ref, v_ref, bg_ref, z_ref, ck_ref, cv_ref, rpb_ref, cw_ref, cb_ref = refs[0:10]
    cast0 = 12
    out0 = cast0 + n_cast + n_mix_in
    fa_ref, fb_ref = refs[out0:out0 + 2]
    bias_ref = refs[out0 + 2 + n_cast + n_mix_out]
    _cast_weights(refs[cast0:cast0 + n_cast], refs[out0 + 2:out0 + 2 + n_cast])
    if cast_mixers:
        _cast_mixer_weights(*refs[cast0 + n_cast:out0], *refs[out0 + 2 + n_cast:out0 + 2 + n_cast + n_mix_out])
    win = NA_ROWS * GRID_W
    for h in range(2):
        tiles = _na_bias_tiles(rpb_ref[h])
        for var in range(NA_ROWS):
            bias_ref[var, h * GRID_W:(h + 1) * GRID_W, :] = jnp.concatenate(
                [tiles[i - var + NA_ROWS - 1] for i in range(NA_ROWS)], axis=1)
    ck_t = _pair_heads_t(ck_ref).astype(BF16)
    cv_ext = _with_ones(_pair_heads_t(cv_ref).T.astype(BF16))

    def rows(it, carry):
        staged = []
        for j in range(ROW_UNROLL):
            r = it * ROW_UNROLL + j
            rs = jnp.clip(r - NA_ROWS // 2, 0, GRID_ROWS - NA_ROWS)
            q0 = pl.multiple_of(r * GRID_W, GRID_W)
            k0 = pl.multiple_of(rs * GRID_W, GRID_W)
            q2 = _split_pair(q_ref[pl.ds(q0, GRID_W), :])
            s_loc = _dot_t(q2, k_ref[pl.ds(k0, win), :]) + bias_ref[r - rs]
            s_ctx = _dot(q2, ck_t)
            staged.append((q0, k0, s_loc, s_ctx))
        for q0, k0, s_loc, s_ctx in staged:
            o = _softmax_pv([s_loc, s_ctx], [_with_ones(v_ref[pl.ds(k0, win), :]), cv_ext])
            fa_ref[pl.ds(q0, GRID_W), :] = _merge_pair(o).astype(BF16)
        return carry

    lax.fori_loop(0, GRID_ROWS // ROW_UNROLL, rows, 0)
    fb_ref[...] = (bg_ref[...] * _short_conv(z_ref[...], cw_ref, cb_ref, DEC_SEQ)).astype(BF16)


def _even_sample(q, k, v, bg, z, ctx_k, ctx_v, rpb, conv_w, conv_b, fa, fb, idx, ffn_w1, ffn_w2, cast_jobs,
                 mixer_w=None):
    first = N_PROMPT // DEC_SEQ
    n_pair = H_A // 2
    n_steps = DEC_BATCH * n_pair
    step_of = lambda b, p: b * n_pair + p
    seq_in = pl.BlockSpec((DEC_SEQ, LANES), lambda b, p: (first + b, p))
    anywhere = pl.BlockSpec(memory_space=pl.ANY)
    ctx = pl.BlockSpec((None, None, 2, HEAD_DIM, PAST_LEN), lambda b, p: (b, idx, p, 0, 0))
    cast_in, cast_out, cast_shape = _cast_specs(cast_jobs, n_steps, step_of)
    n_cast = len(cast_in)
    cast_args = [ffn_w1, ffn_w2] * len(cast_jobs)
    if mixer_w is not None:
        assert n_steps == MIX_CAST_STEPS
        mix_in, mix_out, mix_shape = _mixer_cast_specs(step_of)
        cast_in, cast_out, cast_shape = cast_in + mix_in, cast_out + mix_out, cast_shape + mix_shape
        cast_args += [mixer_w[0], mixer_w[1], mixer_w[2], mixer_w[2]]
    return pl.pallas_call(
        functools.partial(_even_sample_kernel, n_cast=n_cast, cast_mixers=mixer_w is not None),
        grid=(DEC_BATCH, n_pair),
        in_specs=[seq_in, seq_in, seq_in, seq_in, seq_in, ctx, ctx,
                  pl.BlockSpec((None, 2, 2 * NA_ROWS - 1, LANES), lambda b, p: (idx, p, 0, 0)),
                  pl.BlockSpec((None, CONV_W, LANES), lambda b, p: (idx, 0, p)),
                  pl.BlockSpec((None, 1, LANES), lambda b, p: (idx, 0, p)), anywhere, anywhere] + cast_in,
        out_specs=[seq_in, seq_in] + cast_out,
        out_shape=[
            jax.ShapeDtypeStruct((N_TOK, HD_A), BF16),
            jax.ShapeDtypeStruct((N_TOK, C_B), BF16),
        ] + cast_shape,
        input_output_aliases={10: 0, 11: 1},
        scratch_shapes=[pltpu.VMEM((NA_ROWS, 2 * GRID_W, NA_ROWS * GRID_W), F32)],
        compiler_params=_params(2),
        name="even_sample",
    )(q, k, v, bg, z, ctx_k, ctx_v, rpb, conv_w, conv_b, fa, fb, *cast_args)


def _split_groups(q):
    low = _low_half(q.shape[0])
    zero = jnp.zeros((q.shape[0], LANES), BF16)
    pairs = [q[:, j * LANES:(j + 1) * LANES] for j in range(GQA_GROUP)]
    return jnp.concatenate([jnp.where(low, p, zero) for p in pairs] + [jnp.where(low, zero, p) for p in pairs],
                           axis=0)


def _merge_groups(o, m):
    low = _low_half(m)
    half = GQA_GROUP * m
    return jnp.concatenate([jnp.where(low, o[j * m:(j + 1) * m], o[half + j * m:half + (j + 1) * m])
                            for j in range(GQA_GROUP)], axis=-1)


def _odd_prompt_kernel(uc_ref, q_ref, k_ref, v_ref, pw_ref, ps_ref, fc_ref, fd_ref):
    seqs = [slice(b * SEQ, (b + 1) * SEQ) for b in range(SEQ_PER_TILE)]
    scores = [_dot_t(_split_groups(q_ref[rows, :]), k_ref[rows, :]) for rows in seqs]
    for s, rows in zip(scores, seqs):
        o = _softmax_pv([s], [_with_ones(v_ref[rows, :])])
        fd_ref[rows, :] = _merge_groups(o, SEQ).astype(BF16)
    for rows in seqs:
        fc_ref[rows, :] = _pool_mix(uc_ref[rows, :], pw_ref, ps_ref, SEQ).astype(BF16)


def _odd_prompt(uc, q, k, v, pool_w, pool_scale, idx):
    seq = lambda w: pl.BlockSpec((TM, w), lambda t: (t, 0))
    return pl.pallas_call(
        _odd_prompt_kernel,
        grid=(PROMPT_TILES,),
        in_specs=[seq(C_POOL), seq(HD_Q), seq(HD_KV), seq(HD_KV),
                  pl.BlockSpec((None, N_POOL, POOL_C, POOL_C), lambda t: (idx, 0, 0, 0)),
                  pl.BlockSpec((None, 1, C_POOL), lambda t: (idx, 0, 0))],
        out_specs=[seq(C_POOL), seq(HD_Q)],
        out_shape=[
            jax.ShapeDtypeStruct((N_TOK, C_POOL), BF16),
            jax.ShapeDtypeStruct((N_TOK, HD_Q), BF16),
        ],
        compiler_params=_params(1),
        name="odd_prompt",
    )(uc, q, k, v, pool_w, pool_scale)


def _odd_sample_kernel(*refs, n_cast):
    uc_ref, q_ref, k_ref, v_ref, ck_ref, cv_ref, pw_ref, ps_ref = refs[0:8]
    fc_ref, fd_ref = refs[10 + n_cast:12 + n_cast]
    kt_ref, vext_ref, ckpair_ref, cvext_ref = refs[12 + 2 * n_cast:]
    _cast_weights(refs[10:10 + n_cast], refs[12 + n_cast:12 + 2 * n_cast])

    @pl.when(pl.program_id(1) == 0)
    def _():
        fc_ref[...] = _pool_mix(uc_ref[...], pw_ref, ps_ref, DEC_SEQ).astype(BF16)
        kt_ref[...] = k_ref[...].astype(F32).T.astype(BF16)
        vext_ref[...] = _with_ones(v_ref[...])
        ckpair_ref[...] = _pair_heads_t(ck_ref).astype(BF16)
        cvext_ref[...] = _with_ones(_pair_heads_t(cv_ref).T.astype(BF16))

    k_t = kt_ref[...]
    ck_t = ckpair_ref[...]
    low = _low_half(Q_BLK)
    scores = []
    for j in range(GQA_GROUP):
        q2 = _split_pair(q_ref[:, j * LANES:(j + 1) * LANES])
        scores.append([_dot(q2, k_t), _dot(q2, ck_t)])
    outs = []
    for s in scores:
        o = _softmax_pv(s, [vext_ref[...], cvext_ref[...]])
        outs.append(jnp.where(low, o[0:Q_BLK], o[Q_BLK:2 * Q_BLK]))
    fd_ref[...] = jnp.concatenate(outs, axis=-1).astype(BF16)


def _odd_sample(uc, q, k, v, ctx_k, ctx_v, pool_w, pool_scale, fc, fd, idx, ffn_w1, ffn_w2, cast_jobs):
    first = N_PROMPT // DEC_SEQ
    n_qb = DEC_SEQ // Q_BLK
    first_q = N_PROMPT // Q_BLK
    whole_in = lambda w: pl.BlockSpec((DEC_SEQ, w), lambda b, i: (first + b, 0))
    ctx = pl.BlockSpec((None, None, HKV_D, HEAD_DIM, PAST_LEN), lambda b, i: (b, idx, 0, 0, 0))
    cast_in, cast_out, cast_shape = _cast_specs(cast_jobs, DEC_BATCH * n_qb, lambda b, i: b * n_qb + i)
    return pl.pallas_call(
        functools.partial(_odd_sample_kernel, n_cast=len(cast_in)),
        grid=(DEC_BATCH, n_qb),
        in_specs=[whole_in(C_POOL),
                  pl.BlockSpec((Q_BLK, HD_Q), lambda b, i: (first_q + b * n_qb + i, 0)),
                  whole_in(HD_KV), whole_in(HD_KV), ctx, ctx,
                  pl.BlockSpec((None, N_POOL, POOL_C, POOL_C), lambda b, i: (idx, 0, 0, 0)),
                  pl.BlockSpec((None, 1, C_POOL), lambda b, i: (idx, 0, 0)),
                  pl.BlockSpec(memory_space=pl.ANY), pl.BlockSpec(memory_space=pl.ANY)] + cast_in,
        out_specs=[pl.BlockSpec((DEC_SEQ, C_POOL), lambda b, i: (first + b, 0)),
                   pl.BlockSpec((Q_BLK, HD_Q), lambda b, i: (first_q + b * n_qb + i, 0))] + cast_out,
        out_shape=[
            jax.ShapeDtypeStruct((N_TOK, C_POOL), BF16),
            jax.ShapeDtypeStruct((N_TOK, HD_Q), BF16),
        ] + cast_shape,
        input_output_aliases={8: 0, 9: 1},
        scratch_shapes=[pltpu.VMEM((LANES, DEC_SEQ), BF16), pltpu.VMEM((DEC_SEQ, 2 * LANES), BF16),
                        pltpu.VMEM((LANES, PAST_LEN), BF16), pltpu.VMEM((PAST_LEN, 2 * LANES), BF16)],
        compiler_params=_params(2),
        name="odd_sample",
    )(uc, q, k, v, ctx_k, ctx_v, pool_w, pool_scale, fc, fd, *([ffn_w1, ffn_w2] * len(cast_jobs)))


def _rope_tables():
    t = np.arange(DEC_SEQ)
    n_freq = HEAD_DIM // 4
    inv = ROPE_BASE ** (-np.arange(n_freq, dtype=np.float64) / n_freq)
    ang = np.concatenate([(t // GRID_W)[:, None] * inv, (t % GRID_W)[:, None] * inv], axis=-1)
    cos = np.repeat(np.cos(ang), 2, axis=-1)
    sin = np.repeat(np.sin(ang), 2, axis=-1) * np.tile([-1.0, 1.0], HEAD_DIM // 2)
    n_heads = HQ_D + HKV_D
    cos = np.concatenate([np.ones((TM, QK_W)), np.tile(cos, (1, n_heads))], axis=0)
    sin = np.concatenate([np.zeros((TM, QK_W)), np.tile(sin, (1, n_heads))], axis=0)
    return jnp.asarray(cos, F32), jnp.asarray(sin, F32)


def _head_ones():
    head = np.arange(MXU_DIM) // HEAD_DIM
    return jnp.asarray(head[:, None] == head[None, :], BF16)


def kernel(x_prompt, x_sample, cache_a_k, cache_a_v, cache_d_k, cache_d_v, c, c_ctx, mod_w, mod_b, norm_w,
           ffn_w1, ffn_w2, ev_w_in, ev_rpb, ev_conv_w, ev_conv_b, ev_w_out, od_w_in, od_pool_w,
           od_pool_scale, od_q_norm, od_k_norm, od_w_out):
    xs = (x_prompt.reshape(N_PROMPT, D_MODEL), x_sample.reshape(N_SAMPLE, D_MODEL))
    cond = jnp.concatenate([c_ctx[None, :], c, jnp.zeros((COND_PAD - N_COND, D_MODEL), F32)], axis=0)
    mods, w1_a, w2_a = _modulation(cond, mod_w, mod_b, ffn_w1, ffn_w2)
    mods = mods.reshape(DEPTH, COND_PAD, N_MOD, D_MODEL)
    cache_a_k, cache_a_v, cache_d_k, cache_d_v = (jnp.swapaxes(t, -1, -2)
                                                  for t in (cache_a_k, cache_a_v, cache_d_k, cache_d_v))
    rpb = jnp.pad(ev_rpb, ((0, 0), (0, 0), (0, 0), (0, LANES - ev_rpb.shape[-1])))
    conv_b = ev_conv_b[:, None, :]
    pool_w, pool_scale = od_pool_w.astype(BF16), od_pool_scale[:, None, :]
    cos_t, sin_t = _rope_tables()
    ones_bd = _head_ones()
    states = []
    for l in range(DEPTH):
        i = l // 2
        last = l == DEPTH - 1
        if l == 0:
            x, ev_w_in_b = _ffn(xs, mods, norm_w, w1_a, w2_a, l, 0, ev_w_in=ev_w_in)
        else:
            x, = _ffn(xs, mods, norm_w, w1_a, w2_a, l, 0)
        cast_jobs = [(l, 1)] + ([] if last else [(l + 1, 0)])
        if l % 2 == 0:
            q, k, v, bg, z, s_k, s_v = _even_in(x, mods, norm_w, ev_w_in_b, l)
            fa, fb = _even_prompt(q, k, v, bg, z, ev_conv_w, conv_b, i)
            fa, fb, *cast = _even_sample(q, k, v, bg, z, cache_a_k, cache_a_v, rpb, ev_conv_w, conv_b, fa, fb, i,
                                         ffn_w1, ffn_w2, cast_jobs, mixer_w=(ev_w_out, od_w_in, od_w_out))
            ev_w_out_b, od_w_in_b, od_w_out_b = cast[2 * len(cast_jobs):]
            w_out = ev_w_out_b
        else:
            gain = jnp.concatenate([jnp.tile(od_q_norm[i], HQ_D), jnp.tile(od_k_norm[i], HKV_D)])[None, :]
            uc, q, k, v, s_k, s_v = _odd_in(x, mods, norm_w, od_w_in_b, gain, cos_t, sin_t, ones_bd, l)
            fa, fb = _odd_prompt(uc, q, k, v, pool_w, pool_scale, i)
            fa, fb, *cast = _odd_sample(uc, q, k, v, cache_d_k, cache_d_v, pool_w, pool_scale, fa, fb, i,
                                        ffn_w1, ffn_w2, cast_jobs)
            w_out = od_w_out_b
        w1_b, w2_b = cast[0:2]
        if not last:
            w1_a, w2_a = cast[2:4]
        states.append((jnp.swapaxes(s_k, -1, -2), jnp.swapaxes(s_v, -1, -2)))
        xs = tuple(_ffn((x,), mods, norm_w, w1_b, w2_b, l, 2, feats=(fa, fb), w_out=w_out,
                        split_out=last))
    y_prompt = xs[0].reshape(BATCH, SEQ, D_MODEL)
    y_sample = xs[1].reshape(DEC_BATCH, DEC_SEQ, D_MODEL)
    return (y_prompt, y_sample, states[0][0], states[0][1], states[1][0], states[1][1])
```
